```python
import jax, jax.numpy as jnp
from jax import lax
import numpy as np

D_MODEL = 1024
BATCH = 4
SEQ = 4096
DEPTH = 2
DEC_BATCH = 32
DEC_SEQ = 1
PAST_LEN = 16384
PAGE_SIZE = 128

N_A_LAYERS = (DEPTH + 1) // 2
N_C_LAYERS = DEPTH // 2
A_HEADS = 4
A_HEAD_DIM = D_MODEL // 8
A_WIDTH = A_HEADS * A_HEAD_DIM
CHUNK = 128
B_GROUPS = 4
B_WIDTH = D_MODEL // 2
B_GROUP_CH = B_WIDTH // B_GROUPS
C_GROUPS = ((128, 1), (512, 4), (2048, 16))
C_HEADS = 4
C_HEAD_DIM = D_MODEL // 8
C_WIDTH = C_HEADS * C_HEAD_DIM
D_FF = 2752
NORM_EPS = 1e-6
NEG_INF = -1e30
A_IN_WIDTH = 4 * A_WIDTH + 2 * A_HEADS + 2 * B_WIDTH
C_IN_WIDTH = len(C_GROUPS) * 3 * C_WIDTH

kernel_name = "hybrid_mlstm_sgu_dilated_swa_step"

F32 = jnp.float32


def rmsnorm(x, g):
    xf = x.astype(F32)
    y = xf * lax.rsqrt(jnp.mean(xf * xf, axis=-1, keepdims=True) + NORM_EPS)
    return (y * g.astype(F32)).astype(x.dtype)


def swiglu(h, wg, wu, wd):
    return (jax.nn.silu(h @ wg) * (h @ wu)) @ wd


def mlstm_chunk(carry, inp):
    C0, n0, m0 = carry
    q, k, v, ig, lf = inp
    L = q.shape[1]
    Ft = jnp.moveaxis(jnp.cumsum(lf, axis=1), 1, 2)
    it = jnp.moveaxis(ig, 1, 2)
    causal = jnp.tril(jnp.ones((L, L), bool))
    logD = jnp.where(causal, Ft[..., :, None] - Ft[..., None, :] + it[..., None, :], NEG_INF)
    a = m0[..., None] + Ft
    m = jnp.maximum(a, jnp.max(logD, axis=-1))
    S = jnp.einsum('bthd,bshd->bhts', q, k) * jnp.exp(logD - m[..., None])
    inter = jnp.exp(a - m)
    num = jnp.einsum('bhts,bshd->bhtd', S, v) + inter[..., None] * jnp.einsum('bhvk,bthk->bhtv', C0, q)
    den = jnp.sum(S, axis=-1) + inter * jnp.einsum('bhk,bthk->bht', n0, q)
    h = num / jnp.maximum(jnp.abs(den), jnp.exp(-m))[..., None]
    mL = m[..., -1]
    w = jnp.exp(Ft[..., -1:] - Ft + it - mL[..., None])
    decay = jnp.exp(a[..., -1] - mL)
    C1 = decay[..., None, None] * C0 + jnp.einsum('bhs,bshv,bshk->bhvk', w, v, k)
    n1 = decay[..., None] * n0 + jnp.einsum('bhs,bshk->bhk', w, k)
    return (C1, n1, mL), jnp.moveaxis(h, 1, 2)


def mlstm_prompt(q, k, v, ig, lf):
    Bn, S, H, D = q.shape
    nch = S // CHUNK

    def chunks(a):
        return jnp.moveaxis(a.reshape((Bn, nch, CHUNK) + a.shape[2:]), 1, 0)

    init = (jnp.zeros((Bn, H, D, D), F32), jnp.zeros((Bn, H, D), F32), jnp.zeros((Bn, H), F32))
    state, h = lax.scan(mlstm_chunk, init, (chunks(q), chunks(k), chunks(v), chunks(ig), chunks(lf)))
    return state, jnp.moveaxis(h, 0, 1).reshape(Bn, S, H, D)


def sgu_mix(u, vn, w_s, b_s):
    Bn, T, _ = u.shape
    nb = -(-T // CHUNK)
    pad = nb * CHUNK - T
    vp = jnp.pad(vn, ((0, 0), (0, pad), (0, 0))).reshape(Bn, nb, CHUNK, B_GROUPS, B_GROUP_CH)
    wm = jnp.where(jnp.tril(jnp.ones((CHUNK, CHUNK), bool)), w_s.astype(F32), 0.0)
    mixed = jnp.einsum('gts,bnsgc->bntgc', wm, vp) + b_s.astype(F32).T[:, :, None]
    mixed = mixed.reshape(Bn, nb * CHUNK, B_WIDTH)[:, :T]
    return u * mixed


def mixer_ab(h, w_in, b_if, sgu_g, sgu_w, sgu_b, w_out, state):
    Bn, T, _ = h.shape
    z = (h @ w_in).astype(F32)
    s4 = 4 * A_WIDTH
    q, k, v, o, gates, u, gv = jnp.split(
        z, [A_WIDTH, 2 * A_WIDTH, 3 * A_WIDTH, s4, s4 + 2 * A_HEADS, s4 + 2 * A_HEADS + B_WIDTH], axis=-1)
    heads = lambda t: t.reshape(Bn, T, A_HEADS, A_HEAD_DIM)
    q, k, v = heads(q), heads(k) * (A_HEAD_DIM ** -0.5), heads(v)
    gates = gates + b_if.astype(F32)
    ig = gates[..., :A_HEADS]
    lf = jax.nn.log_sigmoid(gates[..., A_HEADS:])
    if state is None:
        new_state, hm = mlstm_prompt(q, k, v, ig, lf)
    else:
        st = tuple(s.astype(F32) for s in state)
        new_state, hm = mlstm_chunk(st, (q, k, v, ig, lf))
    hm = jax.nn.sigmoid(o) * hm.reshape(Bn, T, A_WIDTH)
    vn = rmsnorm(gv, sgu_g)
    yb = sgu_mix(u, vn, sgu_w, sgu_b)
    out = jnp.concatenate([hm, yb], axis=-1).astype(h.dtype) @ w_out
    return out, new_state, vn


def swa_prompt_group(q, k, v, win, dil, slopes):
    Bn, S, H, E = q.shape
    blk = win // dil
    Ls = S // dil
    nb = -(-Ls // blk)
    Lp = nb * blk

    def sub(a):
        a = a.reshape(Bn, Ls, dil, H, E).transpose(0, 2, 1, 3, 4)
        a = jnp.pad(a, ((0, 0), (0, 0), (0, Lp - Ls), (0, 0), (0, 0)))
        return a.reshape(Bn, dil, nb, blk, H, E)

    def band(a):
        prev = jnp.pad(a, ((0, 0), (0, 0), (1, 0), (0, 0), (0, 0), (0, 0)))[:, :, :-1]
        return jnp.concatenate([prev, a], axis=3)

    qb, kk, vv = sub(q), band(sub(k)), band(sub(v))
    qi = jnp.arange(blk)[:, None]
    kc = jnp.arange(2 * blk)[None, :]
    delta = blk + qi - kc
    valid = (delta >= 0) & (delta <= blk)
    first = (jnp.arange(nb)[:, None, None] > 0) | (kc[None] >= blk)
    mask = valid[None] & first
    bias = -slopes[:, None, None] * (delta * dil).astype(F32)[None]
    s = jnp.einsum('brnqhe,brnkhe->brnhqk', qb, kk) * (E ** -0.5) + bias
    s = jnp.where(mask[:, None], s, NEG_INF)
    m = jnp.max(s, axis=-1)
    p = jnp.exp(s - m[..., None])
    l = jnp.sum(p, axis=-1)
    o = jnp.einsum('brnhqk,brnkhe->brnqhe', p, vv)

    def unsub(a):
        a = a.reshape((Bn, dil, Lp) + a.shape[4:])[:, :, :Ls]
        a = jnp.moveaxis(a, 1, 2)
        return a.reshape((Bn, S) + a.shape[3:])

    return unsub(jnp.moveaxis(m, 3, 4)), unsub(jnp.moveaxis(l, 3, 4)), unsub(o)


def swa_sample_group(q, k, v, kb, vb, win, dil, slopes):
    Nb, T = q.shape[:2]
    Lb = kb.shape[1]
    steps = win // dil
    kall = jnp.concatenate([kb, k], axis=1)
    vall = jnp.concatenate([vb, v], axis=1)
    a = jnp.arange(steps + 1)
    idx = Lb + jnp.arange(T)[:, None] - a[None, :] * dil
    valid = idx >= 0
    idx = jnp.maximum(idx, 0)
    kg, vg = kall[:, idx], vall[:, idx]
    bias = -slopes[:, None, None] * (a * dil).astype(F32)[None, None, :]
    s = jnp.einsum('nthe,ntahe->nhta', q, kg) * (q.shape[-1] ** -0.5) + bias
    s = jnp.where(valid, s, NEG_INF)
    m = jnp.max(s, axis=-1)
    p = jnp.exp(s - m[..., None])
    l = jnp.sum(p, axis=-1)
    o = jnp.einsum('nhta,ntahe->nthe', p, vg)
    return jnp.moveaxis(m, 1, 2), jnp.moveaxis(l, 1, 2), o


def mixer_c(h, w_in, w_out, bufs):
    Bn, T, _ = h.shape
    ng = len(C_GROUPS)
    z = (h @ w_in).astype(F32).reshape(Bn, T, ng, 3, C_HEADS, C_HEAD_DIM)
    slopes = (2.0 ** (-8.0 * jnp.arange(1, ng * C_HEADS + 1, dtype=F32) / (ng * C_HEADS))).reshape(ng, C_HEADS)
    ms, ls, os_, new = [], [], [], []
    for g, (win, dil) in enumerate(C_GROUPS):
        q, k, v = z[:, :, g, 0], z[:, :, g, 1], z[:, :, g, 2]
        if bufs is None:
            m, l, o = swa_prompt_group(q, k, v, win, dil, slopes[g])
            keep = min(win, T)
            new.append(jnp.stack([k[:, T - keep:], v[:, T - keep:]], axis=2).astype(h.dtype))
        else:
            buf = bufs[g].astype(F32)
            m, l, o = swa_sample_group(q, k, v, buf[:, :, 0], buf[:, :, 1], win, dil, slopes[g])
            new.append(jnp.stack([k, v], axis=2).astype(h.dtype))
        ms.append(m); ls.append(l); os_.append(o)
    m = jnp.stack(ms); l = jnp.stack(ls); o = jnp.stack(os_)
    w = jnp.exp(m - jnp.max(m, axis=0))
    out = jnp.sum(w[..., None] * o, axis=0) / jnp.sum(w * l, axis=0)[..., None]
    return out.reshape(Bn, T, C_WIDTH).astype(h.dtype) @ w_out, new


def trunk(x, st_C, st_n, st_m, kv_bufs, norm_g, ffn_w_gate, ffn_w_up, ffn_w_down,
          a_w_in, a_b_if, sgu_norm_g, sgu_w, sgu_b, a_w_out, c_w_in, c_w_out, final_norm_g):
    is_prompt = st_C is None
    nC, nn_, nm, nv = [], [], [], []
    nkv = [[] for _ in C_GROUPS]
    for l in range(DEPTH):
        j = l // 2
        x = x + 0.5 * swiglu(rmsnorm(x, norm_g[l, 0]), ffn_w_gate[l, 0], ffn_w_up[l, 0], ffn_w_down[l, 0])
        hn = rmsnorm(x, norm_g[l, 1])
        if l % 2 == 0:
            state = None if is_prompt else (st_C[j], st_n[j], st_m[j])
            y, (C, n, m), vn = mixer_ab(hn, a_w_in[j], a_b_if[j], sgu_norm_g[j], sgu_w[j], sgu_b[j], a_w_out[j], state)
            nC.append(C.astype(x.dtype)); nn_.append(n.astype(x.dtype)); nm.append(m.astype(x.dtype))
            nv.append(vn.astype(x.dtype))
        else:
            bufs = None if is_prompt else tuple(b[j] for b in kv_bufs)
            y, kvs = mixer_c(hn, c_w_in[j], c_w_out[j], bufs)
            for g in range(len(C_GROUPS)):
                nkv[g].append(kvs[g])
        x = x + y
        x = x + 0.5 * swiglu(rmsnorm(x, norm_g[l, 2]), ffn_w_gate[l, 1], ffn_w_up[l, 1], ffn_w_down[l, 1])
    y = rmsnorm(x, final_norm_g)
    return (y, jnp.stack(nC), jnp.stack(nn_), jnp.stack(nm), jnp.stack(nv),
            [jnp.stack(a) for a in nkv])


def setup_inputs(seed: int = 0) -> dict:
    key = jax.random.key(seed)
    ks = jax.random.split(key, 24)
    nrm = lambda k, shape, s: jax.random.normal(k, shape, F32) * s
    inp = {}
    inp["x_prompt"] = nrm(ks[0], (BATCH, SEQ, D_MODEL), 1.0)
    inp["x_sample"] = nrm(ks[1], (DEC_BATCH, DEC_SEQ, D_MODEL), 1.0)
    inp["state_mlstm_C"] = nrm(ks[2], (N_A_LAYERS, DEC_BATCH, A_HEADS, A_HEAD_DIM, A_HEAD_DIM), A_HEAD_DIM ** -0.5)
    inp["state_mlstm_n"] = nrm(ks[3], (N_A_LAYERS, DEC_BATCH, A_HEADS, A_HEAD_DIM), A_HEAD_DIM ** -0.5)
    inp["state_mlstm_m"] = nrm(ks[4], (N_A_LAYERS, DEC_BATCH, A_HEADS), 0.5)
    for g, (win, dil) in enumerate(C_GROUPS):
        L = min(win, PAST_LEN)
        inp["cache_swa_kv%d" % g] = nrm(ks[5 + g], (N_C_LAYERS, DEC_BATCH, L, 2, C_HEADS, C_HEAD_DIM), 1.0)
    inp["norm_g"] = 1.0 + nrm(ks[8], (DEPTH, 3, D_MODEL), 0.02)
    inp["ffn_w_gate"] = nrm(ks[9], (DEPTH, 2, D_MODEL, D_FF), D_MODEL ** -0.5)
    inp["ffn_w_up"] = nrm(ks[10], (DEPTH, 2, D_MODEL, D_FF), D_MODEL ** -0.5)
    inp["ffn_w_down"] = nrm(ks[11], (DEPTH, 2, D_FF, D_MODEL), D_FF ** -0.5)
    inp["a_w_in"] = nrm(ks[12], (N_A_LAYERS, D_MODEL, A_IN_WIDTH), D_MODEL ** -0.5)
    f_bias = jnp.linspace(3.0, 6.0, A_HEADS, dtype=F32)
    inp["a_b_if"] = jnp.concatenate([nrm(ks[13], (N_A_LAYERS, A_HEADS), 0.1),
                                     f_bias + nrm(ks[14], (N_A_LAYERS, A_HEADS), 0.1)], axis=-1)
    inp["sgu_norm_g"] = 1.0 + nrm(ks[15], (N_A_LAYERS, B_WIDTH), 0.02)
    inp["sgu_w"] = nrm(ks[16], (N_A_LAYERS, B_GROUPS, CHUNK, CHUNK), CHUNK ** -0.5)
    inp["sgu_b"] = 1.0 + nrm(ks[17], (N_A_LAYERS, B_GROUPS, CHUNK), 0.1)
    inp["a_w_out"] = nrm(ks[18], (N_A_LAYERS, A_WIDTH + B_WIDTH, D_MODEL), (A_WIDTH + B_WIDTH) ** -0.5)
    inp["c_w_in"] = nrm(ks[19], (N_C_LAYERS, D_MODEL, C_IN_WIDTH), D_MODEL ** -0.5)
    inp["c_w_out"] = nrm(ks[20], (N_C_LAYERS, C_WIDTH, D_MODEL), C_WIDTH ** -0.5)
    inp["final_norm_g"] = 1.0 + nrm(ks[21], (D_MODEL,), 0.02)
    return inp


def reference(x_prompt, x_sample, state_mlstm_C, state_mlstm_n, state_mlstm_m,
              cache_swa_kv0, cache_swa_kv1, cache_swa_kv2,
              norm_g, ffn_w_gate, ffn_w_up, ffn_w_down, a_w_in, a_b_if, sgu_norm_g, sgu_w, sgu_b,
              a_w_out, c_w_in, c_w_out, final_norm_g):
    y_prompt, pC, pn, pm, _, pkv = trunk(
        x_prompt, None, None, None, None, norm_g, ffn_w_gate, ffn_w_up, ffn_w_down,
        a_w_in, a_b_if, sgu_norm_g, sgu_w, sgu_b, a_w_out, c_w_in, c_w_out, final_norm_g)
    y_sample, sC, sn, sm, sv, skv = trunk(
        x_sample, state_mlstm_C, state_mlstm_n, state_mlstm_m, (cache_swa_kv0, cache_swa_kv1, cache_swa_kv2),
        norm_g, ffn_w_gate, ffn_w_up, ffn_w_down,
        a_w_in, a_b_if, sgu_norm_g, sgu_w, sgu_b, a_w_out, c_w_in, c_w_out, final_norm_g)
    return (y_prompt, y_sample, pC, pn, pm, sC, sn, sm, sv,
            pkv[0], pkv[1], pkv[2], skv[0], skv[1], skv[2])
```

```python
import functools

import jax
import jax.numpy as jnp
from jax import lax
from jax.experimental import pallas as pl
from jax.experimental.pallas import tpu as pltpu

F32 = jnp.float32
BF16 = jnp.bfloat16

D_MODEL = 1024
D_FF = 2752
HEADS = 4
HEAD_DIM = 128
WIDTH = HEADS * HEAD_DIM
CHUNK = 128
SWA_GROUPS = ((128, 1), (512, 4), (2048, 16))
NORM_EPS = 1e-6
NEG_INF = -1e30

LANES = 128
FF_CHUNK = 256
FF_PAD = -(-D_FF // FF_CHUNK) * FF_CHUNK
A_IN_PAD = 4 * WIDTH + 2 * WIDTH + LANES
COL_U = 4 * WIDTH
COL_GV = 5 * WIDTH
COL_GATES = 6 * WIDTH
C_GROUP_COLS = 3 * WIDTH

VMEM_LIMIT = 56 * 1024 * 1024


def _params(semantics):
    return pltpu.CompilerParams(dimension_semantics=semantics, vmem_limit_bytes=VMEM_LIMIT)


def _resident(shape):
    nd = len(shape)
    return pl.BlockSpec(shape, lambda *_: (0,) * nd, pipeline_mode=pl.Buffered(1))


def _rms(x, g):
    ms = jnp.mean(x * x, axis=-1, keepdims=True)
    return x * lax.rsqrt(ms + NORM_EPS) * g


def _dot(a, b):
    return jnp.dot(a, b, preferred_element_type=F32)


def _dot_nt(a, b):
    return lax.dot_general(a, b, (((1,), (1,)), ((), ())), preferred_element_type=F32)


def _log_sigmoid(x):
    return jnp.minimum(x, 0.0) - jnp.log1p(jnp.exp(-jnp.abs(x)))


def _ffn_body(*refs, final):
    if final:
        x_ref, g_ref, wg_ref, wu_ref, wd_ref, fg_ref, o_ref, h_scr = refs
    else:
        x_ref, g_ref, wg_ref, wu_ref, wd_ref, o_ref, h_scr = refs
    x = x_ref[...]
    h_scr[...] = _rms(x, g_ref[...]).astype(BF16)
    acc = None
    for c in range(FF_PAD // FF_CHUNK):
        cols = slice(c * FF_CHUNK, (c + 1) * FF_CHUNK)
        h = h_scr[...]
        gate = _dot(h, wg_ref[:, cols])
        up = _dot(h, wu_ref[:, cols])
        act = (gate * jax.nn.sigmoid(gate) * up).astype(BF16)
        part = _dot(act, wd_ref[cols, :])
        acc = part if acc is None else acc + part
    y = x + 0.5 * acc
    if final:
        y = _rms(y, fg_ref[...])
    o_ref[...] = y


def _ffn(x, g, wg, wu, wd, final_g=None, *, tm):
    m = x.shape[0]
    final = final_g is not None
    row = pl.BlockSpec((tm, D_MODEL), lambda i: (i, 0))
    in_specs = [row, _resident((1, D_MODEL)), _resident(wg.shape), _resident(wu.shape), _resident(wd.shape)]
    args = [x, g.reshape(1, D_MODEL), wg, wu, wd]
    if final:
        in_specs.append(_resident((1, D_MODEL)))
        args.append(final_g.reshape(1, D_MODEL))
    return pl.pallas_call(
        functools.partial(_ffn_body, final=final),
        grid=(m // tm,),
        in_specs=in_specs,
        out_specs=row,
        out_shape=jax.ShapeDtypeStruct((m, D_MODEL), F32),
        scratch_shapes=[pltpu.VMEM((tm, D_MODEL), BF16)],
        compiler_params=_params(("arbitrary",)),
        name="ffn_final" if final else "ffn",
    )(*args)


def _mlstm_head(q, k, v, fc, fr, ic, ir, c0, n0, m0, causal):
    log_d = jnp.where(causal, fc - fr + ir, NEG_INF)
    a = m0 + fc
    m = jnp.maximum(a, jnp.max(log_d, axis=-1, keepdims=True))
    qb, kb, vb = q.astype(BF16), k.astype(BF16), v.astype(BF16)
    s = _dot_nt(qb, kb) * jnp.exp(log_d - m)
    inter = jnp.exp(a - m)
    num = _dot(s.astype(BF16), vb) + inter * _dot_nt(qb, c0.astype(BF16))
    den = jnp.sum(s, axis=-1, keepdims=True) + inter * jnp.sum(q * n0, axis=-1, keepdims=True)
    h = num / jnp.maximum(jnp.abs(den), jnp.exp(-m))
    m_last = m[CHUNK - 1:CHUNK, :]
    f_last = fc[CHUNK - 1:CHUNK, :]
    w = jnp.exp(f_last - fc + ic - m_last)
    decay = jnp.exp(m0 + f_last - m_last)
    c1 = decay * c0 + _dot((v * w).T.astype(BF16), kb)
    n1 = decay * n0 + jnp.sum(w * k, axis=0, keepdims=True)
    return h, c1, n1, m_last


def _exact_tri_dot(tri_bf16, x):
    x1 = x.astype(BF16)
    r1 = x - x1.astype(F32)
    x2 = r1.astype(BF16)
    x3 = (r1 - x2.astype(F32)).astype(BF16)
    return _dot(tri_bf16, x1) + _dot(tri_bf16, x2) + _dot(tri_bf16, x3)


def _mixer_ab_body(x_ref, g_ref, win_ref, bif_ref, sg_ref, sw_ref, sbt_ref, wout_ref,
                   y_ref, c_ref, n_ref, m_ref, z_scr, cat_scr):
    nb = x_ref.shape[0]

    @pl.when(pl.program_id(0) == 0)
    def _():
        c_ref[...] = jnp.zeros_like(c_ref)
        n_ref[...] = jnp.zeros_like(n_ref)
        m_ref[...] = jnp.zeros_like(m_ref)

    x = x_ref[...].reshape(nb * CHUNK, D_MODEL)
    hn = _rms(x, g_ref[...]).astype(BF16)
    for c0 in range(0, A_IN_PAD, WIDTH):
        c1 = min(c0 + WIDTH, A_IN_PAD)
        z_scr[:, c0:c1] = _dot(hn, win_ref[:, c0:c1])

    row = lax.broadcasted_iota(jnp.int32, (CHUNK, CHUNK), 0)
    col = lax.broadcasted_iota(jnp.int32, (CHUNK, CHUNK), 1)
    causal = col <= row
    tri = jnp.where(causal, 1.0, 0.0).astype(BF16)
    scale = HEAD_DIM ** -0.5
    sgu_w = [jnp.where(causal, sw_ref[g], 0.0).astype(BF16) for g in range(HEADS)]

    def per_batch(b, carry):
        rows = pl.ds(pl.multiple_of(b * CHUNK, CHUNK), CHUNK)
        gates = z_scr[rows, COL_GATES:COL_GATES + LANES] + bif_ref[...]
        lg = jnp.where(col < HEADS, gates, _log_sigmoid(gates))
        fcum = _exact_tri_dot(tri, lg)
        lg_t = lg.T
        fcum_t = fcum.T
        for h in range(HEADS):
            hs = slice(h * HEAD_DIM, (h + 1) * HEAD_DIM)
            q = z_scr[rows, hs]
            k = z_scr[rows, WIDTH + h * HEAD_DIM:WIDTH + (h + 1) * HEAD_DIM] * scale
            v = z_scr[rows, 2 * WIDTH + h * HEAD_DIM:2 * WIDTH + (h + 1) * HEAD_DIM]
            o = z_scr[rows, 3 * WIDTH + h * HEAD_DIM:3 * WIDTH + (h + 1) * HEAD_DIM]
            hh, c1, n1, m1 = _mlstm_head(
                q, k, v,
                fcum[:, HEADS + h:HEADS + h + 1], fcum_t[HEADS + h:HEADS + h + 1, :],
                lg[:, h:h + 1], lg_t[h:h + 1, :],
                c_ref[b, h], n_ref[b, h], m_ref[b, h][:, 0:1], causal)
            c_ref[b, h] = c1
            n_ref[b, h] = n1
            m_ref[b, h] = jnp.broadcast_to(m1, (1, LANES))
            cat_scr[rows, hs] = (jax.nn.sigmoid(o) * hh).astype(BF16)
        vn = _rms(z_scr[rows, COL_GV:COL_GV + WIDTH], sg_ref[...])
        for g in range(HEADS):
            gs = slice(g * CHUNK, (g + 1) * CHUNK)
            mixed = _dot(sgu_w[g], vn[:, gs].astype(BF16)) + sbt_ref[:, g:g + 1]
            u = z_scr[rows, COL_U + g * CHUNK:COL_U + (g + 1) * CHUNK]
            cat_scr[rows, WIDTH + g * CHUNK:WIDTH + (g + 1) * CHUNK] = (u * mixed).astype(BF16)
        return carry

    lax.fori_loop(0, nb, per_batch, 0)
    y = x + _dot(cat_scr[...], wout_ref[...])
    y_ref[...] = y.reshape(nb, CHUNK, D_MODEL)


def _mixer_ab_prompt(x, g, w_in, b_if, sgu_g, sgu_w, sgu_bt, w_out):
    nb, s, _ = x.shape
    blk = pl.BlockSpec((nb, CHUNK, D_MODEL), lambda c: (0, c, 0))
    return pl.pallas_call(
        _mixer_ab_body,
        grid=(s // CHUNK,),
        in_specs=[blk, _resident((1, D_MODEL)), _resident(w_in.shape), _resident((1, LANES)),
                  _resident((1, WIDTH)), _resident(sgu_w.shape), _resident(sgu_bt.shape), _resident(w_out.shape)],
        out_specs=[blk,
                   pl.BlockSpec((nb, HEADS, HEAD_DIM, HEAD_DIM), lambda c: (0, 0, 0, 0)),
                   pl.BlockSpec((nb, HEADS, 1, HEAD_DIM), lambda c: (0, 0, 0, 0)),
                   pl.BlockSpec((nb, HEADS, 1, LANES), lambda c: (0, 0, 0, 0))],
        out_shape=[jax.ShapeDtypeStruct(x.shape, F32),
                   jax.ShapeDtypeStruct((nb, HEADS, HEAD_DIM, HEAD_DIM), F32),
                   jax.ShapeDtypeStruct((nb, HEADS, 1, HEAD_DIM), F32),
                   jax.ShapeDtypeStruct((nb, HEADS, 1, LANES), F32)],
        scratch_shapes=[pltpu.VMEM((nb * CHUNK, A_IN_PAD), F32), pltpu.VMEM((nb * CHUNK, 2 * WIDTH), BF16)],
        compiler_params=_params(("arbitrary",)),
        name="mixer_ab_prompt",
    )(x, g.reshape(1, D_MODEL), w_in, b_if, sgu_g.reshape(1, WIDTH), sgu_w, sgu_bt, w_out)


def _alibi_slope(group, head):
    n = len(SWA_GROUPS) * HEADS
    return 2.0 ** (-8.0 * (group * HEADS + head + 1) / n)


def _swa_group_body(x_ref, g_ref, w_ref, o_ref, st_ref, kv_ref, q_scr, k_scr, v_scr, *, group, dil, tm, rb):
    it = pl.program_id(2)
    scale = HEAD_DIM ** -0.5
    qi = lax.broadcasted_iota(jnp.int32, (CHUNK, 2 * CHUNK), 0)
    kc = lax.broadcasted_iota(jnp.int32, (CHUNK, 2 * CHUNK), 1)
    delta = CHUNK + qi - kc
    valid = (delta >= 0) & (delta <= CHUNK)
    dist = (delta * dil).astype(F32)
    lane = lax.broadcasted_iota(jnp.int32, (CHUNK, LANES), 1)

    @pl.when(it == 0)
    def _():
        k_scr[:, 0:CHUNK, :] = jnp.zeros((rb, CHUNK, WIDTH), BF16)
        v_scr[:, 0:CHUNK, :] = jnp.zeros((rb, CHUNK, WIDTH), BF16)

    for r in range(rb):
        xs = x_ref[0, :, r * D_MODEL:(r + 1) * D_MODEL]
        hn = _rms(xs, g_ref[...]).astype(BF16)
        q_scr[...] = _dot(hn, w_ref[:, 0:WIDTH]).astype(BF16)
        kf = _dot(hn, w_ref[:, WIDTH:2 * WIDTH])
        k_scr[r, CHUNK:, :] = kf.astype(BF16)
        kv_ref[0, :, r * 2 * WIDTH:r * 2 * WIDTH + WIDTH] = kf[tm - CHUNK:, :]
        vf = _dot(hn, w_ref[:, 2 * WIDTH:3 * WIDTH])
        v_scr[r, CHUNK:, :] = vf.astype(BF16)
        kv_ref[0, :, r * 2 * WIDTH + WIDTH:(r + 1) * 2 * WIDTH] = vf[tm - CHUNK:, :]
        for j in range(tm // CHUNK):
            mask = valid if j > 0 else valid & (kc >= jnp.where(it > 0, 0, CHUNK))
            stats = jnp.zeros((CHUNK, LANES), F32)
            for h in range(HEADS):
                hs = slice(h * HEAD_DIM, (h + 1) * HEAD_DIM)
                qj = q_scr[j * CHUNK:(j + 1) * CHUNK, hs]
                kk = k_scr[r, j * CHUNK:(j + 2) * CHUNK, hs]
                vv = v_scr[r, j * CHUNK:(j + 2) * CHUNK, hs]
                s = _dot_nt(qj, kk) * scale + (-_alibi_slope(group, h)) * dist
                s = jnp.where(mask, s, NEG_INF)
                m = jnp.max(s, axis=-1, keepdims=True)
                p = jnp.exp(s - m)
                l = jnp.sum(p, axis=-1, keepdims=True)
                o_ref[0, j * CHUNK:(j + 1) * CHUNK, r * WIDTH + h * HEAD_DIM:r * WIDTH + (h + 1) * HEAD_DIM] = (
                    _dot(p.astype(BF16), vv))
                stats = jnp.where(lane == h, m, jnp.where(lane == HEADS + h, l, stats))
            st_ref[0, j * CHUNK:(j + 1) * CHUNK, r * LANES:(r + 1) * LANES] = stats
        k_scr[r, 0:CHUNK, :] = k_scr[r, tm:tm + CHUNK, :]
        v_scr[r, 0:CHUNK, :] = v_scr[r, tm:tm + CHUNK, :]


def _swa_group_prompt(x, g, w, *, group, tm, rb):
    win, dil = SWA_GROUPS[group]
    nb, s, _ = x.shape
    sub = s // dil
    xv = x.reshape(nb, sub, dil * D_MODEL)
    grid = (nb, dil // rb, sub // tm)
    o, st, kv = pl.pallas_call(
        functools.partial(_swa_group_body, group=group, dil=dil, tm=tm, rb=rb),
        grid=grid,
        in_specs=[pl.BlockSpec((1, tm, rb * D_MODEL), lambda b, r, t: (b, t, r)),
                  _resident((1, D_MODEL)), _resident(w.shape)],
        out_specs=[pl.BlockSpec((1, tm, rb * WIDTH), lambda b, r, t: (b, t, r)),
                   pl.BlockSpec((1, tm, rb * LANES), lambda b, r, t: (b, t, r)),
                   pl.BlockSpec((1, CHUNK, rb * 2 * WIDTH), lambda b, r, t: (b, 0, r))],
        out_shape=[jax.ShapeDtypeStruct((nb, sub, dil * WIDTH), F32),
                   jax.ShapeDtypeStruct((nb, sub, dil * LANES), F32),
                   jax.ShapeDtypeStruct((nb, CHUNK, dil * 2 * WIDTH), F32)],
        scratch_shapes=[pltpu.VMEM((tm, WIDTH), BF16),
                        pltpu.VMEM((rb, CHUNK + tm, WIDTH), BF16),
                        pltpu.VMEM((rb, CHUNK + tm, WIDTH), BF16)],
        compiler_params=_params(("arbitrary", "arbitrary", "arbitrary")),
        name="swa_group%d" % group,
    )(xv, g.reshape(1, D_MODEL), w)
    return (o.reshape(nb * s, WIDTH), st.reshape(nb * s, LANES), kv.reshape(nb, win, 2 * WIDTH))


def _merge_heads(outs, stats, rows):
    merged = []
    for h in range(HEADS):
        ms = [st[rows, h:h + 1] for st in stats]
        ls = [st[rows, HEADS + h:HEADS + h + 1] for st in stats]
        top = functools.reduce(jnp.maximum, ms)
        ws = [jnp.exp(m - top) for m in ms]
        num = sum(w * o[rows, h * HEAD_DIM:(h + 1) * HEAD_DIM] for w, o in zip(ws, outs))
        den = sum(w * l for w, l in zip(ws, ls))
        merged.append(num / den)
    return merged


def _swa_merge_body(o0_ref, o1_ref, o2_ref, s0_ref, s1_ref, s2_ref, x_ref, w_ref, y_ref, cat_scr):
    merged = _merge_heads((o0_ref, o1_ref, o2_ref), (s0_ref, s1_ref, s2_ref), slice(None))
    for h in range(HEADS):
        cat_scr[:, h * HEAD_DIM:(h + 1) * HEAD_DIM] = merged[h].astype(BF16)
    y_ref[...] = x_ref[...] + _dot(cat_scr[...], w_ref[...])


def _swa_merge(outs, stats, x, w_out, *, tm):
    m = x.shape[0]
    o_spec = pl.BlockSpec((tm, WIDTH), lambda i: (i, 0))
    s_spec = pl.BlockSpec((tm, LANES), lambda i: (i, 0))
    x_spec = pl.BlockSpec((tm, D_MODEL), lambda i: (i, 0))
    return pl.pallas_call(
        _swa_merge_body,
        grid=(m // tm,),
        in_specs=[o_spec] * 3 + [s_spec] * 3 + [x_spec, _resident(w_out.shape)],
        out_specs=x_spec,
        out_shape=jax.ShapeDtypeStruct((m, D_MODEL), F32),
        scratch_shapes=[pltpu.VMEM((tm, WIDTH), BF16)],
        compiler_params=_params(("arbitrary",)),
        name="swa_merge",
    )(*outs, *stats, x, w_out)


def _proj_body(x_ref, g_ref, w_ref, z_ref):
    z_ref[...] = _dot(_rms(x_ref[...], g_ref[...]).astype(BF16), w_ref[...])


def _proj(x, g, w):
    m, n = x.shape[0], w.shape[1]
    return pl.pallas_call(
        _proj_body,
        grid=(1,),
        in_specs=[_resident(x.shape), _resident((1, D_MODEL)), _resident(w.shape)],
        out_specs=pl.BlockSpec((m, n), lambda i: (0, 0)),
        out_shape=jax.ShapeDtypeStruct((m, n), F32),
        compiler_params=_params(("arbitrary",)),
        name="proj_rows",
    )(x, g.reshape(1, D_MODEL), w)


def _out_proj_body(c_ref, w_ref, x_ref, y_ref):
    y_ref[...] = x_ref[...] + _dot(c_ref[...].astype(BF16), w_ref[...])


def _out_proj(cat, w, x):
    return pl.pallas_call(
        _out_proj_body,
        grid=(1,),
        in_specs=[_resident(cat.shape), _resident(w.shape), _resident(x.shape)],
        out_specs=pl.BlockSpec(x.shape, lambda i: (0, 0)),
        out_shape=jax.ShapeDtypeStruct(x.shape, F32),
        compiler_params=_params(("arbitrary",)),
        name="out_proj_rows",
    )(cat, w, x)


def _mixer_ab_step_body(z_ref, bif_ref, sg_ref, w00_ref, b0_ref, c_ref, n_ref, m_ref,
                        cat_ref, c1_ref, n1_ref, m1_ref, vn_ref):
    nb = z_ref.shape[0]
    scale = HEAD_DIM ** -0.5
    eye = (lax.broadcasted_iota(jnp.int32, (HEAD_DIM, HEAD_DIM), 0)
           == lax.broadcasted_iota(jnp.int32, (HEAD_DIM, HEAD_DIM), 1)).astype(F32)
    lane = lax.broadcasted_iota(jnp.int32, (1, LANES), 1)
    for i in range(nb):
        zr = z_ref[i:i + 1, :]
        gates = zr[:, COL_GATES:COL_GATES + LANES] + bif_ref[...]
        m_new = jnp.zeros((1, LANES), F32)
        for h in range(HEADS):
            hs = slice(h * HEAD_DIM, (h + 1) * HEAD_DIM)
            q = zr[:, hs]
            k = zr[:, WIDTH + h * HEAD_DIM:WIDTH + (h + 1) * HEAD_DIM] * scale
            v = zr[:, 2 * WIDTH + h * HEAD_DIM:2 * WIDTH + (h + 1) * HEAD_DIM]
            o = zr[:, 3 * WIDTH + h * HEAD_DIM:3 * WIDTH + (h + 1) * HEAD_DIM]
            ig = gates[:, h:h + 1]
            lf = _log_sigmoid(gates[:, HEADS + h:HEADS + h + 1])
            c0 = c_ref[i, h]
            n0 = n_ref[i, h]
            m0 = m_ref[i:i + 1, h:h + 1]
            a = m0 + lf
            m = jnp.maximum(a, ig)
            s = jnp.sum(q * k, axis=-1, keepdims=True) * jnp.exp(ig - m)
            inter = jnp.exp(a - m)
            cq_col = jnp.sum(c0 * q, axis=-1, keepdims=True)
            cq = jnp.sum(eye * cq_col, axis=0, keepdims=True)
            v_col = jnp.sum(eye * v, axis=-1, keepdims=True)
            num = s * v + inter * cq
            den = s + inter * jnp.sum(n0 * q, axis=-1, keepdims=True)
            hh = num / jnp.maximum(jnp.abs(den), jnp.exp(-m))
            w = jnp.exp(ig - m)
            c1_ref[i, h] = inter * c0 + (w * v_col) * k
            n1_ref[i, h] = inter * n0 + w * k
            m_new = jnp.where(lane == h, m, m_new)
            cat_ref[i:i + 1, hs] = jax.nn.sigmoid(o) * hh
        m1_ref[i:i + 1, :] = m_new
        vn = _rms(zr[:, COL_GV:COL_GV + WIDTH], sg_ref[...])
        vn_ref[i:i + 1, :] = vn
        cat_ref[i:i + 1, WIDTH:2 * WIDTH] = zr[:, COL_U:COL_U + WIDTH] * (w00_ref[...] * vn + b0_ref[...])


def _mixer_ab_step(z, b_if, sgu_g, w00, b0, st_c, st_n, st_m, *, nb):
    n = z.shape[0]
    rows = lambda w: pl.BlockSpec((nb, w), lambda i: (i, 0))
    c_spec = pl.BlockSpec((nb, HEADS, HEAD_DIM, HEAD_DIM), lambda i: (i, 0, 0, 0))
    n_spec = pl.BlockSpec((nb, HEADS, 1, HEAD_DIM), lambda i: (i, 0, 0, 0))
    return pl.pallas_call(
        _mixer_ab_step_body,
        grid=(n // nb,),
        in_specs=[rows(A_IN_PAD), _resident((1, LANES)), _resident((1, WIDTH)), _resident((1, WIDTH)),
                  _resident((1, WIDTH)), c_spec, n_spec, rows(LANES)],
        out_specs=[rows(2 * WIDTH), c_spec, n_spec, rows(LANES), rows(WIDTH)],
        out_shape=[jax.ShapeDtypeStruct((n, 2 * WIDTH), F32),
                   jax.ShapeDtypeStruct(st_c.shape, F32),
                   jax.ShapeDtypeStruct(st_n.shape, F32),
                   jax.ShapeDtypeStruct((n, LANES), F32),
                   jax.ShapeDtypeStruct((n, WIDTH), F32)],
        compiler_params=_params(("arbitrary",)),
        name="mixer_ab_step",
    )(z, b_if, sgu_g.reshape(1, WIDTH), w00, b0, st_c, st_n, st_m)


def _swa_step_body(z_ref, kv0_ref, kv1_ref, kv2_ref, cat_ref, o_scr, st_scr):
    nb = z_ref.shape[0]
    scale = HEAD_DIM ** -0.5
    steps = (CHUNK - lax.broadcasted_iota(jnp.int32, (CHUNK, 1), 0)).astype(F32)
    lane = lax.broadcasted_iota(jnp.int32, (1, LANES), 1)
    for i in range(nb):
        zr = z_ref[i:i + 1, :]
        for gi, kv_ref in enumerate((kv0_ref, kv1_ref, kv2_ref)):
            dil = SWA_GROUPS[gi][1]
            base = gi * C_GROUP_COLS
            stats = jnp.zeros((1, LANES), F32)
            for h in range(HEADS):
                hs = slice(h * HEAD_DIM, (h + 1) * HEAD_DIM)
                q = zr[:, base + h * HEAD_DIM:base + (h + 1) * HEAD_DIM]
                k_new = zr[:, base + WIDTH + h * HEAD_DIM:base + WIDTH + (h + 1) * HEAD_DIM]
                v_new = zr[:, base + 2 * WIDTH + h * HEAD_DIM:base + 2 * WIDTH + (h + 1) * HEAD_DIM]
                kc = kv_ref[i, :, hs]
                vc = kv_ref[i, :, WIDTH + h * HEAD_DIM:WIDTH + (h + 1) * HEAD_DIM]
                s = (jnp.sum(kc * q, axis=-1, keepdims=True) * scale
                     + (-_alibi_slope(gi, h) * dil) * steps)
                s_new = jnp.sum(k_new * q, axis=-1, keepdims=True) * scale
                m = jnp.maximum(jnp.max(s, axis=0, keepdims=True), s_new)
                p = jnp.exp(s - m)
                p_new = jnp.exp(s_new - m)
                l = jnp.sum(p, axis=0, keepdims=True) + p_new
                o_scr[gi, :, hs] = jnp.sum(p * vc, axis=0, keepdims=True) + p_new * v_new
                stats = jnp.where(lane == h, m, jnp.where(lane == HEADS + h, l, stats))
            st_scr[gi] = stats
        merged = _merge_heads([o_scr.at[g] for g in range(3)], [st_scr.at[g] for g in range(3)], slice(None))
        for h in range(HEADS):
            cat_ref[i:i + 1, h * HEAD_DIM:(h + 1) * HEAD_DIM] = merged[h]


def _swa_step(z, caches, *, nb):
    n = z.shape[0]
    views = []
    specs = []
    for (win, dil), cache in zip(SWA_GROUPS, caches):
        rows = cache.shape[1]
        views.append(cache.reshape(n, rows // dil, dil * 2 * WIDTH))
        specs.append(pl.BlockSpec((nb, CHUNK, 2 * WIDTH), lambda i: (i, 0, 0)))
    return pl.pallas_call(
        _swa_step_body,
        grid=(n // nb,),
        in_specs=[pl.BlockSpec((nb, z.shape[1]), lambda i: (i, 0))] + specs,
        out_specs=pl.BlockSpec((nb, WIDTH), lambda i: (i, 0)),
        out_shape=jax.ShapeDtypeStruct((n, WIDTH), F32),
        scratch_shapes=[pltpu.VMEM((3, 1, WIDTH), F32), pltpu.VMEM((3, 1, LANES), F32)],
        compiler_params=_params(("arbitrary",)),
        name="swa_step",
    )(z, *views)


def _pad_cols(w, n):
    return jnp.pad(w, ((0, 0), (0, n - w.shape[1])))


def kernel(x_prompt, x_sample, state_mlstm_C, state_mlstm_n, state_mlstm_m, cache_swa_kv0, cache_swa_kv1, cache_swa_kv2, norm_g, ffn_w_gate, ffn_w_up, ffn_w_down, a_w_in, a_b_if, sgu_norm_g, sgu_w, sgu_b, a_w_out, c_w_in, c_w_out, final_norm_g):
    nb, s, _ = x_prompt.shape
    ns = x_sample.shape[0]
    assert x_sample.shape[1] == 1 and s % max(w for w, _ in SWA_GROUPS) == 0
    for (win, dil), cache in zip(SWA_GROUPS, (cache_swa_kv0, cache_swa_kv1, cache_swa_kv2)):
        assert cache.shape[2] == win and win // dil == CHUNK

    def ffn_weights(l, i):
        wg = _pad_cols(ffn_w_gate[l, i], FF_PAD).astype(BF16)
        wu = _pad_cols(ffn_w_up[l, i], FF_PAD).astype(BF16)
        wd = jnp.pad(ffn_w_down[l, i], ((0, FF_PAD - D_FF), (0, 0))).astype(BF16)
        return wg, wu, wd

    ffn_w = [[ffn_weights(l, i) for i in range(2)] for l in range(2)]
    g_lo, g_hi = 4 * WIDTH, 4 * WIDTH + 2 * HEADS
    a_in = jnp.concatenate([a_w_in[0][:, :g_lo], a_w_in[0][:, g_hi:], _pad_cols(a_w_in[0][:, g_lo:g_hi], LANES)],
                           axis=1).astype(BF16)
    b_if = _pad_cols(a_b_if[0].reshape(1, 2 * HEADS), LANES)
    a_out = a_w_out[0].astype(BF16)
    c_in = c_w_in[0].astype(BF16)
    c_out = c_w_out[0].astype(BF16)
    sgu_bt = _pad_cols(sgu_b[0].T, LANES)
    sgu_w00 = jnp.repeat(sgu_w[0, :, 0, 0], CHUNK).reshape(1, WIDTH)
    sgu_b0 = jnp.repeat(sgu_b[0, :, 0], CHUNK).reshape(1, WIDTH)

    xp = x_prompt.reshape(nb * s, D_MODEL)
    xp = _ffn(xp, norm_g[0, 0], *ffn_w[0][0], tm=512)
    xp, p_c, p_n, p_m = _mixer_ab_prompt(xp.reshape(nb, s, D_MODEL), norm_g[0, 1], a_in, b_if,
                                          sgu_norm_g[0], sgu_w[0], sgu_bt, a_out)
    xp = _ffn(xp.reshape(nb * s, D_MODEL), norm_g[0, 2], *ffn_w[0][1], tm=512)
    xp = _ffn(xp, norm_g[1, 0], *ffn_w[1][0], tm=512)
    xp3 = xp.reshape(nb, s, D_MODEL)
    outs, stats, p_kv = [], [], []
    for gi, (tm, rb) in enumerate(((512, 1), (512, 1), (256, 2))):
        o, st, kv = _swa_group_prompt(xp3, norm_g[1, 1], c_in[:, gi * C_GROUP_COLS:(gi + 1) * C_GROUP_COLS],
                                      group=gi, tm=tm, rb=rb)
        outs.append(o)
        stats.append(st)
        p_kv.append(kv.reshape(1, nb, SWA_GROUPS[gi][0], 2, HEADS, HEAD_DIM))
    xp = _swa_merge(outs, stats, xp, c_out, tm=512)
    y_prompt = _ffn(xp, norm_g[1, 2], *ffn_w[1][1], final_norm_g, tm=512).reshape(nb, s, D_MODEL)

    xs = x_sample.reshape(ns, D_MODEL)
    xs = _ffn(xs, norm_g[0, 0], *ffn_w[0][0], tm=ns)
    z = _proj(xs, norm_g[0, 1], a_in)
    cat, s_c, s_n, s_m, s_v = _mixer_ab_step(
        z, b_if, sgu_norm_g[0], sgu_w00, sgu_b0, state_mlstm_C[0],
        state_mlstm_n[0].reshape(ns, HEADS, 1, HEAD_DIM), _pad_cols(state_mlstm_m[0], LANES), nb=8)
    xs = _out_proj(cat, a_out, xs)
    xs = _ffn(xs, norm_g[0, 2], *ffn_w[0][1], tm=ns)
    xs = _ffn(xs, norm_g[1, 0], *ffn_w[1][0], tm=ns)
    z = _proj(xs, norm_g[1, 1], c_in)
    caches = [c[0].reshape(ns, c.shape[2], 2 * WIDTH) for c in (cache_swa_kv0, cache_swa_kv1, cache_swa_kv2)]
    cat = _swa_step(z, caches, nb=8)
    xs = _out_proj(cat, c_out, xs)
    y_sample = _ffn(xs, norm_g[1, 2], *ffn_w[1][1], final_norm_g, tm=ns).reshape(ns, 1, D_MODEL)
    s_kv = [z[:, gi * C_GROUP_COLS + WIDTH:(gi + 1) * C_GROUP_COLS].reshape(1, ns, 1, 2, HEADS, HEAD_DIM)
            for gi in range(3)]

    return (y_prompt, y_sample,
            p_c.reshape(1, nb, HEADS, HEAD_DIM, HEAD_DIM), p_n.reshape(1, nb, HEADS, HEAD_DIM),
            p_m[:, :, 0, 0].reshape(1, nb, HEADS),
            s_c.reshape(1, ns, HEADS, HEAD_DIM, HEAD_DIM), s_n.reshape(1, ns, HEADS, HEAD_DIM),
            s_m[:, :HEADS].reshape(1, ns, HEADS), s_v.reshape(1, ns, 1, WIDTH),
            p_kv[0], p_kv[1], p_kv[2], s_kv[0], s_kv[1], s_kv[2])
```

```python
import functools

import jax
import jax.numpy as jnp
from jax import lax
from jax.experimental import pallas as pl
from jax.experimental.pallas import tpu as pltpu

F32 = jnp.float32
BF16 = jnp.bfloat16

D_MODEL = 1024
D_FF = 2752
HEADS = 4
HEAD_DIM = 128
WIDTH = HEADS * HEAD_DIM
CHUNK = 128
SWA_GROUPS = ((128, 1), (512, 4), (2048, 16))
NORM_EPS = 1e-6
NEG_INF = -1e30

LANES = 128
FF_CHUNK = 256
FF_PAD = -(-D_FF // FF_CHUNK) * FF_CHUNK
A_IN_PAD = 4 * WIDTH + 2 * WIDTH + LANES
COL_U = 4 * WIDTH
COL_GV = 5 * WIDTH
COL_GATES = 6 * WIDTH
C_GROUP_COLS = 3 * WIDTH

VMEM_LIMIT = 56 * 1024 * 1024


def _params(semantics):
    return pltpu.CompilerParams(dimension_semantics=semantics, vmem_limit_bytes=VMEM_LIMIT)


def _resident(shape):
    nd = len(shape)
    return pl.BlockSpec(shape, lambda *_: (0,) * nd, pipeline_mode=pl.Buffered(1))


def _rms(x, g):
    ms = jnp.mean(x * x, axis=-1, keepdims=True)
    return x * lax.rsqrt(ms + NORM_EPS) * g


def _dot(a, b):
    return jnp.dot(a, b, preferred_element_type=F32)


def _dot_nt(a, b):
    return lax.dot_general(a, b, (((1,), (1,)), ((), ())), preferred_element_type=F32)


def _log_sigmoid(x):
    return jnp.minimum(x, 0.0) - jnp.log1p(jnp.exp(-jnp.abs(x)))


def _ffn_body(*refs, final):
    if final:
        x_ref, g_ref, wg_ref, wu_ref, wd_ref, fg_ref, o_ref, h_scr = refs
    else:
        x_ref, g_ref, wg_ref, wu_ref, wd_ref, o_ref, h_scr = refs
    x = x_ref[...]
    h_scr[...] = _rms(x, g_ref[...]).astype(BF16)
    acc = None
    for c in range(FF_PAD // FF_CHUNK):
        cols = slice(c * FF_CHUNK, (c + 1) * FF_CHUNK)
        h = h_scr[...]
        gate = _dot(h, wg_ref[:, cols])
        up = _dot(h, wu_ref[:, cols])
        act = (gate * jax.nn.sigmoid(gate) * up).astype(BF16)
        part = _dot(act, wd_ref[cols, :])
        acc = part if acc is None else acc + part
    y = x + 0.5 * acc
    if final:
        y = _rms(y, fg_ref[...])
    o_ref[...] = y


def _ffn(x, g, wg, wu, wd, final_g=None, *, tm):
    m = x.shape[0]
    final = final_g is not None
    row = pl.BlockSpec((tm, D_MODEL), lambda i: (i, 0))
    in_specs = [row, _resident((1, D_MODEL)), _resident(wg.shape), _resident(wu.shape), _resident(wd.shape)]
    args = [x, g.reshape(1, D_MODEL), wg, wu, wd]
    if final:
        in_specs.append(_resident((1, D_MODEL)))
        args.append(final_g.reshape(1, D_MODEL))
    return pl.pallas_call(
        functools.partial(_ffn_body, final=final),
        grid=(m // tm,),
        in_specs=in_specs,
        out_specs=row,
        out_shape=jax.ShapeDtypeStruct((m, D_MODEL), F32),
        scratch_shapes=[pltpu.VMEM((tm, D_MODEL), BF16)],
        compiler_params=_params(("arbitrary",)),
        name="ffn_final" if final else "ffn",
    )(*args)


def _mlstm_head(q, k, v, fc, fr, ic, ir, c0, n0, m0, causal):
    log_d = jnp.where(causal, fc - fr + ir, NEG_INF)
    a = m0 + fc
    m = jnp.maximum(a, jnp.max(log_d, axis=-1, keepdims=True))
    qb, kb, vb = q.astype(BF16), k.astype(BF16), v.astype(BF16)
    s = _dot_nt(qb, kb) * jnp.exp(log_d - m)
    inter = jnp.exp(a - m)
    num = _dot(s.astype(BF16), vb) + inter * _dot_nt(qb, c0.astype(BF16))
    den = jnp.sum(s, axis=-1, keepdims=True) + inter * jnp.sum(q * n0, axis=-1, keepdims=True)
    h = num / jnp.maximum(jnp.abs(den), jnp.exp(-m))
    m_last = m[CHUNK - 1:CHUNK, :]
    f_last = fc[CHUNK - 1:CHUNK, :]
    w = jnp.exp(f_last - fc + ic - m_last)
    decay = jnp.exp(m0 + f_last - m_last)
    c1 = decay * c0 + _dot((v * w).T.astype(BF16), kb)
    n1 = decay * n0 + jnp.sum(w * k, axis=0, keepdims=True)
    return h, c1, n1, m_last


def _exact_tri_dot(tri_bf16, x):
    x1 = x.astype(BF16)
    r1 = x - x1.astype(F32)
    x2 = r1.astype(BF16)
    x3 = (r1 - x2.astype(F32)).astype(BF16)
    return _dot(tri_bf16, x1) + _dot(tri_bf16, x2) + _dot(tri_bf16, x3)


def _mixer_ab_body(x_ref, g_ref, win_ref, bif_ref, sg_ref, sw_ref, sbt_ref, wout_ref,
                   y_ref, c_ref, n_ref, m_ref, z_scr, cat_scr):
    nb = x_ref.shape[0]

    @pl.when(pl.program_id(0) == 0)
    def _():
        c_ref[...] = jnp.zeros_like(c_ref)
        n_ref[...] = jnp.zeros_like(n_ref)
        m_ref[...] = jnp.zeros_like(m_ref)

    x = x_ref[...].reshape(nb * CHUNK, D_MODEL)
    hn = _rms(x, g_ref[...]).astype(BF16)
    for c0 in range(0, A_IN_PAD, WIDTH):
        c1 = min(c0 + WIDTH, A_IN_PAD)
        z_scr[:, c0:c1] = _dot(hn, win_ref[:, c0:c1])

    row = lax.broadcasted_iota(jnp.int32, (CHUNK, CHUNK), 0)
    col = lax.broadcasted_iota(jnp.int32, (CHUNK, CHUNK), 1)
    causal = col <= row
    tri = jnp.where(causal, 1.0, 0.0).astype(BF16)
    scale = HEAD_DIM ** -0.5
    sgu_w = [jnp.where(causal, sw_ref[g], 0.0).astype(BF16) for g in range(HEADS)]

    def per_batch(b, carry):
        rows = pl.ds(pl.multiple_of(b * CHUNK, CHUNK), CHUNK)
        gates = z_scr[rows, COL_GATES:COL_GATES + LANES] + bif_ref[...]
        lg = jnp.where(col < HEADS, gates, _log_sigmoid(gates))
        fcum = _exact_tri_dot(tri, lg)
        lg_t = lg.T
        fcum_t = fcum.T
        for h in range(HEADS):
            hs = slice(h * HEAD_DIM, (h + 1) * HEAD_DIM)
            q = z_scr[rows, hs]
            k = z_scr[rows, WIDTH + h * HEAD_DIM:WIDTH + (h + 1) * HEAD_DIM] * scale
            v = z_scr[rows, 2 * WIDTH + h * HEAD_DIM:2 * WIDTH + (h + 1) * HEAD_DIM]
            o = z_scr[rows, 3 * WIDTH + h * HEAD_DIM:3 * WIDTH + (h + 1) * HEAD_DIM]
            hh, c1, n1, m1 = _mlstm_head(
                q, k, v,
                fcum[:, HEADS + h:HEADS + h + 1], fcum_t[HEADS + h:HEADS + h + 1, :],
                lg[:, h:h + 1], lg_t[h:h + 1, :],
                c_ref[b, h], n_ref[b, h], m_ref[b, h][:, 0:1], causal)
            c_ref[b, h] = c1
            n_ref[b, h] = n1
            m_ref[b, h] = jnp.broadcast_to(m1, (1, LANES))
            cat_scr[rows, hs] = (jax.nn.sigmoid(o) * hh).astype(BF16)
        vn = _rms(z_scr[rows, COL_GV:COL_GV + WIDTH], sg_ref[...])
        for g in range(HEADS):
            gs = slice(g * CHUNK, (g + 1) * CHUNK)
            mixed = _dot(sgu_w[g], vn[:, gs].astype(BF16)) + sbt_ref[:, g:g + 1]
            u = z_scr[rows, COL_U + g * CHUNK:COL_U + (g + 1) * CHUNK]
            cat_scr[rows, WIDTH + g * CHUNK:WIDTH + (g + 1) * CHUNK] = (u * mixed).astype(BF16)
        return carry

    lax.fori_loop(0, nb, per_batch, 0)
    y = x + _dot(cat_scr[...], wout_ref[...])
    y_ref[...] = y.reshape(nb, CHUNK, D_MODEL)


def _mixer_ab_prompt(x, g, w_in, b_if, sgu_g, sgu_w, sgu_bt, w_out):
    nb, s, _ = x.shape
    blk = pl.BlockSpec((nb, CHUNK, D_MODEL), lambda c: (0, c, 0))
    return pl.pallas_call(
        _mixer_ab_body,
        grid=(s // CHUNK,),
        in_specs=[blk, _resident((1, D_MODEL)), _resident(w_in.shape), _resident((1, LANES)),
                  _resident((1, WIDTH)), _resident(sgu_w.shape), _resident(sgu_bt.shape), _resident(w_out.shape)],
        out_specs=[blk,
                   pl.BlockSpec((nb, HEADS, HEAD_DIM, HEAD_DIM), lambda c: (0, 0, 0, 0)),
                   pl.BlockSpec((nb, HEADS, 1, HEAD_DIM), lambda c: (0, 0, 0, 0)),
                   pl.BlockSpec((nb, HEADS, 1, LANES), lambda c: (0, 0, 0, 0))],
        out_shape=[jax.ShapeDtypeStruct(x.shape, F32),
                   jax.ShapeDtypeStruct((nb, HEADS, HEAD_DIM, HEAD_DIM), F32),
                   jax.ShapeDtypeStruct((nb, HEADS, 1, HEAD_DIM), F32),
                   jax.ShapeDtypeStruct((nb, HEADS, 1, LANES), F32)],
        scratch_shapes=[pltpu.VMEM((nb * CHUNK, A_IN_PAD), F32), pltpu.VMEM((nb * CHUNK, 2 * WIDTH), BF16)],
        compiler_params=_params(("arbitrary",)),
        name="mixer_ab_prompt",
    )(x, g.reshape(1, D_MODEL), w_in, b_if, sgu_g.reshape(1, WIDTH), sgu_w, sgu_bt, w_out)


def _alibi_slope(group, head):
    n = len(SWA_GROUPS) * HEADS
    return 2.0 ** (-8.0 * (group * HEADS + head + 1) / n)


def _swa_group_body(x_ref, g_ref, w_ref, o_ref, st_ref, kv_ref, q_scr, k_scr, v_scr, *, group, dil, tm, rb):
    it = pl.program_id(2)
    scale = HEAD_DIM ** -0.5
    qi = lax.broadcasted_iota(jnp.int32, (CHUNK, 2 * CHUNK), 0)
    kc = lax.broadcasted_iota(jnp.int32, (CHUNK, 2 * CHUNK), 1)
    delta = CHUNK + qi - kc
    valid = (delta >= 0) & (delta <= CHUNK)
    dist = (delta * dil).astype(F32)
    lane = lax.broadcasted_iota(jnp.int32, (CHUNK, LANES), 1)

    @pl.when(it == 0)
    def _():
        k_scr[:, 0:CHUNK, :] = jnp.zeros((rb, CHUNK, WIDTH), BF16)
        v_scr[:, 0:CHUNK, :] = jnp.zeros((rb, CHUNK, WIDTH), BF16)

    for r in range(rb):
        xs = x_ref[0, :, r * D_MODEL:(r + 1) * D_MODEL]
        hn = _rms(xs, g_ref[...]).astype(BF16)
        q_scr[...] = _dot(hn, w_ref[:, 0:WIDTH]).astype(BF16)
        kf = _dot(hn, w_ref[:, WIDTH:2 * WIDTH])
        k_scr[r, CHUNK:, :] = kf.astype(BF16)
        kv_ref[0, :, r * 2 * WIDTH:r * 2 * WIDTH + WIDTH] = kf[tm - CHUNK:, :]
        vf = _dot(hn, w_ref[:, 2 * WIDTH:3 * WIDTH])
        v_scr[r, CHUNK:, :] = vf.astype(BF16)
        kv_ref[0, :, r * 2 * WIDTH + WIDTH:(r + 1) * 2 * WIDTH] = vf[tm - CHUNK:, :]
        for j in range(tm // CHUNK):
            mask = valid if j > 0 else valid & (kc >= jnp.where(it > 0, 0, CHUNK))
            stats = jnp.zeros((CHUNK, LANES), F32)
            for h in range(HEADS):
                hs = slice(h * HEAD_DIM, (h + 1) * HEAD_DIM)
                qj = q_scr[j * CHUNK:(j + 1) * CHUNK, hs]
                kk = k_scr[r, j * CHUNK:(j + 2) * CHUNK, hs]
                vv = v_scr[r, j * CHUNK:(j + 2) * CHUNK, hs]
                s = _dot_nt(qj, kk) * scale + (-_alibi_slope(group, h)) * dist
                s = jnp.where(mask, s, NEG_INF)
                m = jnp.max(s, axis=-1, keepdims=True)
                p = jnp.exp(s - m)
                l = jnp.sum(p, axis=-1, keepdims=True)
                o_ref[0, j * CHUNK:(j + 1) * CHUNK, r * WIDTH + h * HEAD_DIM:r * WIDTH + (h + 1) * HEAD_DIM] = (
                    _dot(p.astype(BF16), vv))
                stats = jnp.where(lane == h, m, jnp.where(lane == HEADS + h, l, stats))
            st_ref[0, j * CHUNK:(j + 1) * CHUNK, r * LANES:(r + 1) * LANES] = stats
        k_scr[r, 0:CHUNK, :] = k_scr[r, tm:tm + CHUNK, :]
        v_scr[r, 0:CHUNK, :] = v_scr[r, tm:tm + CHUNK, :]


def _swa_group_prompt(x, g, w, *, group, tm, rb):
    win, dil = SWA_GROUPS[group]
    nb, s, _ = x.shape
    sub = s // dil
    xv = x.reshape(nb, sub, dil * D_MODEL)
    grid = (nb, dil // rb, sub // tm)
    o, st, kv = pl.pallas_call(
        functools.partial(_swa_group_body, group=group, dil=dil, tm=tm, rb=rb),
        grid=grid,
        in_specs=[pl.BlockSpec((1, tm, rb * D_MODEL), lambda b, r, t: (b, t, r)),
                  _resident((1, D_MODEL)), _resident(w.shape)],
        out_specs=[pl.BlockSpec((1, tm, rb * WIDTH), lambda b, r, t: (b, t, r)),
                   pl.BlockSpec((1, tm, rb * LANES), lambda b, r, t: (b, t, r)),
                   pl.BlockSpec((1, CHUNK, rb * 2 * WIDTH), lambda b, r, t: (b, 0, r))],
        out_shape=[jax.ShapeDtypeStruct((nb, sub, dil * WIDTH), F32),
                   jax.ShapeDtypeStruct((nb, sub, dil * LANES), F32),
                   jax.ShapeDtypeStruct((nb, CHUNK, dil * 2 * WIDTH), F32)],
        scratch_shapes=[pltpu.VMEM((tm, WIDTH), BF16),
                        pltpu.VMEM((rb, CHUNK + tm, WIDTH), BF16),
                        pltpu.VMEM((rb, CHUNK + tm, WIDTH), BF16)],
        compiler_params=_params(("arbitrary", "arbitrary", "arbitrary")),
        name="swa_group%d" % group,
    )(xv, g.reshape(1, D_MODEL), w)
    return (o.reshape(nb * s, WIDTH), st.reshape(nb * s, LANES), kv.reshape(nb, win, 2 * WIDTH))


def _merge_heads(outs, stats, rows):
    merged = []
    for h in range(HEADS):
        ms = [st[rows, h:h + 1] for st in stats]
        ls = [st[rows, HEADS + h:HEADS + h + 1] for st in stats]
        top = functools.reduce(jnp.maximum, ms)
        ws = [jnp.exp(m - top) for m in ms]
        num = sum(w * o[rows, h * HEAD_DIM:(h + 1) * HEAD_DIM] for w, o in zip(ws, outs))
        den = sum(w * l for w, l in zip(ws, ls))
        merged.append(num / den)
    return merged


def _swa_merge_body(o0_ref, o1_ref, o2_ref, s0_ref, s1_ref, s2_ref, x_ref, w_ref, y_ref, cat_scr):
    merged = _merge_heads((o0_ref, o1_ref, o2_ref), (s0_ref, s1_ref, s2_ref), slice(None))
    for h in range(HEADS):
        cat_scr[:, h * HEAD_DIM:(h + 1) * HEAD_DIM] = merged[h].astype(BF16)
    y_ref[...] = x_ref[...] + _dot(cat_scr[...], w_ref[...])


def _swa_merge(outs, stats, x, w_out, *, tm):
    m = x.shape[0]
    o_spec = pl.BlockSpec((tm, WIDTH), lambda i: (i, 0))
    s_spec = pl.BlockSpec((tm, LANES), lambda i: (i, 0))
    x_spec = pl.BlockSpec((tm, D_MODEL), lambda i: (i, 0))
    return pl.pallas_call(
        _swa_merge_body,
        grid=(m // tm,),
        in_specs=[o_spec] * 3 + [s_spec] * 3 + [x_spec, _resident(w_out.shape)],
        out_specs=x_spec,
        out_shape=jax.ShapeDtypeStruct((m, D_MODEL), F32),
        scratch_shapes=[pltpu.VMEM((tm, WIDTH), BF16)],
        compiler_params=_params(("arbitrary",)),
        name="swa_merge",
    )(*outs, *stats, x, w_out)


def _proj_body(x_ref, g_ref, w_ref, z_ref):
    z_ref[...] = _dot(_rms(x_ref[...], g_ref[...]).astype(BF16), w_ref[...])


def _proj(x, g, w):
    m, n = x.shape[0], w.shape[1]
    return pl.pallas_call(
        _proj_body,
        grid=(1,),
        in_specs=[_resident(x.shape), _resident((1, D_MODEL)), _resident(w.shape)],
        out_specs=pl.BlockSpec((m, n), lambda i: (0, 0)),
        out_shape=jax.ShapeDtypeStruct((m, n), F32),
        compiler_params=_params(("arbitrary",)),
        name="proj_rows",
    )(x, g.reshape(1, D_MODEL), w)


def _out_proj_body(c_ref, w_ref, x_ref, y_ref):
    y_ref[...] = x_ref[...] + _dot(c_ref[...].astype(BF16), w_ref[...])


def _out_proj(cat, w, x):
    return pl.pallas_call(
        _out_proj_body,
        grid=(1,),
        in_specs=[_resident(cat.shape), _resident(w.shape), _resident(x.shape)],
        out_specs=pl.BlockSpec(x.shape, lambda i: (0, 0)),
        out_shape=jax.ShapeDtypeStruct(x.shape, F32),
        compiler_params=_params(("arbitrary",)),
        name="out_proj_rows",
    )(cat, w, x)


def _mixer_ab_step_body(z_ref, bif_ref, sg_ref, w00_ref, b0_ref, c_ref, n_ref, m_ref,
                        cat_ref, c1_ref, n1_ref, m1_ref, vn_ref):
    nb = z_ref.shape[0]
    scale = HEAD_DIM ** -0.5
    eye = (lax.broadcasted_iota(jnp.int32, (HEAD_DIM, HEAD_DIM), 0)
           == lax.broadcasted_iota(jnp.int32, (HEAD_DIM, HEAD_DIM), 1)).astype(F32)
    lane = lax.broadcasted_iota(jnp.int32, (1, LANES), 1)
    for i in range(nb):
        zr = z_ref[i:i + 1, :]
        gates = zr[:, COL_GATES:COL_GATES + LANES] + bif_ref[...]
        m_new = jnp.zeros((1, LANES), F32)
        for h in range(HEADS):
            hs = slice(h * HEAD_DIM, (h + 1) * HEAD_DIM)
            q = zr[:, hs]
            k = zr[:, WIDTH + h * HEAD_DIM:WIDTH + (h + 1) * HEAD_DIM] * scale
            v = zr[:, 2 * WIDTH + h * HEAD_DIM:2 * WIDTH + (h + 1) * HEAD_DIM]
            o = zr[:, 3 * WIDTH + h * HEAD_DIM:3 * WIDTH + (h + 1) * HEAD_DIM]
            ig = gates[:, h:h + 1]
            lf = _log_sigmoid(gates[:, HEADS + h:HEADS + h + 1])
            c0 = c_ref[i, h]
            n0 = n_ref[i, h]
            m0 = m_ref[i:i + 1, h:h + 1]
            a = m0 + lf
            m = jnp.maximum(a, ig)
            s = jnp.sum(q * k, axis=-1, keepdims=True) * jnp.exp(ig - m)
            inter = jnp.exp(a - m)
            cq_col = jnp.sum(c0 * q, axis=-1, keepdims=True)
            cq = jnp.sum(eye * cq_col, axis=0, keepdims=True)
            v_col = jnp.sum(eye * v, axis=-1, keepdims=True)
            num = s * v + inter * cq
            den = s + inter * jnp.sum(n0 * q, axis=-1, keepdims=True)
            hh = num / jnp.maximum(jnp.abs(den), jnp.exp(-m))
            w = jnp.exp(ig - m)
            c1_ref[i, h] = inter * c0 + (w * v_col) * k
            n1_ref[i, h] = inter * n0 + w * k
            m_new = jnp.where(lane == h, m, m_new)
            cat_ref[i:i + 1, hs] = jax.nn.sigmoid(o) * hh
        m1_ref[i:i + 1, :] = m_new
        vn = _rms(zr[:, COL_GV:COL_GV + WIDTH], sg_ref[...])
        vn_ref[i:i + 1, :] = vn
        cat_ref[i:i + 1, WIDTH:2 * WIDTH] = zr[:, COL_U:COL_U + WIDTH] * (w00_ref[...] * vn + b0_ref[...])


def _mixer_ab_step(z, b_if, sgu_g, w00, b0, st_c, st_n, st_m, *, nb):
    n = z.shape[0]
    rows = lambda w: pl.BlockSpec((nb, w), lambda i: (i, 0))
    c_spec = pl.BlockSpec((nb, HEADS, HEAD_DIM, HEAD_DIM), lambda i: (i, 0, 0, 0))
    n_spec = pl.BlockSpec((nb, HEADS, 1, HEAD_DIM), lambda i: (i, 0, 0, 0))
    return pl.pallas_call(
        _mixer_ab_step_body,
        grid=(n // nb,),
        in_specs=[rows(A_IN_PAD), _resident((1, LANES)), _resident((1, WIDTH)), _resident((1, WIDTH)),
                  _resident((1, WIDTH)), c_spec, n_spec, rows(LANES)],
        out_specs=[rows(2 * WIDTH), c_spec, n_spec, rows(LANES), rows(WIDTH)],
        out_shape=[jax.ShapeDtypeStruct((n, 2 * WIDTH), F32),
                   jax.ShapeDtypeStruct(st_c.shape, F32),
                   jax.ShapeDtypeStruct(st_n.shape, F32),
                   jax.ShapeDtypeStruct((n, LANES), F32),
                   jax.ShapeDtypeStruct((n, WIDTH), F32)],
        compiler_params=_params(("arbitrary",)),
        name="mixer_ab_step",
    )(z, b_if, sgu_g.reshape(1, WIDTH), w00, b0, st_c, st_n, st_m)


def _swa_step_body(z_ref, kv0_ref, kv1_ref, kv2_ref, cat_ref):
    nb = z_ref.shape[0]
    scale = HEAD_DIM ** -0.5
    steps = (CHUNK - lax.broadcasted_iota(jnp.int32, (CHUNK, 1, 1), 0)).astype(F32)
    head = lax.broadcasted_iota(jnp.int32, (1, HEADS, 1), 1)
    for i in range(nb):
        ms, ls, os_ = [], [], []
        for gi, kv_ref in enumerate((kv0_ref, kv1_ref, kv2_ref)):
            dil = SWA_GROUPS[gi][1]
            base = gi * 3 * HEADS
            q = z_ref[i, base:base + HEADS, :]
            k_new = z_ref[i, base + HEADS:base + 2 * HEADS, :]
            v_new = z_ref[i, base + 2 * HEADS:base + 3 * HEADS, :]
            kc = kv_ref[i, :, 0, 0, :, :]
            vc = kv_ref[i, :, 0, 1, :, :]
            slope = jnp.zeros((1, HEADS, 1), F32)
            for h in range(HEADS):
                slope = jnp.where(head == h, _alibi_slope(gi, h) * dil, slope)
            s = jnp.sum(kc * q[None], axis=-1, keepdims=True) * scale - slope * steps
            s_new = jnp.sum(k_new * q, axis=-1, keepdims=True) * scale
            m = jnp.maximum(jnp.max(s, axis=0), s_new)
            p = jnp.exp(s - m[None])
            p_new = jnp.exp(s_new - m)
            ms.append(m)
            ls.append(jnp.sum(p, axis=0) + p_new)
            os_.append(jnp.sum(p * vc, axis=0) + p_new * v_new)
        top = functools.reduce(jnp.maximum, ms)
        ws = [jnp.exp(m - top) for m in ms]
        num = sum(w * o for w, o in zip(ws, os_))
        den = sum(w * l for w, l in zip(ws, ls))
        cat_ref[i] = num / den


def _swa_step(z, caches, *, nb):
    n = z.shape[0]
    views = []
    specs = []
    for (win, dil), cache in zip(SWA_GROUPS, caches):
        views.append(cache.reshape(n, win // dil, dil, 2, HEADS, HEAD_DIM))
        specs.append(pl.BlockSpec((nb, CHUNK, 1, 2, HEADS, HEAD_DIM), lambda i: (i, 0, 0, 0, 0, 0)))
    return pl.pallas_call(
        _swa_step_body,
        grid=(n // nb,),
        in_specs=[pl.BlockSpec((nb,) + z.shape[1:], lambda i: (i, 0, 0))] + specs,
        out_specs=pl.BlockSpec((nb, HEADS, HEAD_DIM), lambda i: (i, 0, 0)),
        out_shape=jax.ShapeDtypeStruct((n, HEADS, HEAD_DIM), F32),
        compiler_params=_params(("arbitrary",)),
        name="swa_step",
    )(z, *views)


def _pad_cols(w, n):
    return jnp.pad(w, ((0, 0), (0, n - w.shape[1])))


def kernel(x_prompt, x_sample, state_mlstm_C, state_mlstm_n, state_mlstm_m, cache_swa_kv0, cache_swa_kv1, cache_swa_kv2, norm_g, ffn_w_gate, ffn_w_up, ffn_w_down, a_w_in, a_b_if, sgu_norm_g, sgu_w, sgu_b, a_w_out, c_w_in, c_w_out, final_norm_g):
    nb, s, _ = x_prompt.shape
    ns = x_sample.shape[0]
    assert x_sample.shape[1] == 1 and s % max(w for w, _ in SWA_GROUPS) == 0
    for (win, dil), cache in zip(SWA_GROUPS, (cache_swa_kv0, cache_swa_kv1, cache_swa_kv2)):
        assert cache.shape[2] == win and win // dil == CHUNK

    def ffn_weights(l, i):
        wg = _pad_cols(ffn_w_gate[l, i], FF_PAD).astype(BF16)
        wu = _pad_cols(ffn_w_up[l, i], FF_PAD).astype(BF16)
        wd = jnp.pad(ffn_w_down[l, i], ((0, FF_PAD - D_FF), (0, 0))).astype(BF16)
        return wg, wu, wd

    ffn_w = [[ffn_weights(l, i) for i in range(2)] for l in range(2)]
    g_lo, g_hi = 4 * WIDTH, 4 * WIDTH + 2 * HEADS
    a_in = jnp.concatenate([a_w_in[0][:, :g_lo], a_w_in[0][:, g_hi:], _pad_cols(a_w_in[0][:, g_lo:g_hi], LANES)],
                           axis=1).astype(BF16)
    b_if = _pad_cols(a_b_if[0].reshape(1, 2 * HEADS), LANES)
    a_out = a_w_out[0].astype(BF16)
    c_in = c_w_in[0].astype(BF16)
    c_out = c_w_out[0].astype(BF16)
    sgu_bt = _pad_cols(sgu_b[0].T, LANES)
    sgu_w00 = jnp.repeat(sgu_w[0, :, 0, 0], CHUNK).reshape(1, WIDTH)
    sgu_b0 = jnp.repeat(sgu_b[0, :, 0], CHUNK).reshape(1, WIDTH)

    xp = x_prompt.reshape(nb * s, D_MODEL)
    xp = _ffn(xp, norm_g[0, 0], *ffn_w[0][0], tm=512)
    xp, p_c, p_n, p_m = _mixer_ab_prompt(xp.reshape(nb, s, D_MODEL), norm_g[0, 1], a_in, b_if,
                                          sgu_norm_g[0], sgu_w[0], sgu_bt, a_out)
    xp = _ffn(xp.reshape(nb * s, D_MODEL), norm_g[0, 2], *ffn_w[0][1], tm=512)
    xp = _ffn(xp, norm_g[1, 0], *ffn_w[1][0], tm=512)
    xp3 = xp.reshape(nb, s, D_MODEL)
    outs, stats, p_kv = [], [], []
    for gi, (tm, rb) in enumerate(((512, 1), (512, 1), (256, 2))):
        o, st, kv = _swa_group_prompt(xp3, norm_g[1, 1], c_in[:, gi * C_GROUP_COLS:(gi + 1) * C_GROUP_COLS],
                                      group=gi, tm=tm, rb=rb)
        outs.append(o)
        stats.append(st)
        p_kv.append(kv.reshape(1, nb, SWA_GROUPS[gi][0], 2, HEADS, HEAD_DIM))
    xp = _swa_merge(outs, stats, xp, c_out, tm=512)
    y_prompt = _ffn(xp, norm_g[1, 2], *ffn_w[1][1], final_norm_g, tm=512).reshape(nb, s, D_MODEL)

    xs = x_sample.reshape(ns, D_MODEL)
    xs = _ffn(xs, norm_g[0, 0], *ffn_w[0][0], tm=ns)
    z = _proj(xs, norm_g[0, 1], a_in)
    cat, s_c, s_n, s_m, s_v = _mixer_ab_step(
        z, b_if, sgu_norm_g[0], sgu_w00, sgu_b0, state_mlstm_C[0],
        state_mlstm_n[0].reshape(ns, HEADS, 1, HEAD_DIM), _pad_cols(state_mlstm_m[0], LANES), nb=8)
    xs = _out_proj(cat, a_out, xs)
    xs = _ffn(xs, norm_g[0, 2], *ffn_w[0][1], tm=ns)
    xs = _ffn(xs, norm_g[1, 0], *ffn_w[1][0], tm=ns)
    z = _proj(xs, norm_g[1, 1], c_in)
    cat = _swa_step(z.reshape(ns, 3 * 3 * HEADS, HEAD_DIM), (cache_swa_kv0, cache_swa_kv1, cache_swa_kv2), nb=4)
    xs = _out_proj(cat.reshape(ns, WIDTH), c_out, xs)
    y_sample = _ffn(xs, norm_g[1, 2], *ffn_w[1][1], final_norm_g, tm=ns).reshape(ns, 1, D_MODEL)
    s_kv = [z[:, gi * C_GROUP_COLS + WIDTH:(gi + 1) * C_GROUP_COLS].reshape(1, ns, 1, 2, HEADS, HEAD_DIM)
            for gi in range(3)]

    return (y_prompt, y_sample,
            p_c.reshape(1, nb, HEADS, HEAD_DIM, HEAD_DIM), p_n.reshape(1, nb, HEADS, HEAD_DIM),
            p_m[:, :, 0, 0].reshape(1, nb, HEADS),
            s_c.reshape(1, ns, HEADS, HEAD_DIM, HEAD_DIM), s_n.reshape(1, ns, HEADS, HEAD_DIM),
            s_m[:, :HEADS].reshape(1, ns, HEADS), s_v.reshape(1, ns, 1, WIDTH),
            p_kv[0], p_kv[1], p_kv[2], s_kv[0], s_kv[1], s_kv[2])
```

```python
import functools

import jax
import jax.numpy as jnp
from jax import lax
from jax.experimental import pallas as pl
from jax.experimental.pallas import tpu as pltpu

F32 = jnp.float32
BF16 = jnp.bfloat16

D_MODEL = 1024
D_FF = 2752
HEADS = 4
HEAD_DIM = 128
WIDTH = HEADS * HEAD_DIM
CHUNK = 128
SWA_GROUPS = ((128, 1), (512, 4), (2048, 16))
NORM_EPS = 1e-6
NEG_INF = -1e30

LANES = 128
FF_CHUNK = 256
FF_PAD = -(-D_FF // FF_CHUNK) * FF_CHUNK
A_IN_PAD = 4 * WIDTH + 2 * WIDTH + LANES
COL_U = 4 * WIDTH
COL_GV = 5 * WIDTH
COL_GATES = 6 * WIDTH
C_GROUP_COLS = 3 * WIDTH

VMEM_LIMIT = 56 * 1024 * 1024


def _params(semantics):
    return pltpu.CompilerParams(dimension_semantics=semantics, vmem_limit_bytes=VMEM_LIMIT)


def _resident(shape):
    nd = len(shape)
    return pl.BlockSpec(shape, lambda *_: (0,) * nd, pipeline_mode=pl.Buffered(1))


def _rms(x, g):
    ms = jnp.mean(x * x, axis=-1, keepdims=True)
    return x * lax.rsqrt(ms + NORM_EPS) * g


def _dot(a, b):
    return jnp.dot(a, b, preferred_element_type=F32)


def _dot_nt(a, b):
    return lax.dot_general(a, b, (((1,), (1,)), ((), ())), preferred_element_type=F32)


def _log_sigmoid(x):
    return jnp.minimum(x, 0.0) - jnp.log1p(jnp.exp(-jnp.abs(x)))


def _ffn_body(*refs, final):
    if final:
        x_ref, g_ref, wg_ref, wu_ref, wd_ref, fg_ref, o_ref, h_scr = refs
    else:
        x_ref, g_ref, wg_ref, wu_ref, wd_ref, o_ref, h_scr = refs
    x = x_ref[...]
    h_scr[...] = _rms(x, g_ref[...]).astype(BF16)
    acc = None
    for c in range(FF_PAD // FF_CHUNK):
        cols = slice(c * FF_CHUNK, (c + 1) * FF_CHUNK)
        h = h_scr[...]
        gate = _dot(h, wg_ref[:, cols])
        up = _dot(h, wu_ref[:, cols])
        act = (gate * jax.nn.sigmoid(gate) * up).astype(BF16)
        part = _dot(act, wd_ref[cols, :])
        acc = part if acc is None else acc + part
    y = x + 0.5 * acc
    if final:
        y = _rms(y, fg_ref[...])
    o_ref[...] = y


def _ffn(x, g, wg, wu, wd, final_g=None, *, tm):
    m = x.shape[0]
    final = final_g is not None
    row = pl.BlockSpec((tm, D_MODEL), lambda i: (i, 0))
    in_specs = [row, _resident((1, D_MODEL)), _resident(wg.shape), _resident(wu.shape), _resident(wd.shape)]
    args = [x, g.reshape(1, D_MODEL), wg, wu, wd]
    if final:
        in_specs.append(_resident((1, D_MODEL)))
        args.append(final_g.reshape(1, D_MODEL))
    return pl.pallas_call(
        functools.partial(_ffn_body, final=final),
        grid=(m // tm,),
        in_specs=in_specs,
        out_specs=row,
        out_shape=jax.ShapeDtypeStruct((m, D_MODEL), F32),
        scratch_shapes=[pltpu.VMEM((tm, D_MODEL), BF16)],
        compiler_params=_params(("arbitrary",)),
        name="ffn_final" if final else "ffn",
    )(*args)


def _mlstm_head(q, k, v, fc, fr, ic, ir, c0, n0, m0, causal):
    log_d = jnp.where(causal, fc - fr + ir, NEG_INF)
    a = m0 + fc
    m = jnp.maximum(a, jnp.max(log_d, axis=-1, keepdims=True))
    qb, kb, vb = q.astype(BF16), k.astype(BF16), v.astype(BF16)
    s = _dot_nt(qb, kb) * jnp.exp(log_d - m)
    inter = jnp.exp(a - m)
    num = _dot(s.astype(BF16), vb) + inter * _dot_nt(qb, c0.astype(BF16))
    den = jnp.sum(s, axis=-1, keepdims=True) + inter * jnp.sum(q * n0, axis=-1, keepdims=True)
    h = num / jnp.maximum(jnp.abs(den), jnp.exp(-m))
    m_last = m[CHUNK - 1:CHUNK, :]
    f_last = fc[CHUNK - 1:CHUNK, :]
    w = jnp.exp(f_last - fc + ic - m_last)
    decay = jnp.exp(m0 + f_last - m_last)
    c1 = decay * c0 + _dot((v * w).T.astype(BF16), kb)
    n1 = decay * n0 + jnp.sum(w * k, axis=0, keepdims=True)
    return h, c1, n1, m_last


def _exact_tri_dot(tri_bf16, x):
    x1 = x.astype(BF16)
    r1 = x - x1.astype(F32)
    x2 = r1.astype(BF16)
    x3 = (r1 - x2.astype(F32)).astype(BF16)
    return _dot(tri_bf16, x1) + _dot(tri_bf16, x2) + _dot(tri_bf16, x3)


def _mixer_ab_body(x_ref, g_ref, win_ref, bif_ref, sg_ref, sw_ref, sbt_ref, wout_ref,
                   y_ref, c_ref, n_ref, m_ref, z_scr, cat_scr):
    nb = x_ref.shape[0]

    @pl.when(pl.program_id(0) == 0)
    def _():
        c_ref[...] = jnp.zeros_like(c_ref)
        n_ref[...] = jnp.zeros_like(n_ref)
        m_ref[...] = jnp.zeros_like(m_ref)

    x = x_ref[...].reshape(nb * CHUNK, D_MODEL)
    hn = _rms(x, g_ref[...]).astype(BF16)
    for c0 in range(0, A_IN_PAD, WIDTH):
        c1 = min(c0 + WIDTH, A_IN_PAD)
        z_scr[:, c0:c1] = _dot(hn, win_ref[:, c0:c1])

    row = lax.broadcasted_iota(jnp.int32, (CHUNK, CHUNK), 0)
    col = lax.broadcasted_iota(jnp.int32, (CHUNK, CHUNK), 1)
    causal = col <= row
    tri = jnp.where(causal, 1.0, 0.0).astype(BF16)
    scale = HEAD_DIM ** -0.5
    sgu_w = [jnp.where(causal, sw_ref[g], 0.0).astype(BF16) for g in range(HEADS)]

    def per_batch(b, carry):
        rows = pl.ds(pl.multiple_of(b * CHUNK, CHUNK), CHUNK)
        gates = z_scr[rows, COL_GATES:COL_GATES + LANES] + bif_ref[...]
        lg = jnp.where(col < HEADS, gates, _log_sigmoid(gates))
        fcum = _exact_tri_dot(tri, lg)
        lg_t = lg.T
        fcum_t = fcum.T
        for h in range(HEADS):
            hs = slice(h * HEAD_DIM, (h + 1) * HEAD_DIM)
            q = z_scr[rows, hs]
            k = z_scr[rows, WIDTH + h * HEAD_DIM:WIDTH + (h + 1) * HEAD_DIM] * scale
            v = z_scr[rows, 2 * WIDTH + h * HEAD_DIM:2 * WIDTH + (h + 1) * HEAD_DIM]
            o = z_scr[rows, 3 * WIDTH + h * HEAD_DIM:3 * WIDTH + (h + 1) * HEAD_DIM]
            hh, c1, n1, m1 = _mlstm_head(
                q, k, v,
                fcum[:, HEADS + h:HEADS + h + 1], fcum_t[HEADS + h:HEADS + h + 1, :],
                lg[:, h:h + 1], lg_t[h:h + 1, :],
                c_ref[b, h], n_ref[b, h], m_ref[b, h][:, 0:1], causal)
            c_ref[b, h] = c1
            n_ref[b, h] = n1
            m_ref[b, h] = jnp.broadcast_to(m1, (1, LANES))
            cat_scr[rows, hs] = (jax.nn.sigmoid(o) * hh).astype(BF16)
        vn = _rms(z_scr[rows, COL_GV:COL_GV + WIDTH], sg_ref[...])
        for g in range(HEADS):
            gs = slice(g * CHUNK, (g + 1) * CHUNK)
            mixed = _dot(sgu_w[g], vn[:, gs].astype(BF16)) + sbt_ref[:, g:g + 1]
            u = z_scr[rows, COL_U + g * CHUNK:COL_U + (g + 1) * CHUNK]
            cat_scr[rows, WIDTH + g * CHUNK:WIDTH + (g + 1) * CHUNK] = (u * mixed).astype(BF16)
        return carry

    lax.fori_loop(0, nb, per_batch, 0)
    y = x + _dot(cat_scr[...], wout_ref[...])
    y_ref[...] = y.reshape(nb, CHUNK, D_MODEL)


def _mixer_ab_prompt(x, g, w_in, b_if, sgu_g, sgu_w, sgu_bt, w_out):
    nb, s, _ = x.shape
    blk = pl.BlockSpec((nb, CHUNK, D_MODEL), lambda c: (0, c, 0))
    return pl.pallas_call(
        _mixer_ab_body,
        grid=(s // CHUNK,),
        in_specs=[blk, _resident((1, D_MODEL)), _resident(w_in.shape), _resident((1, LANES)),
                  _resident((1, WIDTH)), _resident(sgu_w.shape), _resident(sgu_bt.shape), _resident(w_out.shape)],
        out_specs=[blk,
                   pl.BlockSpec((nb, HEADS, HEAD_DIM, HEAD_DIM), lambda c: (0, 0, 0, 0)),
                   pl.BlockSpec((nb, HEADS, 1, HEAD_DIM), lambda c: (0, 0, 0, 0)),
                   pl.BlockSpec((nb, HEADS, 1, LANES), lambda c: (0, 0, 0, 0))],
        out_shape=[jax.ShapeDtypeStruct(x.shape, F32),
                   jax.ShapeDtypeStruct((nb, HEADS, HEAD_DIM, HEAD_DIM), F32),
                   jax.ShapeDtypeStruct((nb, HEADS, 1, HEAD_DIM), F32),
                   jax.ShapeDtypeStruct((nb, HEADS, 1, LANES), F32)],
        scratch_shapes=[pltpu.VMEM((nb * CHUNK, A_IN_PAD), F32), pltpu.VMEM((nb * CHUNK, 2 * WIDTH), BF16)],
        compiler_params=_params(("arbitrary",)),
        name="mixer_ab_prompt",
    )(x, g.reshape(1, D_MODEL), w_in, b_if, sgu_g.reshape(1, WIDTH), sgu_w, sgu_bt, w_out)


def _alibi_slope(group, head):
    n = len(SWA_GROUPS) * HEADS
    return 2.0 ** (-8.0 * (group * HEADS + head + 1) / n)


def _rms_deint_body(x_ref, g_ref, h1_ref, h2_ref, slab_scr):
    t = x_ref.shape[1]
    rows_per_pass = 256
    for c in range(t // rows_per_pass):
        rows = slice(c * rows_per_pass, (c + 1) * rows_per_pass)
        hn = _rms(x_ref[0, rows, :], g_ref[...])
        for sl in range(D_MODEL // LANES):
            slab_scr[sl, rows, :] = hn[:, sl * LANES:(sl + 1) * LANES]
    for (_, dil), out_ref in zip(SWA_GROUPS[1:], (h1_ref, h2_ref)):
        for r in range(dil):
            for sl in range(D_MODEL // LANES):
                piece = slab_scr[sl, pl.ds(r, t // dil, stride=dil), :]
                out_ref[0, r, :, sl * LANES:(sl + 1) * LANES] = piece.astype(BF16)


def _rms_deint(x, g, *, tm):
    nb, s, _ = x.shape
    d1, d2 = SWA_GROUPS[1][1], SWA_GROUPS[2][1]
    return pl.pallas_call(
        _rms_deint_body,
        grid=(nb, s // tm),
        in_specs=[pl.BlockSpec((1, tm, D_MODEL), lambda b, t: (b, t, 0)), _resident((1, D_MODEL))],
        out_specs=[pl.BlockSpec((1, d1, tm // d1, D_MODEL), lambda b, t: (b, 0, t, 0)),
                   pl.BlockSpec((1, d2, tm // d2, D_MODEL), lambda b, t: (b, 0, t, 0))],
        out_shape=[jax.ShapeDtypeStruct((nb, d1, s // d1, D_MODEL), BF16),
                   jax.ShapeDtypeStruct((nb, d2, s // d2, D_MODEL), BF16)],
        scratch_shapes=[pltpu.VMEM((D_MODEL // LANES, tm, LANES), F32)],
        compiler_params=_params(("arbitrary", "arbitrary")),
        name="rms_deint",
    )(x, g.reshape(1, D_MODEL))


def _swa_group_body(*refs, group, dil, tm, rb, normed):
    if normed:
        x_ref, w_ref, o_ref, st_ref, q_scr, k_scr, v_scr = refs
    else:
        x_ref, g_ref, w_ref, o_ref, st_ref, q_scr, k_scr, v_scr = refs
    it = pl.program_id(2)
    scale = HEAD_DIM ** -0.5
    qi = lax.broadcasted_iota(jnp.int32, (CHUNK, 2 * CHUNK), 0)
    kc = lax.broadcasted_iota(jnp.int32, (CHUNK, 2 * CHUNK), 1)
    delta = CHUNK + qi - kc
    valid = (delta >= 0) & (delta <= CHUNK)
    dist = (delta * dil).astype(F32)
    lane = lax.broadcasted_iota(jnp.int32, (CHUNK, LANES), 1)

    @pl.when(it == 0)
    def _():
        k_scr[:, 0:CHUNK, :] = jnp.zeros((rb, CHUNK, WIDTH), BF16)
        v_scr[:, 0:CHUNK, :] = jnp.zeros((rb, CHUNK, WIDTH), BF16)

    for r in range(rb):
        hn = x_ref[0, r] if normed else _rms(x_ref[0, r], g_ref[...]).astype(BF16)
        q_scr[...] = _dot(hn, w_ref[:, 0:WIDTH]).astype(BF16)
        k_scr[r, CHUNK:, :] = _dot(hn, w_ref[:, WIDTH:2 * WIDTH]).astype(BF16)
        v_scr[r, CHUNK:, :] = _dot(hn, w_ref[:, 2 * WIDTH:3 * WIDTH]).astype(BF16)
        for j in range(tm // CHUNK):
            rows = slice(j * CHUNK, (j + 1) * CHUNK)
            mask = valid if j > 0 else valid & (kc >= jnp.where(it > 0, 0, CHUNK))
            stats = jnp.zeros((CHUNK, LANES), F32)
            for h in range(HEADS):
                hs = slice(h * HEAD_DIM, (h + 1) * HEAD_DIM)
                qj = q_scr[rows, hs]
                kk = k_scr[r, j * CHUNK:(j + 2) * CHUNK, hs]
                vv = v_scr[r, j * CHUNK:(j + 2) * CHUNK, hs]
                s = _dot_nt(qj, kk) * scale + (-_alibi_slope(group, h)) * dist
                s = jnp.where(mask, s, NEG_INF)
                m = jnp.max(s, axis=-1, keepdims=True)
                p = jnp.exp(s - m)
                l = jnp.sum(p, axis=-1, keepdims=True)
                o_ref[0, r, h, rows, :] = _dot(p.astype(BF16), vv)
                stats = jnp.where(lane == h, m, jnp.where(lane == HEADS + h, l, stats))
            st_ref[0, r, rows, :] = stats
        k_scr[r, 0:CHUNK, :] = k_scr[r, tm:tm + CHUNK, :]
        v_scr[r, 0:CHUNK, :] = v_scr[r, tm:tm + CHUNK, :]


def _swa_group_prompt(x, g, c_in, *, group, tm, rb):
    nb, dil, sub, _ = x.shape
    normed = x.dtype == BF16
    in_specs = [pl.BlockSpec((1, rb, tm, D_MODEL), lambda b, r, t: (b, r, t, 0))]
    args = [x]
    if not normed:
        in_specs.append(_resident((1, D_MODEL)))
        args.append(g.reshape(1, D_MODEL))
    in_specs.append(pl.BlockSpec((D_MODEL, C_GROUP_COLS), lambda b, r, t: (0, group), pipeline_mode=pl.Buffered(1)))
    args.append(c_in)
    return pl.pallas_call(
        functools.partial(_swa_group_body, group=group, dil=dil, tm=tm, rb=rb, normed=normed),
        grid=(nb, dil // rb, sub // tm),
        in_specs=in_specs,
        out_specs=[pl.BlockSpec((1, rb, HEADS, tm, HEAD_DIM), lambda b, r, t: (b, r, 0, t, 0)),
                   pl.BlockSpec((1, rb, tm, LANES), lambda b, r, t: (b, r, t, 0))],
        out_shape=[jax.ShapeDtypeStruct((nb, dil, HEADS, sub, HEAD_DIM), F32),
                   jax.ShapeDtypeStruct((nb, dil, sub, LANES), F32)],
        scratch_shapes=[pltpu.VMEM((tm, WIDTH), BF16),
                        pltpu.VMEM((rb, CHUNK + tm, WIDTH), BF16),
                        pltpu.VMEM((rb, CHUNK + tm, WIDTH), BF16)],
        compiler_params=_params(("arbitrary", "arbitrary", "arbitrary")),
        name="swa_group%d" % group,
    )(*args)


def _kv_tail_body(x_ref, g_ref, wk_ref, wv_ref, kv_ref):
    hn = _rms(x_ref[0], g_ref[...]).astype(BF16)
    kv_ref[0, :, 0:WIDTH] = _dot(hn, wk_ref[...])
    kv_ref[0, :, WIDTH:2 * WIDTH] = _dot(hn, wv_ref[...])


def _kv_tail(x, g, c_in, *, group):
    win = SWA_GROUPS[group][0]
    nb, s, _ = x.shape
    tm = min(win, 512)
    first = (s - win) // tm
    wcol = lambda j: pl.BlockSpec((D_MODEL, WIDTH), lambda b, t: (0, 3 * group + j), pipeline_mode=pl.Buffered(1))
    return pl.pallas_call(
        _kv_tail_body,
        grid=(nb, win // tm),
        in_specs=[pl.BlockSpec((1, tm, D_MODEL), lambda b, t: (b, first + t, 0)), _resident((1, D_MODEL)),
                  wcol(1), wcol(2)],
        out_specs=pl.BlockSpec((1, tm, 2 * WIDTH), lambda b, t: (b, t, 0)),
        out_shape=jax.ShapeDtypeStruct((nb, win, 2 * WIDTH), F32),
        compiler_params=_params(("arbitrary", "arbitrary")),
        name="kv_tail%d" % group,
    )(x, g.reshape(1, D_MODEL), c_in, c_in)


def _swa_merge_body(o0_ref, o1_ref, o2_ref, s0_ref, s1_ref, s2_ref, x_ref, w_ref, y_ref,
                    n1_scr, n2_scr, t1_scr, t2_scr, cat_scr):
    t = x_ref.shape[1]
    for (_, dil), o_ref, s_ref, n_scr, t_scr in ((SWA_GROUPS[1], o1_ref, s1_ref, n1_scr, t1_scr),
                                                  (SWA_GROUPS[2], o2_ref, s2_ref, n2_scr, t2_scr)):
        for r in range(dil):
            t_scr[pl.ds(r, t // dil, stride=dil), :] = s_ref[0, r]
            for h in range(HEADS):
                n_scr[h, pl.ds(r, t // dil, stride=dil), :] = o_ref[0, r, h]
    rows_per_pass = 256
    for c in range(t // rows_per_pass):
        rows = slice(c * rows_per_pass, (c + 1) * rows_per_pass)
        stats = (s0_ref[0, 0, rows, :], t1_scr[rows, :], t2_scr[rows, :])
        for h in range(HEADS):
            outs = (o0_ref[0, 0, h, rows, :], n1_scr[h, rows, :], n2_scr[h, rows, :])
            ms = [st[:, h:h + 1] for st in stats]
            ls = [st[:, HEADS + h:HEADS + h + 1] for st in stats]
            top = functools.reduce(jnp.maximum, ms)
            ws = [jnp.exp(m - top) for m in ms]
            num = sum(w * o for w, o in zip(ws, outs))
            den = sum(w * l for w, l in zip(ws, ls))
            cat_scr[rows, h * HEAD_DIM:(h + 1) * HEAD_DIM] = (num / den).astype(BF16)
    y_ref[0] = x_ref[0] + _dot(cat_scr[...], w_ref[...])


def _swa_merge(outs, stats, x, w_out, *, tm):
    nb, s, _ = x.shape
    in_specs = []
    for (_, dil) in SWA_GROUPS:
        in_specs.append(pl.BlockSpec((1, dil, HEADS, tm // dil, HEAD_DIM), lambda b, t: (b, 0, 0, t, 0)))
    for (_, dil) in SWA_GROUPS:
        in_specs.append(pl.BlockSpec((1, dil, tm // dil, LANES), lambda b, t: (b, 0, t, 0)))
    x_spec = pl.BlockSpec((1, tm, D_MODEL), lambda b, t: (b, t, 0))
    return pl.pallas_call(
        _swa_merge_body,
        grid=(nb, s // tm),
        in_specs=in_specs + [x_spec, _resident(w_out.shape)],
        out_specs=x_spec,
        out_shape=jax.ShapeDtypeStruct(x.shape, F32),
        scratch_shapes=[pltpu.VMEM((HEADS, tm, HEAD_DIM), F32), pltpu.VMEM((HEADS, tm, HEAD_DIM), F32),
                        pltpu.VMEM((tm, LANES), F32), pltpu.VMEM((tm, LANES), F32),
                        pltpu.VMEM((tm, WIDTH), BF16)],
        compiler_params=_params(("arbitrary", "arbitrary")),
        name="swa_merge",
    )(*outs, *stats, x, w_out)


def _proj_body(x_ref, g_ref, w_ref, z_ref):
    z_ref[...] = _dot(_rms(x_ref[...], g_ref[...]).astype(BF16), w_ref[...])


def _proj(x, g, w):
    m, n = x.shape[0], w.shape[1]
    return pl.pallas_call(
        _proj_body,
        grid=(1,),
        in_specs=[_resident(x.shape), _resident((1, D_MODEL)), _resident(w.shape)],
        out_specs=pl.BlockSpec((m, n), lambda i: (0, 0)),
        out_shape=jax.ShapeDtypeStruct((m, n), F32),
        compiler_params=_params(("arbitrary",)),
        name="proj_rows",
    )(x, g.reshape(1, D_MODEL), w)


def _out_proj_body(c_ref, w_ref, x_ref, y_ref):
    y_ref[...] = x_ref[...] + _dot(c_ref[...].astype(BF16), w_ref[...])


def _out_proj(cat, w, x):
    return pl.pallas_call(
        _out_proj_body,
        grid=(1,),
        in_specs=[_resident(cat.shape), _resident(w.shape), _resident(x.shape)],
        out_specs=pl.BlockSpec(x.shape, lambda i: (0, 0)),
        out_shape=jax.ShapeDtypeStruct(x.shape, F32),
        compiler_params=_params(("arbitrary",)),
        name="out_proj_rows",
    )(cat, w, x)


def _mixer_ab_step_body(z_ref, bif_ref, sg_ref, w00_ref, b0_ref, c_ref, n_ref, m_ref,
                        cat_ref, c1_ref, n1_ref, m1_ref, vn_ref):
    nb = z_ref.shape[0]
    scale = HEAD_DIM ** -0.5
    eye = (lax.broadcasted_iota(jnp.int32, (HEAD_DIM, HEAD_DIM), 0)
           == lax.broadcasted_iota(jnp.int32, (HEAD_DIM, HEAD_DIM), 1)).astype(F32)
    lane = lax.broadcasted_iota(jnp.int32, (1, LANES), 1)
    for i in range(nb):
        zr = z_ref[i:i + 1, :]
        gates = zr[:, COL_GATES:COL_GATES + LANES] + bif_ref[...]
        m_new = jnp.zeros((1, LANES), F32)
        for h in range(HEADS):
            hs = slice(h * HEAD_DIM, (h + 1) * HEAD_DIM)
            q = zr[:, hs]
            k = zr[:, WIDTH + h * HEAD_DIM:WIDTH + (h + 1) * HEAD_DIM] * scale
            v = zr[:, 2 * WIDTH + h * HEAD_DIM:2 * WIDTH + (h + 1) * HEAD_DIM]
            o = zr[:, 3 * WIDTH + h * HEAD_DIM:3 * WIDTH + (h + 1) * HEAD_DIM]
            ig = gates[:, h:h + 1]
            lf = _log_sigmoid(gates[:, HEADS + h:HEADS + h + 1])
            c0 = c_ref[i, h]
            n0 = n_ref[i, h]
            m0 = m_ref[i:i + 1, h:h + 1]
            a = m0 + lf
            m = jnp.maximum(a, ig)
            s = jnp.sum(q * k, axis=-1, keepdims=True) * jnp.exp(ig - m)
            inter = jnp.exp(a - m)
            cq_col = jnp.sum(c0 * q, axis=-1, keepdims=True)
            cq = jnp.sum(eye * cq_col, axis=0, keepdims=True)
            v_col = jnp.sum(eye * v, axis=-1, keepdims=True)
            num = s * v + inter * cq
            den = s + inter * jnp.sum(n0 * q, axis=-1, keepdims=True)
            hh = num / jnp.maximum(jnp.abs(den), jnp.exp(-m))
            w = jnp.exp(ig - m)
            c1_ref[i, h] = inter * c0 + (w * v_col) * k
            n1_ref[i, h] = inter * n0 + w * k
            m_new = jnp.where(lane == h, m, m_new)
            cat_ref[i:i + 1, hs] = jax.nn.sigmoid(o) * hh
        m1_ref[i:i + 1, :] = m_new
        vn = _rms(zr[:, COL_GV:COL_GV + WIDTH], sg_ref[...])
        vn_ref[i:i + 1, :] = vn
        cat_ref[i:i + 1, WIDTH:2 * WIDTH] = zr[:, COL_U:COL_U + WIDTH] * (w00_ref[...] * vn + b0_ref[...])


def _mixer_ab_step(z, b_if, sgu_g, w00, b0, st_c, st_n, st_m, *, nb):
    n = z.shape[0]
    rows = lambda w: pl.BlockSpec((nb, w), lambda i: (i, 0))
    c_spec = pl.BlockSpec((nb, HEADS, HEAD_DIM, HEAD_DIM), lambda i: (i, 0, 0, 0))
    n_spec = pl.BlockSpec((nb, HEADS, 1, HEAD_DIM), lambda i: (i, 0, 0, 0))
    return pl.pallas_call(
        _mixer_ab_step_body,
        grid=(n // nb,),
        in_specs=[rows(A_IN_PAD), _resident((1, LANES)), _resident((1, WIDTH)), _resident((1, WIDTH)),
                  _resident((1, WIDTH)), c_spec, n_spec, rows(LANES)],
        out_specs=[rows(2 * WIDTH), c_spec, n_spec, rows(LANES), rows(WIDTH)],
        out_shape=[jax.ShapeDtypeStruct((n, 2 * WIDTH), F32),
                   jax.ShapeDtypeStruct(st_c.shape, F32),
                   jax.ShapeDtypeStruct(st_n.shape, F32),
                   jax.ShapeDtypeStruct((n, LANES), F32),
                   jax.ShapeDtypeStruct((n, WIDTH), F32)],
        compiler_params=_params(("arbitrary",)),
        name="mixer_ab_step",
    )(z, b_if, sgu_g.reshape(1, WIDTH), w00, b0, st_c, st_n, st_m)


def _swa_step_body(z_ref, kv0_ref, kv1_ref, kv2_ref, cat_ref):
    nb = z_ref.shape[0]
    scale = HEAD_DIM ** -0.5
    steps = (CHUNK - lax.broadcasted_iota(jnp.int32, (CHUNK, 1, 1), 0)).astype(F32)
    head = lax.broadcasted_iota(jnp.int32, (1, HEADS, 1), 1)
    for i in range(nb):
        ms, ls, os_ = [], [], []
        for gi, kv_ref in enumerate((kv0_ref, kv1_ref, kv2_ref)):
            dil = SWA_GROUPS[gi][1]
            base = gi * 3 * HEADS
            q = z_ref[i, base:base + HEADS, :]
            k_new = z_ref[i, base + HEADS:base + 2 * HEADS, :]
            v_new = z_ref[i, base + 2 * HEADS:base + 3 * HEADS, :]
            kc = kv_ref[i, :, 0, 0, :, :]
            vc = kv_ref[i, :, 0, 1, :, :]
            slope = jnp.zeros((1, HEADS, 1), F32)
            for h in range(HEADS):
                slope = jnp.where(head == h, _alibi_slope(gi, h) * dil, slope)
            s = jnp.sum(kc * q[None], axis=-1, keepdims=True) * scale - slope * steps
            s_new = jnp.sum(k_new * q, axis=-1, keepdims=True) * scale
            m = jnp.maximum(jnp.max(s, axis=0), s_new)
            p = jnp.exp(s - m[None])
            p_new = jnp.exp(s_new - m)
            ms.append(m)
            ls.append(jnp.sum(p, axis=0) + p_new)
            os_.append(jnp.sum(p * vc, axis=0) + p_new * v_new)
        top = functools.reduce(jnp.maximum, ms)
        ws = [jnp.exp(m - top) for m in ms]
        num = sum(w * o for w, o in zip(ws, os_))
        den = sum(w * l for w, l in zip(ws, ls))
        cat_ref[i] = num / den


def _swa_step(z, caches, *, nb):
    n = z.shape[0]
    views = []
    specs = []
    for (win, dil), cache in zip(SWA_GROUPS, caches):
        views.append(cache.reshape(n, win // dil, dil, 2, HEADS, HEAD_DIM))
        specs.append(pl.BlockSpec((nb, CHUNK, 1, 2, HEADS, HEAD_DIM), lambda i: (i, 0, 0, 0, 0, 0)))
    return pl.pallas_call(
        _swa_step_body,
        grid=(n // nb,),
        in_specs=[pl.BlockSpec((nb,) + z.shape[1:], lambda i: (i, 0, 0))] + specs,
        out_specs=pl.BlockSpec((nb, HEADS, HEAD_DIM), lambda i: (i, 0, 0)),
        out_shape=jax.ShapeDtypeStruct((n, HEADS, HEAD_DIM), F32),
        compiler_params=_params(("arbitrary",)),
        name="swa_step",
    )(z, *views)


def _pad_cols(w, n):
    return jnp.pad(w, ((0, 0), (0, n - w.shape[1])))


def kernel(x_prompt, x_sample, state_mlstm_C, state_mlstm_n, state_mlstm_m, cache_swa_kv0, cache_swa_kv1, cache_swa_kv2, norm_g, ffn_w_gate, ffn_w_up, ffn_w_down, a_w_in, a_b_if, sgu_norm_g, sgu_w, sgu_b, a_w_out, c_w_in, c_w_out, final_norm_g):
    nb, s, _ = x_prompt.shape
    ns = x_sample.shape[0]
    assert x_sample.shape[1] == 1 and s % max(w for w, _ in SWA_GROUPS) == 0
    for (win, dil), cache in zip(SWA_GROUPS, (cache_swa_kv0, cache_swa_kv1, cache_swa_kv2)):
        assert cache.shape[2] == win and win // dil == CHUNK

    def ffn_weights(l, i):
        wg = _pad_cols(ffn_w_gate[l, i], FF_PAD).astype(BF16)
        wu = _pad_cols(ffn_w_up[l, i], FF_PAD).astype(BF16)
        wd = jnp.pad(ffn_w_down[l, i], ((0, FF_PAD - D_FF), (0, 0))).astype(BF16)
        return wg, wu, wd

    ffn_w = [[ffn_weights(l, i) for i in range(2)] for l in range(2)]
    g_lo, g_hi = 4 * WIDTH, 4 * WIDTH + 2 * HEADS
    a_in = jnp.concatenate([a_w_in[0][:, :g_lo], a_w_in[0][:, g_hi:], _pad_cols(a_w_in[0][:, g_lo:g_hi], LANES)],
                           axis=1).astype(BF16)
    b_if = _pad_cols(a_b_if[0].reshape(1, 2 * HEADS), LANES)
    a_out = a_w_out[0].astype(BF16)
    c_in = c_w_in[0].astype(BF16)
    c_out = c_w_out[0].astype(BF16)
    sgu_bt = _pad_cols(sgu_b[0].T, LANES)
    sgu_w00 = jnp.repeat(sgu_w[0, :, 0, 0], CHUNK).reshape(1, WIDTH)
    sgu_b0 = jnp.repeat(sgu_b[0, :, 0], CHUNK).reshape(1, WIDTH)

    xp = x_prompt.reshape(nb * s, D_MODEL)
    xp = _ffn(xp, norm_g[0, 0], *ffn_w[0][0], tm=512)
    xp, p_c, p_n, p_m = _mixer_ab_prompt(xp.reshape(nb, s, D_MODEL), norm_g[0, 1], a_in, b_if,
                                          sgu_norm_g[0], sgu_w[0], sgu_bt, a_out)
    xp = _ffn(xp.reshape(nb * s, D_MODEL), norm_g[0, 2], *ffn_w[0][1], tm=512)
    xp = _ffn(xp, norm_g[1, 0], *ffn_w[1][0], tm=512)
    xp3 = xp.reshape(nb, s, D_MODEL)
    hn1, hn2 = _rms_deint(xp3, norm_g[1, 1], tm=1024)
    outs, stats = [], []
    for gi, (xin, tm, rb) in enumerate(((xp3.reshape(nb, 1, s, D_MODEL), 512, 1), (hn1, 512, 1), (hn2, 256, 2))):
        o, st = _swa_group_prompt(xin, norm_g[1, 1], c_in, group=gi, tm=tm, rb=rb)
        outs.append(o)
        stats.append(st)
    p_kv = [_kv_tail(xp3, norm_g[1, 1], c_in, group=gi).reshape(1, nb, SWA_GROUPS[gi][0], 2, HEADS, HEAD_DIM)
            for gi in range(3)]
    xp = _swa_merge(outs, stats, xp3, c_out, tm=1024).reshape(nb * s, D_MODEL)
    y_prompt = _ffn(xp, norm_g[1, 2], *ffn_w[1][1], final_norm_g, tm=512).reshape(nb, s, D_MODEL)

    xs = x_sample.reshape(ns, D_MODEL)
    xs = _ffn(xs, norm_g[0, 0], *ffn_w[0][0], tm=ns)
    z = _proj(xs, norm_g[0, 1], a_in)
    cat, s_c, s_n, s_m, s_v = _mixer_ab_step(
        z, b_if, sgu_norm_g[0], sgu_w00, sgu_b0, state_mlstm_C[0],
        state_mlstm_n[0].reshape(ns, HEADS, 1, HEAD_DIM), _pad_cols(state_mlstm_m[0], LANES), nb=8)
    xs = _out_proj(cat, a_out, xs)
    xs = _ffn(xs, norm_g[0, 2], *ffn_w[0][1], tm=ns)
    xs = _ffn(xs, norm_g[1, 0], *ffn_w[1][0], tm=ns)
    z = _proj(xs, norm_g[1, 1], c_in)
    cat = _swa_step(z.reshape(ns, 3 * 3 * HEADS, HEAD_DIM), (cache_swa_kv0, cache_swa_kv1, cache_swa_kv2), nb=4)
    xs = _out_proj(cat.reshape(ns, WIDTH), c_out, xs)
    y_sample = _ffn(xs, norm_g[1, 2], *ffn_w[1][1], final_norm_g, tm=ns).reshape(ns, 1, D_MODEL)
    s_kv = [z[:, gi * C_GROUP_COLS + WIDTH:(gi + 1) * C_GROUP_COLS].reshape(1, ns, 1, 2, HEADS, HEAD_DIM)
            for gi in range(3)]

    return (y_prompt, y_sample,
            p_c.reshape(1, nb, HEADS, HEAD_DIM, HEAD_DIM), p_n.reshape(1, nb, HEADS, HEAD_DIM),
            p_m[:, :, 0, 0].reshape(1, nb, HEADS),
            s_c.reshape(1, ns, HEADS, HEAD_DIM, HEAD_DIM), s_n.reshape(1, ns, HEADS, HEAD_DIM),
            s_m[:, :HEADS].reshape(1, ns, HEADS), s_v.reshape(1, ns, 1, WIDTH),
            p_kv[0], p_kv[1], p_kv[2], s_kv[0], s_kv[1], s_kv[2])
```

```python
import functools

import jax
import jax.numpy as jnp
from jax import lax
from jax.experimental import pallas as pl
from jax.experimental.pallas import tpu as pltpu

F32 = jnp.float32
BF16 = jnp.bfloat16

D_MODEL = 1024
D_FF = 2752
HEADS = 4
HEAD_DIM = 128
WIDTH = HEADS * HEAD_DIM
CHUNK = 128
SWA_GROUPS = ((128, 1), (512, 4), (2048, 16))
NORM_EPS = 1e-6
NEG_INF = -1e30

LANES = 128
FF_CHUNK = 256
A_IN_PAD = 4 * WIDTH + 2 * WIDTH + LANES
COL_U = 4 * WIDTH
COL_GV = 5 * WIDTH
COL_GATES = 6 * WIDTH
C_GROUP_COLS = 3 * WIDTH

VMEM_LIMIT = 56 * 1024 * 1024


def _params(semantics):
    return pltpu.CompilerParams(dimension_semantics=semantics, vmem_limit_bytes=VMEM_LIMIT)


def _resident(shape):
    nd = len(shape)
    return pl.BlockSpec(shape, lambda *_: (0,) * nd, pipeline_mode=pl.Buffered(1))


def _rms(x, g):
    ms = jnp.mean(x * x, axis=-1, keepdims=True)
    return x * lax.rsqrt(ms + NORM_EPS) * g


def _dot(a, b):
    return jnp.dot(a, b, preferred_element_type=F32)


def _dot_nt(a, b):
    return lax.dot_general(a, b, (((1,), (1,)), ((), ())), preferred_element_type=F32)


def _log_sigmoid(x):
    return jnp.minimum(x, 0.0) - jnp.log1p(jnp.exp(-jnp.abs(x)))


def _ffn_body(*refs, final):
    if final:
        x_ref, g_ref, wg_ref, wu_ref, wd_ref, fg_ref, o_ref, h_scr = refs
    else:
        x_ref, g_ref, wg_ref, wu_ref, wd_ref, o_ref, h_scr = refs
    x = x_ref[...]
    h_scr[...] = _rms(x, g_ref[...]).astype(BF16)
    acc = None
    for c0 in range(0, D_FF, FF_CHUNK):
        cols = slice(c0, min(c0 + FF_CHUNK, D_FF))
        h = h_scr[...]
        gate = _dot(h, wg_ref[0, 0, :, cols])
        up = _dot(h, wu_ref[0, 0, :, cols])
        act = (gate * jax.nn.sigmoid(gate) * up).astype(BF16)
        part = _dot(act, wd_ref[0, 0, cols, :])
        acc = part if acc is None else acc + part
    y = x + 0.5 * acc
    if final:
        y = _rms(y, fg_ref[...])
    o_ref[...] = y


def _ffn(x, g, weights, layer, which, final_g=None, *, tm):
    m = x.shape[0]
    final = final_g is not None
    wg, wu, wd = weights
    row = pl.BlockSpec((tm, D_MODEL), lambda i: (i, 0))
    wspec = lambda w: pl.BlockSpec((1, 1) + w.shape[2:], lambda i: (layer, which, 0, 0), pipeline_mode=pl.Buffered(1))
    in_specs = [row, _resident((1, D_MODEL)), wspec(wg), wspec(wu), wspec(wd)]
    args = [x, g.reshape(1, D_MODEL), wg, wu, wd]
    if final:
        in_specs.append(_resident((1, D_MODEL)))
        args.append(final_g.reshape(1, D_MODEL))
    return pl.pallas_call(
        functools.partial(_ffn_body, final=final),
        grid=(m // tm,),
        in_specs=in_specs,
        out_specs=row,
        out_shape=jax.ShapeDtypeStruct((m, D_MODEL), F32),
        scratch_shapes=[pltpu.VMEM((tm, D_MODEL), BF16)],
        compiler_params=_params(("arbitrary",)),
        name="ffn_final" if final else "ffn",
    )(*args)


def _mlstm_head(q, k, v, fc, fr, ic, ir, c0, n0, m0, causal):
    log_d = jnp.where(causal, fc - fr + ir, NEG_INF)
    a = m0 + fc
    m = jnp.maximum(a, jnp.max(log_d, axis=-1, keepdims=True))
    qb, kb, vb = q.astype(BF16), k.astype(BF16), v.astype(BF16)
    s = _dot_nt(qb, kb) * jnp.exp(log_d - m)
    inter = jnp.exp(a - m)
    num = _dot(s.astype(BF16), vb) + inter * _dot_nt(qb, c0.astype(BF16))
    den = jnp.sum(s, axis=-1, keepdims=True) + inter * jnp.sum(q * n0, axis=-1, keepdims=True)
    h = num / jnp.maximum(jnp.abs(den), jnp.exp(-m))
    m_last = m[CHUNK - 1:CHUNK, :]
    f_last = fc[CHUNK - 1:CHUNK, :]
    w = jnp.exp(f_last - fc + ic - m_last)
    decay = jnp.exp(m0 + f_last - m_last)
    c1 = decay * c0 + _dot((v * w).T.astype(BF16), kb)
    n1 = decay * n0 + jnp.sum(w * k, axis=0, keepdims=True)
    return h, c1, n1, m_last


def _exact_tri_dot(tri_bf16, x):
    x1 = x.astype(BF16)
    r1 = x - x1.astype(F32)
    x2 = r1.astype(BF16)
    x3 = (r1 - x2.astype(F32)).astype(BF16)
    return _dot(tri_bf16, x1) + _dot(tri_bf16, x2) + _dot(tri_bf16, x3)


def _mixer_ab_body(x_ref, g_ref, win_ref, bif_ref, sg_ref, sw_ref, sbt_ref, wout_ref,
                   y_ref, c_ref, n_ref, m_ref, z_scr, cat_scr):
    nb = x_ref.shape[0]

    @pl.when(pl.program_id(0) == 0)
    def _():
        c_ref[...] = jnp.zeros_like(c_ref)
        n_ref[...] = jnp.zeros_like(n_ref)
        m_ref[...] = jnp.zeros_like(m_ref)

    x = x_ref[...].reshape(nb * CHUNK, D_MODEL)
    hn = _rms(x, g_ref[...]).astype(BF16)
    for c0 in range(0, A_IN_PAD, WIDTH):
        c1 = min(c0 + WIDTH, A_IN_PAD)
        z_scr[:, c0:c1] = _dot(hn, win_ref[:, c0:c1])

    row = lax.broadcasted_iota(jnp.int32, (CHUNK, CHUNK), 0)
    col = lax.broadcasted_iota(jnp.int32, (CHUNK, CHUNK), 1)
    causal = col <= row
    tri = jnp.where(causal, 1.0, 0.0).astype(BF16)
    scale = HEAD_DIM ** -0.5
    sgu_w = [jnp.where(causal, sw_ref[g], 0.0).astype(BF16) for g in range(HEADS)]

    for b in range(nb):
        rows = slice(b * CHUNK, (b + 1) * CHUNK)
        gates = z_scr[rows, COL_GATES:COL_GATES + LANES] + bif_ref[...]
        lg = jnp.where(col < HEADS, gates, _log_sigmoid(gates))
        fcum = _exact_tri_dot(tri, lg)
        lg_t = lg.T
        fcum_t = fcum.T
        for h in range(HEADS):
            hs = slice(h * HEAD_DIM, (h + 1) * HEAD_DIM)
            q = z_scr[rows, hs]
            k = z_scr[rows, WIDTH + h * HEAD_DIM:WIDTH + (h + 1) * HEAD_DIM] * scale
            v = z_scr[rows, 2 * WIDTH + h * HEAD_DIM:2 * WIDTH + (h + 1) * HEAD_DIM]
            o = z_scr[rows, 3 * WIDTH + h * HEAD_DIM:3 * WIDTH + (h + 1) * HEAD_DIM]
            hh, c1, n1, m1 = _mlstm_head(
                q, k, v,
                fcum[:, HEADS + h:HEADS + h + 1], fcum_t[HEADS + h:HEADS + h + 1, :],
                lg[:, h:h + 1], lg_t[h:h + 1, :],
                c_ref[b, h], n_ref[b, h], m_ref[b, h][:, 0:1], causal)
            c_ref[b, h] = c1
            n_ref[b, h] = n1
            m_ref[b, h] = jnp.broadcast_to(m1, (1, LANES))
            cat_scr[rows, hs] = (jax.nn.sigmoid(o) * hh).astype(BF16)
        vn = _rms(z_scr[rows, COL_GV:COL_GV + WIDTH], sg_ref[...])
        for g in range(HEADS):
            gs = slice(g * CHUNK, (g + 1) * CHUNK)
            mixed = _dot(sgu_w[g], vn[:, gs].astype(BF16)) + sbt_ref[:, g:g + 1]
            u = z_scr[rows, COL_U + g * CHUNK:COL_U + (g + 1) * CHUNK]
            cat_scr[rows, WIDTH + g * CHUNK:WIDTH + (g + 1) * CHUNK] = (u * mixed).astype(BF16)

    y = x + _dot(cat_scr[...], wout_ref[...])
    y_ref[...] = y.reshape(nb, CHUNK, D_MODEL)


def _mixer_ab_prompt(x, g, w_in, b_if, sgu_g, sgu_w, sgu_bt, w_out):
    nb, s, _ = x.shape
    blk = pl.BlockSpec((nb, CHUNK, D_MODEL), lambda c: (0, c, 0))
    return pl.pallas_call(
        _mixer_ab_body,
        grid=(s // CHUNK,),
        in_specs=[blk, _resident((1, D_MODEL)), _resident(w_in.shape), _resident((1, LANES)),
                  _resident((1, WIDTH)), _resident(sgu_w.shape), _resident(sgu_bt.shape), _resident(w_out.shape)],
        out_specs=[blk,
                   pl.BlockSpec((nb, HEADS, HEAD_DIM, HEAD_DIM), lambda c: (0, 0, 0, 0)),
                   pl.BlockSpec((nb, HEADS, 1, HEAD_DIM), lambda c: (0, 0, 0, 0)),
                   pl.BlockSpec((nb, HEADS, 1, LANES), lambda c: (0, 0, 0, 0))],
        out_shape=[jax.ShapeDtypeStruct(x.shape, F32),
                   jax.ShapeDtypeStruct((nb, HEADS, HEAD_DIM, HEAD_DIM), F32),
                   jax.ShapeDtypeStruct((nb, HEADS, 1, HEAD_DIM), F32),
                   jax.ShapeDtypeStruct((nb, HEADS, 1, LANES), F32)],
        scratch_shapes=[pltpu.VMEM((nb * CHUNK, A_IN_PAD), F32), pltpu.VMEM((nb * CHUNK, 2 * WIDTH), BF16)],
        compiler_params=_params(("arbitrary",)),
        name="mixer_ab_prompt",
    )(x, g.reshape(1, D_MODEL), w_in, b_if, sgu_g.reshape(1, WIDTH), sgu_w, sgu_bt, w_out)


def _alibi_slope(group, head):
    n = len(SWA_GROUPS) * HEADS
    return 2.0 ** (-8.0 * (group * HEADS + head + 1) / n)


def _rms_deint_body(x_ref, g_ref, h1_ref, h2_ref, slab_scr):
    t = x_ref.shape[1]
    rows_per_pass = 256
    for c in range(t // rows_per_pass):
        rows = slice(c * rows_per_pass, (c + 1) * rows_per_pass)
        hn = _rms(x_ref[0, rows, :], g_ref[...])
        for sl in range(D_MODEL // LANES):
            slab_scr[sl, rows, :] = hn[:, sl * LANES:(sl + 1) * LANES]
    for (_, dil), out_ref in zip(SWA_GROUPS[1:], (h1_ref, h2_ref)):
        for r in range(dil):
            for sl in range(D_MODEL // LANES):
                piece = slab_scr[sl, pl.ds(r, t // dil, stride=dil), :]
                out_ref[0, r, :, sl * LANES:(sl + 1) * LANES] = piece.astype(BF16)


def _rms_deint(x, g, *, tm):
    nb, s, _ = x.shape
    d1, d2 = SWA_GROUPS[1][1], SWA_GROUPS[2][1]
    return pl.pallas_call(
        _rms_deint_body,
        grid=(nb, s // tm),
        in_specs=[pl.BlockSpec((1, tm, D_MODEL), lambda b, t: (b, t, 0)), _resident((1, D_MODEL))],
        out_specs=[pl.BlockSpec((1, d1, tm // d1, D_MODEL), lambda b, t: (b, 0, t, 0)),
                   pl.BlockSpec((1, d2, tm // d2, D_MODEL), lambda b, t: (b, 0, t, 0))],
        out_shape=[jax.ShapeDtypeStruct((nb, d1, s // d1, D_MODEL), BF16),
                   jax.ShapeDtypeStruct((nb, d2, s // d2, D_MODEL), BF16)],
        scratch_shapes=[pltpu.VMEM((D_MODEL // LANES, tm, LANES), F32)],
        compiler_params=_params(("arbitrary", "arbitrary")),
        name="rms_deint",
    )(x, g.reshape(1, D_MODEL))


def _swa_group_body(*refs, group, dil, tm, rb, normed):
    if normed:
        x_ref, w_ref, o_ref, st_ref, q_scr, k_scr, v_scr = refs
    else:
        x_ref, g_ref, w_ref, o_ref, st_ref, q_scr, k_scr, v_scr = refs
    it = pl.program_id(2)
    scale = HEAD_DIM ** -0.5
    qi = lax.broadcasted_iota(jnp.int32, (CHUNK, 2 * CHUNK), 0)
    kc = lax.broadcasted_iota(jnp.int32, (CHUNK, 2 * CHUNK), 1)
    delta = CHUNK + qi - kc
    valid = (delta >= 0) & (delta <= CHUNK)
    dist = (delta * dil).astype(F32)
    lane = lax.broadcasted_iota(jnp.int32, (CHUNK, LANES), 1)

    @pl.when(it == 0)
    def _():
        k_scr[:, 0:CHUNK, :] = jnp.zeros((rb, CHUNK, WIDTH), BF16)
        v_scr[:, 0:CHUNK, :] = jnp.zeros((rb, CHUNK, WIDTH), BF16)

    for r in range(rb):
        hn = x_ref[0, r] if normed else _rms(x_ref[0, r], g_ref[...]).astype(BF16)
        q_scr[...] = _dot(hn, w_ref[:, 0:WIDTH]).astype(BF16)
        k_scr[r, CHUNK:, :] = _dot(hn, w_ref[:, WIDTH:2 * WIDTH]).astype(BF16)
        v_scr[r, CHUNK:, :] = _dot(hn, w_ref[:, 2 * WIDTH:3 * WIDTH]).astype(BF16)
        for j in range(tm // CHUNK):
            rows = slice(j * CHUNK, (j + 1) * CHUNK)
            mask = valid if j > 0 else valid & (kc >= jnp.where(it > 0, 0, CHUNK))
            stats = jnp.zeros((CHUNK, LANES), F32)
            for h in range(HEADS):
                hs = slice(h * HEAD_DIM, (h + 1) * HEAD_DIM)
                qj = q_scr[rows, hs]
                kk = k_scr[r, j * CHUNK:(j + 2) * CHUNK, hs]
                vv = v_scr[r, j * CHUNK:(j + 2) * CHUNK, hs]
                s = _dot_nt(qj, kk) * scale + (-_alibi_slope(group, h)) * dist
                s = jnp.where(mask, s, NEG_INF)
                m = jnp.max(s, axis=-1, keepdims=True)
                p = jnp.exp(s - m)
                l = jnp.sum(p, axis=-1, keepdims=True)
                o_ref[0, r, h, rows, :] = _dot(p.astype(BF16), vv)
                stats = jnp.where(lane == h, m, jnp.where(lane == HEADS + h, l, stats))
            st_ref[0, r, rows, :] = stats
        k_scr[r, 0:CHUNK, :] = k_scr[r, tm:tm + CHUNK, :]
        v_scr[r, 0:CHUNK, :] = v_scr[r, tm:tm + CHUNK, :]


def _swa_group_prompt(x, g, c_in, *, group, tm, rb):
    nb, dil, sub, _ = x.shape
    normed = x.dtype == BF16
    in_specs = [pl.BlockSpec((1, rb, tm, D_MODEL), lambda b, r, t: (b, r, t, 0))]
    args = [x]
    if not normed:
        in_specs.append(_resident((1, D_MODEL)))
        args.append(g.reshape(1, D_MODEL))
    in_specs.append(pl.BlockSpec((D_MODEL, C_GROUP_COLS), lambda b, r, t: (0, group), pipeline_mode=pl.Buffered(1)))
    args.append(c_in)
    return pl.pallas_call(
        functools.partial(_swa_group_body, group=group, dil=dil, tm=tm, rb=rb, normed=normed),
        grid=(nb, dil // rb, sub // tm),
        in_specs=in_specs,
        out_specs=[pl.BlockSpec((1, rb, HEADS, tm, HEAD_DIM), lambda b, r, t: (b, r, 0, t, 0)),
                   pl.BlockSpec((1, rb, tm, LANES), lambda b, r, t: (b, r, t, 0))],
        out_shape=[jax.ShapeDtypeStruct((nb, dil, HEADS, sub, HEAD_DIM), F32),
                   jax.ShapeDtypeStruct((nb, dil, sub, LANES), F32)],
        scratch_shapes=[pltpu.VMEM((tm, WIDTH), BF16),
                        pltpu.VMEM((rb, CHUNK + tm, WIDTH), BF16),
                        pltpu.VMEM((rb, CHUNK + tm, WIDTH), BF16)],
        compiler_params=_params(("arbitrary", "arbitrary", "arbitrary")),
        name="swa_group%d" % group,
    )(*args)


def _kv_tail_body(x_ref, g_ref, wk_ref, wv_ref, kv_ref):
    hn = _rms(x_ref[0], g_ref[...]).astype(BF16)
    for j, w_ref in enumerate((wk_ref, wv_ref)):
        kv = _dot(hn, w_ref[...])
        for h in range(HEADS):
            kv_ref[0, 0, :, j, h, :] = kv[:, h * HEAD_DIM:(h + 1) * HEAD_DIM]


def _kv_tail(x, g, c_in, *, group):
    win = SWA_GROUPS[group][0]
    nb, s, _ = x.shape
    tm = min(win, 512)
    first = (s - win) // tm
    wcol = lambda j: pl.BlockSpec((D_MODEL, WIDTH), lambda b, t: (0, 3 * group + j), pipeline_mode=pl.Buffered(1))
    return pl.pallas_call(
        _kv_tail_body,
        grid=(nb, win // tm),
        in_specs=[pl.BlockSpec((1, tm, D_MODEL), lambda b, t: (b, first + t, 0)), _resident((1, D_MODEL)),
                  wcol(1), wcol(2)],
        out_specs=pl.BlockSpec((1, 1, tm, 2, HEADS, HEAD_DIM), lambda b, t: (0, b, t, 0, 0, 0)),
        out_shape=jax.ShapeDtypeStruct((1, nb, win, 2, HEADS, HEAD_DIM), F32),
        compiler_params=_params(("arbitrary", "arbitrary")),
        name="kv_tail%d" % group,
    )(x, g.reshape(1, D_MODEL), c_in, c_in)


def _swa_merge_body(o0_ref, o1_ref, o2_ref, s0_ref, s1_ref, s2_ref, x_ref, w_ref, y_ref,
                    n1_scr, n2_scr, t1_scr, t2_scr, cat_scr):
    t = x_ref.shape[1]
    for (_, dil), o_ref, s_ref, n_scr, t_scr in ((SWA_GROUPS[1], o1_ref, s1_ref, n1_scr, t1_scr),
                                                  (SWA_GROUPS[2], o2_ref, s2_ref, n2_scr, t2_scr)):
        for r in range(dil):
            t_scr[pl.ds(r, t // dil, stride=dil), :] = s_ref[0, r]
            for h in range(HEADS):
                n_scr[h, pl.ds(r, t // dil, stride=dil), :] = o_ref[0, r, h]
    rows_per_pass = 256
    for c in range(t // rows_per_pass):
        rows = slice(c * rows_per_pass, (c + 1) * rows_per_pass)
        stats = (s0_ref[0, 0, rows, :], t1_scr[rows, :], t2_scr[rows, :])
        for h in range(HEADS):
            outs = (o0_ref[0, 0, h, rows, :], n1_scr[h, rows, :], n2_scr[h, rows, :])
            ms = [st[:, h:h + 1] for st in stats]
            ls = [st[:, HEADS + h:HEADS + h + 1] for st in stats]
            top = functools.reduce(jnp.maximum, ms)
            ws = [jnp.exp(m - top) for m in ms]
            num = sum(w * o for w, o in zip(ws, outs))
            den = sum(w * l for w, l in zip(ws, ls))
            cat_scr[rows, h * HEAD_DIM:(h + 1) * HEAD_DIM] = (num / den).astype(BF16)
    y_ref[0] = x_ref[0] + _dot(cat_scr[...], w_ref[...])


def _swa_merge(outs, stats, x, w_out, *, tm):
    nb, s, _ = x.shape
    in_specs = []
    for (_, dil) in SWA_GROUPS:
        in_specs.append(pl.BlockSpec((1, dil, HEADS, tm // dil, HEAD_DIM), lambda b, t: (b, 0, 0, t, 0)))
    for (_, dil) in SWA_GROUPS:
        in_specs.append(pl.BlockSpec((1, dil, tm // dil, LANES), lambda b, t: (b, 0, t, 0)))
    x_spec = pl.BlockSpec((1, tm, D_MODEL), lambda b, t: (b, t, 0))
    return pl.pallas_call(
        _swa_merge_body,
        grid=(nb, s // tm),
        in_specs=in_specs + [x_spec, _resident(w_out.shape)],
        out_specs=x_spec,
        out_shape=jax.ShapeDtypeStruct(x.shape, F32),
        scratch_shapes=[pltpu.VMEM((HEADS, tm, HEAD_DIM), F32), pltpu.VMEM((HEADS, tm, HEAD_DIM), F32),
                        pltpu.VMEM((tm, LANES), F32), pltpu.VMEM((tm, LANES), F32),
                        pltpu.VMEM((tm, WIDTH), BF16)],
        compiler_params=_params(("arbitrary", "arbitrary")),
        name="swa_merge",
    )(*outs, *stats, x, w_out)


def _proj_body(x_ref, g_ref, w_ref, z_ref):
    z_ref[...] = _dot(_rms(x_ref[...], g_ref[...]).astype(BF16), w_ref[...])


def _proj(x, g, w):
    m, n = x.shape[0], w.shape[1]
    return pl.pallas_call(
        _proj_body,
        grid=(1,),
        in_specs=[_resident(x.shape), _resident((1, D_MODEL)), _resident(w.shape)],
        out_specs=pl.BlockSpec((m, n), lambda i: (0, 0)),
        out_shape=jax.ShapeDtypeStruct((m, n), F32),
        compiler_params=_params(("arbitrary",)),
        name="proj_rows",
    )(x, g.reshape(1, D_MODEL), w)


def _out_proj_body(c_ref, w_ref, x_ref, y_ref):
    y_ref[...] = x_ref[...] + _dot(c_ref[...].astype(BF16), w_ref[...])


def _out_proj(cat, w, x):
    return pl.pallas_call(
        _out_proj_body,
        grid=(1,),
        in_specs=[_resident(cat.shape), _resident(w.shape), _resident(x.shape)],
        out_specs=pl.BlockSpec(x.shape, lambda i: (0, 0)),
        out_shape=jax.ShapeDtypeStruct(x.shape, F32),
        compiler_params=_params(("arbitrary",)),
        name="out_proj_rows",
    )(cat, w, x)


def _mixer_ab_step_body(z_ref, bif_ref, sg_ref, w00_ref, b0_ref, c_ref, n_ref, m_ref,
                        cat_ref, c1_ref, n1_ref, m1_ref, vn_ref):
    nb = z_ref.shape[0]
    scale = HEAD_DIM ** -0.5
    eye = (lax.broadcasted_iota(jnp.int32, (HEAD_DIM, HEAD_DIM), 0)
           == lax.broadcasted_iota(jnp.int32, (HEAD_DIM, HEAD_DIM), 1)).astype(F32)
    lane = lax.broadcasted_iota(jnp.int32, (1, LANES), 1)
    for i in range(nb):
        zr = z_ref[i:i + 1, :]
        gates = zr[:, COL_GATES:COL_GATES + LANES] + bif_ref[...]
        m_new = jnp.zeros((1, LANES), F32)
        for h in range(HEADS):
            hs = slice(h * HEAD_DIM, (h + 1) * HEAD_DIM)
            q = zr[:, hs]
            k = zr[:, WIDTH + h * HEAD_DIM:WIDTH + (h + 1) * HEAD_DIM] * scale
            v = zr[:, 2 * WIDTH + h * HEAD_DIM:2 * WIDTH + (h + 1) * HEAD_DIM]
            o = zr[:, 3 * WIDTH + h * HEAD_DIM:3 * WIDTH + (h + 1) * HEAD_DIM]
            ig = gates[:, h:h + 1]
            lf = _log_sigmoid(gates[:, HEADS + h:HEADS + h + 1])
            c0 = c_ref[i, h]
            n0 = n_ref[i, h]
            m0 = m_ref[i:i + 1, h:h + 1]
            a = m0 + lf
            m = jnp.maximum(a, ig)
            s = jnp.sum(q * k, axis=-1, keepdims=True) * jnp.exp(ig - m)
            inter = jnp.exp(a - m)
            cq_col = jnp.sum(c0 * q, axis=-1, keepdims=True)
            cq = jnp.sum(eye * cq_col, axis=0, keepdims=True)
            v_col = jnp.sum(eye * v, axis=-1, keepdims=True)
            num = s * v + inter * cq
            den = s + inter * jnp.sum(n0 * q, axis=-1, keepdims=True)
            hh = num / jnp.maximum(jnp.abs(den), jnp.exp(-m))
            w = jnp.exp(ig - m)
            c1_ref[i, h] = inter * c0 + (w * v_col) * k
            n1_ref[i, h] = inter * n0 + w * k
            m_new = jnp.where(lane == h, m, m_new)
            cat_ref[i:i + 1, hs] = jax.nn.sigmoid(o) * hh
        m1_ref[i:i + 1, :] = m_new
        vn = _rms(zr[:, COL_GV:COL_GV + WIDTH], sg_ref[...])
        vn_ref[i:i + 1, :] = vn
        cat_ref[i:i + 1, WIDTH:2 * WIDTH] = zr[:, COL_U:COL_U + WIDTH] * (w00_ref[...] * vn + b0_ref[...])


def _mixer_ab_step(z, b_if, sgu_g, w00, b0, st_c, st_n, st_m, *, nb):
    n = z.shape[0]
    rows = lambda w: pl.BlockSpec((nb, w), lambda i: (i, 0))
    c_spec = pl.BlockSpec((nb, HEADS, HEAD_DIM, HEAD_DIM), lambda i: (i, 0, 0, 0))
    n_spec = pl.BlockSpec((nb, HEADS, 1, HEAD_DIM), lambda i: (i, 0, 0, 0))
    return pl.pallas_call(
        _mixer_ab_step_body,
        grid=(n // nb,),
        in_specs=[rows(A_IN_PAD), _resident((1, LANES)), _resident((1, WIDTH)), _resident((1, WIDTH)),
                  _resident((1, WIDTH)), c_spec, n_spec, rows(LANES)],
        out_specs=[rows(2 * WIDTH), c_spec, n_spec, rows(LANES), rows(WIDTH)],
        out_shape=[jax.ShapeDtypeStruct((n, 2 * WIDTH), F32),
                   jax.ShapeDtypeStruct(st_c.shape, F32),
                   jax.ShapeDtypeStruct(st_n.shape, F32),
                   jax.ShapeDtypeStruct((n, LANES), F32),
                   jax.ShapeDtypeStruct((n, WIDTH), F32)],
        compiler_params=_params(("arbitrary",)),
        name="mixer_ab_step",
    )(z, b_if, sgu_g.reshape(1, WIDTH), w00, b0, st_c, st_n, st_m)


def _swa_step_body(z_ref, kv0_ref, kv1_ref, kv2_ref, cat_ref):
    nb = z_ref.shape[0]
    scale = HEAD_DIM ** -0.5
    steps = (CHUNK - lax.broadcasted_iota(jnp.int32, (CHUNK, 1, 1), 0)).astype(F32)
    head = lax.broadcasted_iota(jnp.int32, (1, HEADS, 1), 1)
    for i in range(nb):
        ms, ls, os_ = [], [], []
        for gi, kv_ref in enumerate((kv0_ref, kv1_ref, kv2_ref)):
            dil = SWA_GROUPS[gi][1]
            base = gi * 3 * HEADS
            q = z_ref[i, base:base + HEADS, :]
            k_new = z_ref[i, base + HEADS:base + 2 * HEADS, :]
            v_new = z_ref[i, base + 2 * HEADS:base + 3 * HEADS, :]
            kc = kv_ref[i, :, 0, 0, :, :]
            vc = kv_ref[i, :, 0, 1, :, :]
            slope = jnp.zeros((1, HEADS, 1), F32)
            for h in range(HEADS):
                slope = jnp.where(head == h, _alibi_slope(gi, h) * dil, slope)
            s = jnp.sum(kc * q[None], axis=-1, keepdims=True) * scale - slope * steps
            s_new = jnp.sum(k_new * q, axis=-1, keepdims=True) * scale
            m = jnp.maximum(jnp.max(s, axis=0), s_new)
            p = jnp.exp(s - m[None])
            p_new = jnp.exp(s_new - m)
            ms.append(m)
            ls.append(jnp.sum(p, axis=0) + p_new)
            os_.append(jnp.sum(p * vc, axis=0) + p_new * v_new)
        top = functools.reduce(jnp.maximum, ms)
        ws = [jnp.exp(m - top) for m in ms]
        num = sum(w * o for w, o in zip(ws, os_))
        den = sum(w * l for w, l in zip(ws, ls))
        cat_ref[i] = num / den


def _swa_step(z, caches, *, nb):
    n = z.shape[0]
    views = []
    specs = []
    for (win, dil), cache in zip(SWA_GROUPS, caches):
        views.append(cache.reshape(n, win // dil, dil, 2, HEADS, HEAD_DIM))
        specs.append(pl.BlockSpec((nb, CHUNK, 1, 2, HEADS, HEAD_DIM), lambda i: (i, 0, 0, 0, 0, 0)))
    return pl.pallas_call(
        _swa_step_body,
        grid=(n // nb,),
        in_specs=[pl.BlockSpec((nb,) + z.shape[1:], lambda i: (i, 0, 0))] + specs,
        out_specs=pl.BlockSpec((nb, HEADS, HEAD_DIM), lambda i: (i, 0, 0)),
        out_shape=jax.ShapeDtypeStruct((n, HEADS, HEAD_DIM), F32),
        compiler_params=_params(("arbitrary",)),
        name="swa_step",
    )(z, *views)


def _pad_cols(w, n):
    return jnp.pad(w, ((0, 0), (0, n - w.shape[1])))


def kernel(x_prompt, x_sample, state_mlstm_C, state_mlstm_n, state_mlstm_m, cache_swa_kv0, cache_swa_kv1, cache_swa_kv2, norm_g, ffn_w_gate, ffn_w_up, ffn_w_down, a_w_in, a_b_if, sgu_norm_g, sgu_w, sgu_b, a_w_out, c_w_in, c_w_out, final_norm_g):
    nb, s, _ = x_prompt.shape
    ns = x_sample.shape[0]
    assert x_sample.shape[1] == 1 and s % max(w for w, _ in SWA_GROUPS) == 0
    for (win, dil), cache in zip(SWA_GROUPS, (cache_swa_kv0, cache_swa_kv1, cache_swa_kv2)):
        assert cache.shape[2] == win and win // dil == CHUNK

    ffn_w = (ffn_w_gate.astype(BF16), ffn_w_up.astype(BF16), ffn_w_down.astype(BF16))
    g_lo, g_hi = 4 * WIDTH, 4 * WIDTH + 2 * HEADS
    a_in = jnp.concatenate([a_w_in[0][:, :g_lo], a_w_in[0][:, g_hi:], _pad_cols(a_w_in[0][:, g_lo:g_hi], LANES)],
                           axis=1).astype(BF16)
    b_if = _pad_cols(a_b_if[0].reshape(1, 2 * HEADS), LANES)
    a_out = a_w_out[0].astype(BF16)
    c_in = c_w_in[0].astype(BF16)
    c_out = c_w_out[0].astype(BF16)
    sgu_bt = _pad_cols(sgu_b[0].T, LANES)
    sgu_w00 = jnp.repeat(sgu_w[0, :, 0, 0], CHUNK).reshape(1, WIDTH)
    sgu_b0 = jnp.repeat(sgu_b[0, :, 0], CHUNK).reshape(1, WIDTH)

    xp = x_prompt.reshape(nb * s, D_MODEL)
    xp = _ffn(xp, norm_g[0, 0], ffn_w, 0, 0, tm=512)
    xp, p_c, p_n, p_m = _mixer_ab_prompt(xp.reshape(nb, s, D_MODEL), norm_g[0, 1], a_in, b_if,
                                          sgu_norm_g[0], sgu_w[0], sgu_bt, a_out)
    xp = _ffn(xp.reshape(nb * s, D_MODEL), norm_g[0, 2], ffn_w, 0, 1, tm=512)
    xp = _ffn(xp, norm_g[1, 0], ffn_w, 1, 0, tm=512)
    xp3 = xp.reshape(nb, s, D_MODEL)
    hn1, hn2 = _rms_deint(xp3, norm_g[1, 1], tm=1024)
    outs, stats = [], []
    for gi, (xin, tm, rb) in enumerate(((xp3.reshape(nb, 1, s, D_MODEL), 512, 1), (hn1, 512, 1), (hn2, 256, 2))):
        o, st = _swa_group_prompt(xin, norm_g[1, 1], c_in, group=gi, tm=tm, rb=rb)
        outs.append(o)
        stats.append(st)
    p_kv = [_kv_tail(xp3, norm_g[1, 1], c_in, group=gi) for gi in range(3)]
    xp = _swa_merge(outs, stats, xp3, c_out, tm=1024).reshape(nb * s, D_MODEL)
    y_prompt = _ffn(xp, norm_g[1, 2], ffn_w, 1, 1, final_norm_g, tm=512).reshape(nb, s, D_MODEL)

    xs = x_sample.reshape(ns, D_MODEL)
    xs = _ffn(xs, norm_g[0, 0], ffn_w, 0, 0, tm=ns)
    z = _proj(xs, norm_g[0, 1], a_in)
    cat, s_c, s_n, s_m, s_v = _mixer_ab_step(
        z, b_if, sgu_norm_g[0], sgu_w00, sgu_b0, state_mlstm_C[0],
        state_mlstm_n[0].reshape(ns, HEADS, 1, HEAD_DIM), _pad_cols(state_mlstm_m[0], LANES), nb=8)
    xs = _out_proj(cat, a_out, xs)
    xs = _ffn(xs, norm_g[0, 2], ffn_w, 0, 1, tm=ns)
    xs = _ffn(xs, norm_g[1, 0], ffn_w, 1, 0, tm=ns)
    z = _proj(xs, norm_g[1, 1], c_in)
    cat = _swa_step(z.reshape(ns, 3 * 3 * HEADS, HEAD_DIM), (cache_swa_kv0, cache_swa_kv1, cache_swa_kv2), nb=4)
    xs = _out_proj(cat.reshape(ns, WIDTH), c_out, xs)
    y_sample = _ffn(xs, norm_g[1, 2], ffn_w, 1, 1, final_norm_g, tm=ns).reshape(ns, 1, D_MODEL)
    s_kv = [z[:, gi * C_GROUP_COLS + WIDTH:(gi + 1) * C_GROUP_COLS].reshape(1, ns, 1, 2, HEADS, HEAD_DIM)
            for gi in range(3)]

    return (y_prompt, y_sample,
            p_c.reshape(1, nb, HEADS, HEAD_DIM, HEAD_DIM), p_n.reshape(1, nb, HEADS, HEAD_DIM),
            p_m[:, :, 0, 0].reshape(1, nb, HEADS),
            s_c.reshape(1, ns, HEADS, HEAD_DIM, HEAD_DIM), s_n.reshape(1, ns, HEADS, HEAD_DIM),
            s_m[:, :HEADS].reshape(1, ns, HEADS), s_v.reshape(1, ns, 1, WIDTH),
            p_kv[0], p_kv[1], p_kv[2], s_kv[0], s_kv[1], s_kv[2])
```

```python
import functools

import jax
import jax.numpy as jnp
from jax import lax
from jax.experimental import pallas as pl
from jax.experimental.pallas import tpu as pltpu

F32 = jnp.float32
BF16 = jnp.bfloat16

D_MODEL = 1024
D_FF = 2752
HEADS = 4
HEAD_DIM = 128
WIDTH = HEADS * HEAD_DIM
CHUNK = 128
SWA_GROUPS = ((128, 1), (512, 4), (2048, 16))
NORM_EPS = 1e-6
NEG_INF = -1e30

LANES = 128
FF_CHUNK = 256
A_IN_PAD = 4 * WIDTH + 2 * WIDTH + LANES
COL_U = 4 * WIDTH
COL_GV = 5 * WIDTH
COL_GATES = 6 * WIDTH
C_GROUP_COLS = 3 * WIDTH

VMEM_LIMIT = 56 * 1024 * 1024


def _params(semantics):
    return pltpu.CompilerParams(dimension_semantics=semantics, vmem_limit_bytes=VMEM_LIMIT)


def _resident(shape):
    nd = len(shape)
    return pl.BlockSpec(shape, lambda *_: (0,) * nd, pipeline_mode=pl.Buffered(1))


def _rms(x, g):
    ms = jnp.mean(x * x, axis=-1, keepdims=True)
    return x * lax.rsqrt(ms + NORM_EPS) * g


def _dot(a, b):
    return jnp.dot(a, b, preferred_element_type=F32)


def _dot_nt(a, b):
    return lax.dot_general(a, b, (((1,), (1,)), ((), ())), preferred_element_type=F32)


def _log_sigmoid(x):
    return jnp.minimum(x, 0.0) - jnp.log1p(jnp.exp(-jnp.abs(x)))


def _ffn_body(*refs, final):
    if final:
        x_ref, g_ref, wg_ref, wu_ref, wd_ref, fg_ref, o_ref, h_scr = refs
    else:
        x_ref, g_ref, wg_ref, wu_ref, wd_ref, o_ref, h_scr = refs
    x = x_ref[...]
    h_scr[...] = _rms(x, g_ref[...]).astype(h_scr.dtype)
    acc = None
    for c0 in range(0, D_FF, FF_CHUNK):
        cols = slice(c0, min(c0 + FF_CHUNK, D_FF))
        h = h_scr[...]
        gate = _dot(h, wg_ref[0, 0, :, cols])
        up = _dot(h, wu_ref[0, 0, :, cols])
        act = (gate * jax.nn.sigmoid(gate) * up).astype(h_scr.dtype)
        part = _dot(act, wd_ref[0, 0, cols, :])
        acc = part if acc is None else acc + part
    y = x + 0.5 * acc
    if final:
        y = _rms(y, fg_ref[...])
    o_ref[...] = y


def _ffn(x, g, weights, layer, which, final_g=None, *, tm):
    m = x.shape[0]
    final = final_g is not None
    wg, wu, wd = weights
    row = pl.BlockSpec((tm, D_MODEL), lambda i: (i, 0))
    wspec = lambda w: pl.BlockSpec((1, 1) + w.shape[2:], lambda i: (layer, which, 0, 0), pipeline_mode=pl.Buffered(1))
    in_specs = [row, _resident((1, D_MODEL)), wspec(wg), wspec(wu), wspec(wd)]
    args = [x, g.reshape(1, D_MODEL), wg, wu, wd]
    if final:
        in_specs.append(_resident((1, D_MODEL)))
        args.append(final_g.reshape(1, D_MODEL))
    return pl.pallas_call(
        functools.partial(_ffn_body, final=final),
        grid=(m // tm,),
        in_specs=in_specs,
        out_specs=row,
        out_shape=jax.ShapeDtypeStruct((m, D_MODEL), F32),
        scratch_shapes=[pltpu.VMEM((tm, D_MODEL), wg.dtype)],
        compiler_params=_params(("arbitrary",)),
        name="ffn_final" if final else "ffn",
    )(*args)


def _mlstm_head(q, k, v, fc, fr, ic, ir, c0, n0, m0, causal):
    log_d = jnp.where(causal, fc - fr + ir, NEG_INF)
    a = m0 + fc
    m = jnp.maximum(a, jnp.max(log_d, axis=-1, keepdims=True))
    qb, kb, vb = q.astype(BF16), k.astype(BF16), v.astype(BF16)
    s = _dot_nt(qb, kb) * jnp.exp(log_d - m)
    inter = jnp.exp(a - m)
    num = _dot(s.astype(BF16), vb) + inter * _dot_nt(qb, c0.astype(BF16))
    den = jnp.sum(s, axis=-1, keepdims=True) + inter * jnp.sum(q * n0, axis=-1, keepdims=True)
    h = num / jnp.maximum(jnp.abs(den), jnp.exp(-m))
    m_last = m[CHUNK - 1:CHUNK, :]
    f_last = fc[CHUNK - 1:CHUNK, :]
    w = jnp.exp(f_last - fc + ic - m_last)
    decay = jnp.exp(m0 + f_last - m_last)
    c1 = decay * c0 + _dot((v * w).T.astype(BF16), kb)
    n1 = decay * n0 + jnp.sum(w * k, axis=0, keepdims=True)
    return h, c1, n1, m_last


def _exact_tri_dot(tri_bf16, x):
    x1 = x.astype(BF16)
    r1 = x - x1.astype(F32)
    x2 = r1.astype(BF16)
    x3 = (r1 - x2.astype(F32)).astype(BF16)
    return _dot(tri_bf16, x1) + _dot(tri_bf16, x2) + _dot(tri_bf16, x3)


def _mixer_ab_body(x_ref, g_ref, win_ref, bif_ref, sg_ref, sw_ref, sbt_ref, wout_ref,
                   y_ref, c_ref, n_ref, m_ref, z_scr, cat_scr):
    nb = x_ref.shape[0]

    @pl.when(pl.program_id(0) == 0)
    def _():
        c_ref[...] = jnp.zeros_like(c_ref)
        n_ref[...] = jnp.zeros_like(n_ref)
        m_ref[...] = jnp.zeros_like(m_ref)

    x = x_ref[...].reshape(nb * CHUNK, D_MODEL)
    hn = _rms(x, g_ref[...]).astype(BF16)
    for c0 in range(0, A_IN_PAD, WIDTH):
        c1 = min(c0 + WIDTH, A_IN_PAD)
        z_scr[:, c0:c1] = _dot(hn, win_ref[:, c0:c1])

    row = lax.broadcasted_iota(jnp.int32, (CHUNK, CHUNK), 0)
    col = lax.broadcasted_iota(jnp.int32, (CHUNK, CHUNK), 1)
    causal = col <= row
    tri = jnp.where(causal, 1.0, 0.0).astype(BF16)
    scale = HEAD_DIM ** -0.5
    sgu_w = [jnp.where(causal, sw_ref[g], 0.0).astype(BF16) for g in range(HEADS)]

    for b in range(nb):
        rows = slice(b * CHUNK, (b + 1) * CHUNK)
        gates = z_scr[rows, COL_GATES:COL_GATES + LANES] + bif_ref[...]
        lg = jnp.where(col < HEADS, gates, _log_sigmoid(gates))
        fcum = _exact_tri_dot(tri, lg)
        lg_t = lg.T
        fcum_t = fcum.T
        for h in range(HEADS):
            hs = slice(h * HEAD_DIM, (h + 1) * HEAD_DIM)
            q = z_scr[rows, hs]
            k = z_scr[rows, WIDTH + h * HEAD_DIM:WIDTH + (h + 1) * HEAD_DIM] * scale
            v = z_scr[rows, 2 * WIDTH + h * HEAD_DIM:2 * WIDTH + (h + 1) * HEAD_DIM]
            o = z_scr[rows, 3 * WIDTH + h * HEAD_DIM:3 * WIDTH + (h + 1) * HEAD_DIM]
            hh, c1, n1, m1 = _mlstm_head(
                q, k, v,
                fcum[:, HEADS + h:HEADS + h + 1], fcum_t[HEADS + h:HEADS + h + 1, :],
                lg[:, h:h + 1], lg_t[h:h + 1, :],
                c_ref[b, h], n_ref[b, h], m_ref[b, h][:, 0:1], causal)
            c_ref[b, h] = c1
            n_ref[b, h] = n1
            m_ref[b, h] = jnp.broadcast_to(m1, (1, LANES))
            cat_scr[rows, hs] = (jax.nn.sigmoid(o) * hh).astype(BF16)
        vn = _rms(z_scr[rows, COL_GV:COL_GV + WIDTH], sg_ref[...])
        for g in range(HEADS):
            gs = slice(g * CHUNK, (g + 1) * CHUNK)
            mixed = _dot(sgu_w[g], vn[:, gs].astype(BF16)) + sbt_ref[:, g:g + 1]
            u = z_scr[rows, COL_U + g * CHUNK:COL_U + (g + 1) * CHUNK]
            cat_scr[rows, WIDTH + g * CHUNK:WIDTH + (g + 1) * CHUNK] = (u * mixed).astype(BF16)

    y = x + _dot(cat_scr[...], wout_ref[...])
    y_ref[...] = y.reshape(nb, CHUNK, D_MODEL)


def _mixer_ab_prompt(x, g, w_in, b_if, sgu_g, sgu_w, sgu_bt, w_out):
    nb, s, _ = x.shape
    blk = pl.BlockSpec((nb, CHUNK, D_MODEL), lambda c: (0, c, 0))
    return pl.pallas_call(
        _mixer_ab_body,
        grid=(s // CHUNK,),
        in_specs=[blk, _resident((1, D_MODEL)), _resident(w_in.shape), _resident((1, LANES)),
                  _resident((1, WIDTH)), _resident(sgu_w.shape), _resident(sgu_bt.shape), _resident(w_out.shape)],
        out_specs=[blk,
                   pl.BlockSpec((nb, HEADS, HEAD_DIM, HEAD_DIM), lambda c: (0, 0, 0, 0)),
                   pl.BlockSpec((nb, HEADS, 1, HEAD_DIM), lambda c: (0, 0, 0, 0)),
                   pl.BlockSpec((nb, HEADS, 1, LANES), lambda c: (0, 0, 0, 0))],
        out_shape=[jax.ShapeDtypeStruct(x.shape, F32),
                   jax.ShapeDtypeStruct((nb, HEADS, HEAD_DIM, HEAD_DIM), F32),
                   jax.ShapeDtypeStruct((nb, HEADS, 1, HEAD_DIM), F32),
                   jax.ShapeDtypeStruct((nb, HEADS, 1, LANES), F32)],
        scratch_shapes=[pltpu.VMEM((nb * CHUNK, A_IN_PAD), F32), pltpu.VMEM((nb * CHUNK, 2 * WIDTH), BF16)],
        compiler_params=_params(("arbitrary",)),
        name="mixer_ab_prompt",
    )(x, g.reshape(1, D_MODEL), w_in, b_if, sgu_g.reshape(1, WIDTH), sgu_w, sgu_bt, w_out)


def _alibi_slope(group, head):
    n = len(SWA_GROUPS) * HEADS
    return 2.0 ** (-8.0 * (group * HEADS + head + 1) / n)


def _rms_deint_body(x_ref, g_ref, h1_ref, h2_ref, slab_scr):
    t = x_ref.shape[1]
    rows_per_pass = 256
    for c in range(t // rows_per_pass):
        rows = slice(c * rows_per_pass, (c + 1) * rows_per_pass)
        hn = _rms(x_ref[0, rows, :], g_ref[...])
        for sl in range(D_MODEL // LANES):
            slab_scr[sl, rows, :] = hn[:, sl * LANES:(sl + 1) * LANES]
    for (_, dil), out_ref in zip(SWA_GROUPS[1:], (h1_ref, h2_ref)):
        for r in range(dil):
            for sl in range(D_MODEL // LANES):
                piece = slab_scr[sl, pl.ds(r, t // dil, stride=dil), :]
                out_ref[0, r, :, sl * LANES:(sl + 1) * LANES] = piece.astype(BF16)


def _rms_deint(x, g, *, tm):
    nb, s, _ = x.shape
    d1, d2 = SWA_GROUPS[1][1], SWA_GROUPS[2][1]
    return pl.pallas_call(
        _rms_deint_body,
        grid=(nb, s // tm),
        in_specs=[pl.BlockSpec((1, tm, D_MODEL), lambda b, t: (b, t, 0)), _resident((1, D_MODEL))],
        out_specs=[pl.BlockSpec((1, d1, tm // d1, D_MODEL), lambda b, t: (b, 0, t, 0)),
                   pl.BlockSpec((1, d2, tm // d2, D_MODEL), lambda b, t: (b, 0, t, 0))],
        out_shape=[jax.ShapeDtypeStruct((nb, d1, s // d1, D_MODEL), BF16),
                   jax.ShapeDtypeStruct((nb, d2, s // d2, D_MODEL), BF16)],
        scratch_shapes=[pltpu.VMEM((D_MODEL // LANES, tm, LANES), F32)],
        compiler_params=_params(("arbitrary", "arbitrary")),
        name="rms_deint",
    )(x, g.reshape(1, D_MODEL))


def _swa_group_body(*refs, group, dil, tm, rb, normed):
    if normed:
        x_ref, w_ref, o_ref, st_ref, q_scr, k_scr, v_scr = refs
    else:
        x_ref, g_ref, w_ref, o_ref, st_ref, q_scr, k_scr, v_scr = refs
    it = pl.program_id(2)
    scale = HEAD_DIM ** -0.5
    qi = lax.broadcasted_iota(jnp.int32, (CHUNK, 2 * CHUNK), 0)
    kc = lax.broadcasted_iota(jnp.int32, (CHUNK, 2 * CHUNK), 1)
    delta = CHUNK + qi - kc
    valid = (delta >= 0) & (delta <= CHUNK)
    dist = (delta * dil).astype(F32)
    lane = lax.broadcasted_iota(jnp.int32, (CHUNK, LANES), 1)

    @pl.when(it == 0)
    def _():
        k_scr[:, 0:CHUNK, :] = jnp.zeros((rb, CHUNK, WIDTH), BF16)
        v_scr[:, 0:CHUNK, :] = jnp.zeros((rb, CHUNK, WIDTH), BF16)

    for r in range(rb):
        hn = x_ref[0, r] if normed else _rms(x_ref[0, r], g_ref[...]).astype(BF16)
        q_scr[...] = _dot(hn, w_ref[:, 0:WIDTH]).astype(BF16)
        k_scr[r, CHUNK:, :] = _dot(hn, w_ref[:, WIDTH:2 * WIDTH]).astype(BF16)
        v_scr[r, CHUNK:, :] = _dot(hn, w_ref[:, 2 * WIDTH:3 * WIDTH]).astype(BF16)
        for j in range(tm // CHUNK):
            rows = slice(j * CHUNK, (j + 1) * CHUNK)
            mask = valid if j > 0 else valid & (kc >= jnp.where(it > 0, 0, CHUNK))
            stats = jnp.zeros((CHUNK, LANES), F32)
            for h in range(HEADS):
                hs = slice(h * HEAD_DIM, (h + 1) * HEAD_DIM)
                qj = q_scr[rows, hs]
                kk = k_scr[r, j * CHUNK:(j + 2) * CHUNK, hs]
                vv = v_scr[r, j * CHUNK:(j + 2) * CHUNK, hs]
                s = _dot_nt(qj, kk) * scale + (-_alibi_slope(group, h)) * dist
                s = jnp.where(mask, s, NEG_INF)
                m = jnp.max(s, axis=-1, keepdims=True)
                p = jnp.exp(s - m)
                l = jnp.sum(p, axis=-1, keepdims=True)
                o_ref[0, r, h, rows, :] = _dot(p.astype(BF16), vv)
                stats = jnp.where(lane == h, m, jnp.where(lane == HEADS + h, l, stats))
            st_ref[0, r, rows, :] = stats
        k_scr[r, 0:CHUNK, :] = k_scr[r, tm:tm + CHUNK, :]
        v_scr[r, 0:CHUNK, :] = v_scr[r, tm:tm + CHUNK, :]


def _swa_group_prompt(x, g, c_in, *, group, tm, rb):
    nb, dil, sub, _ = x.shape
    normed = x.dtype == BF16
    in_specs = [pl.BlockSpec((1, rb, tm, D_MODEL), lambda b, r, t: (b, r, t, 0))]
    args = [x]
    if not normed:
        in_specs.append(_resident((1, D_MODEL)))
        args.append(g.reshape(1, D_MODEL))
    in_specs.append(pl.BlockSpec((D_MODEL, C_GROUP_COLS), lambda b, r, t: (0, group), pipeline_mode=pl.Buffered(1)))
    args.append(c_in)
    return pl.pallas_call(
        functools.partial(_swa_group_body, group=group, dil=dil, tm=tm, rb=rb, normed=normed),
        grid=(nb, dil // rb, sub // tm),
        in_specs=in_specs,
        out_specs=[pl.BlockSpec((1, rb, HEADS, tm, HEAD_DIM), lambda b, r, t: (b, r, 0, t, 0)),
                   pl.BlockSpec((1, rb, tm, LANES), lambda b, r, t: (b, r, t, 0))],
        out_shape=[jax.ShapeDtypeStruct((nb, dil, HEADS, sub, HEAD_DIM), F32),
                   jax.ShapeDtypeStruct((nb, dil, sub, LANES), F32)],
        scratch_shapes=[pltpu.VMEM((tm, WIDTH), BF16),
                        pltpu.VMEM((rb, CHUNK + tm, WIDTH), BF16),
                        pltpu.VMEM((rb, CHUNK + tm, WIDTH), BF16)],
        compiler_params=_params(("arbitrary", "arbitrary", "arbitrary")),
        name="swa_group%d" % group,
    )(*args)


def _kv_tail_body(x_ref, g_ref, wk_ref, wv_ref, kv_ref):
    hn = _rms(x_ref[0], g_ref[...]).astype(BF16)
    for j, w_ref in enumerate((wk_ref, wv_ref)):
        kv = _dot(hn, w_ref[...])
        for h in range(HEADS):
            kv_ref[0, 0, :, j, h, :] = kv[:, h * HEAD_DIM:(h + 1) * HEAD_DIM]


def _kv_tail(x, g, c_in, *, group):
    win = SWA_GROUPS[group][0]
    nb, s, _ = x.shape
    tm = min(win, 512)
    first = (s - win) // tm
    wcol = lambda j: pl.BlockSpec((D_MODEL, WIDTH), lambda b, t: (0, 3 * group + j), pipeline_mode=pl.Buffered(1))
    return pl.pallas_call(
        _kv_tail_body,
        grid=(nb, win // tm),
        in_specs=[pl.BlockSpec((1, tm, D_MODEL), lambda b, t: (b, first + t, 0)), _resident((1, D_MODEL)),
                  wcol(1), wcol(2)],
        out_specs=pl.BlockSpec((1, 1, tm, 2, HEADS, HEAD_DIM), lambda b, t: (0, b, t, 0, 0, 0)),
        out_shape=jax.ShapeDtypeStruct((1, nb, win, 2, HEADS, HEAD_DIM), F32),
        compiler_params=_params(("arbitrary", "arbitrary")),
        name="kv_tail%d" % group,
    )(x, g.reshape(1, D_MODEL), c_in, c_in)


def _swa_merge_body(o0_ref, o1_ref, o2_ref, s0_ref, s1_ref, s2_ref, x_ref, w_ref, y_ref,
                    n1_scr, n2_scr, t1_scr, t2_scr, cat_scr):
    t = x_ref.shape[1]
    for (_, dil), o_ref, s_ref, n_scr, t_scr in ((SWA_GROUPS[1], o1_ref, s1_ref, n1_scr, t1_scr),
                                                  (SWA_GROUPS[2], o2_ref, s2_ref, n2_scr, t2_scr)):
        for r in range(dil):
            t_scr[pl.ds(r, t // dil, stride=dil), :] = s_ref[0, r]
            for h in range(HEADS):
                n_scr[h, pl.ds(r, t // dil, stride=dil), :] = o_ref[0, r, h]
    rows_per_pass = 256
    for c in range(t // rows_per_pass):
        rows = slice(c * rows_per_pass, (c + 1) * rows_per_pass)
        stats = (s0_ref[0, 0, rows, :], t1_scr[rows, :], t2_scr[rows, :])
        for h in range(HEADS):
            outs = (o0_ref[0, 0, h, rows, :], n1_scr[h, rows, :], n2_scr[h, rows, :])
            ms = [st[:, h:h + 1] for st in stats]
            ls = [st[:, HEADS + h:HEADS + h + 1] for st in stats]
            top = functools.reduce(jnp.maximum, ms)
            ws = [jnp.exp(m - top) for m in ms]
            num = sum(w * o for w, o in zip(ws, outs))
            den = sum(w * l for w, l in zip(ws, ls))
            cat_scr[rows, h * HEAD_DIM:(h + 1) * HEAD_DIM] = (num / den).astype(BF16)
    y_ref[0] = x_ref[0] + _dot(cat_scr[...], w_ref[...])


def _swa_merge(outs, stats, x, w_out, *, tm):
    nb, s, _ = x.shape
    in_specs = []
    for (_, dil) in SWA_GROUPS:
        in_specs.append(pl.BlockSpec((1, dil, HEADS, tm // dil, HEAD_DIM), lambda b, t: (b, 0, 0, t, 0)))
    for (_, dil) in SWA_GROUPS:
        in_specs.append(pl.BlockSpec((1, dil, tm // dil, LANES), lambda b, t: (b, 0, t, 0)))
    x_spec = pl.BlockSpec((1, tm, D_MODEL), lambda b, t: (b, t, 0))
    return pl.pallas_call(
        _swa_merge_body,
        grid=(nb, s // tm),
        in_specs=in_specs + [x_spec, _resident(w_out.shape)],
        out_specs=x_spec,
        out_shape=jax.ShapeDtypeStruct(x.shape, F32),
        scratch_shapes=[pltpu.VMEM((HEADS, tm, HEAD_DIM), F32), pltpu.VMEM((HEADS, tm, HEAD_DIM), F32),
                        pltpu.VMEM((tm, LANES), F32), pltpu.VMEM((tm, LANES), F32),
                        pltpu.VMEM((tm, WIDTH), BF16)],
        compiler_params=_params(("arbitrary", "arbitrary")),
        name="swa_merge",
    )(*outs, *stats, x, w_out)


def _proj_body(x_ref, g_ref, w_ref, z_ref):
    z_ref[...] = _dot(_rms(x_ref[...], g_ref[...]).astype(BF16), w_ref[...])


def _proj(x, g, w):
    m, n = x.shape[0], w.shape[1]
    return pl.pallas_call(
        _proj_body,
        grid=(1,),
        in_specs=[_resident(x.shape), _resident((1, D_MODEL)), _resident(w.shape)],
        out_specs=pl.BlockSpec((m, n), lambda i: (0, 0)),
        out_shape=jax.ShapeDtypeStruct((m, n), F32),
        compiler_params=_params(("arbitrary",)),
        name="proj_rows",
    )(x, g.reshape(1, D_MODEL), w)


def _out_proj_body(c_ref, w_ref, x_ref, y_ref):
    y_ref[...] = x_ref[...] + _dot(c_ref[...].astype(BF16), w_ref[...])


def _out_proj(cat, w, x):
    return pl.pallas_call(
        _out_proj_body,
        grid=(1,),
        in_specs=[_resident(cat.shape), _resident(w.shape), _resident(x.shape)],
        out_specs=pl.BlockSpec(x.shape, lambda i: (0, 0)),
        out_shape=jax.ShapeDtypeStruct(x.shape, F32),
        compiler_params=_params(("arbitrary",)),
        name="out_proj_rows",
    )(cat, w, x)


def _mixer_ab_step_body(z_ref, bif_ref, sg_ref, w00_ref, b0_ref, c_ref, n_ref, m_ref,
                        cat_ref, c1_ref, n1_ref, m1_ref, vn_ref):
    nb = z_ref.shape[0]
    scale = HEAD_DIM ** -0.5
    eye = (lax.broadcasted_iota(jnp.int32, (HEAD_DIM, HEAD_DIM), 0)
           == lax.broadcasted_iota(jnp.int32, (HEAD_DIM, HEAD_DIM), 1)).astype(F32)
    lane = lax.broadcasted_iota(jnp.int32, (1, LANES), 1)
    for i in range(nb):
        zr = z_ref[i:i + 1, :]
        gates = zr[:, COL_GATES:COL_GATES + LANES] + bif_ref[...]
        m_new = jnp.zeros((1, LANES), F32)
        for h in range(HEADS):
            hs = slice(h * HEAD_DIM, (h + 1) * HEAD_DIM)
            q = zr[:, hs]
            k = zr[:, WIDTH + h * HEAD_DIM:WIDTH + (h + 1) * HEAD_DIM] * scale
            v = zr[:, 2 * WIDTH + h * HEAD_DIM:2 * WIDTH + (h + 1) * HEAD_DIM]
            o = zr[:, 3 * WIDTH + h * HEAD_DIM:3 * WIDTH + (h + 1) * HEAD_DIM]
            ig = gates[:, h:h + 1]
            lf = _log_sigmoid(gates[:, HEADS + h:HEADS + h + 1])
            c0 = c_ref[i, h]
            n0 = n_ref[i, h]
            m0 = m_ref[i:i + 1, h:h + 1]
            a = m0 + lf
            m = jnp.maximum(a, ig)
            s = jnp.sum(q * k, axis=-1, keepdims=True) * jnp.exp(ig - m)
            inter = jnp.exp(a - m)
            cq_col = jnp.sum(c0 * q, axis=-1, keepdims=True)
            cq = jnp.sum(eye * cq_col, axis=0, keepdims=True)
            v_col = jnp.sum(eye * v, axis=-1, keepdims=True)
            num = s * v + inter * cq
            den = s + inter * jnp.sum(n0 * q, axis=-1, keepdims=True)
            hh = num / jnp.maximum(jnp.abs(den), jnp.exp(-m))
            w = jnp.exp(ig - m)
            c1_ref[i, h] = inter * c0 + (w * v_col) * k
            n1_ref[i, h] = inter * n0 + w * k
            m_new = jnp.where(lane == h, m, m_new)
            cat_ref[i:i + 1, hs] = jax.nn.sigmoid(o) * hh
        m1_ref[i:i + 1, :] = m_new
        vn = _rms(zr[:, COL_GV:COL_GV + WIDTH], sg_ref[...])
        vn_ref[i:i + 1, :] = vn
        cat_ref[i:i + 1, WIDTH:2 * WIDTH] = zr[:, COL_U:COL_U + WIDTH] * (w00_ref[...] * vn + b0_ref[...])


def _mixer_ab_step(z, b_if, sgu_g, w00, b0, st_c, st_n, st_m, *, nb):
    n = z.shape[0]
    rows = lambda w: pl.BlockSpec((nb, w), lambda i: (i, 0))
    c_spec = pl.BlockSpec((nb, HEADS, HEAD_DIM, HEAD_DIM), lambda i: (i, 0, 0, 0))
    n_spec = pl.BlockSpec((nb, HEADS, 1, HEAD_DIM), lambda i: (i, 0, 0, 0))
    return pl.pallas_call(
        _mixer_ab_step_body,
        grid=(n // nb,),
        in_specs=[rows(A_IN_PAD), _resident((1, LANES)), _resident((1, WIDTH)), _resident((1, WIDTH)),
                  _resident((1, WIDTH)), c_spec, n_spec, rows(LANES)],
        out_specs=[rows(2 * WIDTH), c_spec, n_spec, rows(LANES), rows(WIDTH)],
        out_shape=[jax.ShapeDtypeStruct((n, 2 * WIDTH), F32),
                   jax.ShapeDtypeStruct(st_c.shape, F32),
                   jax.ShapeDtypeStruct(st_n.shape, F32),
                   jax.ShapeDtypeStruct((n, LANES), F32),
                   jax.ShapeDtypeStruct((n, WIDTH), F32)],
        compiler_params=_params(("arbitrary",)),
        name="mixer_ab_step",
    )(z, b_if, sgu_g.reshape(1, WIDTH), w00, b0, st_c, st_n, st_m)


def _swa_step_body(z_ref, kv0_ref, kv1_ref, kv2_ref, cat_ref):
    nb = z_ref.shape[0]
    scale = HEAD_DIM ** -0.5
    steps = (CHUNK - lax.broadcasted_iota(jnp.int32, (CHUNK, 1, 1), 0)).astype(F32)
    head = lax.broadcasted_iota(jnp.int32, (1, HEADS, 1), 1)
    for i in range(nb):
        ms, ls, os_ = [], [], []
        for gi, kv_ref in enumerate((kv0_ref, kv1_ref, kv2_ref)):
            dil = SWA_GROUPS[gi][1]
            base = gi * 3 * HEADS
            q = z_ref[i, base:base + HEADS, :]
            k_new = z_ref[i, base + HEADS:base + 2 * HEADS, :]
            v_new = z_ref[i, base + 2 * HEADS:base + 3 * HEADS, :]
            kc = kv_ref[i, :, 0, 0, :, :]
            vc = kv_ref[i, :, 0, 1, :, :]
            slope = jnp.zeros((1, HEADS, 1), F32)
            for h in range(HEADS):
                slope = jnp.where(head == h, _alibi_slope(gi, h) * dil, slope)
            s = jnp.sum(kc * q[None], axis=-1, keepdims=True) * scale - slope * steps
            s_new = jnp.sum(k_new * q, axis=-1, keepdims=True) * scale
            m = jnp.maximum(jnp.max(s, axis=0), s_new)
            p = jnp.exp(s - m[None])
            p_new = jnp.exp(s_new - m)
            ms.append(m)
            ls.append(jnp.sum(p, axis=0) + p_new)
            os_.append(jnp.sum(p * vc, axis=0) + p_new * v_new)
        top = functools.reduce(jnp.maximum, ms)
        ws = [jnp.exp(m - top) for m in ms]
        num = sum(w * o for w, o in zip(ws, os_))
        den = sum(w * l for w, l in zip(ws, ls))
        cat_ref[i] = num / den


def _swa_step(z, caches, *, nb):
    n = z.shape[0]
    views = []
    specs = []
    for (win, dil), cache in zip(SWA_GROUPS, caches):
        views.append(cache.reshape(n, win // dil, dil, 2, HEADS, HEAD_DIM))
        specs.append(pl.BlockSpec((nb, CHUNK, 1, 2, HEADS, HEAD_DIM), lambda i: (i, 0, 0, 0, 0, 0)))
    return pl.pallas_call(
        _swa_step_body,
        grid=(n // nb,),
        in_specs=[pl.BlockSpec((nb,) + z.shape[1:], lambda i: (i, 0, 0))] + specs,
        out_specs=pl.BlockSpec((nb, HEADS, HEAD_DIM), lambda i: (i, 0, 0)),
        out_shape=jax.ShapeDtypeStruct((n, HEADS, HEAD_DIM), F32),
        compiler_params=_params(("arbitrary",)),
        name="swa_step",
    )(z, *views)


def _pad_cols(w, n):
    return jnp.pad(w, ((0, 0), (0, n - w.shape[1])))


def kernel(x_prompt, x_sample, state_mlstm_C, state_mlstm_n, state_mlstm_m, cache_swa_kv0, cache_swa_kv1, cache_swa_kv2, norm_g, ffn_w_gate, ffn_w_up, ffn_w_down, a_w_in, a_b_if, sgu_norm_g, sgu_w, sgu_b, a_w_out, c_w_in, c_w_out, final_norm_g):
    nb, s, _ = x_prompt.shape
    ns = x_sample.shape[0]
    assert x_sample.shape[1] == 1 and s % max(w for w, _ in SWA_GROUPS) == 0
    for (win, dil), cache in zip(SWA_GROUPS, (cache_swa_kv0, cache_swa_kv1, cache_swa_kv2)):
        assert cache.shape[2] == win and win // dil == CHUNK

    ffn_w = (ffn_w_gate, ffn_w_up, ffn_w_down)
    g_lo, g_hi = 4 * WIDTH, 4 * WIDTH + 2 * HEADS
    a_in = jnp.concatenate([a_w_in[0][:, :g_lo], a_w_in[0][:, g_hi:], _pad_cols(a_w_in[0][:, g_lo:g_hi], LANES)],
                           axis=1).astype(BF16)
    b_if = _pad_cols(a_b_if[0].reshape(1, 2 * HEADS), LANES)
    a_out = a_w_out[0].astype(BF16)
    c_in = c_w_in[0].astype(BF16)
    c_out = c_w_out[0].astype(BF16)
    sgu_bt = _pad_cols(sgu_b[0].T, LANES)
    sgu_w00 = jnp.repeat(sgu_w[0, :, 0, 0], CHUNK).reshape(1, WIDTH)
    sgu_b0 = jnp.repeat(sgu_b[0, :, 0], CHUNK).reshape(1, WIDTH)

    xp = x_prompt.reshape(nb * s, D_MODEL)
    xp = _ffn(xp, norm_g[0, 0], ffn_w, 0, 0, tm=512)
    xp, p_c, p_n, p_m = _mixer_ab_prompt(xp.reshape(nb, s, D_MODEL), norm_g[0, 1], a_in, b_if,
                                          sgu_norm_g[0], sgu_w[0], sgu_bt, a_out)
    xp = _ffn(xp.reshape(nb * s, D_MODEL), norm_g[0, 2], ffn_w, 0, 1, tm=512)
    xp = _ffn(xp, norm_g[1, 0], ffn_w, 1, 0, tm=512)
    xp3 = xp.reshape(nb, s, D_MODEL)
    hn1, hn2 = _rms_deint(xp3, norm_g[1, 1], tm=1024)
    outs, stats = [], []
    for gi, (xin, tm, rb) in enumerate(((xp3.reshape(nb, 1, s, D_MODEL), 512, 1), (hn1, 512, 1), (hn2, 256, 2))):
        o, st = _swa_group_prompt(xin, norm_g[1, 1], c_in, group=gi, tm=tm, rb=rb)
        outs.append(o)
        stats.append(st)
    p_kv = [_kv_tail(xp3, norm_g[1, 1], c_in, group=gi) for gi in range(3)]
    xp = _swa_merge(outs, stats, xp3, c_out, tm=1024).reshape(nb * s, D_MODEL)
    y_prompt = _ffn(xp, norm_g[1, 2], ffn_w, 1, 1, final_norm_g, tm=512).reshape(nb, s, D_MODEL)

    xs = x_sample.reshape(ns, D_MODEL)
    xs = _ffn(xs, norm_g[0, 0], ffn_w, 0, 0, tm=ns)
    z = _proj(xs, norm_g[0, 1], a_in)
    cat, s_c, s_n, s_m, s_v = _mixer_ab_step(
        z, b_if, sgu_norm_g[0], sgu_w00, sgu_b0, state_mlstm_C[0],
        state_mlstm_n[0].reshape(ns, HEADS, 1, HEAD_DIM), _pad_cols(state_mlstm_m[0], LANES), nb=8)
    xs = _out_proj(cat, a_out, xs)
    xs = _ffn(xs, norm_g[0, 2], ffn_w, 0, 1, tm=ns)
    xs = _ffn(xs, norm_g[1, 0], ffn_w, 1, 0, tm=ns)
    z = _proj(xs, norm_g[1, 1], c_in)
    cat = _swa_step(z.reshape(ns, 3 * 3 * HEADS, HEAD_DIM), (cache_swa_kv0, cache_swa_kv1, cache_swa_kv2), nb=4)
    xs = _out_proj(cat.reshape(ns, WIDTH), c_out, xs)
    y_sample = _ffn(xs, norm_g[1, 2], ffn_w, 1, 1, final_norm_g, tm=ns).reshape(ns, 1, D_MODEL)
    s_kv = [z[:, gi * C_GROUP_COLS + WIDTH:(gi + 1) * C_GROUP_COLS].reshape(1, ns, 1, 2, HEADS, HEAD_DIM)
            for gi in range(3)]

    return (y_prompt, y_sample,
            p_c.reshape(1, nb, HEADS, HEAD_DIM, HEAD_DIM), p_n.reshape(1, nb, HEADS, HEAD_DIM),
            p_m[:, :, 0, 0].reshape(1, nb, HEADS),
            s_c.reshape(1, ns, HEADS, HEAD_DIM, HEAD_DIM), s_n.reshape(1, ns, HEADS, HEAD_DIM),
            s_m[:, :HEADS].reshape(1, ns, HEADS), s_v.reshape(1, ns, 1, WIDTH),
            p_kv[0], p_kv[1], p_kv[2], s_kv[0], s_kv[1], s_kv[2])
```

```python
import functools

import jax
import jax.numpy as jnp
from jax import lax
from jax.experimental import pallas as pl
from jax.experimental.pallas import tpu as pltpu

F32 = jnp.float32
BF16 = jnp.bfloat16

D_MODEL = 1024
D_FF = 2752
HEADS = 4
HEAD_DIM = 128
WIDTH = HEADS * HEAD_DIM
CHUNK = 128
SWA_GROUPS = ((128, 1), (512, 4), (2048, 16))
NORM_EPS = 1e-6
NEG_INF = -1e30

LANES = 128
BF16_ROWS = 16
PROJ_SLAB = 256
FF_CHUNK = 256
A_IN_PAD = 4 * WIDTH + 2 * WIDTH + LANES
COL_U = 4 * WIDTH
COL_GV = 5 * WIDTH
COL_GATES = 6 * WIDTH
C_GROUP_COLS = 3 * WIDTH

VMEM_LIMIT = 56 * 1024 * 1024


def _params(semantics):
    return pltpu.CompilerParams(dimension_semantics=semantics, vmem_limit_bytes=VMEM_LIMIT)


def _resident(shape):
    nd = len(shape)
    return pl.BlockSpec(shape, lambda *_: (0,) * nd, pipeline_mode=pl.Buffered(1))


def _rms(x, g):
    ms = jnp.mean(x * x, axis=-1, keepdims=True)
    return x * lax.rsqrt(ms + NORM_EPS) * g


def _dot(a, b):
    return jnp.dot(a, b, preferred_element_type=F32)


def _dot_nt(a, b):
    return lax.dot_general(a, b, (((1,), (1,)), ((), ())), preferred_element_type=F32)


def _log_sigmoid(x):
    return jnp.minimum(x, 0.0) - jnp.log1p(jnp.exp(-jnp.abs(x)))


def _ffn_body(*refs, final):
    if final:
        x_ref, g_ref, wg_ref, wu_ref, wd_ref, fg_ref, o_ref, h_scr = refs
    else:
        x_ref, g_ref, wg_ref, wu_ref, wd_ref, o_ref, h_scr = refs
    x = x_ref[...]
    h_scr[...] = _rms(x, g_ref[...]).astype(h_scr.dtype)
    acc = None
    for c0 in range(0, D_FF, FF_CHUNK):
        cols = slice(c0, min(c0 + FF_CHUNK, D_FF))
        h = h_scr[...]
        gate = _dot(h, wg_ref[0, 0, :, cols])
        up = _dot(h, wu_ref[0, 0, :, cols])
        act = (gate * jax.nn.sigmoid(gate) * up).astype(h_scr.dtype)
        part = _dot(act, wd_ref[0, 0, cols, :])
        acc = part if acc is None else acc + part
    y = x + 0.5 * acc
    if final:
        y = _rms(y, fg_ref[...])
    o_ref[...] = y


def _ffn(x, g, weights, layer, which, final_g=None, *, tm):
    m = x.shape[0]
    final = final_g is not None
    wg, wu, wd = weights
    row = pl.BlockSpec((tm, D_MODEL), lambda i: (i, 0))
    wspec = lambda w: pl.BlockSpec((1, 1) + w.shape[2:], lambda i: (layer, which, 0, 0), pipeline_mode=pl.Buffered(1))
    in_specs = [row, _resident((1, D_MODEL)), wspec(wg), wspec(wu), wspec(wd)]
    args = [x, g.reshape(1, D_MODEL), wg, wu, wd]
    if final:
        in_specs.append(_resident((1, D_MODEL)))
        args.append(final_g.reshape(1, D_MODEL))
    return pl.pallas_call(
        functools.partial(_ffn_body, final=final),
        grid=(m // tm,),
        in_specs=in_specs,
        out_specs=row,
        out_shape=jax.ShapeDtypeStruct((m, D_MODEL), F32),
        scratch_shapes=[pltpu.VMEM((tm, D_MODEL), wg.dtype)],
        compiler_params=_params(("arbitrary",)),
        name="ffn_final" if final else "ffn",
    )(*args)


def _cummax_lanes(x):
    lane = lax.broadcasted_iota(jnp.int32, x.shape, 1)
    d = 1
    while d < x.shape[1]:
        x = jnp.maximum(x, jnp.where(lane >= d, pltpu.roll(x, d, axis=1), NEG_INF))
        d *= 2
    return x


def _exact_tri_dot(tri_bf16, x):
    x1 = x.astype(BF16)
    r1 = x - x1.astype(F32)
    x2 = r1.astype(BF16)
    x3 = (r1 - x2.astype(F32)).astype(BF16)
    n = x.shape[1]
    r = _dot(tri_bf16, jnp.concatenate([x1, x2, x3], axis=1))
    return r[:, 0:n] + r[:, n:2 * n] + r[:, 2 * n:3 * n]


def _mixer_ab_chunk(xn_ref, xp_ref, g_ref, win_ref, bif_ref, sg_ref, sw_ref, sbt_ref, wout_ref,
                   y_ref, c_ref, n_ref, m_ref, z_cur, z_nxt, cat_scr):
    nb = xn_ref.shape[0]
    row = lax.broadcasted_iota(jnp.int32, (CHUNK, CHUNK), 0)
    col = lax.broadcasted_iota(jnp.int32, (CHUNK, CHUNK), 1)
    causal = col <= row
    keys_before = row <= col
    tri = jnp.where(causal, 1.0, 0.0).astype(BF16)
    scale = HEAD_DIM ** -0.5

    hn = _rms(xn_ref[...].reshape(nb * CHUNK, D_MODEL), g_ref[...]).astype(BF16)
    slabs = [(c0, min(c0 + PROJ_SLAB, A_IN_PAD)) for c0 in range(0, A_IN_PAD, PROJ_SLAB)]

    def project(count):
        for _ in range(min(count, len(slabs))):
            c0, c1 = slabs.pop(0)
            z_nxt[:, c0:c1] = _dot(hn, win_ref[:, c0:c1])

    sgu_bias = [jnp.broadcast_to(sbt_ref[:, g:g + 1], (CHUNK, CHUNK)) for g in range(HEADS)]
    vn = [_rms(z_cur[b * CHUNK:(b + 1) * CHUNK, COL_GV:COL_GV + WIDTH], sg_ref[...]) for b in range(nb)]
    for g in range(HEADS):
        gs = slice(g * CHUNK, (g + 1) * CHUNK)
        mixed = _dot(jnp.where(causal, sw_ref[g], 0.0).astype(BF16),
                     jnp.concatenate([vn[b][:, gs] for b in range(nb)], axis=1).astype(BF16))
        for b in range(nb):
            rows = slice(b * CHUNK, (b + 1) * CHUNK)
            u = z_cur[rows, COL_U + g * CHUNK:COL_U + (g + 1) * CHUNK]
            cat_scr[rows, WIDTH + g * CHUNK:WIDTH + (g + 1) * CHUNK] = (
                u * (mixed[:, b * CHUNK:(b + 1) * CHUNK] + sgu_bias[g])).astype(BF16)
    project(1)

    pieces_per_stage = -(-(len(slabs)) // (3 * nb))
    gate_terms = []
    for b in range(nb):
        rows = slice(b * CHUNK, (b + 1) * CHUNK)
        gates = z_cur[rows, COL_GATES:COL_GATES + LANES] + bif_ref[...]
        lg = jnp.where(col < HEADS, gates, _log_sigmoid(gates))
        gate_terms.append((lg, _exact_tri_dot(tri, lg)))
        project(pieces_per_stage)

    heads = []
    for b, (lg, fcum) in enumerate(gate_terms):
        rows = slice(b * CHUNK, (b + 1) * CHUNK)
        lg_t = lg.T
        fcum_t = fcum.T
        gmax = _cummax_lanes(lg_t[0:2 * HEADS, :] - jnp.concatenate([fcum_t[HEADS:2 * HEADS, :]] * 2, axis=0))
        for h in range(HEADS):
            q = z_cur[rows, h * HEAD_DIM:(h + 1) * HEAD_DIM]
            k = z_cur[rows, WIDTH + h * HEAD_DIM:WIDTH + (h + 1) * HEAD_DIM] * scale
            v_t = z_cur[rows, 2 * WIDTH + h * HEAD_DIM:2 * WIDTH + (h + 1) * HEAD_DIM].T
            f_r = fcum_t[HEADS + h:HEADS + h + 1, :]
            i_r = lg_t[h:h + 1, :]
            g_c = lg[:, h:h + 1] - fcum[:, HEADS + h:HEADS + h + 1]
            c0, n0, m0 = c_ref[b, h], n_ref[b, h], m_ref[b, h]
            m = f_r + jnp.maximum(m0, gmax[h:h + 1, :])
            inter = jnp.exp(m0 + f_r - m)
            qb, kb = q.astype(BF16), k.astype(BF16)
            against_q = _dot_nt(
                jnp.concatenate([k, c0, jnp.broadcast_to(n0, (BF16_ROWS, HEAD_DIM))], axis=0).astype(BF16), qb)
            m_last = m[:, CHUNK - 1:CHUNK]
            f_last = f_r[:, CHUNK - 1:CHUNK]
            w = jnp.exp(f_last - f_r + i_r - m_last)
            decay = jnp.exp(m0[:, 0:1] + f_last - m_last)
            against_k = _dot(
                jnp.concatenate([v_t * w, jnp.broadcast_to(w, (BF16_ROWS, CHUNK))], axis=0).astype(BF16), kb)
            c_ref[b, h] = decay * c0 + against_k[0:HEAD_DIM]
            n_ref[b, h] = decay * n0 + against_k[HEAD_DIM:HEAD_DIM + 1]
            m_ref[b, h] = jnp.broadcast_to(m_last, (1, LANES))
            heads.append((against_q, v_t, f_r, g_c, m, inter))
        project(pieces_per_stage)

    partial = []
    for i, (against_q, v_t, f_r, g_c, m, inter) in enumerate(heads):
        kq = against_q[0:CHUNK]
        cq = against_q[CHUNK:CHUNK + HEAD_DIM]
        nq = against_q[CHUNK + HEAD_DIM:CHUNK + HEAD_DIM + 1]
        s_t = kq * jnp.exp(jnp.where(keys_before, (f_r - m) + g_c, NEG_INF))
        den = jnp.sum(s_t, axis=0, keepdims=True) + inter * nq
        partial.append((_dot(v_t.astype(BF16), s_t.astype(BF16)), inter * cq,
                        jnp.maximum(jnp.abs(den), jnp.exp(-m))))
        if i % HEADS == HEADS - 1:
            project(pieces_per_stage)

    for i, (sv, carried, den) in enumerate(partial):
        b, h = divmod(i, HEADS)
        rows = slice(b * CHUNK, (b + 1) * CHUNK)
        o = z_cur[rows, 3 * WIDTH + h * HEAD_DIM:3 * WIDTH + (h + 1) * HEAD_DIM]
        cat_scr[rows, h * HEAD_DIM:(h + 1) * HEAD_DIM] = (jax.nn.sigmoid(o) * ((sv + carried) / den).T).astype(BF16)

    project(len(slabs))
    y = xp_ref[...].reshape(nb * CHUNK, D_MODEL) + _dot(cat_scr[...], wout_ref[...])
    y_ref[...] = y.reshape(nb, CHUNK, D_MODEL)


def _mixer_ab_body(*refs):
    *io_refs, z0_scr, z1_scr, cat_scr = refs
    c_ref, n_ref, m_ref = io_refs[-3:]
    step = pl.program_id(0)

    @pl.when(step <= 1)
    def _():
        c_ref[...] = jnp.zeros_like(c_ref)
        n_ref[...] = jnp.zeros_like(n_ref)
        m_ref[...] = jnp.zeros_like(m_ref)

    @pl.when(step == 0)
    def _():
        z1_scr[...] = jnp.zeros_like(z1_scr)

    @pl.when(step % 2 == 0)
    def _():
        _mixer_ab_chunk(*io_refs, z1_scr, z0_scr, cat_scr)

    @pl.when(step % 2 == 1)
    def _():
        _mixer_ab_chunk(*io_refs, z0_scr, z1_scr, cat_scr)


def _mixer_ab_prompt(x, g, w_in, b_if, sgu_g, sgu_w, sgu_bt, w_out):
    nb, s, _ = x.shape
    n_chunks = s // CHUNK
    blk = lambda index: pl.BlockSpec((nb, CHUNK, D_MODEL), index)
    nxt = blk(lambda c: (0, jnp.minimum(c, n_chunks - 1), 0))
    prev = blk(lambda c: (0, jnp.maximum(c - 1, 0), 0))
    z_shape = pltpu.VMEM((nb * CHUNK, A_IN_PAD), F32)
    return pl.pallas_call(
        _mixer_ab_body,
        grid=(n_chunks + 1,),
        in_specs=[nxt, prev, _resident((1, D_MODEL)), _resident(w_in.shape), _resident((1, LANES)),
                  _resident((1, WIDTH)), _resident(sgu_w.shape), _resident(sgu_bt.shape), _resident(w_out.shape)],
        out_specs=[prev,
                   pl.BlockSpec((nb, HEADS, HEAD_DIM, HEAD_DIM), lambda c: (0, 0, 0, 0)),
                   pl.BlockSpec((nb, HEADS, 1, HEAD_DIM), lambda c: (0, 0, 0, 0)),
                   pl.BlockSpec((nb, HEADS, 1, LANES), lambda c: (0, 0, 0, 0))],
        out_shape=[jax.ShapeDtypeStruct(x.shape, F32),
                   jax.ShapeDtypeStruct((nb, HEADS, HEAD_DIM, HEAD_DIM), F32),
                   jax.ShapeDtypeStruct((nb, HEADS, 1, HEAD_DIM), F32),
                   jax.ShapeDtypeStruct((nb, HEADS, 1, LANES), F32)],
        scratch_shapes=[z_shape, z_shape, pltpu.VMEM((nb * CHUNK, 2 * WIDTH), BF16)],
        compiler_params=_params(("arbitrary",)),
        name="mixer_ab_prompt",
    )(x, x, g.reshape(1, D_MODEL), w_in, b_if, sgu_g.reshape(1, WIDTH), sgu_w, sgu_bt, w_out)


def _alibi_slope(group, head):
    n = len(SWA_GROUPS) * HEADS
    return 2.0 ** (-8.0 * (group * HEADS + head + 1) / n)


def _rms_deint_body(x_ref, g_ref, h1_ref, h2_ref, slab_scr):
    t = x_ref.shape[1]
    rows_per_pass = 256
    for c in range(t // rows_per_pass):
        rows = slice(c * rows_per_pass, (c + 1) * rows_per_pass)
        hn = _rms(x_ref[0, rows, :], g_ref[...])
        for sl in range(D_MODEL // LANES):
            slab_scr[sl, rows, :] = hn[:, sl * LANES:(sl + 1) * LANES]
    for (_, dil), out_ref in zip(SWA_GROUPS[1:], (h1_ref, h2_ref)):
        for r in range(dil):
            for sl in range(D_MODEL // LANES):
                piece = slab_scr[sl, pl.ds(r, t // dil, stride=dil), :]
                out_ref[0, r, :, sl * LANES:(sl + 1) * LANES] = piece.astype(BF16)


def _rms_deint(x, g, *, tm):
    nb, s, _ = x.shape
    d1, d2 = SWA_GROUPS[1][1], SWA_GROUPS[2][1]
    return pl.pallas_call(
        _rms_deint_body,
        grid=(nb, s // tm),
        in_specs=[pl.BlockSpec((1, tm, D_MODEL), lambda b, t: (b, t, 0)), _resident((1, D_MODEL))],
        out_specs=[pl.BlockSpec((1, d1, tm // d1, D_MODEL), lambda b, t: (b, 0, t, 0)),
                   pl.BlockSpec((1, d2, tm // d2, D_MODEL), lambda b, t: (b, 0, t, 0))],
        out_shape=[jax.ShapeDtypeStruct((nb, d1, s // d1, D_MODEL), BF16),
                   jax.ShapeDtypeStruct((nb, d2, s // d2, D_MODEL), BF16)],
        scratch_shapes=[pltpu.VMEM((D_MODEL // LANES, tm, LANES), F32)],
        compiler_params=_params(("arbitrary", "arbitrary")),
        name="rms_deint",
    )(x, g.reshape(1, D_MODEL))


def _swa_group_body(*refs, group, dil, tm, rb, normed):
    if normed:
        x_ref, w_ref, o_ref, st_ref, q_scr, k_scr, v_scr = refs
    else:
        x_ref, g_ref, w_ref, o_ref, st_ref, q_scr, k_scr, v_scr = refs
    it = pl.program_id(2)
    scale = HEAD_DIM ** -0.5
    qi = lax.broadcasted_iota(jnp.int32, (CHUNK, 2 * CHUNK), 0)
    kc = lax.broadcasted_iota(jnp.int32, (CHUNK, 2 * CHUNK), 1)
    delta = CHUNK + qi - kc
    valid = (delta >= 0) & (delta <= CHUNK)
    dist = (delta * dil).astype(F32)
    lane = lax.broadcasted_iota(jnp.int32, (CHUNK, LANES), 1)

    @pl.when(it == 0)
    def _():
        k_scr[:, 0:CHUNK, :] = jnp.zeros((rb, CHUNK, WIDTH), BF16)
        v_scr[:, 0:CHUNK, :] = jnp.zeros((rb, CHUNK, WIDTH), BF16)

    for r in range(rb):
        hn = x_ref[0, r] if normed else _rms(x_ref[0, r], g_ref[...]).astype(BF16)
        q_scr[...] = _dot(hn, w_ref[:, 0:WIDTH]).astype(BF16)
        k_scr[r, CHUNK:, :] = _dot(hn, w_ref[:, WIDTH:2 * WIDTH]).astype(BF16)
        v_scr[r, CHUNK:, :] = _dot(hn, w_ref[:, 2 * WIDTH:3 * WIDTH]).astype(BF16)
        for j in range(tm // CHUNK):
            rows = slice(j * CHUNK, (j + 1) * CHUNK)
            mask = valid if j > 0 else valid & (kc >= jnp.where(it > 0, 0, CHUNK))
            stats = jnp.zeros((CHUNK, LANES), F32)
            for h in range(HEADS):
                hs = slice(h * HEAD_DIM, (h + 1) * HEAD_DIM)
                qj = q_scr[rows, hs]
                kk = k_scr[r, j * CHUNK:(j + 2) * CHUNK, hs]
                vv = v_scr[r, j * CHUNK:(j + 2) * CHUNK, hs]
                s = _dot_nt(qj, kk) * scale + (-_alibi_slope(group, h)) * dist
                s = jnp.where(mask, s, NEG_INF)
                m = jnp.max(s, axis=-1, keepdims=True)
                p = jnp.exp(s - m)
                l = jnp.sum(p, axis=-1, keepdims=True)
                o_ref[0, r, h, rows, :] = _dot(p.astype(BF16), vv)
                stats = jnp.where(lane == h, m, jnp.where(lane == HEADS + h, l, stats))
            st_ref[0, r, rows, :] = stats
        k_scr[r, 0:CHUNK, :] = k_scr[r, tm:tm + CHUNK, :]
        v_scr[r, 0:CHUNK, :] = v_scr[r, tm:tm + CHUNK, :]


def _swa_group_prompt(x, g, c_in, *, group, tm, rb):
    nb, dil, sub, _ = x.shape
    normed = x.dtype == BF16
    in_specs = [pl.BlockSpec((1, rb, tm, D_MODEL), lambda b, r, t: (b, r, t, 0))]
    args = [x]
    if not normed:
        in_specs.append(_resident((1, D_MODEL)))
        args.append(g.reshape(1, D_MODEL))
    in_specs.append(pl.BlockSpec((D_MODEL, C_GROUP_COLS), lambda b, r, t: (0, group), pipeline_mode=pl.Buffered(1)))
    args.append(c_in)
    return pl.pallas_call(
        functools.partial(_swa_group_body, group=group, dil=dil, tm=tm, rb=rb, normed=normed),
        grid=(nb, dil // rb, sub // tm),
        in_specs=in_specs,
        out_specs=[pl.BlockSpec((1, rb, HEADS, tm, HEAD_DIM), lambda b, r, t: (b, r, 0, t, 0)),
                   pl.BlockSpec((1, rb, tm, LANES), lambda b, r, t: (b, r, t, 0))],
        out_shape=[jax.ShapeDtypeStruct((nb, dil, HEADS, sub, HEAD_DIM), F32),
                   jax.ShapeDtypeStruct((nb, dil, sub, LANES), F32)],
        scratch_shapes=[pltpu.VMEM((tm, WIDTH), BF16),
                        pltpu.VMEM((rb, CHUNK + tm, WIDTH), BF16),
                        pltpu.VMEM((rb, CHUNK + tm, WIDTH), BF16)],
        compiler_params=_params(("arbitrary", "arbitrary", "arbitrary")),
        name="swa_group%d" % group,
    )(*args)


def _kv_tail_body(x_ref, g_ref, wk_ref, wv_ref, kv_ref):
    hn = _rms(x_ref[0], g_ref[...]).astype(BF16)
    for j, w_ref in enumerate((wk_ref, wv_ref)):
        kv = _dot(hn, w_ref[...])
        for h in range(HEADS):
            kv_ref[0, 0, :, j, h, :] = kv[:, h * HEAD_DIM:(h + 1) * HEAD_DIM]


def _kv_tail(x, g, c_in, *, group):
    win = SWA_GROUPS[group][0]
    nb, s, _ = x.shape
    tm = min(win, 512)
    first = (s - win) // tm
    wcol = lambda j: pl.BlockSpec((D_MODEL, WIDTH), lambda b, t: (0, 3 * group + j), pipeline_mode=pl.Buffered(1))
    return pl.pallas_call(
        _kv_tail_body,
        grid=(nb, win // tm),
        in_specs=[pl.BlockSpec((1, tm, D_MODEL), lambda b, t: (b, first + t, 0)), _resident((1, D_MODEL)),
                  wcol(1), wcol(2)],
        out_specs=pl.BlockSpec((1, 1, tm, 2, HEADS, HEAD_DIM), lambda b, t: (0, b, t, 0, 0, 0)),
        out_shape=jax.ShapeDtypeStruct((1, nb, win, 2, HEADS, HEAD_DIM), F32),
        compiler_params=_params(("arbitrary", "arbitrary")),
        name="kv_tail%d" % group,
    )(x, g.reshape(1, D_MODEL), c_in, c_in)


def _swa_merge_body(o0_ref, o1_ref, o2_ref, s0_ref, s1_ref, s2_ref, x_ref, w_ref, y_ref,
                    n1_scr, n2_scr, t1_scr, t2_scr, cat_scr):
    t = x_ref.shape[1]
    for (_, dil), o_ref, s_ref, n_scr, t_scr in ((SWA_GROUPS[1], o1_ref, s1_ref, n1_scr, t1_scr),
                                                  (SWA_GROUPS[2], o2_ref, s2_ref, n2_scr, t2_scr)):
        for r in range(dil):
            t_scr[pl.ds(r, t // dil, stride=dil), :] = s_ref[0, r]
            for h in range(HEADS):
                n_scr[h, pl.ds(r, t // dil, stride=dil), :] = o_ref[0, r, h]
    rows_per_pass = 256
    for c in range(t // rows_per_pass):
        rows = slice(c * rows_per_pass, (c + 1) * rows_per_pass)
        stats = (s0_ref[0, 0, rows, :], t1_scr[rows, :], t2_scr[rows, :])
        for h in range(HEADS):
            outs = (o0_ref[0, 0, h, rows, :], n1_scr[h, rows, :], n2_scr[h, rows, :])
            ms = [st[:, h:h + 1] for st in stats]
            ls = [st[:, HEADS + h:HEADS + h + 1] for st in stats]
            top = functools.reduce(jnp.maximum, ms)
            ws = [jnp.exp(m - top) for m in ms]
            num = sum(w * o for w, o in zip(ws, outs))
            den = sum(w * l for w, l in zip(ws, ls))
            cat_scr[rows, h * HEAD_DIM:(h + 1) * HEAD_DIM] = (num / den).astype(BF16)
    y_ref[0] = x_ref[0] + _dot(cat_scr[...], w_ref[...])


def _swa_merge(outs, stats, x, w_out, *, tm):
    nb, s, _ = x.shape
    in_specs = []
    for (_, dil) in SWA_GROUPS:
        in_specs.append(pl.BlockSpec((1, dil, HEADS, tm // dil, HEAD_DIM), lambda b, t: (b, 0, 0, t, 0)))
    for (_, dil) in SWA_GROUPS:
        in_specs.append(pl.BlockSpec((1, dil, tm // dil, LANES), lambda b, t: (b, 0, t, 0)))
    x_spec = pl.BlockSpec((1, tm, D_MODEL), lambda b, t: (b, t, 0))
    return pl.pallas_call(
        _swa_merge_body,
        grid=(nb, s // tm),
        in_specs=in_specs + [x_spec, _resident(w_out.shape)],
        out_specs=x_spec,
        out_shape=jax.ShapeDtypeStruct(x.shape, F32),
        scratch_shapes=[pltpu.VMEM((HEADS, tm, HEAD_DIM), F32), pltpu.VMEM((HEADS, tm, HEAD_DIM), F32),
                        pltpu.VMEM((tm, LANES), F32), pltpu.VMEM((tm, LANES), F32),
                        pltpu.VMEM((tm, WIDTH), BF16)],
        compiler_params=_params(("arbitrary", "arbitrary")),
        name="swa_merge",
    )(*outs, *stats, x, w_out)


def _proj_body(x_ref, g_ref, w_ref, z_ref):
    z_ref[...] = _dot(_rms(x_ref[...], g_ref[...]).astype(BF16), w_ref[...])


def _proj(x, g, w):
    m, n = x.shape[0], w.shape[1]
    return pl.pallas_call(
        _proj_body,
        grid=(1,),
        in_specs=[_resident(x.shape), _resident((1, D_MODEL)), _resident(w.shape)],
        out_specs=pl.BlockSpec((m, n), lambda i: (0, 0)),
        out_shape=jax.ShapeDtypeStruct((m, n), F32),
        compiler_params=_params(("arbitrary",)),
        name="proj_rows",
    )(x, g.reshape(1, D_MODEL), w)


def _out_proj_body(c_ref, w_ref, x_ref, y_ref):
    y_ref[...] = x_ref[...] + _dot(c_ref[...].astype(BF16), w_ref[...])


def _out_proj(cat, w, x):
    return pl.pallas_call(
        _out_proj_body,
        grid=(1,),
        in_specs=[_resident(cat.shape), _resident(w.shape), _resident(x.shape)],
        out_specs=pl.BlockSpec(x.shape, lambda i: (0, 0)),
        out_shape=jax.ShapeDtypeStruct(x.shape, F32),
        compiler_params=_params(("arbitrary",)),
        name="out_proj_rows",
    )(cat, w, x)


def _mixer_ab_step_body(z_ref, bif_ref, sg_ref, w00_ref, b0_ref, c_ref, n_ref, m_ref,
                        cat_ref, c1_ref, n1_ref, m1_ref, vn_ref):
    nb = z_ref.shape[0]
    scale = HEAD_DIM ** -0.5
    eye = (lax.broadcasted_iota(jnp.int32, (HEAD_DIM, HEAD_DIM), 0)
           == lax.broadcasted_iota(jnp.int32, (HEAD_DIM, HEAD_DIM), 1)).astype(F32)
    lane = lax.broadcasted_iota(jnp.int32, (1, LANES), 1)
    for i in range(nb):
        zr = z_ref[i:i + 1, :]
        gates = zr[:, COL_GATES:COL_GATES + LANES] + bif_ref[...]
        m_new = jnp.zeros((1, LANES), F32)
        for h in range(HEADS):
            hs = slice(h * HEAD_DIM, (h + 1) * HEAD_DIM)
            q = zr[:, hs]
            k = zr[:, WIDTH + h * HEAD_DIM:WIDTH + (h + 1) * HEAD_DIM] * scale
            v = zr[:, 2 * WIDTH + h * HEAD_DIM:2 * WIDTH + (h + 1) * HEAD_DIM]
            o = zr[:, 3 * WIDTH + h * HEAD_DIM:3 * WIDTH + (h + 1) * HEAD_DIM]
            ig = gates[:, h:h + 1]
            lf = _log_sigmoid(gates[:, HEADS + h:HEADS + h + 1])
            c0 = c_ref[i, h]
            n0 = n_ref[i, h]
            m0 = m_ref[i:i + 1, h:h + 1]
            a = m0 + lf
            m = jnp.maximum(a, ig)
            s = jnp.sum(q * k, axis=-1, keepdims=True) * jnp.exp(ig - m)
            inter = jnp.exp(a - m)
            cq_col = jnp.sum(c0 * q, axis=-1, keepdims=True)
            cq = jnp.sum(eye * cq_col, axis=0, keepdims=True)
            v_col = jnp.sum(eye * v, axis=-1, keepdims=True)
            num = s * v + inter * cq
            den = s + inter * jnp.sum(n0 * q, axis=-1, keepdims=True)
            hh = num / jnp.maximum(jnp.abs(den), jnp.exp(-m))
            w = jnp.exp(ig - m)
            c1_ref[i, h] = inter * c0 + (w * v_col) * k
            n1_ref[i, h] = inter * n0 + w * k
            m_new = jnp.where(lane == h, m, m_new)
            cat_ref[i:i + 1, hs] = jax.nn.sigmoid(o) * hh
        m1_ref[i:i + 1, :] = m_new
        vn = _rms(zr[:, COL_GV:COL_GV + WIDTH], sg_ref[...])
        vn_ref[i:i + 1, :] = vn
        cat_ref[i:i + 1, WIDTH:2 * WIDTH] = zr[:, COL_U:COL_U + WIDTH] * (w00_ref[...] * vn + b0_ref[...])


def _mixer_ab_step(z, b_if, sgu_g, w00, b0, st_c, st_n, st_m, *, nb):
    n = z.shape[0]
    rows = lambda w: pl.BlockSpec((nb, w), lambda i: (i, 0))
    c_spec = pl.BlockSpec((nb, HEADS, HEAD_DIM, HEAD_DIM), lambda i: (i, 0, 0, 0))
    n_spec = pl.BlockSpec((nb, HEADS, 1, HEAD_DIM), lambda i: (i, 0, 0, 0))
    return pl.pallas_call(
        _mixer_ab_step_body,
        grid=(n // nb,),
        in_specs=[rows(A_IN_PAD), _resident((1, LANES)), _resident((1, WIDTH)), _resident((1, WIDTH)),
                  _resident((1, WIDTH)), c_spec, n_spec, rows(LANES)],
        out_specs=[rows(2 * WIDTH), c_spec, n_spec, rows(LANES), rows(WIDTH)],
        out_shape=[jax.ShapeDtypeStruct((n, 2 * WIDTH), F32),
                   jax.ShapeDtypeStruct(st_c.shape, F32),
                   jax.ShapeDtypeStruct(st_n.shape, F32),
                   jax.ShapeDtypeStruct((n, LANES), F32),
                   jax.ShapeDtypeStruct((n, WIDTH), F32)],
        compiler_params=_params(("arbitrary",)),
        name="mixer_ab_step",
    )(z, b_if, sgu_g.reshape(1, WIDTH), w00, b0, st_c, st_n, st_m)


def _swa_step_body(z_ref, kv0_ref, kv1_ref, kv2_ref, cat_ref):
    nb = z_ref.shape[0]
    scale = HEAD_DIM ** -0.5
    steps = (CHUNK - lax.broadcasted_iota(jnp.int32, (CHUNK, 1, 1), 0)).astype(F32)
    head = lax.broadcasted_iota(jnp.int32, (1, HEADS, 1), 1)
    for i in range(nb):
        ms, ls, os_ = [], [], []
        for gi, kv_ref in enumerate((kv0_ref, kv1_ref, kv2_ref)):
            dil = SWA_GROUPS[gi][1]
            base = gi * 3 * HEADS
            q = z_ref[i, base:base + HEADS, :]
            k_new = z_ref[i, base + HEADS:base + 2 * HEADS, :]
            v_new = z_ref[i, base + 2 * HEADS:base + 3 * HEADS, :]
            kc = kv_ref[i, :, 0, 0, :, :]
            vc = kv_ref[i, :, 0, 1, :, :]
            slope = jnp.zeros((1, HEADS, 1), F32)
            for h in range(HEADS):
                slope = jnp.where(head == h, _alibi_slope(gi, h) * dil, slope)
            s = jnp.sum(kc * q[None], axis=-1, keepdims=True) * scale - slope * steps
            s_new = jnp.sum(k_new * q, axis=-1, keepdims=True) * scale
            m = jnp.maximum(jnp.max(s, axis=0), s_new)
            p = jnp.exp(s - m[None])
            p_new = jnp.exp(s_new - m)
            ms.append(m)
            ls.append(jnp.sum(p, axis=0) + p_new)
            os_.append(jnp.sum(p * vc, axis=0) + p_new * v_new)
        top = functools.reduce(jnp.maximum, ms)
        ws = [jnp.exp(m - top) for m in ms]
        num = sum(w * o for w, o in zip(ws, os_))
        den = sum(w * l for w, l in zip(ws, ls))
        cat_ref[i] = num / den


def _swa_step(z, caches, *, nb):
    n = z.shape[0]
    views = []
    specs = []
    for (win, dil), cache in zip(SWA_GROUPS, caches):
        views.append(cache.reshape(n, win // dil, dil, 2, HEADS, HEAD_DIM))
        specs.append(pl.BlockSpec((nb, CHUNK, 1, 2, HEADS, HEAD_DIM), lambda i: (i, 0, 0, 0, 0, 0)))
    return pl.pallas_call(
        _swa_step_body,
        grid=(n // nb,),
        in_specs=[pl.BlockSpec((nb,) + z.shape[1:], lambda i: (i, 0, 0))] + specs,
        out_specs=pl.BlockSpec((nb, HEADS, HEAD_DIM), lambda i: (i, 0, 0)),
        out_shape=jax.ShapeDtypeStruct((n, HEADS, HEAD_DIM), F32),
        compiler_params=_params(("arbitrary",)),
        name="swa_step",
    )(z, *views)


def _pad_cols(w, n):
    return jnp.pad(w, ((0, 0), (0, n - w.shape[1])))


def kernel(x_prompt, x_sample, state_mlstm_C, state_mlstm_n, state_mlstm_m, cache_swa_kv0, cache_swa_kv1, cache_swa_kv2, norm_g, ffn_w_gate, ffn_w_up, ffn_w_down, a_w_in, a_b_if, sgu_norm_g, sgu_w, sgu_b, a_w_out, c_w_in, c_w_out, final_norm_g):
    nb, s, _ = x_prompt.shape
    ns = x_sample.shape[0]
    assert x_sample.shape[1] == 1 and s % max(w for w, _ in SWA_GROUPS) == 0
    for (win, dil), cache in zip(SWA_GROUPS, (cache_swa_kv0, cache_swa_kv1, cache_swa_kv2)):
        assert cache.shape[2] == win and win // dil == CHUNK

    ffn_w = (ffn_w_gate.astype(BF16), ffn_w_up.astype(BF16), ffn_w_down.astype(BF16))
    g_lo, g_hi = 4 * WIDTH, 4 * WIDTH + 2 * HEADS
    a_in = jnp.concatenate([a_w_in[0][:, :g_lo], a_w_in[0][:, g_hi:], _pad_cols(a_w_in[0][:, g_lo:g_hi], LANES)],
                           axis=1).astype(BF16)
    b_if = _pad_cols(a_b_if[0].reshape(1, 2 * HEADS), LANES)
    a_out = a_w_out[0].astype(BF16)
    c_in = c_w_in[0].astype(BF16)
    c_out = c_w_out[0].astype(BF16)
    sgu_bt = _pad_cols(sgu_b[0].T, LANES)
    sgu_w00 = jnp.repeat(sgu_w[0, :, 0, 0], CHUNK).reshape(1, WIDTH)
    sgu_b0 = jnp.repeat(sgu_b[0, :, 0], CHUNK).reshape(1, WIDTH)

    xp = x_prompt.reshape(nb * s, D_MODEL)
    xp = _ffn(xp, norm_g[0, 0], ffn_w, 0, 0, tm=512)
    xp, p_c, p_n, p_m = _mixer_ab_prompt(xp.reshape(nb, s, D_MODEL), norm_g[0, 1], a_in, b_if,
                                          sgu_norm_g[0], sgu_w[0], sgu_bt, a_out)
    xp = _ffn(xp.reshape(nb * s, D_MODEL), norm_g[0, 2], ffn_w, 0, 1, tm=512)
    xp = _ffn(xp, norm_g[1, 0], ffn_w, 1, 0, tm=512)
    xp3 = xp.reshape(nb, s, D_MODEL)
    hn1, hn2 = _rms_deint(xp3, norm_g[1, 1], tm=1024)
    outs, stats = [], []
    for gi, (xin, tm, rb) in enumerate(((xp3.reshape(nb, 1, s, D_MODEL), 512, 1), (hn1, 512, 1), (hn2, 256, 2))):
        o, st = _swa_group_prompt(xin, norm_g[1, 1], c_in, group=gi, tm=tm, rb=rb)
        outs.append(o)
        stats.append(st)
    p_kv = [_kv_tail(xp3, norm_g[1, 1], c_in, group=gi) for gi in range(3)]
    xp = _swa_merge(outs, stats, xp3, c_out, tm=1024).reshape(nb * s, D_MODEL)
    y_prompt = _ffn(xp, norm_g[1, 2], ffn_w, 1, 1, final_norm_g, tm=512).reshape(nb, s, D_MODEL)

    xs = x_sample.reshape(ns, D_MODEL)
    xs = _ffn(xs, norm_g[0, 0], ffn_w, 0, 0, tm=ns)
    z = _proj(xs, norm_g[0, 1], a_in)
    cat, s_c, s_n, s_m, s_v = _mixer_ab_step(
        z, b_if, sgu_norm_g[0], sgu_w00, sgu_b0, state_mlstm_C[0],
        state_mlstm_n[0].reshape(ns, HEADS, 1, HEAD_DIM), _pad_cols(state_mlstm_m[0], LANES), nb=8)
    xs = _out_proj(cat, a_out, xs)
    xs = _ffn(xs, norm_g[0, 2], ffn_w, 0, 1, tm=ns)
    xs = _ffn(xs, norm_g[1, 0], ffn_w, 1, 0, tm=ns)
    z = _proj(xs, norm_g[1, 1], c_in)
    cat = _swa_step(z.reshape(ns, 3 * 3 * HEADS, HEAD_DIM), (cache_swa_kv0, cache_swa_kv1, cache_swa_kv2), nb=4)
    xs = _out_proj(cat.reshape(ns, WIDTH), c_out, xs)
    y_sample = _ffn(xs, norm_g[1, 2], ffn_w, 1, 1, final_norm_g, tm=ns).reshape(ns, 1, D_MODEL)
    s_kv = [z[:, gi * C_GROUP_COLS + WIDTH:(gi + 1) * C_GROUP_COLS].reshape(1, ns, 1, 2, HEADS, HEAD_DIM)
            for gi in range(3)]

    return (y_prompt, y_sample,
            p_c.reshape(1, nb, HEADS, HEAD_DIM, HEAD_DIM), p_n.reshape(1, nb, HEADS, HEAD_DIM),
            p_m[:, :, 0, 0].reshape(1, nb, HEADS),
            s_c.reshape(1, ns, HEADS, HEAD_DIM, HEAD_DIM), s_n.reshape(1, ns, HEADS, HEAD_DIM),
            s_m[:, :HEADS].reshape(1, ns, HEADS), s_v.reshape(1, ns, 1, WIDTH),
            p_kv[0], p_kv[1], p_kv[2], s_kv[0], s_kv[1], s_kv[2])
```

```python
import functools

import jax
import jax.numpy as jnp
from jax import lax
from jax.experimental import pallas as pl
from jax.experimental.pallas import tpu as pltpu

F32 = jnp.float32
BF16 = jnp.bfloat16

D_MODEL = 1024
D_FF = 2752
HEADS = 4
HEAD_DIM = 128
WIDTH = HEADS * HEAD_DIM
CHUNK = 128
SWA_GROUPS = ((128, 1), (512, 4), (2048, 16))
NORM_EPS = 1e-6
NEG_INF = -1e30

LANES = 128
BF16_ROWS = 16
PROJ_SLAB = 256
FF_CHUNK = 256
A_IN_PAD = 4 * WIDTH + 2 * WIDTH + LANES
COL_U = 4 * WIDTH
COL_GV = 5 * WIDTH
COL_GATES = 6 * WIDTH
C_GROUP_COLS = 3 * WIDTH

VMEM_LIMIT = 56 * 1024 * 1024


def _params(semantics):
    return pltpu.CompilerParams(dimension_semantics=semantics, vmem_limit_bytes=VMEM_LIMIT)


def _resident(shape):
    nd = len(shape)
    return pl.BlockSpec(shape, lambda *_: (0,) * nd, pipeline_mode=pl.Buffered(1))


def _rms(x, g):
    ms = jnp.mean(x * x, axis=-1, keepdims=True)
    return x * lax.rsqrt(ms + NORM_EPS) * g


def _dot(a, b):
    return jnp.dot(a, b, preferred_element_type=F32)


def _dot_nt(a, b):
    return lax.dot_general(a, b, (((1,), (1,)), ((), ())), preferred_element_type=F32)


def _log_sigmoid(x):
    return jnp.minimum(x, 0.0) - jnp.log1p(jnp.exp(-jnp.abs(x)))


def _swiglu_rows(h_scr, rows, wg_ref, wu_ref, wd_ref):
    acc = None
    for c0 in range(0, D_FF, FF_CHUNK):
        cols = slice(c0, min(c0 + FF_CHUNK, D_FF))
        h = h_scr[0:rows, :]
        gate = _dot(h, wg_ref[0, 0, :, cols])
        up = _dot(h, wu_ref[0, 0, :, cols])
        act = (gate * jax.nn.sigmoid(gate) * up).astype(h_scr.dtype)
        part = _dot(act, wd_ref[0, 0, cols, :])
        acc = part if acc is None else acc + part
    return acc


def _ffn_body(*refs, final):
    if final:
        x_ref, xs_ref, g_ref, wg_ref, wu_ref, wd_ref, fg_ref, o_ref, os_ref, h_scr = refs
    else:
        x_ref, xs_ref, g_ref, wg_ref, wu_ref, wd_ref, o_ref, os_ref, h_scr = refs
    tm, ns = x_ref.shape[0], xs_ref.shape[0]
    last = pl.num_programs(0) - 1

    def finish(x, acc):
        y = x + 0.5 * acc
        return _rms(y, fg_ref[...]) if final else y

    x = x_ref[...]
    h_scr[0:tm, :] = _rms(x, g_ref[...]).astype(h_scr.dtype)

    @pl.when(pl.program_id(0) != last)
    def _():
        o_ref[...] = finish(x, _swiglu_rows(h_scr, tm, wg_ref, wu_ref, wd_ref))

    @pl.when(pl.program_id(0) == last)
    def _():
        xs = xs_ref[...]
        h_scr[tm:tm + ns, :] = _rms(xs, g_ref[...]).astype(h_scr.dtype)
        acc = _swiglu_rows(h_scr, tm + ns, wg_ref, wu_ref, wd_ref)
        o_ref[...] = finish(x, acc[0:tm])
        os_ref[...] = finish(xs, acc[tm:tm + ns])


def _ffn(x, xs, g, weights, layer, which, final_g=None, *, tm):
    m, ns = x.shape[0], xs.shape[0]
    final = final_g is not None
    wg, wu, wd = weights
    row = pl.BlockSpec((tm, D_MODEL), lambda i: (i, 0))
    wspec = lambda w: pl.BlockSpec((1, 1) + w.shape[2:], lambda i: (layer, which, 0, 0), pipeline_mode=pl.Buffered(1))
    in_specs = [row, _resident(xs.shape), _resident((1, D_MODEL)), wspec(wg), wspec(wu), wspec(wd)]
    args = [x, xs, g.reshape(1, D_MODEL), wg, wu, wd]
    if final:
        in_specs.append(_resident((1, D_MODEL)))
        args.append(final_g.reshape(1, D_MODEL))
    return pl.pallas_call(
        functools.partial(_ffn_body, final=final),
        grid=(m // tm,),
        in_specs=in_specs,
        out_specs=[row, pl.BlockSpec(xs.shape, lambda i: (0, 0))],
        out_shape=[jax.ShapeDtypeStruct((m, D_MODEL), F32), jax.ShapeDtypeStruct(xs.shape, F32)],
        scratch_shapes=[pltpu.VMEM((tm + ns, D_MODEL), wg.dtype)],
        compiler_params=_params(("arbitrary",)),
        name="ffn_final" if final else "ffn",
    )(*args)


def _cummax_lanes(x):
    lane = lax.broadcasted_iota(jnp.int32, x.shape, 1)
    d = 1
    while d < x.shape[1]:
        x = jnp.maximum(x, jnp.where(lane >= d, pltpu.roll(x, d, axis=1), NEG_INF))
        d *= 2
    return x


def _exact_tri_dot(tri_bf16, x):
    x1 = x.astype(BF16)
    r1 = x - x1.astype(F32)
    x2 = r1.astype(BF16)
    x3 = (r1 - x2.astype(F32)).astype(BF16)
    n = x.shape[1]
    r = _dot(tri_bf16, jnp.concatenate([x1, x2, x3], axis=1))
    return r[:, 0:n] + r[:, n:2 * n] + r[:, 2 * n:3 * n]


def _mixer_ab_chunk(xn_ref, xp_ref, g_ref, win_ref, bif_ref, sg_ref, sw_ref, sbt_ref, wout_ref,
                   y_ref, c_ref, n_ref, m_ref, z_cur, z_nxt, cat_scr):
    nb = xn_ref.shape[0]
    row = lax.broadcasted_iota(jnp.int32, (CHUNK, CHUNK), 0)
    col = lax.broadcasted_iota(jnp.int32, (CHUNK, CHUNK), 1)
    causal = col <= row
    keys_before = row <= col
    tri = jnp.where(causal, 1.0, 0.0).astype(BF16)
    scale = HEAD_DIM ** -0.5

    hn = _rms(xn_ref[...].reshape(nb * CHUNK, D_MODEL), g_ref[...]).astype(BF16)
    slabs = [(c0, min(c0 + PROJ_SLAB, A_IN_PAD)) for c0 in range(0, A_IN_PAD, PROJ_SLAB)]

    def project(count):
        for _ in range(min(count, len(slabs))):
            c0, c1 = slabs.pop(0)
            z_nxt[:, c0:c1] = _dot(hn, win_ref[:, c0:c1])

    sgu_bias = [jnp.broadcast_to(sbt_ref[:, g:g + 1], (CHUNK, CHUNK)) for g in range(HEADS)]
    vn = [_rms(z_cur[b * CHUNK:(b + 1) * CHUNK, COL_GV:COL_GV + WIDTH], sg_ref[...]) for b in range(nb)]
    for g in range(HEADS):
        gs = slice(g * CHUNK, (g + 1) * CHUNK)
        mixed = _dot(jnp.where(causal, sw_ref[g], 0.0).astype(BF16),
                     jnp.concatenate([vn[b][:, gs] for b in range(nb)], axis=1).astype(BF16))
        for b in range(nb):
            rows = slice(b * CHUNK, (b + 1) * CHUNK)
            u = z_cur[rows, COL_U + g * CHUNK:COL_U + (g + 1) * CHUNK]
            cat_scr[rows, WIDTH + g * CHUNK:WIDTH + (g + 1) * CHUNK] = (
                u * (mixed[:, b * CHUNK:(b + 1) * CHUNK] + sgu_bias[g])).astype(BF16)
    project(1)

    pieces_per_stage = -(-(len(slabs)) // (3 * nb))
    gate_terms = []
    for b in range(nb):
        rows = slice(b * CHUNK, (b + 1) * CHUNK)
        gates = z_cur[rows, COL_GATES:COL_GATES + LANES] + bif_ref[...]
        lg = jnp.where(col < HEADS, gates, _log_sigmoid(gates))
        gate_terms.append((lg, _exact_tri_dot(tri, lg)))
        project(pieces_per_stage)

    heads = []
    for b, (lg, fcum) in enumerate(gate_terms):
        rows = slice(b * CHUNK, (b + 1) * CHUNK)
        lg_t = lg.T
        fcum_t = fcum.T
        gmax = _cummax_lanes(lg_t[0:2 * HEADS, :] - jnp.concatenate([fcum_t[HEADS:2 * HEADS, :]] * 2, axis=0))
        for h in range(HEADS):
            q = z_cur[rows, h * HEAD_DIM:(h + 1) * HEAD_DIM]
            k = z_cur[rows, WIDTH + h * HEAD_DIM:WIDTH + (h + 1) * HEAD_DIM] * scale
            v_t = z_cur[rows, 2 * WIDTH + h * HEAD_DIM:2 * WIDTH + (h + 1) * HEAD_DIM].T
            f_r = fcum_t[HEADS + h:HEADS + h + 1, :]
            i_r = lg_t[h:h + 1, :]
            g_c = lg[:, h:h + 1] - fcum[:, HEADS + h:HEADS + h + 1]
            c0, n0, m0 = c_ref[b, h], n_ref[b, h], m_ref[b, h]
            m = f_r + jnp.maximum(m0, gmax[h:h + 1, :])
            inter = jnp.exp(m0 + f_r - m)
            qb, kb = q.astype(BF16), k.astype(BF16)
            against_q = _dot_nt(
                jnp.concatenate([k, c0, jnp.broadcast_to(n0, (BF16_ROWS, HEAD_DIM))], axis=0).astype(BF16), qb)
            m_last = m[:, CHUNK - 1:CHUNK]
            f_last = f_r[:, CHUNK - 1:CHUNK]
            w = jnp.exp(f_last - f_r + i_r - m_last)
            decay = jnp.exp(m0[:, 0:1] + f_last - m_last)
            against_k = _dot(
                jnp.concatenate([v_t * w, jnp.broadcast_to(w, (BF16_ROWS, CHUNK))], axis=0).astype(BF16), kb)
            c_ref[b, h] = decay * c0 + against_k[0:HEAD_DIM]
            n_ref[b, h] = decay * n0 + against_k[HEAD_DIM:HEAD_DIM + 1]
            m_ref[b, h] = jnp.broadcast_to(m_last, (1, LANES))
            heads.append((against_q, v_t, f_r, g_c, m, inter))
        project(pieces_per_stage)

    partial = []
    for i, (against_q, v_t, f_r, g_c, m, inter) in enumerate(heads):
        kq = against_q[0:CHUNK]
        cq = against_q[CHUNK:CHUNK + HEAD_DIM]
        nq = against_q[CHUNK + HEAD_DIM:CHUNK + HEAD_DIM + 1]
        s_t = kq * jnp.exp(jnp.where(keys_before, (f_r - m) + g_c, NEG_INF))
        den = jnp.sum(s_t, axis=0, keepdims=True) + inter * nq
        partial.append((_dot(v_t.astype(BF16), s_t.astype(BF16)), inter * cq,
                        jnp.maximum(jnp.abs(den), jnp.exp(-m))))
        if i % HEADS == HEADS - 1:
            project(pieces_per_stage)

    for i, (sv, carried, den) in enumerate(partial):
        b, h = divmod(i, HEADS)
        rows = slice(b * CHUNK, (b + 1) * CHUNK)
        o = z_cur[rows, 3 * WIDTH + h * HEAD_DIM:3 * WIDTH + (h + 1) * HEAD_DIM]
        cat_scr[rows, h * HEAD_DIM:(h + 1) * HEAD_DIM] = (jax.nn.sigmoid(o) * ((sv + carried) / den).T).astype(BF16)

    project(len(slabs))
    y = xp_ref[...].reshape(nb * CHUNK, D_MODEL) + _dot(cat_scr[...], wout_ref[...])
    y_ref[...] = y.reshape(nb, CHUNK, D_MODEL)


def _mixer_ab_body(*refs):
    *io_refs, z0_scr, z1_scr, cat_scr = refs
    c_ref, n_ref, m_ref = io_refs[-3:]
    step = pl.program_id(0)

    @pl.when(step <= 1)
    def _():
        c_ref[...] = jnp.zeros_like(c_ref)
        n_ref[...] = jnp.zeros_like(n_ref)
        m_ref[...] = jnp.zeros_like(m_ref)

    @pl.when(step == 0)
    def _():
        z1_scr[...] = jnp.zeros_like(z1_scr)

    @pl.when(step % 2 == 0)
    def _():
        _mixer_ab_chunk(*io_refs, z1_scr, z0_scr, cat_scr)

    @pl.when(step % 2 == 1)
    def _():
        _mixer_ab_chunk(*io_refs, z0_scr, z1_scr, cat_scr)


def _mixer_ab_prompt(x, g, w_in, b_if, sgu_g, sgu_w, sgu_bt, w_out):
    nb, s, _ = x.shape
    n_chunks = s // CHUNK
    blk = lambda index: pl.BlockSpec((nb, CHUNK, D_MODEL), index)
    nxt = blk(lambda c: (0, jnp.minimum(c, n_chunks - 1), 0))
    prev = blk(lambda c: (0, jnp.maximum(c - 1, 0), 0))
    z_shape = pltpu.VMEM((nb * CHUNK, A_IN_PAD), F32)
    return pl.pallas_call(
        _mixer_ab_body,
        grid=(n_chunks + 1,),
        in_specs=[nxt, prev, _resident((1, D_MODEL)), _resident(w_in.shape), _resident((1, LANES)),
                  _resident((1, WIDTH)), _resident(sgu_w.shape), _resident(sgu_bt.shape), _resident(w_out.shape)],
        out_specs=[prev,
                   pl.BlockSpec((nb, HEADS, HEAD_DIM, HEAD_DIM), lambda c: (0, 0, 0, 0)),
                   pl.BlockSpec((nb, HEADS, 1, HEAD_DIM), lambda c: (0, 0, 0, 0)),
                   pl.BlockSpec((nb, HEADS, 1, LANES), lambda c: (0, 0, 0, 0))],
        out_shape=[jax.ShapeDtypeStruct(x.shape, F32),
                   jax.ShapeDtypeStruct((nb, HEADS, HEAD_DIM, HEAD_DIM), F32),
                   jax.ShapeDtypeStruct((nb, HEADS, 1, HEAD_DIM), F32),
                   jax.ShapeDtypeStruct((nb, HEADS, 1, LANES), F32)],
        scratch_shapes=[z_shape, z_shape, pltpu.VMEM((nb * CHUNK, 2 * WIDTH), BF16)],
        compiler_params=_params(("arbitrary",)),
        name="mixer_ab_prompt",
    )(x, x, g.reshape(1, D_MODEL), w_in, b_if, sgu_g.reshape(1, WIDTH), sgu_w, sgu_bt, w_out)


def _alibi_slope(group, head):
    n = len(SWA_GROUPS) * HEADS
    return 2.0 ** (-8.0 * (group * HEADS + head + 1) / n)


def _rms_deint_body(x_ref, g_ref, h1_ref, h2_ref, slab_scr):
    t = x_ref.shape[1]
    rows_per_pass = 256
    for c in range(t // rows_per_pass):
        rows = slice(c * rows_per_pass, (c + 1) * rows_per_pass)
        hn = _rms(x_ref[0, rows, :], g_ref[...])
        for sl in range(D_MODEL // LANES):
            slab_scr[sl, rows, :] = hn[:, sl * LANES:(sl + 1) * LANES]
    for (_, dil), out_ref in zip(SWA_GROUPS[1:], (h1_ref, h2_ref)):
        for r in range(dil):
            for sl in range(D_MODEL // LANES):
                piece = slab_scr[sl, pl.ds(r, t // dil, stride=dil), :]
                out_ref[0, r, :, sl * LANES:(sl + 1) * LANES] = piece.astype(BF16)


def _rms_deint(x, g, *, tm):
    nb, s, _ = x.shape
    d1, d2 = SWA_GROUPS[1][1], SWA_GROUPS[2][1]
    return pl.pallas_call(
        _rms_deint_body,
        grid=(nb, s // tm),
        in_specs=[pl.BlockSpec((1, tm, D_MODEL), lambda b, t: (b, t, 0)), _resident((1, D_MODEL))],
        out_specs=[pl.BlockSpec((1, d1, tm // d1, D_MODEL), lambda b, t: (b, 0, t, 0)),
                   pl.BlockSpec((1, d2, tm // d2, D_MODEL), lambda b, t: (b, 0, t, 0))],
        out_shape=[jax.ShapeDtypeStruct((nb, d1, s // d1, D_MODEL), BF16),
                   jax.ShapeDtypeStruct((nb, d2, s // d2, D_MODEL), BF16)],
        scratch_shapes=[pltpu.VMEM((D_MODEL // LANES, tm, LANES), F32)],
        compiler_params=_params(("arbitrary", "arbitrary")),
        name="rms_deint",
    )(x, g.reshape(1, D_MODEL))


def _swa_group_body(*refs, group, dil, tm, rb, normed):
    if normed:
        x_ref, w_ref, o_ref, st_ref, q_scr, k_scr, v_scr = refs
    else:
        x_ref, g_ref, w_ref, o_ref, st_ref, q_scr, k_scr, v_scr = refs
    it = pl.program_id(2)
    scale = HEAD_DIM ** -0.5
    qi = lax.broadcasted_iota(jnp.int32, (CHUNK, 2 * CHUNK), 0)
    kc = lax.broadcasted_iota(jnp.int32, (CHUNK, 2 * CHUNK), 1)
    delta = CHUNK + qi - kc
    valid = (delta >= 0) & (delta <= CHUNK)
    dist = (delta * dil).astype(F32)
    lane = lax.broadcasted_iota(jnp.int32, (CHUNK, LANES), 1)

    @pl.when(it == 0)
    def _():
        k_scr[:, 0:CHUNK, :] = jnp.zeros((rb, CHUNK, WIDTH), BF16)
        v_scr[:, 0:CHUNK, :] = jnp.zeros((rb, CHUNK, WIDTH), BF16)

    for r in range(rb):
        hn = x_ref[0, r] if normed else _rms(x_ref[0, r], g_ref[...]).astype(BF16)
        q_scr[...] = _dot(hn, w_ref[:, 0:WIDTH]).astype(BF16)
        k_scr[r, CHUNK:, :] = _dot(hn, w_ref[:, WIDTH:2 * WIDTH]).astype(BF16)
        v_scr[r, CHUNK:, :] = _dot(hn, w_ref[:, 2 * WIDTH:3 * WIDTH]).astype(BF16)
        for j in range(tm // CHUNK):
            rows = slice(j * CHUNK, (j + 1) * CHUNK)
            mask = valid if j > 0 else valid & (kc >= jnp.where(it > 0, 0, CHUNK))
            stats = jnp.zeros((CHUNK, LANES), F32)
            for h in range(HEADS):
                hs = slice(h * HEAD_DIM, (h + 1) * HEAD_DIM)
                qj = q_scr[rows, hs]
                kk = k_scr[r, j * CHUNK:(j + 2) * CHUNK, hs]
                vv = v_scr[r, j * CHUNK:(j + 2) * CHUNK, hs]
                s = _dot_nt(qj, kk) * scale + (-_alibi_slope(group, h)) * dist
                s = jnp.where(mask, s, NEG_INF)
                m = jnp.max(s, axis=-1, keepdims=True)
                p = jnp.exp(s - m)
                l = jnp.sum(p, axis=-1, keepdims=True)
                o_ref[0, r, h, rows, :] = _dot(p.astype(BF16), vv)
                stats = jnp.where(lane == h, m, jnp.where(lane == HEADS + h, l, stats))
            st_ref[0, r, rows, :] = stats
        k_scr[r, 0:CHUNK, :] = k_scr[r, tm:tm + CHUNK, :]
        v_scr[r, 0:CHUNK, :] = v_scr[r, tm:tm + CHUNK, :]


def _swa_group_prompt(x, g, c_in, *, group, tm, rb):
    nb, dil, sub, _ = x.shape
    normed = x.dtype == BF16
    in_specs = [pl.BlockSpec((1, rb, tm, D_MODEL), lambda b, r, t: (b, r, t, 0))]
    args = [x]
    if not normed:
        in_specs.append(_resident((1, D_MODEL)))
        args.append(g.reshape(1, D_MODEL))
    in_specs.append(pl.BlockSpec((D_MODEL, C_GROUP_COLS), lambda b, r, t: (0, group), pipeline_mode=pl.Buffered(1)))
    args.append(c_in)
    return pl.pallas_call(
        functools.partial(_swa_group_body, group=group, dil=dil, tm=tm, rb=rb, normed=normed),
        grid=(nb, dil // rb, sub // tm),
        in_specs=in_specs,
        out_specs=[pl.BlockSpec((1, rb, HEADS, tm, HEAD_DIM), lambda b, r, t: (b, r, 0, t, 0)),
                   pl.BlockSpec((1, rb, tm, LANES), lambda b, r, t: (b, r, t, 0))],
        out_shape=[jax.ShapeDtypeStruct((nb, dil, HEADS, sub, HEAD_DIM), F32),
                   jax.ShapeDtypeStruct((nb, dil, sub, LANES), F32)],
        scratch_shapes=[pltpu.VMEM((tm, WIDTH), BF16),
                        pltpu.VMEM((rb, CHUNK + tm, WIDTH), BF16),
                        pltpu.VMEM((rb, CHUNK + tm, WIDTH), BF16)],
        compiler_params=_params(("arbitrary", "arbitrary", "arbitrary")),
        name="swa_group%d" % group,
    )(*args)


def _kv_tail_body(x_ref, g_ref, wk_ref, wv_ref, kv_ref):
    hn = _rms(x_ref[0], g_ref[...]).astype(BF16)
    for j, w_ref in enumerate((wk_ref, wv_ref)):
        kv = _dot(hn, w_ref[...])
        for h in range(HEADS):
            kv_ref[0, 0, :, j, h, :] = kv[:, h * HEAD_DIM:(h + 1) * HEAD_DIM]


def _kv_tail(x, g, c_in, *, group):
    win = SWA_GROUPS[group][0]
    nb, s, _ = x.shape
    tm = min(win, 512)
    first = (s - win) // tm
    wcol = lambda j: pl.BlockSpec((D_MODEL, WIDTH), lambda b, t: (0, 3 * group + j), pipeline_mode=pl.Buffered(1))
    return pl.pallas_call(
        _kv_tail_body,
        grid=(nb, win // tm),
        in_specs=[pl.BlockSpec((1, tm, D_MODEL), lambda b, t: (b, first + t, 0)), _resident((1, D_MODEL)),
                  wcol(1), wcol(2)],
        out_specs=pl.BlockSpec((1, 1, tm, 2, HEADS, HEAD_DIM), lambda b, t: (0, b, t, 0, 0, 0)),
        out_shape=jax.ShapeDtypeStruct((1, nb, win, 2, HEADS, HEAD_DIM), F32),
        compiler_params=_params(("arbitrary", "arbitrary")),
        name="kv_tail%d" % group,
    )(x, g.reshape(1, D_MODEL), c_in, c_in)


def _swa_merge_body(o0_ref, o1_ref, o2_ref, s0_ref, s1_ref, s2_ref, x_ref, w_ref, y_ref,
                    n1_scr, n2_scr, t1_scr, t2_scr, cat_scr):
    t = x_ref.shape[1]
    for (_, dil), o_ref, s_ref, n_scr, t_scr in ((SWA_GROUPS[1], o1_ref, s1_ref, n1_scr, t1_scr),
                                                  (SWA_GROUPS[2], o2_ref, s2_ref, n2_scr, t2_scr)):
        for r in range(dil):
            t_scr[pl.ds(r, t // dil, stride=dil), :] = s_ref[0, r]
            for h in range(HEADS):
                n_scr[h, pl.ds(r, t // dil, stride=dil), :] = o_ref[0, r, h]
    rows_per_pass = 256
    for c in range(t // rows_per_pass):
        rows = slice(c * rows_per_pass, (c + 1) * rows_per_pass)
        stats = (s0_ref[0, 0, rows, :], t1_scr[rows, :], t2_scr[rows, :])
        for h in range(HEADS):
            outs = (o0_ref[0, 0, h, rows, :], n1_scr[h, rows, :], n2_scr[h, rows, :])
            ms = [st[:, h:h + 1] for st in stats]
            ls = [st[:, HEADS + h:HEADS + h + 1] for st in stats]
            top = functools.reduce(jnp.maximum, ms)
            ws = [jnp.exp(m - top) for m in ms]
            num = sum(w * o for w, o in zip(ws, outs))
            den = sum(w * l for w, l in zip(ws, ls))
            cat_scr[rows, h * HEAD_DIM:(h + 1) * HEAD_DIM] = (num / den).astype(BF16)
    y_ref[0] = x_ref[0] + _dot(cat_scr[...], w_ref[...])


def _swa_merge(outs, stats, x, w_out, *, tm):
    nb, s, _ = x.shape
    in_specs = []
    for (_, dil) in SWA_GROUPS:
        in_specs.append(pl.BlockSpec((1, dil, HEADS, tm // dil, HEAD_DIM), lambda b, t: (b, 0, 0, t, 0)))
    for (_, dil) in SWA_GROUPS:
        in_specs.append(pl.BlockSpec((1, dil, tm // dil, LANES), lambda b, t: (b, 0, t, 0)))
    x_spec = pl.BlockSpec((1, tm, D_MODEL), lambda b, t: (b, t, 0))
    return pl.pallas_call(
        _swa_merge_body,
        grid=(nb, s // tm),
        in_specs=in_specs + [x_spec, _resident(w_out.shape)],
        out_specs=x_spec,
        out_shape=jax.ShapeDtypeStruct(x.shape, F32),
        scratch_shapes=[pltpu.VMEM((HEADS, tm, HEAD_DIM), F32), pltpu.VMEM((HEADS, tm, HEAD_DIM), F32),
                        pltpu.VMEM((tm, LANES), F32), pltpu.VMEM((tm, LANES), F32),
                        pltpu.VMEM((tm, WIDTH), BF16)],
        compiler_params=_params(("arbitrary", "arbitrary")),
        name="swa_merge",
    )(*outs, *stats, x, w_out)


def _proj_body(x_ref, g_ref, w_ref, z_ref):
    z_ref[...] = _dot(_rms(x_ref[...], g_ref[...]).astype(BF16), w_ref[...])


def _proj(x, g, w):
    m, n = x.shape[0], w.shape[1]
    return pl.pallas_call(
        _proj_body,
        grid=(1,),
        in_specs=[_resident(x.shape), _resident((1, D_MODEL)), _resident(w.shape)],
        out_specs=pl.BlockSpec((m, n), lambda i: (0, 0)),
        out_shape=jax.ShapeDtypeStruct((m, n), F32),
        compiler_params=_params(("arbitrary",)),
        name="proj_rows",
    )(x, g.reshape(1, D_MODEL), w)


def _out_proj_body(c_ref, w_ref, x_ref, y_ref):
    y_ref[...] = x_ref[...] + _dot(c_ref[...].astype(BF16), w_ref[...])


def _out_proj(cat, w, x):
    return pl.pallas_call(
        _out_proj_body,
        grid=(1,),
        in_specs=[_resident(cat.shape), _resident(w.shape), _resident(x.shape)],
        out_specs=pl.BlockSpec(x.shape, lambda i: (0, 0)),
        out_shape=jax.ShapeDtypeStruct(x.shape, F32),
        compiler_params=_params(("arbitrary",)),
        name="out_proj_rows",
    )(cat, w, x)


def _mixer_ab_step_body(z_ref, bif_ref, sg_ref, w00_ref, b0_ref, c_ref, n_ref, m_ref,
                        cat_ref, c1_ref, n1_ref, m1_ref, vn_ref):
    nb = z_ref.shape[0]
    scale = HEAD_DIM ** -0.5
    eye = (lax.broadcasted_iota(jnp.int32, (HEAD_DIM, HEAD_DIM), 0)
           == lax.broadcasted_iota(jnp.int32, (HEAD_DIM, HEAD_DIM), 1)).astype(F32)
    lane = lax.broadcasted_iota(jnp.int32, (1, LANES), 1)
    for i in range(nb):
        zr = z_ref[i:i + 1, :]
        gates = zr[:, COL_GATES:COL_GATES + LANES] + bif_ref[...]
        m_new = jnp.zeros((1, LANES), F32)
        for h in range(HEADS):
            hs = slice(h * HEAD_DIM, (h + 1) * HEAD_DIM)
            q = zr[:, hs]
            k = zr[:, WIDTH + h * HEAD_DIM:WIDTH + (h + 1) * HEAD_DIM] * scale
            v = zr[:, 2 * WIDTH + h * HEAD_DIM:2 * WIDTH + (h + 1) * HEAD_DIM]
            o = zr[:, 3 * WIDTH + h * HEAD_DIM:3 * WIDTH + (h + 1) * HEAD_DIM]
            ig = gates[:, h:h + 1]
            lf = _log_sigmoid(gates[:, HEADS + h:HEADS + h + 1])
            c0 = c_ref[i, h]
            n0 = n_ref[i, h]
            m0 = m_ref[i:i + 1, h:h + 1]
            a = m0 + lf
            m = jnp.maximum(a, ig)
            s = jnp.sum(q * k, axis=-1, keepdims=True) * jnp.exp(ig - m)
            inter = jnp.exp(a - m)
            cq_col = jnp.sum(c0 * q, axis=-1, keepdims=True)
            cq = jnp.sum(eye * cq_col, axis=0, keepdims=True)
            v_col = jnp.sum(eye * v, axis=-1, keepdims=True)
            num = s * v + inter * cq
            den = s + inter * jnp.sum(n0 * q, axis=-1, keepdims=True)
            hh = num / jnp.maximum(jnp.abs(den), jnp.exp(-m))
            w = jnp.exp(ig - m)
            c1_ref[i, h] = inter * c0 + (w * v_col) * k
            n1_ref[i, h] = inter * n0 + w * k
            m_new = jnp.where(lane == h, m, m_new)
            cat_ref[i:i + 1, hs] = jax.nn.sigmoid(o) * hh
        m1_ref[i:i + 1, :] = m_new
        vn = _rms(zr[:, COL_GV:COL_GV + WIDTH], sg_ref[...])
        vn_ref[i:i + 1, :] = vn
        cat_ref[i:i + 1, WIDTH:2 * WIDTH] = zr[:, COL_U:COL_U + WIDTH] * (w00_ref[...] * vn + b0_ref[...])


def _mixer_ab_step(z, b_if, sgu_g, w00, b0, st_c, st_n, st_m, *, nb):
    n = z.shape[0]
    rows = lambda w: pl.BlockSpec((nb, w), lambda i: (i, 0))
    c_spec = pl.BlockSpec((nb, HEADS, HEAD_DIM, HEAD_DIM), lambda i: (i, 0, 0, 0))
    n_spec = pl.BlockSpec((nb, HEADS, 1, HEAD_DIM), lambda i: (i, 0, 0, 0))
    return pl.pallas_call(
        _mixer_ab_step_body,
        grid=(n // nb,),
        in_specs=[rows(A_IN_PAD), _resident((1, LANES)), _resident((1, WIDTH)), _resident((1, WIDTH)),
                  _resident((1, WIDTH)), c_spec, n_spec, rows(LANES)],
        out_specs=[rows(2 * WIDTH), c_spec, n_spec, rows(LANES), rows(WIDTH)],
        out_shape=[jax.ShapeDtypeStruct((n, 2 * WIDTH), F32),
                   jax.ShapeDtypeStruct(st_c.shape, F32),
                   jax.ShapeDtypeStruct(st_n.shape, F32),
                   jax.ShapeDtypeStruct((n, LANES), F32),
                   jax.ShapeDtypeStruct((n, WIDTH), F32)],
        compiler_params=_params(("arbitrary",)),
        name="mixer_ab_step",
    )(z, b_if, sgu_g.reshape(1, WIDTH), w00, b0, st_c, st_n, st_m)


def _swa_step_body(z_ref, kv0_ref, kv1_ref, kv2_ref, cat_ref):
    nb = z_ref.shape[0]
    scale = HEAD_DIM ** -0.5
    steps = (CHUNK - lax.broadcasted_iota(jnp.int32, (CHUNK, 1, 1), 0)).astype(F32)
    head = lax.broadcasted_iota(jnp.int32, (1, HEADS, 1), 1)
    for i in range(nb):
        ms, ls, os_ = [], [], []
        for gi, kv_ref in enumerate((kv0_ref, kv1_ref, kv2_ref)):
            dil = SWA_GROUPS[gi][1]
            base = gi * 3 * HEADS
            q = z_ref[i, base:base + HEADS, :]
            k_new = z_ref[i, base + HEADS:base + 2 * HEADS, :]
            v_new = z_ref[i, base + 2 * HEADS:base + 3 * HEADS, :]
            kc = kv_ref[i, :, 0, 0, :, :]
            vc = kv_ref[i, :, 0, 1, :, :]
            slope = jnp.zeros((1, HEADS, 1), F32)
            for h in range(HEADS):
                slope = jnp.where(head == h, _alibi_slope(gi, h) * dil, slope)
            s = jnp.sum(kc * q[None], axis=-1, keepdims=True) * scale - slope * steps
            s_new = jnp.sum(k_new * q, axis=-1, keepdims=True) * scale
            m = jnp.maximum(jnp.max(s, axis=0), s_new)
            p = jnp.exp(s - m[None])
            p_new = jnp.exp(s_new - m)
            ms.append(m)
            ls.append(jnp.sum(p, axis=0) + p_new)
            os_.append(jnp.sum(p * vc, axis=0) + p_new * v_new)
        top = functools.reduce(jnp.maximum, ms)
        ws = [jnp.exp(m - top) for m in ms]
        num = sum(w * o for w, o in zip(ws, os_))
        den = sum(w * l for w, l in zip(ws, ls))
        cat_ref[i] = num / den


def _swa_step(z, caches, *, nb):
    n = z.shape[0]
    views = []
    specs = []
    for (win, dil), cache in zip(SWA_GROUPS, caches):
        views.append(cache.reshape(n, win // dil, dil, 2, HEADS, HEAD_DIM))
        specs.append(pl.BlockSpec((nb, CHUNK, 1, 2, HEADS, HEAD_DIM), lambda i: (i, 0, 0, 0, 0, 0)))
    return pl.pallas_call(
        _swa_step_body,
        grid=(n // nb,),
        in_specs=[pl.BlockSpec((nb,) + z.shape[1:], lambda i: (i, 0, 0))] + specs,
        out_specs=pl.BlockSpec((nb, HEADS, HEAD_DIM), lambda i: (i, 0, 0)),
        out_shape=jax.ShapeDtypeStruct((n, HEADS, HEAD_DIM), F32),
        compiler_params=_params(("arbitrary",)),
        name="swa_step",
    )(z, *views)


def _pad_cols(w, n):
    return jnp.pad(w, ((0, 0), (0, n - w.shape[1])))


def kernel(x_prompt, x_sample, state_mlstm_C, state_mlstm_n, state_mlstm_m, cache_swa_kv0, cache_swa_kv1, cache_swa_kv2, norm_g, ffn_w_gate, ffn_w_up, ffn_w_down, a_w_in, a_b_if, sgu_norm_g, sgu_w, sgu_b, a_w_out, c_w_in, c_w_out, final_norm_g):
    nb, s, _ = x_prompt.shape
    ns = x_sample.shape[0]
    assert x_sample.shape[1] == 1 and s % max(w for w, _ in SWA_GROUPS) == 0
    for (win, dil), cache in zip(SWA_GROUPS, (cache_swa_kv0, cache_swa_kv1, cache_swa_kv2)):
        assert cache.shape[2] == win and win // dil == CHUNK

    ffn_w = (ffn_w_gate.astype(BF16), ffn_w_up.astype(BF16), ffn_w_down.astype(BF16))
    g_lo, g_hi = 4 * WIDTH, 4 * WIDTH + 2 * HEADS
    a_in = jnp.concatenate([a_w_in[0][:, :g_lo], a_w_in[0][:, g_hi:], _pad_cols(a_w_in[0][:, g_lo:g_hi], LANES)],
                           axis=1).astype(BF16)
    b_if = _pad_cols(a_b_if[0].reshape(1, 2 * HEADS), LANES)
    a_out = a_w_out[0].astype(BF16)
    c_in = c_w_in[0].astype(BF16)
    c_out = c_w_out[0].astype(BF16)
    sgu_bt = _pad_cols(sgu_b[0].T, LANES)
    sgu_w00 = jnp.repeat(sgu_w[0, :, 0, 0], CHUNK).reshape(1, WIDTH)
    sgu_b0 = jnp.repeat(sgu_b[0, :, 0], CHUNK).reshape(1, WIDTH)

    xp = x_prompt.reshape(nb * s, D_MODEL)
    xs = x_sample.reshape(ns, D_MODEL)
    xp, xs = _ffn(xp, xs, norm_g[0, 0], ffn_w, 0, 0, tm=512)
    xp, p_c, p_n, p_m = _mixer_ab_prompt(xp.reshape(nb, s, D_MODEL), norm_g[0, 1], a_in, b_if,
                                          sgu_norm_g[0], sgu_w[0], sgu_bt, a_out)
    z = _proj(xs, norm_g[0, 1], a_in)
    cat, s_c, s_n, s_m, s_v = _mixer_ab_step(
        z, b_if, sgu_norm_g[0], sgu_w00, sgu_b0, state_mlstm_C[0],
        state_mlstm_n[0].reshape(ns, HEADS, 1, HEAD_DIM), _pad_cols(state_mlstm_m[0], LANES), nb=8)
    xs = _out_proj(cat, a_out, xs)
    xp, xs = _ffn(xp.reshape(nb * s, D_MODEL), xs, norm_g[0, 2], ffn_w, 0, 1, tm=512)
    xp, xs = _ffn(xp, xs, norm_g[1, 0], ffn_w, 1, 0, tm=512)
    xp3 = xp.reshape(nb, s, D_MODEL)
    hn1, hn2 = _rms_deint(xp3, norm_g[1, 1], tm=1024)
    outs, stats = [], []
    for gi, (xin, tm, rb) in enumerate(((xp3.reshape(nb, 1, s, D_MODEL), 512, 1), (hn1, 512, 1), (hn2, 256, 2))):
        o, st = _swa_group_prompt(xin, norm_g[1, 1], c_in, group=gi, tm=tm, rb=rb)
        outs.append(o)
        stats.append(st)
    p_kv = [_kv_tail(xp3, norm_g[1, 1], c_in, group=gi) for gi in range(3)]
    xp = _swa_merge(outs, stats, xp3, c_out, tm=1024).reshape(nb * s, D_MODEL)
    z = _proj(xs, norm_g[1, 1], c_in)
    cat = _swa_step(z.reshape(ns, 3 * 3 * HEADS, HEAD_DIM), (cache_swa_kv0, cache_swa_kv1, cache_swa_kv2), nb=4)
    xs = _out_proj(cat.reshape(ns, WIDTH), c_out, xs)
    y_prompt, y_sample = _ffn(xp, xs, norm_g[1, 2], ffn_w, 1, 1, final_norm_g, tm=512)
    y_prompt = y_prompt.reshape(nb, s, D_MODEL)
    y_sample = y_sample.reshape(ns, 1, D_MODEL)
    s_kv = [z[:, gi * C_GROUP_COLS + WIDTH:(gi + 1) * C_GROUP_COLS].reshape(1, ns, 1, 2, HEADS, HEAD_DIM)
            for gi in range(3)]

    return (y_prompt, y_sample,
            p_c.reshape(1, nb, HEADS, HEAD_DIM, HEAD_DIM), p_n.reshape(1, nb, HEADS, HEAD_DIM),
            p_m[:, :, 0, 0].reshape(1, nb, HEADS),
            s_c.reshape(1, ns, HEADS, HEAD_DIM, HEAD_DIM), s_n.reshape(1, ns, HEADS, HEAD_DIM),
            s_m[:, :HEADS].reshape(1, ns, HEADS), s_v.reshape(1, ns, 1, WIDTH),
            p_kv[0], p_kv[1], p_kv[2], s_kv[0], s_kv[1], s_kv[2])
```

```python
import functools

import jax
import jax.numpy as jnp
from jax import lax
from jax.experimental import pallas as pl
from jax.experimental.pallas import tpu as pltpu

F32 = jnp.float32
BF16 = jnp.bfloat16

D_MODEL = 1024
D_FF = 2752
HEADS = 4
HEAD_DIM = 128
WIDTH = HEADS * HEAD_DIM
CHUNK = 128
SWA_GROUPS = ((128, 1), (512, 4), (2048, 16))
NORM_EPS = 1e-6
NEG_INF = -1e30

LANES = 128
BF16_ROWS = 16
PROJ_SLAB = 256
FF_CHUNK = 256
A_IN_PAD = 4 * WIDTH + 2 * WIDTH + LANES
COL_U = 4 * WIDTH
COL_GV = 5 * WIDTH
COL_GATES = 6 * WIDTH
C_GROUP_COLS = 3 * WIDTH

VMEM_LIMIT = 56 * 1024 * 1024


def _params(semantics):
    return pltpu.CompilerParams(dimension_semantics=semantics, vmem_limit_bytes=VMEM_LIMIT)


def _resident(shape):
    nd = len(shape)
    return pl.BlockSpec(shape, lambda *_: (0,) * nd, pipeline_mode=pl.Buffered(1))


def _rms(x, g):
    ms = jnp.mean(x * x, axis=-1, keepdims=True)
    return x * lax.rsqrt(ms + NORM_EPS) * g


def _dot(a, b):
    return jnp.dot(a, b, preferred_element_type=F32)


def _dot_nt(a, b):
    return lax.dot_general(a, b, (((1,), (1,)), ((), ())), preferred_element_type=F32)


def _log_sigmoid(x):
    return jnp.minimum(x, 0.0) - jnp.log1p(jnp.exp(-jnp.abs(x)))


def _swiglu_rows(h_scr, rows, wg_ref, wu_ref, wd_ref):
    acc = None
    for c0 in range(0, D_FF, FF_CHUNK):
        cols = slice(c0, min(c0 + FF_CHUNK, D_FF))
        h = h_scr[0:rows, :]
        gate = _dot_nt(h, wg_ref[0, 0, cols, :])
        up = _dot_nt(h, wu_ref[0, 0, cols, :])
        act = (gate * jax.nn.sigmoid(gate) * up).astype(h_scr.dtype)
        part = _dot(act, wd_ref[0, 0, cols, :])
        acc = part if acc is None else acc + part
    return acc


def _ffn_body(*refs, final):
    if final:
        x_ref, xs_ref, g_ref, wg_ref, wu_ref, wd_ref, fg_ref, o_ref, os_ref, h_scr = refs
    else:
        x_ref, xs_ref, g_ref, wg_ref, wu_ref, wd_ref, o_ref, os_ref, h_scr = refs
    tm, ns = x_ref.shape[0], xs_ref.shape[0]
    last = pl.num_programs(0) - 1

    def finish(x, acc):
        y = x + 0.5 * acc
        return _rms(y, fg_ref[...]) if final else y

    x = x_ref[...]
    h_scr[0:tm, :] = _rms(x, g_ref[...]).astype(h_scr.dtype)

    @pl.when(pl.program_id(0) != last)
    def _():
        o_ref[...] = finish(x, _swiglu_rows(h_scr, tm, wg_ref, wu_ref, wd_ref))

    @pl.when(pl.program_id(0) == last)
    def _():
        xs = xs_ref[...]
        h_scr[tm:tm + ns, :] = _rms(xs, g_ref[...]).astype(h_scr.dtype)
        acc = _swiglu_rows(h_scr, tm + ns, wg_ref, wu_ref, wd_ref)
        o_ref[...] = finish(x, acc[0:tm])
        os_ref[...] = finish(xs, acc[tm:tm + ns])


def _ffn(x, xs, g, weights, layer, which, final_g=None, *, tm):
    m, ns = x.shape[0], xs.shape[0]
    final = final_g is not None
    wg, wu, wd = weights
    row = pl.BlockSpec((tm, D_MODEL), lambda i: (i, 0))
    wspec = lambda w: pl.BlockSpec((1, 1) + w.shape[2:], lambda i: (layer, which, 0, 0), pipeline_mode=pl.Buffered(1))
    in_specs = [row, _resident(xs.shape), _resident((1, D_MODEL)), wspec(wg), wspec(wu), wspec(wd)]
    args = [x, xs, g.reshape(1, D_MODEL), wg, wu, wd]
    if final:
        in_specs.append(_resident((1, D_MODEL)))
        args.append(final_g.reshape(1, D_MODEL))
    return pl.pallas_call(
        functools.partial(_ffn_body, final=final),
        grid=(m // tm,),
        in_specs=in_specs,
        out_specs=[row, pl.BlockSpec(xs.shape, lambda i: (0, 0))],
        out_shape=[jax.ShapeDtypeStruct((m, D_MODEL), F32), jax.ShapeDtypeStruct(xs.shape, F32)],
        scratch_shapes=[pltpu.VMEM((tm + ns, D_MODEL), wg.dtype)],
        compiler_params=_params(("arbitrary",)),
        name="ffn_final" if final else "ffn",
    )(*args)


def _cummax_lanes(x):
    lane = lax.broadcasted_iota(jnp.int32, x.shape, 1)
    d = 1
    while d < x.shape[1]:
        x = jnp.maximum(x, jnp.where(lane >= d, pltpu.roll(x, d, axis=1), NEG_INF))
        d *= 2
    return x


def _exact_tri_dot(tri_bf16, x):
    x1 = x.astype(BF16)
    r1 = x - x1.astype(F32)
    x2 = r1.astype(BF16)
    x3 = (r1 - x2.astype(F32)).astype(BF16)
    n = x.shape[1]
    r = _dot(tri_bf16, jnp.concatenate([x1, x2, x3], axis=1))
    return r[:, 0:n] + r[:, n:2 * n] + r[:, 2 * n:3 * n]


def _mixer_ab_chunk(xn_ref, xp_ref, g_ref, win_ref, bif_ref, sg_ref, sw_ref, sbt_ref, wout_ref,
                   y_ref, c_ref, n_ref, m_ref, z_cur, z_nxt, cat_scr):
    nb = xn_ref.shape[0]
    row = lax.broadcasted_iota(jnp.int32, (CHUNK, CHUNK), 0)
    col = lax.broadcasted_iota(jnp.int32, (CHUNK, CHUNK), 1)
    causal = col <= row
    keys_before = row <= col
    tri = jnp.where(causal, 1.0, 0.0).astype(BF16)
    scale = HEAD_DIM ** -0.5

    hn = _rms(xn_ref[...].reshape(nb * CHUNK, D_MODEL), g_ref[...]).astype(BF16)
    slabs = [(c0, min(c0 + PROJ_SLAB, A_IN_PAD)) for c0 in range(0, A_IN_PAD, PROJ_SLAB)]

    def project(count):
        for _ in range(min(count, len(slabs))):
            c0, c1 = slabs.pop(0)
            z_nxt[:, c0:c1] = _dot(hn, win_ref[:, c0:c1])

    sgu_bias = [jnp.broadcast_to(sbt_ref[:, g:g + 1], (CHUNK, CHUNK)) for g in range(HEADS)]
    vn = [_rms(z_cur[b * CHUNK:(b + 1) * CHUNK, COL_GV:COL_GV + WIDTH], sg_ref[...]) for b in range(nb)]
    for g in range(HEADS):
        gs = slice(g * CHUNK, (g + 1) * CHUNK)
        mixed = _dot(jnp.where(causal, sw_ref[g], 0.0).astype(BF16),
                     jnp.concatenate([vn[b][:, gs] for b in range(nb)], axis=1).astype(BF16))
        for b in range(nb):
            rows = slice(b * CHUNK, (b + 1) * CHUNK)
            u = z_cur[rows, COL_U + g * CHUNK:COL_U + (g + 1) * CHUNK]
            cat_scr[rows, WIDTH + g * CHUNK:WIDTH + (g + 1) * CHUNK] = (
                u * (mixed[:, b * CHUNK:(b + 1) * CHUNK] + sgu_bias[g])).astype(BF16)
    project(1)

    pieces_per_stage = -(-(len(slabs)) // (3 * nb))
    gate_terms = []
    for b in range(nb):
        rows = slice(b * CHUNK, (b + 1) * CHUNK)
        gates = z_cur[rows, COL_GATES:COL_GATES + LANES] + bif_ref[...]
        lg = jnp.where(col < HEADS, gates, _log_sigmoid(gates))
        gate_terms.append((lg, _exact_tri_dot(tri, lg)))
        project(pieces_per_stage)

    heads = []
    for b, (lg, fcum) in enumerate(gate_terms):
        rows = slice(b * CHUNK, (b + 1) * CHUNK)
        lg_t = lg.T
        fcum_t = fcum.T
        gmax = _cummax_lanes(lg_t[0:2 * HEADS, :] - jnp.concatenate([fcum_t[HEADS:2 * HEADS, :]] * 2, axis=0))
        for h in range(HEADS):
            q = z_cur[rows, h * HEAD_DIM:(h + 1) * HEAD_DIM]
            k = z_cur[rows, WIDTH + h * HEAD_DIM:WIDTH + (h + 1) * HEAD_DIM] * scale
            v_t = z_cur[rows, 2 * WIDTH + h * HEAD_DIM:2 * WIDTH + (h + 1) * HEAD_DIM].T
            f_r = fcum_t[HEADS + h:HEADS + h + 1, :]
            i_r = lg_t[h:h + 1, :]
            g_c = lg[:, h:h + 1] - fcum[:, HEADS + h:HEADS + h + 1]
            c0, n0, m0 = c_ref[b, h], n_ref[b, h], m_ref[b, h]
            m = f_r + jnp.maximum(m0, gmax[h:h + 1, :])
            inter = jnp.exp(m0 + f_r - m)
            qb, kb = q.astype(BF16), k.astype(BF16)
            against_q = _dot_nt(
                jnp.concatenate([k, c0, jnp.broadcast_to(n0, (BF16_ROWS, HEAD_DIM))], axis=0).astype(BF16), qb)
            m_last = m[:, CHUNK - 1:CHUNK]
            f_last = f_r[:, CHUNK - 1:CHUNK]
            w = jnp.exp(f_last - f_r + i_r - m_last)
            decay = jnp.exp(m0[:, 0:1] + f_last - m_last)
            against_k = _dot(
                jnp.concatenate([v_t * w, jnp.broadcast_to(w, (BF16_ROWS, CHUNK))], axis=0).astype(BF16), kb)
            c_ref[b, h] = decay * c0 + against_k[0:HEAD_DIM]
            n_ref[b, h] = decay * n0 + against_k[HEAD_DIM:HEAD_DIM + 1]
            m_ref[b, h] = jnp.broadcast_to(m_last, (1, LANES))
            heads.append((against_q, v_t, f_r, g_c, m, inter))
        project(pieces_per_stage)

    partial = []
    for i, (against_q, v_t, f_r, g_c, m, inter) in enumerate(heads):
        kq = against_q[0:CHUNK]
        cq = against_q[CHUNK:CHUNK + HEAD_DIM]
        nq = against_q[CHUNK + HEAD_DIM:CHUNK + HEAD_DIM + 1]
        s_t = kq * jnp.exp(jnp.where(keys_before, (f_r - m) + g_c, NEG_INF))
        den = jnp.sum(s_t, axis=0, keepdims=True) + inter * nq
        partial.append((_dot(v_t.astype(BF16), s_t.astype(BF16)), inter * cq,
                        jnp.maximum(jnp.abs(den), jnp.exp(-m))))
        if i % HEADS == HEADS - 1:
            project(pieces_per_stage)

    for i, (sv, carried, den) in enumerate(partial):
        b, h = divmod(i, HEADS)
        rows = slice(b * CHUNK, (b + 1) * CHUNK)
        o = z_cur[rows, 3 * WIDTH + h * HEAD_DIM:3 * WIDTH + (h + 1) * HEAD_DIM]
        cat_scr[rows, h * HEAD_DIM:(h + 1) * HEAD_DIM] = (jax.nn.sigmoid(o) * ((sv + carried) / den).T).astype(BF16)

    project(len(slabs))
    y = xp_ref[...].reshape(nb * CHUNK, D_MODEL) + _dot(cat_scr[...], wout_ref[...])
    y_ref[...] = y.reshape(nb, CHUNK, D_MODEL)


def _mixer_ab_body(*refs):
    *io_refs, z0_scr, z1_scr, cat_scr = refs
    c_ref, n_ref, m_ref = io_refs[-3:]
    step = pl.program_id(0)

    @pl.when(step <= 1)
    def _():
        c_ref[...] = jnp.zeros_like(c_ref)
        n_ref[...] = jnp.zeros_like(n_ref)
        m_ref[...] = jnp.zeros_like(m_ref)

    @pl.when(step == 0)
    def _():
        z1_scr[...] = jnp.zeros_like(z1_scr)

    @pl.when(step % 2 == 0)
    def _():
        _mixer_ab_chunk(*io_refs, z1_scr, z0_scr, cat_scr)

    @pl.when(step % 2 == 1)
    def _():
        _mixer_ab_chunk(*io_refs, z0_scr, z1_scr, cat_scr)


def _mixer_ab_prompt(x, g, w_in, b_if, sgu_g, sgu_w, sgu_bt, w_out):
    nb, s, _ = x.shape
    n_chunks = s // CHUNK
    blk = lambda index: pl.BlockSpec((nb, CHUNK, D_MODEL), index)
    nxt = blk(lambda c: (0, jnp.minimum(c, n_chunks - 1), 0))
    prev = blk(lambda c: (0, jnp.maximum(c - 1, 0), 0))
    z_shape = pltpu.VMEM((nb * CHUNK, A_IN_PAD), F32)
    return pl.pallas_call(
        _mixer_ab_body,
        grid=(n_chunks + 1,),
        in_specs=[nxt, prev, _resident((1, D_MODEL)), _resident(w_in.shape), _resident((1, LANES)),
                  _resident((1, WIDTH)), _resident(sgu_w.shape), _resident(sgu_bt.shape), _resident(w_out.shape)],
        out_specs=[prev,
                   pl.BlockSpec((nb, HEADS, HEAD_DIM, HEAD_DIM), lambda c: (0, 0, 0, 0)),
                   pl.BlockSpec((nb, HEADS, 1, HEAD_DIM), lambda c: (0, 0, 0, 0)),
                   pl.BlockSpec((nb, HEADS, 1, LANES), lambda c: (0, 0, 0, 0))],
        out_shape=[jax.ShapeDtypeStruct(x.shape, F32),
                   jax.ShapeDtypeStruct((nb, HEADS, HEAD_DIM, HEAD_DIM), F32),
                   jax.ShapeDtypeStruct((nb, HEADS, 1, HEAD_DIM), F32),
                   jax.ShapeDtypeStruct((nb, HEADS, 1, LANES), F32)],
        scratch_shapes=[z_shape, z_shape, pltpu.VMEM((nb * CHUNK, 2 * WIDTH), BF16)],
        compiler_params=_params(("arbitrary",)),
        name="mixer_ab_prompt",
    )(x, x, g.reshape(1, D_MODEL), w_in, b_if, sgu_g.reshape(1, WIDTH), sgu_w, sgu_bt, w_out)


def _alibi_slope(group, head):
    n = len(SWA_GROUPS) * HEADS
    return 2.0 ** (-8.0 * (group * HEADS + head + 1) / n)


def _rms_deint_body(x_ref, g_ref, h1_ref, h2_ref, slab_scr):
    t = x_ref.shape[1]
    rows_per_pass = 256
    for c in range(t // rows_per_pass):
        rows = slice(c * rows_per_pass, (c + 1) * rows_per_pass)
        hn = _rms(x_ref[0, rows, :], g_ref[...])
        for sl in range(D_MODEL // LANES):
            slab_scr[sl, rows, :] = hn[:, sl * LANES:(sl + 1) * LANES]
    for (_, dil), out_ref in zip(SWA_GROUPS[1:], (h1_ref, h2_ref)):
        for r in range(dil):
            for sl in range(D_MODEL // LANES):
                piece = slab_scr[sl, pl.ds(r, t // dil, stride=dil), :]
                out_ref[0, r, :, sl * LANES:(sl + 1) * LANES] = piece.astype(BF16)


def _rms_deint(x, g, *, tm):
    nb, s, _ = x.shape
    d1, d2 = SWA_GROUPS[1][1], SWA_GROUPS[2][1]
    return pl.pallas_call(
        _rms_deint_body,
        grid=(nb, s // tm),
        in_specs=[pl.BlockSpec((1, tm, D_MODEL), lambda b, t: (b, t, 0)), _resident((1, D_MODEL))],
        out_specs=[pl.BlockSpec((1, d1, tm // d1, D_MODEL), lambda b, t: (b, 0, t, 0)),
                   pl.BlockSpec((1, d2, tm // d2, D_MODEL), lambda b, t: (b, 0, t, 0))],
        out_shape=[jax.ShapeDtypeStruct((nb, d1, s // d1, D_MODEL), BF16),
                   jax.ShapeDtypeStruct((nb, d2, s // d2, D_MODEL), BF16)],
        scratch_shapes=[pltpu.VMEM((D_MODEL // LANES, tm, LANES), F32)],
        compiler_params=_params(("arbitrary", "arbitrary")),
        name="rms_deint",
    )(x, g.reshape(1, D_MODEL))


def _swa_group_body(*refs, group, dil, tm, rb, normed):
    if normed:
        x_ref, w_ref, o_ref, st_ref, q_scr, k_scr, v_scr = refs
    else:
        x_ref, g_ref, w_ref, o_ref, st_ref, q_scr, k_scr, v_scr = refs
    it = pl.program_id(2)
    scale = HEAD_DIM ** -0.5
    qi = lax.broadcasted_iota(jnp.int32, (CHUNK, 2 * CHUNK), 0)
    kc = lax.broadcasted_iota(jnp.int32, (CHUNK, 2 * CHUNK), 1)
    delta = CHUNK + qi - kc
    valid = (delta >= 0) & (delta <= CHUNK)
    dist = (delta * dil).astype(F32)
    lane = lax.broadcasted_iota(jnp.int32, (CHUNK, LANES), 1)

    @pl.when(it == 0)
    def _():
        k_scr[:, 0:CHUNK, :] = jnp.zeros((rb, CHUNK, WIDTH), BF16)
        v_scr[:, 0:CHUNK, :] = jnp.zeros((rb, CHUNK, WIDTH), BF16)

    for r in range(rb):
        hn = x_ref[0, r] if normed else _rms(x_ref[0, r], g_ref[...]).astype(BF16)
        q_scr[...] = _dot(hn, w_ref[:, 0:WIDTH]).astype(BF16)
        k_scr[r, CHUNK:, :] = _dot(hn, w_ref[:, WIDTH:2 * WIDTH]).astype(BF16)
        v_scr[r, CHUNK:, :] = _dot(hn, w_ref[:, 2 * WIDTH:3 * WIDTH]).astype(BF16)
        for j in range(tm // CHUNK):
            rows = slice(j * CHUNK, (j + 1) * CHUNK)
            mask = valid if j > 0 else valid & (kc >= jnp.where(it > 0, 0, CHUNK))
            stats = jnp.zeros((CHUNK, LANES), F32)
            for h in range(HEADS):
                hs = slice(h * HEAD_DIM, (h + 1) * HEAD_DIM)
                qj = q_scr[rows, hs]
                kk = k_scr[r, j * CHUNK:(j + 2) * CHUNK, hs]
                vv = v_scr[r, j * CHUNK:(j + 2) * CHUNK, hs]
                s = _dot_nt(qj, kk) * scale + (-_alibi_slope(group, h)) * dist
                s = jnp.where(mask, s, NEG_INF)
                m = jnp.max(s, axis=-1, keepdims=True)
                p = jnp.exp(s - m)
                l = jnp.sum(p, axis=-1, keepdims=True)
                o_ref[0, r, h, rows, :] = _dot(p.astype(BF16), vv)
                stats = jnp.where(lane == h, m, jnp.where(lane == HEADS + h, l, stats))
            st_ref[0, r, rows, :] = stats
        k_scr[r, 0:CHUNK, :] = k_scr[r, tm:tm + CHUNK, :]
        v_scr[r, 0:CHUNK, :] = v_scr[r, tm:tm + CHUNK, :]


def _swa_group_prompt(x, g, c_in, *, group, tm, rb):
    nb, dil, sub, _ = x.shape
    normed = x.dtype == BF16
    in_specs = [pl.BlockSpec((1, rb, tm, D_MODEL), lambda b, r, t: (b, r, t, 0))]
    args = [x]
    if not normed:
        in_specs.append(_resident((1, D_MODEL)))
        args.append(g.reshape(1, D_MODEL))
    in_specs.append(pl.BlockSpec((D_MODEL, C_GROUP_COLS), lambda b, r, t: (0, group), pipeline_mode=pl.Buffered(1)))
    args.append(c_in)
    return pl.pallas_call(
        functools.partial(_swa_group_body, group=group, dil=dil, tm=tm, rb=rb, normed=normed),
        grid=(nb, dil // rb, sub // tm),
        in_specs=in_specs,
        out_specs=[pl.BlockSpec((1, rb, HEADS, tm, HEAD_DIM), lambda b, r, t: (b, r, 0, t, 0)),
                   pl.BlockSpec((1, rb, tm, LANES), lambda b, r, t: (b, r, t, 0))],
        out_shape=[jax.ShapeDtypeStruct((nb, dil, HEADS, sub, HEAD_DIM), F32),
                   jax.ShapeDtypeStruct((nb, dil, sub, LANES), F32)],
        scratch_shapes=[pltpu.VMEM((tm, WIDTH), BF16),
                        pltpu.VMEM((rb, CHUNK + tm, WIDTH), BF16),
                        pltpu.VMEM((rb, CHUNK + tm, WIDTH), BF16)],
        compiler_params=_params(("arbitrary", "arbitrary", "arbitrary")),
        name="swa_group%d" % group,
    )(*args)


def _kv_tail_body(x_ref, g_ref, wk_ref, wv_ref, kv_ref):
    hn = _rms(x_ref[0], g_ref[...]).astype(BF16)
    for j, w_ref in enumerate((wk_ref, wv_ref)):
        kv = _dot(hn, w_ref[...])
        for h in range(HEADS):
            kv_ref[0, 0, :, j, h, :] = kv[:, h * HEAD_DIM:(h + 1) * HEAD_DIM]


def _kv_tail(x, g, c_in, *, group):
    win = SWA_GROUPS[group][0]
    nb, s, _ = x.shape
    tm = min(win, 512)
    first = (s - win) // tm
    wcol = lambda j: pl.BlockSpec((D_MODEL, WIDTH), lambda b, t: (0, 3 * group + j), pipeline_mode=pl.Buffered(1))
    return pl.pallas_call(
        _kv_tail_body,
        grid=(nb, win // tm),
        in_specs=[pl.BlockSpec((1, tm, D_MODEL), lambda b, t: (b, first + t, 0)), _resident((1, D_MODEL)),
                  wcol(1), wcol(2)],
        out_specs=pl.BlockSpec((1, 1, tm, 2, HEADS, HEAD_DIM), lambda b, t: (0, b, t, 0, 0, 0)),
        out_shape=jax.ShapeDtypeStruct((1, nb, win, 2, HEADS, HEAD_DIM), F32),
        compiler_params=_params(("arbitrary", "arbitrary")),
        name="kv_tail%d" % group,
    )(x, g.reshape(1, D_MODEL), c_in, c_in)


def _swa_merge_body(o0_ref, o1_ref, o2_ref, s0_ref, s1_ref, s2_ref, x_ref, w_ref, y_ref,
                    n1_scr, n2_scr, t1_scr, t2_scr, cat_scr):
    t = x_ref.shape[1]
    for (_, dil), o_ref, s_ref, n_scr, t_scr in ((SWA_GROUPS[1], o1_ref, s1_ref, n1_scr, t1_scr),
                                                  (SWA_GROUPS[2], o2_ref, s2_ref, n2_scr, t2_scr)):
        for r in range(dil):
            t_scr[pl.ds(r, t // dil, stride=dil), :] = s_ref[0, r]
            for h in range(HEADS):
                n_scr[h, pl.ds(r, t // dil, stride=dil), :] = o_ref[0, r, h]
    rows_per_pass = 256
    for c in range(t // rows_per_pass):
        rows = slice(c * rows_per_pass, (c + 1) * rows_per_pass)
        stats = (s0_ref[0, 0, rows, :], t1_scr[rows, :], t2_scr[rows, :])
        for h in range(HEADS):
            outs = (o0_ref[0, 0, h, rows, :], n1_scr[h, rows, :], n2_scr[h, rows, :])
            ms = [st[:, h:h + 1] for st in stats]
            ls = [st[:, HEADS + h:HEADS + h + 1] for st in stats]
            top = functools.reduce(jnp.maximum, ms)
            ws = [jnp.exp(m - top) for m in ms]
            num = sum(w * o for w, o in zip(ws, outs))
            den = sum(w * l for w, l in zip(ws, ls))
            cat_scr[rows, h * HEAD_DIM:(h + 1) * HEAD_DIM] = (num / den).astype(BF16)
    y_ref[0] = x_ref[0] + _dot(cat_scr[...], w_ref[...])


def _swa_merge(outs, stats, x, w_out, *, tm):
    nb, s, _ = x.shape
    in_specs = []
    for (_, dil) in SWA_GROUPS:
        in_specs.append(pl.BlockSpec((1, dil, HEADS, tm // dil, HEAD_DIM), lambda b, t: (b, 0, 0, t, 0)))
    for (_, dil) in SWA_GROUPS:
        in_specs.append(pl.BlockSpec((1, dil, tm // dil, LANES), lambda b, t: (b, 0, t, 0)))
    x_spec = pl.BlockSpec((1, tm, D_MODEL), lambda b, t: (b, t, 0))
    return pl.pallas_call(
        _swa_merge_body,
        grid=(nb, s // tm),
        in_specs=in_specs + [x_spec, _resident(w_out.shape)],
        out_specs=x_spec,
        out_shape=jax.ShapeDtypeStruct(x.shape, F32),
        scratch_shapes=[pltpu.VMEM((HEADS, tm, HEAD_DIM), F32), pltpu.VMEM((HEADS, tm, HEAD_DIM), F32),
                        pltpu.VMEM((tm, LANES), F32), pltpu.VMEM((tm, LANES), F32),
                        pltpu.VMEM((tm, WIDTH), BF16)],
        compiler_params=_params(("arbitrary", "arbitrary")),
        name="swa_merge",
    )(*outs, *stats, x, w_out)


def _proj_body(x_ref, g_ref, w_ref, z_ref):
    z_ref[...] = _dot(_rms(x_ref[...], g_ref[...]).astype(BF16), w_ref[...])


def _proj(x, g, w):
    m, n = x.shape[0], w.shape[1]
    return pl.pallas_call(
        _proj_body,
        grid=(1,),
        in_specs=[_resident(x.shape), _resident((1, D_MODEL)), _resident(w.shape)],
        out_specs=pl.BlockSpec((m, n), lambda i: (0, 0)),
        out_shape=jax.ShapeDtypeStruct((m, n), F32),
        compiler_params=_params(("arbitrary",)),
        name="proj_rows",
    )(x, g.reshape(1, D_MODEL), w)


def _out_proj_body(c_ref, w_ref, x_ref, y_ref):
    y_ref[...] = x_ref[...] + _dot(c_ref[...].astype(BF16), w_ref[...])


def _out_proj(cat, w, x):
    return pl.pallas_call(
        _out_proj_body,
        grid=(1,),
        in_specs=[_resident(cat.shape), _resident(w.shape), _resident(x.shape)],
        out_specs=pl.BlockSpec(x.shape, lambda i: (0, 0)),
        out_shape=jax.ShapeDtypeStruct(x.shape, F32),
        compiler_params=_params(("arbitrary",)),
        name="out_proj_rows",
    )(cat, w, x)


def _mixer_ab_step_body(z_ref, bif_ref, sg_ref, w00_ref, b0_ref, c_ref, n_ref, m_ref,
                        cat_ref, c1_ref, n1_ref, m1_ref, vn_ref):
    nb = z_ref.shape[0]
    scale = HEAD_DIM ** -0.5
    eye = (lax.broadcasted_iota(jnp.int32, (HEAD_DIM, HEAD_DIM), 0)
           == lax.broadcasted_iota(jnp.int32, (HEAD_DIM, HEAD_DIM), 1)).astype(F32)
    lane = lax.broadcasted_iota(jnp.int32, (1, LANES), 1)
    for i in range(nb):
        zr = z_ref[i:i + 1, :]
        gates = zr[:, COL_GATES:COL_GATES + LANES] + bif_ref[...]
        m_new = jnp.zeros((1, LANES), F32)
        for h in range(HEADS):
            hs = slice(h * HEAD_DIM, (h + 1) * HEAD_DIM)
            q = zr[:, hs]
            k = zr[:, WIDTH + h * HEAD_DIM:WIDTH + (h + 1) * HEAD_DIM] * scale
            v = zr[:, 2 * WIDTH + h * HEAD_DIM:2 * WIDTH + (h + 1) * HEAD_DIM]
            o = zr[:, 3 * WIDTH + h * HEAD_DIM:3 * WIDTH + (h + 1) * HEAD_DIM]
            ig = gates[:, h:h + 1]
            lf = _log_sigmoid(gates[:, HEADS + h:HEADS + h + 1])
            c0 = c_ref[i, h]
            n0 = n_ref[i, h]
            m0 = m_ref[i:i + 1, h:h + 1]
            a = m0 + lf
            m = jnp.maximum(a, ig)
            s = jnp.sum(q * k, axis=-1, keepdims=True) * jnp.exp(ig - m)
            inter = jnp.exp(a - m)
            cq_col = jnp.sum(c0 * q, axis=-1, keepdims=True)
            cq = jnp.sum(eye * cq_col, axis=0, keepdims=True)
            v_col = jnp.sum(eye * v, axis=-1, keepdims=True)
            num = s * v + inter * cq
            den = s + inter * jnp.sum(n0 * q, axis=-1, keepdims=True)
            hh = num / jnp.maximum(jnp.abs(den), jnp.exp(-m))
            w = jnp.exp(ig - m)
            c1_ref[i, h] = inter * c0 + (w * v_col) * k
            n1_ref[i, h] = inter * n0 + w * k
            m_new = jnp.where(lane == h, m, m_new)
            cat_ref[i:i + 1, hs] = jax.nn.sigmoid(o) * hh
        m1_ref[i:i + 1, :] = m_new
        vn = _rms(zr[:, COL_GV:COL_GV + WIDTH], sg_ref[...])
        vn_ref[i:i + 1, :] = vn
        cat_ref[i:i + 1, WIDTH:2 * WIDTH] = zr[:, COL_U:COL_U + WIDTH] * (w00_ref[...] * vn + b0_ref[...])


def _mixer_ab_step(z, b_if, sgu_g, w00, b0, st_c, st_n, st_m, *, nb):
    n = z.shape[0]
    rows = lambda w: pl.BlockSpec((nb, w), lambda i: (i, 0))
    c_spec = pl.BlockSpec((nb, HEADS, HEAD_DIM, HEAD_DIM), lambda i: (i, 0, 0, 0))
    n_spec = pl.BlockSpec((nb, HEADS, 1, HEAD_DIM), lambda i: (i, 0, 0, 0))
    return pl.pallas_call(
        _mixer_ab_step_body,
        grid=(n // nb,),
        in_specs=[rows(A_IN_PAD), _resident((1, LANES)), _resident((1, WIDTH)), _resident((1, WIDTH)),
                  _resident((1, WIDTH)), c_spec, n_spec, rows(LANES)],
        out_specs=[rows(2 * WIDTH), c_spec, n_spec, rows(LANES), rows(WIDTH)],
        out_shape=[jax.ShapeDtypeStruct((n, 2 * WIDTH), F32),
                   jax.ShapeDtypeStruct(st_c.shape, F32),
                   jax.ShapeDtypeStruct(st_n.shape, F32),
                   jax.ShapeDtypeStruct((n, LANES), F32),
                   jax.ShapeDtypeStruct((n, WIDTH), F32)],
        compiler_params=_params(("arbitrary",)),
        name="mixer_ab_step",
    )(z, b_if, sgu_g.reshape(1, WIDTH), w00, b0, st_c, st_n, st_m)


def _swa_step_body(z_ref, kv0_ref, kv1_ref, kv2_ref, cat_ref):
    nb = z_ref.shape[0]
    scale = HEAD_DIM ** -0.5
    steps = (CHUNK - lax.broadcasted_iota(jnp.int32, (CHUNK, 1, 1), 0)).astype(F32)
    head = lax.broadcasted_iota(jnp.int32, (1, HEADS, 1), 1)
    for i in range(nb):
        ms, ls, os_ = [], [], []
        for gi, kv_ref in enumerate((kv0_ref, kv1_ref, kv2_ref)):
            dil = SWA_GROUPS[gi][1]
            base = gi * 3 * HEADS
            q = z_ref[i, base:base + HEADS, :]
            k_new = z_ref[i, base + HEADS:base + 2 * HEADS, :]
            v_new = z_ref[i, base + 2 * HEADS:base + 3 * HEADS, :]
            kc = kv_ref[i, :, 0, 0, :, :]
            vc = kv_ref[i, :, 0, 1, :, :]
            slope = jnp.zeros((1, HEADS, 1), F32)
            for h in range(HEADS):
                slope = jnp.where(head == h, _alibi_slope(gi, h) * dil, slope)
            s = jnp.sum(kc * q[None], axis=-1, keepdims=True) * scale - slope * steps
            s_new = jnp.sum(k_new * q, axis=-1, keepdims=True) * scale
            m = jnp.maximum(jnp.max(s, axis=0), s_new)
            p = jnp.exp(s - m[None])
            p_new = jnp.exp(s_new - m)
            ms.append(m)
            ls.append(jnp.sum(p, axis=0) + p_new)
            os_.append(jnp.sum(p * vc, axis=0) + p_new * v_new)
        top = functools.reduce(jnp.maximum, ms)
        ws = [jnp.exp(m - top) for m in ms]
        num = sum(w * o for w, o in zip(ws, os_))
        den = sum(w * l for w, l in zip(ws, ls))
        cat_ref[i] = num / den


def _swa_step(z, caches, *, nb):
    n = z.shape[0]
    views = []
    specs = []
    for (win, dil), cache in zip(SWA_GROUPS, caches):
        views.append(cache.reshape(n, win // dil, dil, 2, HEADS, HEAD_DIM))
        specs.append(pl.BlockSpec((nb, CHUNK, 1, 2, HEADS, HEAD_DIM), lambda i: (i, 0, 0, 0, 0, 0)))
    return pl.pallas_call(
        _swa_step_body,
        grid=(n // nb,),
        in_specs=[pl.BlockSpec((nb,) + z.shape[1:], lambda i: (i, 0, 0))] + specs,
        out_specs=pl.BlockSpec((nb, HEADS, HEAD_DIM), lambda i: (i, 0, 0)),
        out_shape=jax.ShapeDtypeStruct((n, HEADS, HEAD_DIM), F32),
        compiler_params=_params(("arbitrary",)),
        name="swa_step",
    )(z, *views)


def _pad_cols(w, n):
    return jnp.pad(w, ((0, 0), (0, n - w.shape[1])))


def kernel(x_prompt, x_sample, state_mlstm_C, state_mlstm_n, state_mlstm_m, cache_swa_kv0, cache_swa_kv1, cache_swa_kv2, norm_g, ffn_w_gate, ffn_w_up, ffn_w_down, a_w_in, a_b_if, sgu_norm_g, sgu_w, sgu_b, a_w_out, c_w_in, c_w_out, final_norm_g):
    nb, s, _ = x_prompt.shape
    ns = x_sample.shape[0]
    assert x_sample.shape[1] == 1 and s % max(w for w, _ in SWA_GROUPS) == 0
    for (win, dil), cache in zip(SWA_GROUPS, (cache_swa_kv0, cache_swa_kv1, cache_swa_kv2)):
        assert cache.shape[2] == win and win // dil == CHUNK

    ffn_w = (jnp.swapaxes(ffn_w_gate, 2, 3), jnp.swapaxes(ffn_w_up, 2, 3), ffn_w_down)
    g_lo, g_hi = 4 * WIDTH, 4 * WIDTH + 2 * HEADS
    a_in = jnp.concatenate([a_w_in[0][:, :g_lo], a_w_in[0][:, g_hi:], _pad_cols(a_w_in[0][:, g_lo:g_hi], LANES)],
                           axis=1).astype(BF16)
    b_if = _pad_cols(a_b_if[0].reshape(1, 2 * HEADS), LANES)
    a_out = a_w_out[0].astype(BF16)
    c_in = c_w_in[0].astype(BF16)
    c_out = c_w_out[0].astype(BF16)
    sgu_bt = _pad_cols(sgu_b[0].T, LANES)
    sgu_w00 = jnp.repeat(sgu_w[0, :, 0, 0], CHUNK).reshape(1, WIDTH)
    sgu_b0 = jnp.repeat(sgu_b[0, :, 0], CHUNK).reshape(1, WIDTH)

    xp = x_prompt.reshape(nb * s, D_MODEL)
    xs = x_sample.reshape(ns, D_MODEL)
    xp, xs = _ffn(xp, xs, norm_g[0, 0], ffn_w, 0, 0, tm=512)
    xp, p_c, p_n, p_m = _mixer_ab_prompt(xp.reshape(nb, s, D_MODEL), norm_g[0, 1], a_in, b_if,
                                          sgu_norm_g[0], sgu_w[0], sgu_bt, a_out)
    z = _proj(xs, norm_g[0, 1], a_in)
    cat, s_c, s_n, s_m, s_v = _mixer_ab_step(
        z, b_if, sgu_norm_g[0], sgu_w00, sgu_b0, state_mlstm_C[0],
        state_mlstm_n[0].reshape(ns, HEADS, 1, HEAD_DIM), _pad_cols(state_mlstm_m[0], LANES), nb=8)
    xs = _out_proj(cat, a_out, xs)
    xp, xs = _ffn(xp.reshape(nb * s, D_MODEL), xs, norm_g[0, 2], ffn_w, 0, 1, tm=512)
    xp, xs = _ffn(xp, xs, norm_g[1, 0], ffn_w, 1, 0, tm=512)
    xp3 = xp.reshape(nb, s, D_MODEL)
    hn1, hn2 = _rms_deint(xp3, norm_g[1, 1], tm=1024)
    outs, stats = [], []
    for gi, (xin, tm, rb) in enumerate(((xp3.reshape(nb, 1, s, D_MODEL), 512, 1), (hn1, 512, 1), (hn2, 256, 2))):
        o, st = _swa_group_prompt(xin, norm_g[1, 1], c_in, group=gi, tm=tm, rb=rb)
        outs.append(o)
        stats.append(st)
    p_kv = [_kv_tail(xp3, norm_g[1, 1], c_in, group=gi) for gi in range(3)]
    xp = _swa_merge(outs, stats, xp3, c_out, tm=1024).reshape(nb * s, D_MODEL)
    z = _proj(xs, norm_g[1, 1], c_in)
    cat = _swa_step(z.reshape(ns, 3 * 3 * HEADS, HEAD_DIM), (cache_swa_kv0, cache_swa_kv1, cache_swa_kv2), nb=4)
    xs = _out_proj(cat.reshape(ns, WIDTH), c_out, xs)
    y_prompt, y_sample = _ffn(xp, xs, norm_g[1, 2], ffn_w, 1, 1, final_norm_g, tm=512)
    y_prompt = y_prompt.reshape(nb, s, D_MODEL)
    y_sample = y_sample.reshape(ns, 1, D_MODEL)
    s_kv = [z[:, gi * C_GROUP_COLS + WIDTH:(gi + 1) * C_GROUP_COLS].reshape(1, ns, 1, 2, HEADS, HEAD_DIM)
            for gi in range(3)]

    return (y_prompt, y_sample,
            p_c.reshape(1, nb, HEADS, HEAD_DIM, HEAD_DIM), p_n.reshape(1, nb, HEADS, HEAD_DIM),
            p_m[:, :, 0, 0].reshape(1, nb, HEADS),
            s_c.reshape(1, ns, HEADS, HEAD_DIM, HEAD_DIM), s_n.reshape(1, ns, HEADS, HEAD_DIM),
            s_m[:, :HEADS].reshape(1, ns, HEADS), s_v.reshape(1, ns, 1, WIDTH),
            p_kv[0], p_kv[1], p_kv[2], s_kv[0], s_kv[1], s_kv[2])
```

```python
import functools

import jax
import jax.numpy as jnp
from jax import lax
from jax.experimental import pallas as pl
from jax.experimental.pallas import tpu as pltpu

F32 = jnp.float32
BF16 = jnp.bfloat16

D_MODEL = 1024
D_FF = 2752
HEADS = 4
HEAD_DIM = 128
WIDTH = HEADS * HEAD_DIM
CHUNK = 128
SWA_GROUPS = ((128, 1), (512, 4), (2048, 16))
NORM_EPS = 1e-6
NEG_INF = -1e30

LANES = 128
BF16_ROWS = 16
PROJ_SLAB = 256
FF_CHUNK = 256
A_IN_PAD = 4 * WIDTH + 2 * WIDTH + LANES
COL_U = 4 * WIDTH
COL_GV = 5 * WIDTH
COL_GATES = 6 * WIDTH
C_GROUP_COLS = 3 * WIDTH

VMEM_LIMIT = 56 * 1024 * 1024


def _params(semantics):
    return pltpu.CompilerParams(dimension_semantics=semantics, vmem_limit_bytes=VMEM_LIMIT)


def _resident(shape):
    nd = len(shape)
    return pl.BlockSpec(shape, lambda *_: (0,) * nd, pipeline_mode=pl.Buffered(1))


def _rms(x, g):
    ms = jnp.mean(x * x, axis=-1, keepdims=True)
    return x * lax.rsqrt(ms + NORM_EPS) * g


def _dot(a, b):
    return jnp.dot(a, b, preferred_element_type=F32)


def _dot_nt(a, b):
    return lax.dot_general(a, b, (((1,), (1,)), ((), ())), preferred_element_type=F32)


def _log_sigmoid(x):
    return jnp.minimum(x, 0.0) - jnp.log1p(jnp.exp(-jnp.abs(x)))


def _swiglu_rows(h_scr, rows, wg_ref, wu_ref, wd_ref):
    acc = None
    for c0 in range(0, D_FF, FF_CHUNK):
        cols = slice(c0, min(c0 + FF_CHUNK, D_FF))
        h = h_scr[0:rows, :]
        gate = _dot_nt(h, wg_ref[0, 0, cols, :])
        up = _dot_nt(h, wu_ref[0, 0, cols, :])
        act = (gate * jax.nn.sigmoid(gate) * up).astype(h_scr.dtype)
        part = _dot(act, wd_ref[0, 0, cols, :])
        acc = part if acc is None else acc + part
    return acc


def _ffn_body(*refs, final):
    if final:
        x_ref, xs_ref, g_ref, wg_ref, wu_ref, wd_ref, fg_ref, o_ref, os_ref, h_scr = refs
    else:
        x_ref, xs_ref, g_ref, wg_ref, wu_ref, wd_ref, o_ref, os_ref, h_scr = refs
    tm, ns = x_ref.shape[0], xs_ref.shape[0]
    last = pl.num_programs(0) - 1

    def finish(x, acc):
        y = x + 0.5 * acc
        return _rms(y, fg_ref[...]) if final else y

    x = x_ref[...]
    h_scr[0:tm, :] = _rms(x, g_ref[...]).astype(h_scr.dtype)

    @pl.when(pl.program_id(0) != last)
    def _():
        o_ref[...] = finish(x, _swiglu_rows(h_scr, tm, wg_ref, wu_ref, wd_ref))

    @pl.when(pl.program_id(0) == last)
    def _():
        xs = xs_ref[...]
        h_scr[tm:tm + ns, :] = _rms(xs, g_ref[...]).astype(h_scr.dtype)
        acc = _swiglu_rows(h_scr, tm + ns, wg_ref, wu_ref, wd_ref)
        o_ref[...] = finish(x, acc[0:tm])
        os_ref[...] = finish(xs, acc[tm:tm + ns])


def _ffn(x, xs, g, weights, layer, which, final_g=None, *, tm):
    m, ns = x.shape[0], xs.shape[0]
    final = final_g is not None
    wg, wu, wd = weights
    row = pl.BlockSpec((tm, D_MODEL), lambda i: (i, 0))
    wspec = lambda w: pl.BlockSpec((1, 1) + w.shape[2:], lambda i: (layer, which, 0, 0), pipeline_mode=pl.Buffered(1))
    in_specs = [row, _resident(xs.shape), _resident((1, D_MODEL)), wspec(wg), wspec(wu), wspec(wd)]
    args = [x, xs, g.reshape(1, D_MODEL), wg, wu, wd]
    if final:
        in_specs.append(_resident((1, D_MODEL)))
        args.append(final_g.reshape(1, D_MODEL))
    return pl.pallas_call(
        functools.partial(_ffn_body, final=final),
        grid=(m // tm,),
        in_specs=in_specs,
        out_specs=[row, pl.BlockSpec(xs.shape, lambda i: (0, 0))],
        out_shape=[jax.ShapeDtypeStruct((m, D_MODEL), F32), jax.ShapeDtypeStruct(xs.shape, F32)],
        scratch_shapes=[pltpu.VMEM((tm + ns, D_MODEL), wg.dtype)],
        compiler_params=_params(("arbitrary",)),
        name="ffn_final" if final else "ffn",
    )(*args)


def _cummax_lanes(x):
    lane = lax.broadcasted_iota(jnp.int32, x.shape, 1)
    d = 1
    while d < x.shape[1]:
        x = jnp.maximum(x, jnp.where(lane >= d, pltpu.roll(x, d, axis=1), NEG_INF))
        d *= 2
    return x


def _exact_tri_dot(tri_bf16, x):
    x1 = x.astype(BF16)
    r1 = x - x1.astype(F32)
    x2 = r1.astype(BF16)
    x3 = (r1 - x2.astype(F32)).astype(BF16)
    n = x.shape[1]
    r = _dot(tri_bf16, jnp.concatenate([x1, x2, x3], axis=1))
    return r[:, 0:n] + r[:, n:2 * n] + r[:, 2 * n:3 * n]


def _mixer_ab_chunk(xn_ref, xp_ref, g_ref, win_ref, bif_ref, sg_ref, sw_ref, sbt_ref, wout_ref,
                   y_ref, c_ref, n_ref, m_ref, z_cur, z_nxt, cat_scr):
    nb = xn_ref.shape[0]
    row = lax.broadcasted_iota(jnp.int32, (CHUNK, CHUNK), 0)
    col = lax.broadcasted_iota(jnp.int32, (CHUNK, CHUNK), 1)
    causal = col <= row
    keys_before = row <= col
    tri = jnp.where(causal, 1.0, 0.0).astype(BF16)
    scale = HEAD_DIM ** -0.5

    hn = _rms(xn_ref[...].reshape(nb * CHUNK, D_MODEL), g_ref[...]).astype(BF16)
    slabs = [(c0, min(c0 + PROJ_SLAB, A_IN_PAD)) for c0 in range(0, A_IN_PAD, PROJ_SLAB)]

    def project(count):
        for _ in range(min(count, len(slabs))):
            c0, c1 = slabs.pop(0)
            z_nxt[:, c0:c1] = _dot(hn, win_ref[:, c0:c1])

    sgu_bias = [jnp.broadcast_to(sbt_ref[:, g:g + 1], (CHUNK, CHUNK)) for g in range(HEADS)]
    vn = [_rms(z_cur[b * CHUNK:(b + 1) * CHUNK, COL_GV:COL_GV + WIDTH], sg_ref[...]) for b in range(nb)]
    for g in range(HEADS):
        gs = slice(g * CHUNK, (g + 1) * CHUNK)
        mixed = _dot(jnp.where(causal, sw_ref[g], 0.0).astype(BF16),
                     jnp.concatenate([vn[b][:, gs] for b in range(nb)], axis=1).astype(BF16))
        for b in range(nb):
            rows = slice(b * CHUNK, (b + 1) * CHUNK)
            u = z_cur[rows, COL_U + g * CHUNK:COL_U + (g + 1) * CHUNK]
            cat_scr[rows, WIDTH + g * CHUNK:WIDTH + (g + 1) * CHUNK] = (
                u * (mixed[:, b * CHUNK:(b + 1) * CHUNK] + sgu_bias[g])).astype(BF16)
    project(1)

    pieces_per_stage = -(-(len(slabs)) // (3 * nb))
    gate_terms = []
    for b in range(nb):
        rows = slice(b * CHUNK, (b + 1) * CHUNK)
        gates = z_cur[rows, COL_GATES:COL_GATES + LANES] + bif_ref[...]
        lg = jnp.where(col < HEADS, gates, _log_sigmoid(gates))
        gate_terms.append((lg, _exact_tri_dot(tri, lg)))
        project(pieces_per_stage)

    heads = []
    for b, (lg, fcum) in enumerate(gate_terms):
        rows = slice(b * CHUNK, (b + 1) * CHUNK)
        lg_t = lg.T
        fcum_t = fcum.T
        gmax = _cummax_lanes(lg_t[0:2 * HEADS, :] - jnp.concatenate([fcum_t[HEADS:2 * HEADS, :]] * 2, axis=0))
        for h in range(HEADS):
            q = z_cur[rows, h * HEAD_DIM:(h + 1) * HEAD_DIM]
            k = z_cur[rows, WIDTH + h * HEAD_DIM:WIDTH + (h + 1) * HEAD_DIM] * scale
            v_t = z_cur[rows, 2 * WIDTH + h * HEAD_DIM:2 * WIDTH + (h + 1) * HEAD_DIM].T
            f_r = fcum_t[HEADS + h:HEADS + h + 1, :]
            i_r = lg_t[h:h + 1, :]
            g_c = lg[:, h:h + 1] - fcum[:, HEADS + h:HEADS + h + 1]
            c0, n0, m0 = c_ref[b, h], n_ref[b, h], m_ref[b, h]
            m = f_r + jnp.maximum(m0, gmax[h:h + 1, :])
            inter = jnp.exp(m0 + f_r - m)
            qb, kb = q.astype(BF16), k.astype(BF16)
            against_q = _dot_nt(
                jnp.concatenate([k, c0, jnp.broadcast_to(n0, (BF16_ROWS, HEAD_DIM))], axis=0).astype(BF16), qb)
            m_last = m[:, CHUNK - 1:CHUNK]
            f_last = f_r[:, CHUNK - 1:CHUNK]
            w = jnp.exp(f_last - f_r + i_r - m_last)
            decay = jnp.exp(m0[:, 0:1] + f_last - m_last)
            against_k = _dot(
                jnp.concatenate([v_t * w, jnp.broadcast_to(w, (BF16_ROWS, CHUNK))], axis=0).astype(BF16), kb)
            c_ref[b, h] = decay * c0 + against_k[0:HEAD_DIM]
            n_ref[b, h] = decay * n0 + against_k[HEAD_DIM:HEAD_DIM + 1]
            m_ref[b, h] = jnp.broadcast_to(m_last, (1, LANES))
            heads.append((against_q, v_t, f_r, g_c, m, inter))
        project(pieces_per_stage)

    partial = []
    for i, (against_q, v_t, f_r, g_c, m, inter) in enumerate(heads):
        kq = against_q[0:CHUNK]
        cq = against_q[CHUNK:CHUNK + HEAD_DIM]
        nq = against_q[CHUNK + HEAD_DIM:CHUNK + HEAD_DIM + 1]
        s_t = kq * jnp.exp(jnp.where(keys_before, (f_r - m) + g_c, NEG_INF))
        den = jnp.sum(s_t, axis=0, keepdims=True) + inter * nq
        partial.append((_dot(v_t.astype(BF16), s_t.astype(BF16)), inter * cq,
                        jnp.maximum(jnp.abs(den), jnp.exp(-m))))
        if i % HEADS == HEADS - 1:
            project(pieces_per_stage)

    for i, (sv, carried, den) in enumerate(partial):
        b, h = divmod(i, HEADS)
        rows = slice(b * CHUNK, (b + 1) * CHUNK)
        o = z_cur[rows, 3 * WIDTH + h * HEAD_DIM:3 * WIDTH + (h + 1) * HEAD_DIM]
        cat_scr[rows, h * HEAD_DIM:(h + 1) * HEAD_DIM] = (jax.nn.sigmoid(o) * ((sv + carried) / den).T).astype(BF16)

    project(len(slabs))
    y = xp_ref[...].reshape(nb * CHUNK, D_MODEL) + _dot(cat_scr[...], wout_ref[...])
    y_ref[...] = y.reshape(nb, CHUNK, D_MODEL)


def _mixer_ab_body(*refs):
    *io_refs, z0_scr, z1_scr, cat_scr = refs
    c_ref, n_ref, m_ref = io_refs[-3:]
    step = pl.program_id(0)

    @pl.when(step <= 1)
    def _():
        c_ref[...] = jnp.zeros_like(c_ref)
        n_ref[...] = jnp.zeros_like(n_ref)
        m_ref[...] = jnp.zeros_like(m_ref)

    @pl.when(step == 0)
    def _():
        z1_scr[...] = jnp.zeros_like(z1_scr)

    @pl.when(step % 2 == 0)
    def _():
        _mixer_ab_chunk(*io_refs, z1_scr, z0_scr, cat_scr)

    @pl.when(step % 2 == 1)
    def _():
        _mixer_ab_chunk(*io_refs, z0_scr, z1_scr, cat_scr)


def _mixer_ab_prompt(x, g, w_in, b_if, sgu_g, sgu_w, sgu_bt, w_out):
    nb, s, _ = x.shape
    n_chunks = s // CHUNK
    blk = lambda index: pl.BlockSpec((nb, CHUNK, D_MODEL), index)
    nxt = blk(lambda c: (0, jnp.minimum(c, n_chunks - 1), 0))
    prev = blk(lambda c: (0, jnp.maximum(c - 1, 0), 0))
    z_shape = pltpu.VMEM((nb * CHUNK, A_IN_PAD), F32)
    return pl.pallas_call(
        _mixer_ab_body,
        grid=(n_chunks + 1,),
        in_specs=[nxt, prev, _resident((1, D_MODEL)), _resident(w_in.shape), _resident((1, LANES)),
                  _resident((1, WIDTH)), _resident(sgu_w.shape), _resident(sgu_bt.shape), _resident(w_out.shape)],
        out_specs=[prev,
                   pl.BlockSpec((nb, HEADS, HEAD_DIM, HEAD_DIM), lambda c: (0, 0, 0, 0)),
                   pl.BlockSpec((nb, HEADS, 1, HEAD_DIM), lambda c: (0, 0, 0, 0)),
                   pl.BlockSpec((nb, HEADS, 1, LANES), lambda c: (0, 0, 0, 0))],
        out_shape=[jax.ShapeDtypeStruct(x.shape, F32),
                   jax.ShapeDtypeStruct((nb, HEADS, HEAD_DIM, HEAD_DIM), F32),
                   jax.ShapeDtypeStruct((nb, HEADS, 1, HEAD_DIM), F32),
                   jax.ShapeDtypeStruct((nb, HEADS, 1, LANES), F32)],
        scratch_shapes=[z_shape, z_shape, pltpu.VMEM((nb * CHUNK, 2 * WIDTH), BF16)],
        compiler_params=_params(("arbitrary",)),
        name="mixer_ab_prompt",
    )(x, x, g.reshape(1, D_MODEL), w_in, b_if, sgu_g.reshape(1, WIDTH), sgu_w, sgu_bt, w_out)


def _alibi_slope(group, head):
    n = len(SWA_GROUPS) * HEADS
    return 2.0 ** (-8.0 * (group * HEADS + head + 1) / n)


def _swa_group_body(x_ref, g_ref, w_ref, o_ref, st_ref, slab_scr, q_scr, k_scr, v_scr, *, group, dil, tile):
    step = pl.program_id(1)
    sub = tile // dil
    nsb = sub // CHUNK
    rows_per_proj = min(tile, 512)
    res_per_proj = rows_per_proj // sub
    scale = HEAD_DIM ** -0.5
    qi = lax.broadcasted_iota(jnp.int32, (CHUNK, 2 * CHUNK), 0)
    kc = lax.broadcasted_iota(jnp.int32, (CHUNK, 2 * CHUNK), 1)
    delta = CHUNK + qi - kc
    valid = (delta >= 0) & (delta <= CHUNK)
    valid_first = valid & (kc >= jnp.where(step > 0, 0, CHUNK))
    dist = (delta * dil).astype(F32)
    lane = lax.broadcasted_iota(jnp.int32, (CHUNK, LANES), 1)

    @pl.when(step == 0)
    def _():
        k_scr[:, 0:CHUNK, :] = jnp.zeros((dil, CHUNK, WIDTH), BF16)
        v_scr[:, 0:CHUNK, :] = jnp.zeros((dil, CHUNK, WIDTH), BF16)

    for c in range(tile // 256):
        rows = slice(c * 256, (c + 1) * 256)
        hn = _rms(x_ref[0, rows, :], g_ref[...])
        for sl in range(D_MODEL // LANES):
            slab_scr[sl, rows, :] = hn[:, sl * LANES:(sl + 1) * LANES]

    def token_rows(r, first, count):
        start = first * dil + r
        return slice(start, start + count) if dil == 1 else pl.ds(start, count, stride=dil)

    def project(p):
        parts = []
        for r in range(p * res_per_proj, (p + 1) * res_per_proj):
            parts.append(jnp.concatenate([slab_scr[sl, token_rows(r, 0, sub), :] for sl in range(D_MODEL // LANES)],
                                         axis=1).astype(BF16))
        hn = parts[0] if len(parts) == 1 else jnp.concatenate(parts, axis=0)
        for j, scr in enumerate((q_scr, k_scr, v_scr)):
            z = _dot(hn, w_ref[:, j * WIDTH:(j + 1) * WIDTH]).astype(BF16)
            for i, r in enumerate(range(p * res_per_proj, (p + 1) * res_per_proj)):
                off = 0 if j == 0 else CHUNK
                scr[r, off:off + sub, :] = z[i * sub:(i + 1) * sub, :]

    def scores(p):
        out = []
        for r in range(p * res_per_proj, (p + 1) * res_per_proj):
            for j in range(nsb):
                mask = valid_first if j == 0 else valid
                for h in range(HEADS):
                    hs = slice(h * HEAD_DIM, (h + 1) * HEAD_DIM)
                    s = _dot_nt(q_scr[r, j * CHUNK:(j + 1) * CHUNK, hs], k_scr[r, j * CHUNK:(j + 2) * CHUNK, hs])
                    s = jnp.where(mask, s * scale + (-_alibi_slope(group, h)) * dist, NEG_INF)
                    m = jnp.max(s, axis=-1, keepdims=True)
                    p_ = jnp.exp(s - m)
                    out.append((r, j, h, m, jnp.sum(p_, axis=-1, keepdims=True), p_.astype(BF16)))
        return out

    def values(items):
        stats = None
        for r, j, h, m, l, p_ in items:
            hs = slice(h * HEAD_DIM, (h + 1) * HEAD_DIM)
            o_ref[0, h, token_rows(r, j * CHUNK, CHUNK), :] = _dot(p_, v_scr[r, j * CHUNK:(j + 2) * CHUNK, hs])
            base = jnp.zeros((CHUNK, LANES), F32) if h == 0 else stats
            stats = jnp.where(lane == h, m, jnp.where(lane == HEADS + h, l, base))
            if h == HEADS - 1:
                st_ref[0, token_rows(r, j * CHUNK, CHUNK), :] = stats

    n_proj = tile // rows_per_proj
    project(0)
    pending = None
    for p in range(n_proj):
        items = scores(p)
        if p + 1 < n_proj:
            project(p + 1)
        if pending is not None:
            values(pending)
        pending = items
    values(pending)
    for r in range(dil):
        k_scr[r, 0:CHUNK, :] = k_scr[r, sub:sub + CHUNK, :]
        v_scr[r, 0:CHUNK, :] = v_scr[r, sub:sub + CHUNK, :]


def _swa_group_prompt(x, g, c_in, *, group, tile):
    nb, s, _ = x.shape
    dil = SWA_GROUPS[group][1]
    sub = tile // dil
    return pl.pallas_call(
        functools.partial(_swa_group_body, group=group, dil=dil, tile=tile),
        grid=(nb, s // tile),
        in_specs=[pl.BlockSpec((1, tile, D_MODEL), lambda b, t: (b, t, 0)), _resident((1, D_MODEL)),
                  pl.BlockSpec((D_MODEL, C_GROUP_COLS), lambda b, t: (0, group), pipeline_mode=pl.Buffered(1))],
        out_specs=[pl.BlockSpec((1, HEADS, tile, HEAD_DIM), lambda b, t: (b, 0, t, 0)),
                   pl.BlockSpec((1, tile, LANES), lambda b, t: (b, t, 0))],
        out_shape=[jax.ShapeDtypeStruct((nb, HEADS, s, HEAD_DIM), F32),
                   jax.ShapeDtypeStruct((nb, s, LANES), F32)],
        scratch_shapes=[pltpu.VMEM((D_MODEL // LANES, tile, LANES), F32),
                        pltpu.VMEM((dil, sub, WIDTH), BF16),
                        pltpu.VMEM((dil, CHUNK + sub, WIDTH), BF16),
                        pltpu.VMEM((dil, CHUNK + sub, WIDTH), BF16)],
        compiler_params=_params(("arbitrary", "arbitrary")),
        name="swa_group%d" % group,
    )(x, g.reshape(1, D_MODEL), c_in)


def _kv_tail_body(x_ref, g_ref, wk_ref, wv_ref, kv_ref):
    hn = _rms(x_ref[0], g_ref[...]).astype(BF16)
    for j, w_ref in enumerate((wk_ref, wv_ref)):
        kv = _dot(hn, w_ref[...])
        for h in range(HEADS):
            kv_ref[0, 0, :, j, h, :] = kv[:, h * HEAD_DIM:(h + 1) * HEAD_DIM]


def _kv_tail(x, g, c_in, *, group):
    win = SWA_GROUPS[group][0]
    nb, s, _ = x.shape
    tm = min(win, 512)
    first = (s - win) // tm
    wcol = lambda j: pl.BlockSpec((D_MODEL, WIDTH), lambda b, t: (0, 3 * group + j), pipeline_mode=pl.Buffered(1))
    return pl.pallas_call(
        _kv_tail_body,
        grid=(nb, win // tm),
        in_specs=[pl.BlockSpec((1, tm, D_MODEL), lambda b, t: (b, first + t, 0)), _resident((1, D_MODEL)),
                  wcol(1), wcol(2)],
        out_specs=pl.BlockSpec((1, 1, tm, 2, HEADS, HEAD_DIM), lambda b, t: (0, b, t, 0, 0, 0)),
        out_shape=jax.ShapeDtypeStruct((1, nb, win, 2, HEADS, HEAD_DIM), F32),
        compiler_params=_params(("arbitrary", "arbitrary")),
        name="kv_tail%d" % group,
    )(x, g.reshape(1, D_MODEL), c_in, c_in)


def _swa_merge_body(o0_ref, o1_ref, o2_ref, s0_ref, s1_ref, s2_ref, x_ref, w_ref, y_ref, cat_scr):
    t = x_ref.shape[1]
    rows_per_pass = 256
    for c in range(t // rows_per_pass):
        rows = slice(c * rows_per_pass, (c + 1) * rows_per_pass)
        stats = [s_ref[0, rows, :] for s_ref in (s0_ref, s1_ref, s2_ref)]
        for h in range(HEADS):
            ms = [st[:, h:h + 1] for st in stats]
            ls = [st[:, HEADS + h:HEADS + h + 1] for st in stats]
            top = functools.reduce(jnp.maximum, ms)
            ws = [jnp.exp(m - top) for m in ms]
            num = sum(w * o_ref[0, h, rows, :] for w, o_ref in zip(ws, (o0_ref, o1_ref, o2_ref)))
            den = sum(w * l for w, l in zip(ws, ls))
            cat_scr[rows, h * HEAD_DIM:(h + 1) * HEAD_DIM] = (num / den).astype(BF16)
    y_ref[0] = x_ref[0] + _dot(cat_scr[...], w_ref[...])


def _swa_merge(outs, stats, x, w_out, *, tm):
    nb, s, _ = x.shape
    o_spec = pl.BlockSpec((1, HEADS, tm, HEAD_DIM), lambda b, t: (b, 0, t, 0))
    s_spec = pl.BlockSpec((1, tm, LANES), lambda b, t: (b, t, 0))
    x_spec = pl.BlockSpec((1, tm, D_MODEL), lambda b, t: (b, t, 0))
    return pl.pallas_call(
        _swa_merge_body,
        grid=(nb, s // tm),
        in_specs=[o_spec] * 3 + [s_spec] * 3 + [x_spec, _resident(w_out.shape)],
        out_specs=x_spec,
        out_shape=jax.ShapeDtypeStruct(x.shape, F32),
        scratch_shapes=[pltpu.VMEM((tm, WIDTH), BF16)],
        compiler_params=_params(("arbitrary", "arbitrary")),
        name="swa_merge",
    )(*outs, *stats, x, w_out)


def _proj_body(x_ref, g_ref, w_ref, z_ref):
    z_ref[...] = _dot(_rms(x_ref[...], g_ref[...]).astype(BF16), w_ref[...])


def _proj(x, g, w):
    m, n = x.shape[0], w.shape[1]
    return pl.pallas_call(
        _proj_body,
        grid=(1,),
        in_specs=[_resident(x.shape), _resident((1, D_MODEL)), _resident(w.shape)],
        out_specs=pl.BlockSpec((m, n), lambda i: (0, 0)),
        out_shape=jax.ShapeDtypeStruct((m, n), F32),
        compiler_params=_params(("arbitrary",)),
        name="proj_rows",
    )(x, g.reshape(1, D_MODEL), w)


def _out_proj_body(c_ref, w_ref, x_ref, y_ref):
    y_ref[...] = x_ref[...] + _dot(c_ref[...].astype(BF16), w_ref[...])


def _out_proj(cat, w, x):
    return pl.pallas_call(
        _out_proj_body,
        grid=(1,),
        in_specs=[_resident(cat.shape), _resident(w.shape), _resident(x.shape)],
        out_specs=pl.BlockSpec(x.shape, lambda i: (0, 0)),
        out_shape=jax.ShapeDtypeStruct(x.shape, F32),
        compiler_params=_params(("arbitrary",)),
        name="out_proj_rows",
    )(cat, w, x)


def _mixer_ab_step_body(z_ref, bif_ref, sg_ref, w00_ref, b0_ref, c_ref, n_ref, m_ref,
                        cat_ref, c1_ref, n1_ref, m1_ref, vn_ref):
    nb = z_ref.shape[0]
    scale = HEAD_DIM ** -0.5
    eye = (lax.broadcasted_iota(jnp.int32, (HEAD_DIM, HEAD_DIM), 0)
           == lax.broadcasted_iota(jnp.int32, (HEAD_DIM, HEAD_DIM), 1)).astype(F32)
    lane = lax.broadcasted_iota(jnp.int32, (1, LANES), 1)
    for i in range(nb):
        zr = z_ref[i:i + 1, :]
        gates = zr[:, COL_GATES:COL_GATES + LANES] + bif_ref[...]
        m_new = jnp.zeros((1, LANES), F32)
        for h in range(HEADS):
            hs = slice(h * HEAD_DIM, (h + 1) * HEAD_DIM)
            q = zr[:, hs]
            k = zr[:, WIDTH + h * HEAD_DIM:WIDTH + (h + 1) * HEAD_DIM] * scale
            v = zr[:, 2 * WIDTH + h * HEAD_DIM:2 * WIDTH + (h + 1) * HEAD_DIM]
            o = zr[:, 3 * WIDTH + h * HEAD_DIM:3 * WIDTH + (h + 1) * HEAD_DIM]
            ig = gates[:, h:h + 1]
            lf = _log_sigmoid(gates[:, HEADS + h:HEADS + h + 1])
            c0 = c_ref[i, h]
            n0 = n_ref[i, h]
            m0 = m_ref[i:i + 1, h:h + 1]
            a = m0 + lf
            m = jnp.maximum(a, ig)
            s = jnp.sum(q * k, axis=-1, keepdims=True) * jnp.exp(ig - m)
            inter = jnp.exp(a - m)
            cq_col = jnp.sum(c0 * q, axis=-1, keepdims=True)
            cq = jnp.sum(eye * cq_col, axis=0, keepdims=True)
            v_col = jnp.sum(eye * v, axis=-1, keepdims=True)
            num = s * v + inter * cq
            den = s + inter * jnp.sum(n0 * q, axis=-1, keepdims=True)
            hh = num / jnp.maximum(jnp.abs(den), jnp.exp(-m))
            w = jnp.exp(ig - m)
            c1_ref[i, h] = inter * c0 + (w * v_col) * k
            n1_ref[i, h] = inter * n0 + w * k
            m_new = jnp.where(lane == h, m, m_new)
            cat_ref[i:i + 1, hs] = jax.nn.sigmoid(o) * hh
        m1_ref[i:i + 1, :] = m_new
        vn = _rms(zr[:, COL_GV:COL_GV + WIDTH], sg_ref[...])
        vn_ref[i:i + 1, :] = vn
        cat_ref[i:i + 1, WIDTH:2 * WIDTH] = zr[:, COL_U:COL_U + WIDTH] * (w00_ref[...] * vn + b0_ref[...])


def _mixer_ab_step(z, b_if, sgu_g, w00, b0, st_c, st_n, st_m, *, nb):
    n = z.shape[0]
    rows = lambda w: pl.BlockSpec((nb, w), lambda i: (i, 0))
    c_spec = pl.BlockSpec((nb, HEADS, HEAD_DIM, HEAD_DIM), lambda i: (i, 0, 0, 0))
    n_spec = pl.BlockSpec((nb, HEADS, 1, HEAD_DIM), lambda i: (i, 0, 0, 0))
    return pl.pallas_call(
        _mixer_ab_step_body,
        grid=(n // nb,),
        in_specs=[rows(A_IN_PAD), _resident((1, LANES)), _resident((1, WIDTH)), _resident((1, WIDTH)),
                  _resident((1, WIDTH)), c_spec, n_spec, rows(LANES)],
        out_specs=[rows(2 * WIDTH), c_spec, n_spec, rows(LANES), rows(WIDTH)],
        out_shape=[jax.ShapeDtypeStruct((n, 2 * WIDTH), F32),
                   jax.ShapeDtypeStruct(st_c.shape, F32),
                   jax.ShapeDtypeStruct(st_n.shape, F32),
                   jax.ShapeDtypeStruct((n, LANES), F32),
                   jax.ShapeDtypeStruct((n, WIDTH), F32)],
        compiler_params=_params(("arbitrary",)),
        name="mixer_ab_step",
    )(z, b_if, sgu_g.reshape(1, WIDTH), w00, b0, st_c, st_n, st_m)


def _swa_step_body(z_ref, kv0_ref, kv1_ref, kv2_ref, cat_ref):
    nb = z_ref.shape[0]
    scale = HEAD_DIM ** -0.5
    steps = (CHUNK - lax.broadcasted_iota(jnp.int32, (CHUNK, 1, 1), 0)).astype(F32)
    head = lax.broadcasted_iota(jnp.int32, (1, HEADS, 1), 1)
    for i in range(nb):
        ms, ls, os_ = [], [], []
        for gi, kv_ref in enumerate((kv0_ref, kv1_ref, kv2_ref)):
            dil = SWA_GROUPS[gi][1]
            base = gi * 3 * HEADS
            q = z_ref[i, base:base + HEADS, :]
            k_new = z_ref[i, base + HEADS:base + 2 * HEADS, :]
            v_new = z_ref[i, base + 2 * HEADS:base + 3 * HEADS, :]
            kc = kv_ref[i, :, 0, 0, :, :]
            vc = kv_ref[i, :, 0, 1, :, :]
            slope = jnp.zeros((1, HEADS, 1), F32)
            for h in range(HEADS):
                slope = jnp.where(head == h, _alibi_slope(gi, h) * dil, slope)
            s = jnp.sum(kc * q[None], axis=-1, keepdims=True) * scale - slope * steps
            s_new = jnp.sum(k_new * q, axis=-1, keepdims=True) * scale
            m = jnp.maximum(jnp.max(s, axis=0), s_new)
            p = jnp.exp(s - m[None])
            p_new = jnp.exp(s_new - m)
            ms.append(m)
            ls.append(jnp.sum(p, axis=0) + p_new)
            os_.append(jnp.sum(p * vc, axis=0) + p_new * v_new)
        top = functools.reduce(jnp.maximum, ms)
        ws = [jnp.exp(m - top) for m in ms]
        num = sum(w * o for w, o in zip(ws, os_))
        den = sum(w * l for w, l in zip(ws, ls))
        cat_ref[i] = num / den


def _swa_step(z, caches, *, nb):
    n = z.shape[0]
    views = []
    specs = []
    for (win, dil), cache in zip(SWA_GROUPS, caches):
        views.append(cache.reshape(n, win // dil, dil, 2, HEADS, HEAD_DIM))
        specs.append(pl.BlockSpec((nb, CHUNK, 1, 2, HEADS, HEAD_DIM), lambda i: (i, 0, 0, 0, 0, 0)))
    return pl.pallas_call(
        _swa_step_body,
        grid=(n // nb,),
        in_specs=[pl.BlockSpec((nb,) + z.shape[1:], lambda i: (i, 0, 0))] + specs,
        out_specs=pl.BlockSpec((nb, HEADS, HEAD_DIM), lambda i: (i, 0, 0)),
        out_shape=jax.ShapeDtypeStruct((n, HEADS, HEAD_DIM), F32),
        compiler_params=_params(("arbitrary",)),
        name="swa_step",
    )(z, *views)


def _pad_cols(w, n):
    return jnp.pad(w, ((0, 0), (0, n - w.shape[1])))


def kernel(x_prompt, x_sample, state_mlstm_C, state_mlstm_n, state_mlstm_m, cache_swa_kv0, cache_swa_kv1, cache_swa_kv2, norm_g, ffn_w_gate, ffn_w_up, ffn_w_down, a_w_in, a_b_if, sgu_norm_g, sgu_w, sgu_b, a_w_out, c_w_in, c_w_out, final_norm_g):
    nb, s, _ = x_prompt.shape
    ns = x_sample.shape[0]
    assert x_sample.shape[1] == 1 and s % max(w for w, _ in SWA_GROUPS) == 0
    for (win, dil), cache in zip(SWA_GROUPS, (cache_swa_kv0, cache_swa_kv1, cache_swa_kv2)):
        assert cache.shape[2] == win and win // dil == CHUNK

    ffn_w = (jnp.swapaxes(ffn_w_gate, 2, 3), jnp.swapaxes(ffn_w_up, 2, 3), ffn_w_down)
    g_lo, g_hi = 4 * WIDTH, 4 * WIDTH + 2 * HEADS
    a_in = jnp.concatenate([a_w_in[0][:, :g_lo], a_w_in[0][:, g_hi:], _pad_cols(a_w_in[0][:, g_lo:g_hi], LANES)],
                           axis=1).astype(BF16)
    b_if = _pad_cols(a_b_if[0].reshape(1, 2 * HEADS), LANES)
    a_out = a_w_out[0].astype(BF16)
    c_in = c_w_in[0].astype(BF16)
    c_out = c_w_out[0].astype(BF16)
    sgu_bt = _pad_cols(sgu_b[0].T, LANES)
    sgu_w00 = jnp.repeat(sgu_w[0, :, 0, 0], CHUNK).reshape(1, WIDTH)
    sgu_b0 = jnp.repeat(sgu_b[0, :, 0], CHUNK).reshape(1, WIDTH)

    xp = x_prompt.reshape(nb * s, D_MODEL)
    xs = x_sample.reshape(ns, D_MODEL)
    xp, xs = _ffn(xp, xs, norm_g[0, 0], ffn_w, 0, 0, tm=512)
    xp, p_c, p_n, p_m = _mixer_ab_prompt(xp.reshape(nb, s, D_MODEL), norm_g[0, 1], a_in, b_if,
                                          sgu_norm_g[0], sgu_w[0], sgu_bt, a_out)
    z = _proj(xs, norm_g[0, 1], a_in)
    cat, s_c, s_n, s_m, s_v = _mixer_ab_step(
        z, b_if, sgu_norm_g[0], sgu_w00, sgu_b0, state_mlstm_C[0],
        state_mlstm_n[0].reshape(ns, HEADS, 1, HEAD_DIM), _pad_cols(state_mlstm_m[0], LANES), nb=8)
    xs = _out_proj(cat, a_out, xs)
    xp, xs = _ffn(xp.reshape(nb * s, D_MODEL), xs, norm_g[0, 2], ffn_w, 0, 1, tm=512)
    xp, xs = _ffn(xp, xs, norm_g[1, 0], ffn_w, 1, 0, tm=512)
    xp3 = xp.reshape(nb, s, D_MODEL)
    outs, stats = [], []
    for gi, tile in enumerate((512, 1024, 2048)):
        o, st = _swa_group_prompt(xp3, norm_g[1, 1], c_in, group=gi, tile=tile)
        outs.append(o)
        stats.append(st)
    p_kv = [_kv_tail(xp3, norm_g[1, 1], c_in, group=gi) for gi in range(3)]
    xp = _swa_merge(outs, stats, xp3, c_out, tm=1024).reshape(nb * s, D_MODEL)
    z = _proj(xs, norm_g[1, 1], c_in)
    cat = _swa_step(z.reshape(ns, 3 * 3 * HEADS, HEAD_DIM), (cache_swa_kv0, cache_swa_kv1, cache_swa_kv2), nb=4)
    xs = _out_proj(cat.reshape(ns, WIDTH), c_out, xs)
    y_prompt, y_sample = _ffn(xp, xs, norm_g[1, 2], ffn_w, 1, 1, final_norm_g, tm=512)
    y_prompt = y_prompt.reshape(nb, s, D_MODEL)
    y_sample = y_sample.reshape(ns, 1, D_MODEL)
    s_kv = [z[:, gi * C_GROUP_COLS + WIDTH:(gi + 1) * C_GROUP_COLS].reshape(1, ns, 1, 2, HEADS, HEAD_DIM)
            for gi in range(3)]

    return (y_prompt, y_sample,
            p_c.reshape(1, nb, HEADS, HEAD_DIM, HEAD_DIM), p_n.reshape(1, nb, HEADS, HEAD_DIM),
            p_m[:, :, 0, 0].reshape(1, nb, HEADS),
            s_c.reshape(1, ns, HEADS, HEAD_DIM, HEAD_DIM), s_n.reshape(1, ns, HEADS, HEAD_DIM),
            s_m[:, :HEADS].reshape(1, ns, HEADS), s_v.reshape(1, ns, 1, WIDTH),
            p_kv[0], p_kv[1], p_kv[2], s_kv[0], s_kv[1], s_kv[2])
```

```python
import functools

import jax
import jax.numpy as jnp
from jax import lax
from jax.experimental import pallas as pl
from jax.experimental.pallas import tpu as pltpu

F32 = jnp.float32
BF16 = jnp.bfloat16

D_MODEL = 1024
D_FF = 2752
HEADS = 4
HEAD_DIM = 128
WIDTH = HEADS * HEAD_DIM
CHUNK = 128
SWA_GROUPS = ((128, 1), (512, 4), (2048, 16))
NORM_EPS = 1e-6
NEG_INF = -1e30

LANES = 128
BF16_ROWS = 16
PROJ_SLAB = 256
FF_CHUNK = 256
A_IN_PAD = 4 * WIDTH + 2 * WIDTH + LANES
COL_U = 4 * WIDTH
COL_GV = 5 * WIDTH
COL_GATES = 6 * WIDTH
C_GROUP_COLS = 3 * WIDTH

VMEM_LIMIT = 60 * 1024 * 1024


def _params(semantics):
    return pltpu.CompilerParams(dimension_semantics=semantics, vmem_limit_bytes=VMEM_LIMIT)


def _resident(shape):
    nd = len(shape)
    return pl.BlockSpec(shape, lambda *_: (0,) * nd, pipeline_mode=pl.Buffered(1))


def _rms(x, g):
    ms = jnp.mean(x * x, axis=-1, keepdims=True)
    return x * lax.rsqrt(ms + NORM_EPS) * g


def _dot(a, b):
    return jnp.dot(a, b, preferred_element_type=F32)


def _dot_nt(a, b):
    return lax.dot_general(a, b, (((1,), (1,)), ((), ())), preferred_element_type=F32)


def _log_sigmoid(x):
    return jnp.minimum(x, 0.0) - jnp.log1p(jnp.exp(-jnp.abs(x)))


def _swiglu_rows(h_scr, rows, wg_ref, wu_ref, wd_ref):
    acc = None
    for c0 in range(0, D_FF, FF_CHUNK):
        cols = slice(c0, min(c0 + FF_CHUNK, D_FF))
        h = h_scr[0:rows, :]
        gate = _dot_nt(h, wg_ref[0, 0, cols, :])
        up = _dot_nt(h, wu_ref[0, 0, cols, :])
        act = (gate * jax.nn.sigmoid(gate) * up).astype(h_scr.dtype)
        part = _dot(act, wd_ref[0, 0, cols, :])
        acc = part if acc is None else acc + part
    return acc


def _ffn_body(*refs, final):
    if final:
        x_ref, xs_ref, g_ref, wg_ref, wu_ref, wd_ref, fg_ref, o_ref, os_ref, h_scr = refs
    else:
        x_ref, xs_ref, g_ref, wg_ref, wu_ref, wd_ref, o_ref, os_ref, h_scr = refs
    tm, ns = x_ref.shape[0], xs_ref.shape[0]
    last = pl.num_programs(0) - 1

    def finish(x, acc):
        y = x + 0.5 * acc
        return _rms(y, fg_ref[...]) if final else y

    x = x_ref[...]
    h_scr[0:tm, :] = _rms(x, g_ref[...]).astype(h_scr.dtype)

    @pl.when(pl.program_id(0) != last)
    def _():
        o_ref[...] = finish(x, _swiglu_rows(h_scr, tm, wg_ref, wu_ref, wd_ref))

    @pl.when(pl.program_id(0) == last)
    def _():
        xs = xs_ref[...]
        h_scr[tm:tm + ns, :] = _rms(xs, g_ref[...]).astype(h_scr.dtype)
        acc = _swiglu_rows(h_scr, tm + ns, wg_ref, wu_ref, wd_ref)
        o_ref[...] = finish(x, acc[0:tm])
        os_ref[...] = finish(xs, acc[tm:tm + ns])


def _ffn(x, xs, g, weights, layer, which, final_g=None, *, tm):
    m, ns = x.shape[0], xs.shape[0]
    final = final_g is not None
    wg, wu, wd = weights
    row = pl.BlockSpec((tm, D_MODEL), lambda i: (i, 0))
    wspec = lambda w: pl.BlockSpec((1, 1) + w.shape[2:], lambda i: (layer, which, 0, 0), pipeline_mode=pl.Buffered(1))
    in_specs = [row, _resident(xs.shape), _resident((1, D_MODEL)), wspec(wg), wspec(wu), wspec(wd)]
    args = [x, xs, g.reshape(1, D_MODEL), wg, wu, wd]
    if final:
        in_specs.append(_resident((1, D_MODEL)))
        args.append(final_g.reshape(1, D_MODEL))
    return pl.pallas_call(
        functools.partial(_ffn_body, final=final),
        grid=(m // tm,),
        in_specs=in_specs,
        out_specs=[row, pl.BlockSpec(xs.shape, lambda i: (0, 0))],
        out_shape=[jax.ShapeDtypeStruct((m, D_MODEL), F32), jax.ShapeDtypeStruct(xs.shape, F32)],
        scratch_shapes=[pltpu.VMEM((tm + ns, D_MODEL), wg.dtype)],
        compiler_params=_params(("arbitrary",)),
        name="ffn_final" if final else "ffn",
    )(*args)


def _cummax_lanes(x):
    lane = lax.broadcasted_iota(jnp.int32, x.shape, 1)
    d = 1
    while d < x.shape[1]:
        x = jnp.maximum(x, jnp.where(lane >= d, pltpu.roll(x, d, axis=1), NEG_INF))
        d *= 2
    return x


def _exact_tri_dot(tri_bf16, x):
    x1 = x.astype(BF16)
    r1 = x - x1.astype(F32)
    x2 = r1.astype(BF16)
    x3 = (r1 - x2.astype(F32)).astype(BF16)
    n = x.shape[1]
    r = _dot(tri_bf16, jnp.concatenate([x1, x2, x3], axis=1))
    return r[:, 0:n] + r[:, n:2 * n] + r[:, 2 * n:3 * n]


def _mixer_ab_chunk(xn_ref, xp_ref, g_ref, win_ref, bif_ref, sg_ref, sw_ref, sbt_ref, wout_ref,
                   y_ref, c_ref, n_ref, m_ref, z_cur, z_nxt, cat_scr):
    nb = xn_ref.shape[0]
    row = lax.broadcasted_iota(jnp.int32, (CHUNK, CHUNK), 0)
    col = lax.broadcasted_iota(jnp.int32, (CHUNK, CHUNK), 1)
    causal = col <= row
    keys_before = row <= col
    tri = jnp.where(causal, 1.0, 0.0).astype(BF16)
    scale = HEAD_DIM ** -0.5

    hn = _rms(xn_ref[...].reshape(nb * CHUNK, D_MODEL), g_ref[...]).astype(BF16)
    slabs = [(c0, min(c0 + PROJ_SLAB, A_IN_PAD)) for c0 in range(0, A_IN_PAD, PROJ_SLAB)]

    def project(count):
        for _ in range(min(count, len(slabs))):
            c0, c1 = slabs.pop(0)
            z_nxt[:, c0:c1] = _dot(hn, win_ref[:, c0:c1])

    sgu_bias = [jnp.broadcast_to(sbt_ref[:, g:g + 1], (CHUNK, CHUNK)) for g in range(HEADS)]
    vn = [_rms(z_cur[b * CHUNK:(b + 1) * CHUNK, COL_GV:COL_GV + WIDTH], sg_ref[...]) for b in range(nb)]
    for g in range(HEADS):
        gs = slice(g * CHUNK, (g + 1) * CHUNK)
        mixed = _dot(jnp.where(causal, sw_ref[g], 0.0).astype(BF16),
                     jnp.concatenate([vn[b][:, gs] for b in range(nb)], axis=1).astype(BF16))
        for b in range(nb):
            rows = slice(b * CHUNK, (b + 1) * CHUNK)
            u = z_cur[rows, COL_U + g * CHUNK:COL_U + (g + 1) * CHUNK]
            cat_scr[rows, WIDTH + g * CHUNK:WIDTH + (g + 1) * CHUNK] = (
                u * (mixed[:, b * CHUNK:(b + 1) * CHUNK] + sgu_bias[g])).astype(BF16)
    project(1)

    pieces_per_stage = -(-(len(slabs)) // (3 * nb))
    gate_terms = []
    for b in range(nb):
        rows = slice(b * CHUNK, (b + 1) * CHUNK)
        gates = z_cur[rows, COL_GATES:COL_GATES + LANES] + bif_ref[...]
        lg = jnp.where(col < HEADS, gates, _log_sigmoid(gates))
        gate_terms.append((lg, _exact_tri_dot(tri, lg)))
        project(pieces_per_stage)

    heads = []
    for b, (lg, fcum) in enumerate(gate_terms):
        rows = slice(b * CHUNK, (b + 1) * CHUNK)
        lg_t = lg.T
        fcum_t = fcum.T
        gmax = _cummax_lanes(lg_t[0:2 * HEADS, :] - jnp.concatenate([fcum_t[HEADS:2 * HEADS, :]] * 2, axis=0))
        for h in range(HEADS):
            q = z_cur[rows, h * HEAD_DIM:(h + 1) * HEAD_DIM]
            k = z_cur[rows, WIDTH + h * HEAD_DIM:WIDTH + (h + 1) * HEAD_DIM] * scale
            v_t = z_cur[rows, 2 * WIDTH + h * HEAD_DIM:2 * WIDTH + (h + 1) * HEAD_DIM].T
            f_r = fcum_t[HEADS + h:HEADS + h + 1, :]
            i_r = lg_t[h:h + 1, :]
            g_c = lg[:, h:h + 1] - fcum[:, HEADS + h:HEADS + h + 1]
            c0, n0, m0 = c_ref[b, h], n_ref[b, h], m_ref[b, h]
            m = f_r + jnp.maximum(m0, gmax[h:h + 1, :])
            inter = jnp.exp(m0 + f_r - m)
            qb, kb = q.astype(BF16), k.astype(BF16)
            against_q = _dot_nt(
                jnp.concatenate([k, c0, jnp.broadcast_to(n0, (BF16_ROWS, HEAD_DIM))], axis=0).astype(BF16), qb)
            m_last = m[:, CHUNK - 1:CHUNK]
            f_last = f_r[:, CHUNK - 1:CHUNK]
            w = jnp.exp(f_last - f_r + i_r - m_last)
            decay = jnp.exp(m0[:, 0:1] + f_last - m_last)
            against_k = _dot(
                jnp.concatenate([v_t * w, jnp.broadcast_to(w, (BF16_ROWS, CHUNK))], axis=0).astype(BF16), kb)
            c_ref[b, h] = decay * c0 + against_k[0:HEAD_DIM]
            n_ref[b, h] = decay * n0 + against_k[HEAD_DIM:HEAD_DIM + 1]
            m_ref[b, h] = jnp.broadcast_to(m_last, (1, LANES))
            heads.append((against_q, v_t, f_r, g_c, m, inter))
        project(pieces_per_stage)

    partial = []
    for i, (against_q, v_t, f_r, g_c, m, inter) in enumerate(heads):
        kq = against_q[0:CHUNK]
        cq = against_q[CHUNK:CHUNK + HEAD_DIM]
        nq = against_q[CHUNK + HEAD_DIM:CHUNK + HEAD_DIM + 1]
        s_t = kq * jnp.exp(jnp.where(keys_before, (f_r - m) + g_c, NEG_INF))
        den = jnp.sum(s_t, axis=0, keepdims=True) + inter * nq
        partial.append((_dot(v_t.astype(BF16), s_t.astype(BF16)), inter * cq,
                        jnp.maximum(jnp.abs(den), jnp.exp(-m))))
        if i % HEADS == HEADS - 1:
            project(pieces_per_stage)

    for i, (sv, carried, den) in enumerate(partial):
        b, h = divmod(i, HEADS)
        rows = slice(b * CHUNK, (b + 1) * CHUNK)
        o = z_cur[rows, 3 * WIDTH + h * HEAD_DIM:3 * WIDTH + (h + 1) * HEAD_DIM]
        cat_scr[rows, h * HEAD_DIM:(h + 1) * HEAD_DIM] = (jax.nn.sigmoid(o) * ((sv + carried) / den).T).astype(BF16)

    project(len(slabs))
    y = xp_ref[...].reshape(nb * CHUNK, D_MODEL) + _dot(cat_scr[...], wout_ref[...])
    y_ref[...] = y.reshape(nb, CHUNK, D_MODEL)


def _mixer_ab_body(*refs):
    *io_refs, z0_scr, z1_scr, cat_scr = refs
    c_ref, n_ref, m_ref = io_refs[-3:]
    step = pl.program_id(0)

    @pl.when(step <= 1)
    def _():
        c_ref[...] = jnp.zeros_like(c_ref)
        n_ref[...] = jnp.zeros_like(n_ref)
        m_ref[...] = jnp.zeros_like(m_ref)

    @pl.when(step == 0)
    def _():
        z1_scr[...] = jnp.zeros_like(z1_scr)

    @pl.when(step % 2 == 0)
    def _():
        _mixer_ab_chunk(*io_refs, z1_scr, z0_scr, cat_scr)

    @pl.when(step % 2 == 1)
    def _():
        _mixer_ab_chunk(*io_refs, z0_scr, z1_scr, cat_scr)


def _mixer_ab_prompt(x, g, w_in, b_if, sgu_g, sgu_w, sgu_bt, w_out):
    nb, s, _ = x.shape
    n_chunks = s // CHUNK
    blk = lambda index: pl.BlockSpec((nb, CHUNK, D_MODEL), index)
    nxt = blk(lambda c: (0, jnp.minimum(c, n_chunks - 1), 0))
    prev = blk(lambda c: (0, jnp.maximum(c - 1, 0), 0))
    z_shape = pltpu.VMEM((nb * CHUNK, A_IN_PAD), F32)
    return pl.pallas_call(
        _mixer_ab_body,
        grid=(n_chunks + 1,),
        in_specs=[nxt, prev, _resident((1, D_MODEL)), _resident(w_in.shape), _resident((1, LANES)),
                  _resident((1, WIDTH)), _resident(sgu_w.shape), _resident(sgu_bt.shape), _resident(w_out.shape)],
        out_specs=[prev,
                   pl.BlockSpec((nb, HEADS, HEAD_DIM, HEAD_DIM), lambda c: (0, 0, 0, 0)),
                   pl.BlockSpec((nb, HEADS, 1, HEAD_DIM), lambda c: (0, 0, 0, 0)),
                   pl.BlockSpec((nb, HEADS, 1, LANES), lambda c: (0, 0, 0, 0))],
        out_shape=[jax.ShapeDtypeStruct(x.shape, F32),
                   jax.ShapeDtypeStruct((nb, HEADS, HEAD_DIM, HEAD_DIM), F32),
                   jax.ShapeDtypeStruct((nb, HEADS, 1, HEAD_DIM), F32),
                   jax.ShapeDtypeStruct((nb, HEADS, 1, LANES), F32)],
        scratch_shapes=[z_shape, z_shape, pltpu.VMEM((nb * CHUNK, 2 * WIDTH), BF16)],
        compiler_params=_params(("arbitrary",)),
        name="mixer_ab_prompt",
    )(x, x, g.reshape(1, D_MODEL), w_in, b_if, sgu_g.reshape(1, WIDTH), sgu_w, sgu_bt, w_out)


def _alibi_slope(group, head):
    n = len(SWA_GROUPS) * HEADS
    return 2.0 ** (-8.0 * (group * HEADS + head + 1) / n)


def _slab_pitch(dil):
    return dil + 8 if dil % 8 == 0 else dil


def _swa_group_body(x_ref, g_ref, w_ref, o_ref, st_ref, slab_scr, q_scr, k_scr, v_scr, *, group, dil, tile):
    step = pl.program_id(1)
    sub = tile // dil
    nsb = sub // CHUNK
    rows_per_proj = min(tile, 512)
    res_per_proj = rows_per_proj // sub
    scale = HEAD_DIM ** -0.5
    qi = lax.broadcasted_iota(jnp.int32, (CHUNK, 2 * CHUNK), 0)
    kc = lax.broadcasted_iota(jnp.int32, (CHUNK, 2 * CHUNK), 1)
    delta = CHUNK + qi - kc
    valid = (delta >= 0) & (delta <= CHUNK)
    valid_first = valid & (kc >= jnp.where(step > 0, 0, CHUNK))
    dist = (delta * dil).astype(F32)
    lane = lax.broadcasted_iota(jnp.int32, (CHUNK, LANES), 1)

    @pl.when(step == 0)
    def _():
        k_scr[:, 0:CHUNK, :] = jnp.zeros((dil, CHUNK, WIDTH), BF16)
        v_scr[:, 0:CHUNK, :] = jnp.zeros((dil, CHUNK, WIDTH), BF16)

    pitch = _slab_pitch(dil)
    for c in range(tile // 256):
        hn = _rms(x_ref[0, c * 256:(c + 1) * 256, :], g_ref[...])
        groups = [(0, 256)] if pitch == dil else [(g * dil, dil) for g in range(256 // dil)]
        for first, count in groups:
            dst = (c * 256 + first) // dil * pitch
            for sl in range(D_MODEL // LANES):
                slab_scr[sl, dst:dst + count, :] = hn[first:first + count, sl * LANES:(sl + 1) * LANES]

    def token_rows(r, first, count):
        start = first * dil + r
        return slice(start, start + count) if dil == 1 else pl.ds(start, count, stride=dil)

    def project(p):
        parts = []
        for r in range(p * res_per_proj, (p + 1) * res_per_proj):
            src = slice(0, sub) if dil == 1 else pl.ds(r, sub, stride=pitch)
            parts.append(jnp.concatenate([slab_scr[sl, src, :] for sl in range(D_MODEL // LANES)],
                                         axis=1).astype(BF16))
        hn = parts[0] if len(parts) == 1 else jnp.concatenate(parts, axis=0)
        for j, scr in enumerate((q_scr, k_scr, v_scr)):
            z = _dot(hn, w_ref[:, j * WIDTH:(j + 1) * WIDTH]).astype(BF16)
            for i, r in enumerate(range(p * res_per_proj, (p + 1) * res_per_proj)):
                off = 0 if j == 0 else CHUNK
                scr[r, off:off + sub, :] = z[i * sub:(i + 1) * sub, :]

    def scores(p):
        out = []
        for r in range(p * res_per_proj, (p + 1) * res_per_proj):
            for j in range(nsb):
                mask = valid_first if j == 0 else valid
                for h in range(HEADS):
                    hs = slice(h * HEAD_DIM, (h + 1) * HEAD_DIM)
                    s = _dot_nt(q_scr[r, j * CHUNK:(j + 1) * CHUNK, hs], k_scr[r, j * CHUNK:(j + 2) * CHUNK, hs])
                    s = jnp.where(mask, s * scale + (-_alibi_slope(group, h)) * dist, NEG_INF)
                    m = jnp.max(s, axis=-1, keepdims=True)
                    p_ = jnp.exp(s - m)
                    out.append((r, j, h, m, jnp.sum(p_, axis=-1, keepdims=True), p_.astype(BF16)))
        return out

    def values(items):
        stats = None
        for r, j, h, m, l, p_ in items:
            hs = slice(h * HEAD_DIM, (h + 1) * HEAD_DIM)
            o_ref[0, h, token_rows(r, j * CHUNK, CHUNK), :] = _dot(p_, v_scr[r, j * CHUNK:(j + 2) * CHUNK, hs])
            base = jnp.zeros((CHUNK, LANES), F32) if h == 0 else stats
            stats = jnp.where(lane == h, m, jnp.where(lane == HEADS + h, l, base))
            if h == HEADS - 1:
                st_ref[0, token_rows(r, j * CHUNK, CHUNK), :] = stats

    n_proj = tile // rows_per_proj
    project(0)
    pending = None
    for p in range(n_proj):
        items = scores(p)
        if p + 1 < n_proj:
            project(p + 1)
        if pending is not None:
            values(pending)
        pending = items
    values(pending)
    for r in range(dil):
        k_scr[r, 0:CHUNK, :] = k_scr[r, sub:sub + CHUNK, :]
        v_scr[r, 0:CHUNK, :] = v_scr[r, sub:sub + CHUNK, :]


def _swa_group_prompt(x, g, c_in, *, group, tile):
    nb, s, _ = x.shape
    dil = SWA_GROUPS[group][1]
    sub = tile // dil
    return pl.pallas_call(
        functools.partial(_swa_group_body, group=group, dil=dil, tile=tile),
        grid=(nb, s // tile),
        in_specs=[pl.BlockSpec((1, tile, D_MODEL), lambda b, t: (b, t, 0)), _resident((1, D_MODEL)),
                  pl.BlockSpec((D_MODEL, C_GROUP_COLS), lambda b, t: (0, group), pipeline_mode=pl.Buffered(1))],
        out_specs=[pl.BlockSpec((1, HEADS, tile, HEAD_DIM), lambda b, t: (b, 0, t, 0)),
                   pl.BlockSpec((1, tile, LANES), lambda b, t: (b, t, 0))],
        out_shape=[jax.ShapeDtypeStruct((nb, HEADS, s, HEAD_DIM), F32),
                   jax.ShapeDtypeStruct((nb, s, LANES), F32)],
        scratch_shapes=[pltpu.VMEM((D_MODEL // LANES, sub * _slab_pitch(dil), LANES), F32),
                        pltpu.VMEM((dil, sub, WIDTH), BF16),
                        pltpu.VMEM((dil, CHUNK + sub, WIDTH), BF16),
                        pltpu.VMEM((dil, CHUNK + sub, WIDTH), BF16)],
        compiler_params=_params(("arbitrary", "arbitrary")),
        name="swa_group%d" % group,
    )(x, g.reshape(1, D_MODEL), c_in)


def _kv_tail_body(x_ref, g_ref, wk_ref, wv_ref, kv_ref):
    tm = x_ref.shape[1]
    hn = _rms(x_ref[0], g_ref[...]).astype(BF16)
    for j, w_ref in enumerate((wk_ref, wv_ref)):
        kv = _dot(hn, w_ref[...])
        for h in range(HEADS):
            kv_ref[pl.ds(j * HEADS + h, tm, stride=2 * HEADS), :] = kv[:, h * HEAD_DIM:(h + 1) * HEAD_DIM]


def _kv_tail(x, g, c_in, *, group):
    win = SWA_GROUPS[group][0]
    nb, s, _ = x.shape
    tm = min(win, 512)
    first = (s - win) // tm
    steps = win // tm
    wcol = lambda j: pl.BlockSpec((D_MODEL, WIDTH), lambda b, t: (0, 3 * group + j), pipeline_mode=pl.Buffered(1))
    kv = pl.pallas_call(
        _kv_tail_body,
        grid=(nb, steps),
        in_specs=[pl.BlockSpec((1, tm, D_MODEL), lambda b, t: (b, first + t, 0)), _resident((1, D_MODEL)),
                  wcol(1), wcol(2)],
        out_specs=pl.BlockSpec((tm * 2 * HEADS, HEAD_DIM), lambda b, t: (b * steps + t, 0)),
        out_shape=jax.ShapeDtypeStruct((nb * win * 2 * HEADS, HEAD_DIM), F32),
        compiler_params=_params(("arbitrary", "arbitrary")),
        name="kv_tail%d" % group,
    )(x, g.reshape(1, D_MODEL), c_in, c_in)
    return kv.reshape(1, nb, win, 2, HEADS, HEAD_DIM)


def _swa_merge_body(o0_ref, o1_ref, o2_ref, s0_ref, s1_ref, s2_ref, x_ref, w_ref, y_ref, cat_scr):
    t = x_ref.shape[1]
    rows_per_pass = 256
    for c in range(t // rows_per_pass):
        rows = slice(c * rows_per_pass, (c + 1) * rows_per_pass)
        stats = [s_ref[0, rows, :] for s_ref in (s0_ref, s1_ref, s2_ref)]
        for h in range(HEADS):
            ms = [st[:, h:h + 1] for st in stats]
            ls = [st[:, HEADS + h:HEADS + h + 1] for st in stats]
            top = functools.reduce(jnp.maximum, ms)
            ws = [jnp.exp(m - top) for m in ms]
            num = sum(w * o_ref[0, h, rows, :] for w, o_ref in zip(ws, (o0_ref, o1_ref, o2_ref)))
            den = sum(w * l for w, l in zip(ws, ls))
            cat_scr[rows, h * HEAD_DIM:(h + 1) * HEAD_DIM] = (num / den).astype(BF16)
    y_ref[0] = x_ref[0] + _dot(cat_scr[...], w_ref[...])


def _swa_merge(outs, stats, x, w_out, *, tm):
    nb, s, _ = x.shape
    o_spec = pl.BlockSpec((1, HEADS, tm, HEAD_DIM), lambda b, t: (b, 0, t, 0))
    s_spec = pl.BlockSpec((1, tm, LANES), lambda b, t: (b, t, 0))
    x_spec = pl.BlockSpec((1, tm, D_MODEL), lambda b, t: (b, t, 0))
    return pl.pallas_call(
        _swa_merge_body,
        grid=(nb, s // tm),
        in_specs=[o_spec] * 3 + [s_spec] * 3 + [x_spec, _resident(w_out.shape)],
        out_specs=x_spec,
        out_shape=jax.ShapeDtypeStruct(x.shape, F32),
        scratch_shapes=[pltpu.VMEM((tm, WIDTH), BF16)],
        compiler_params=_params(("arbitrary", "arbitrary")),
        name="swa_merge",
    )(*outs, *stats, x, w_out)


def _proj_body(x_ref, g_ref, w_ref, z_ref):
    z_ref[...] = _dot(_rms(x_ref[...], g_ref[...]).astype(BF16), w_ref[...])


def _proj(x, g, w):
    m, n = x.shape[0], w.shape[1]
    return pl.pallas_call(
        _proj_body,
        grid=(1,),
        in_specs=[_resident(x.shape), _resident((1, D_MODEL)), _resident(w.shape)],
        out_specs=pl.BlockSpec((m, n), lambda i: (0, 0)),
        out_shape=jax.ShapeDtypeStruct((m, n), F32),
        compiler_params=_params(("arbitrary",)),
        name="proj_rows",
    )(x, g.reshape(1, D_MODEL), w)


def _out_proj_body(c_ref, w_ref, x_ref, y_ref):
    y_ref[...] = x_ref[...] + _dot(c_ref[...].astype(BF16), w_ref[...])


def _out_proj(cat, w, x):
    return pl.pallas_call(
        _out_proj_body,
        grid=(1,),
        in_specs=[_resident(cat.shape), _resident(w.shape), _resident(x.shape)],
        out_specs=pl.BlockSpec(x.shape, lambda i: (0, 0)),
        out_shape=jax.ShapeDtypeStruct(x.shape, F32),
        compiler_params=_params(("arbitrary",)),
        name="out_proj_rows",
    )(cat, w, x)


def _mixer_ab_step_body(z_ref, bif_ref, sg_ref, w00_ref, b0_ref, c_ref, n_ref, m_ref,
                        cat_ref, c1_ref, n1_ref, m1_ref, vn_ref):
    nb = z_ref.shape[0]
    scale = HEAD_DIM ** -0.5
    eye = (lax.broadcasted_iota(jnp.int32, (HEAD_DIM, HEAD_DIM), 0)
           == lax.broadcasted_iota(jnp.int32, (HEAD_DIM, HEAD_DIM), 1)).astype(F32)
    lane = lax.broadcasted_iota(jnp.int32, (1, LANES), 1)
    for i in range(nb):
        zr = z_ref[i:i + 1, :]
        gates = zr[:, COL_GATES:COL_GATES + LANES] + bif_ref[...]
        m_new = jnp.zeros((1, LANES), F32)
        for h in range(HEADS):
            hs = slice(h * HEAD_DIM, (h + 1) * HEAD_DIM)
            q = zr[:, hs]
            k = zr[:, WIDTH + h * HEAD_DIM:WIDTH + (h + 1) * HEAD_DIM] * scale
            v = zr[:, 2 * WIDTH + h * HEAD_DIM:2 * WIDTH + (h + 1) * HEAD_DIM]
            o = zr[:, 3 * WIDTH + h * HEAD_DIM:3 * WIDTH + (h + 1) * HEAD_DIM]
            ig = gates[:, h:h + 1]
            lf = _log_sigmoid(gates[:, HEADS + h:HEADS + h + 1])
            c0 = c_ref[i, h]
            n0 = n_ref[i, h]
            m0 = m_ref[i:i + 1, h:h + 1]
            a = m0 + lf
            m = jnp.maximum(a, ig)
            s = jnp.sum(q * k, axis=-1, keepdims=True) * jnp.exp(ig - m)
            inter = jnp.exp(a - m)
            cq_col = jnp.sum(c0 * q, axis=-1, keepdims=True)
            cq = jnp.sum(eye * cq_col, axis=0, keepdims=True)
            v_col = jnp.sum(eye * v, axis=-1, keepdims=True)
            num = s * v + inter * cq
            den = s + inter * jnp.sum(n0 * q, axis=-1, keepdims=True)
            hh = num / jnp.maximum(jnp.abs(den), jnp.exp(-m))
            w = jnp.exp(ig - m)
            c1_ref[i, h] = inter * c0 + (w * v_col) * k
            n1_ref[i, h] = inter * n0 + w * k
            m_new = jnp.where(lane == h, m, m_new)
            cat_ref[i:i + 1, hs] = jax.nn.sigmoid(o) * hh
        m1_ref[i:i + 1, :] = m_new
        vn = _rms(zr[:, COL_GV:COL_GV + WIDTH], sg_ref[...])
        vn_ref[i:i + 1, :] = vn
        cat_ref[i:i + 1, WIDTH:2 * WIDTH] = zr[:, COL_U:COL_U + WIDTH] * (w00_ref[...] * vn + b0_ref[...])


def _mixer_ab_step(z, b_if, sgu_g, w00, b0, st_c, st_n, st_m, *, nb):
    n = z.shape[0]
    rows = lambda w: pl.BlockSpec((nb, w), lambda i: (i, 0))
    c_spec = pl.BlockSpec((nb, HEADS, HEAD_DIM, HEAD_DIM), lambda i: (i, 0, 0, 0))
    n_spec = pl.BlockSpec((nb, HEADS, 1, HEAD_DIM), lambda i: (i, 0, 0, 0))
    return pl.pallas_call(
        _mixer_ab_step_body,
        grid=(n // nb,),
        in_specs=[rows(A_IN_PAD), _resident((1, LANES)), _resident((1, WIDTH)), _resident((1, WIDTH)),
                  _resident((1, WIDTH)), c_spec, n_spec, rows(LANES)],
        out_specs=[rows(2 * WIDTH), c_spec, n_spec, rows(LANES), rows(WIDTH)],
        out_shape=[jax.ShapeDtypeStruct((n, 2 * WIDTH), F32),
                   jax.ShapeDtypeStruct(st_c.shape, F32),
                   jax.ShapeDtypeStruct(st_n.shape, F32),
                   jax.ShapeDtypeStruct((n, LANES), F32),
                   jax.ShapeDtypeStruct((n, WIDTH), F32)],
        compiler_params=_params(("arbitrary",)),
        name="mixer_ab_step",
    )(z, b_if, sgu_g.reshape(1, WIDTH), w00, b0, st_c, st_n, st_m)


def _swa_step_body(z_ref, kv0_ref, kv1_ref, kv2_ref, cat_ref):
    nb = z_ref.shape[0]
    scale = HEAD_DIM ** -0.5
    steps = (CHUNK - lax.broadcasted_iota(jnp.int32, (CHUNK, 1, 1), 0)).astype(F32)
    head = lax.broadcasted_iota(jnp.int32, (1, HEADS, 1), 1)
    for i in range(nb):
        ms, ls, os_ = [], [], []
        for gi, kv_ref in enumerate((kv0_ref, kv1_ref, kv2_ref)):
            dil = SWA_GROUPS[gi][1]
            base = gi * 3 * HEADS
            q = z_ref[i, base:base + HEADS, :]
            k_new = z_ref[i, base + HEADS:base + 2 * HEADS, :]
            v_new = z_ref[i, base + 2 * HEADS:base + 3 * HEADS, :]
            kc = kv_ref[i, :, 0, 0, :, :]
            vc = kv_ref[i, :, 0, 1, :, :]
            slope = jnp.zeros((1, HEADS, 1), F32)
            for h in range(HEADS):
                slope = jnp.where(head == h, _alibi_slope(gi, h) * dil, slope)
            s = jnp.sum(kc * q[None], axis=-1, keepdims=True) * scale - slope * steps
            s_new = jnp.sum(k_new * q, axis=-1, keepdims=True) * scale
            m = jnp.maximum(jnp.max(s, axis=0), s_new)
            p = jnp.exp(s - m[None])
            p_new = jnp.exp(s_new - m)
            ms.append(m)
            ls.append(jnp.sum(p, axis=0) + p_new)
            os_.append(jnp.sum(p * vc, axis=0) + p_new * v_new)
        top = functools.reduce(jnp.maximum, ms)
        ws = [jnp.exp(m - top) for m in ms]
        num = sum(w * o for w, o in zip(ws, os_))
        den = sum(w * l for w, l in zip(ws, ls))
        cat_ref[i] = num / den


def _swa_step(z, caches, *, nb):
    n = z.shape[0]
    views = []
    specs = []
    for (win, dil), cache in zip(SWA_GROUPS, caches):
        views.append(cache.reshape(n, win // dil, dil, 2, HEADS, HEAD_DIM))
        specs.append(pl.BlockSpec((nb, CHUNK, 1, 2, HEADS, HEAD_DIM), lambda i: (i, 0, 0, 0, 0, 0)))
    return pl.pallas_call(
        _swa_step_body,
        grid=(n // nb,),
        in_specs=[pl.BlockSpec((nb,) + z.shape[1:], lambda i: (i, 0, 0))] + specs,
        out_specs=pl.BlockSpec((nb, HEADS, HEAD_DIM), lambda i: (i, 0, 0)),
        out_shape=jax.ShapeDtypeStruct((n, HEADS, HEAD_DIM), F32),
        compiler_params=_params(("arbitrary",)),
        name="swa_step",
    )(z, *views)


def _pad_cols(w, n):
    return jnp.pad(w, ((0, 0), (0, n - w.shape[1])))


def kernel(x_prompt, x_sample, state_mlstm_C, state_mlstm_n, state_mlstm_m, cache_swa_kv0, cache_swa_kv1, cache_swa_kv2, norm_g, ffn_w_gate, ffn_w_up, ffn_w_down, a_w_in, a_b_if, sgu_norm_g, sgu_w, sgu_b, a_w_out, c_w_in, c_w_out, final_norm_g):
    nb, s, _ = x_prompt.shape
    ns = x_sample.shape[0]
    assert x_sample.shape[1] == 1 and s % max(w for w, _ in SWA_GROUPS) == 0
    for (win, dil), cache in zip(SWA_GROUPS, (cache_swa_kv0, cache_swa_kv1, cache_swa_kv2)):
        assert cache.shape[2] == win and win // dil == CHUNK

    ffn_w = (jnp.swapaxes(ffn_w_gate, 2, 3), jnp.swapaxes(ffn_w_up, 2, 3), ffn_w_down)
    g_lo, g_hi = 4 * WIDTH, 4 * WIDTH + 2 * HEADS
    a_in = jnp.concatenate([a_w_in[0][:, :g_lo], a_w_in[0][:, g_hi:], _pad_cols(a_w_in[0][:, g_lo:g_hi], LANES)],
                           axis=1).astype(BF16)
    b_if = _pad_cols(a_b_if[0].reshape(1, 2 * HEADS), LANES)
    a_out = a_w_out[0].astype(BF16)
    c_in = c_w_in[0].astype(BF16)
    c_out = c_w_out[0].astype(BF16)
    sgu_bt = _pad_cols(sgu_b[0].T, LANES)
    sgu_w00 = jnp.repeat(sgu_w[0, :, 0, 0], CHUNK).reshape(1, WIDTH)
    sgu_b0 = jnp.repeat(sgu_b[0, :, 0], CHUNK).reshape(1, WIDTH)

    xp = x_prompt.reshape(nb * s, D_MODEL)
    xs = x_sample.reshape(ns, D_MODEL)
    xp, xs = _ffn(xp, xs, norm_g[0, 0], ffn_w, 0, 0, tm=512)
    xp, p_c, p_n, p_m = _mixer_ab_prompt(xp.reshape(nb, s, D_MODEL), norm_g[0, 1], a_in, b_if,
                                          sgu_norm_g[0], sgu_w[0], sgu_bt, a_out)
    z = _proj(xs, norm_g[0, 1], a_in)
    cat, s_c, s_n, s_m, s_v = _mixer_ab_step(
        z, b_if, sgu_norm_g[0], sgu_w00, sgu_b0, state_mlstm_C[0],
        state_mlstm_n[0].reshape(ns, HEADS, 1, HEAD_DIM), _pad_cols(state_mlstm_m[0], LANES), nb=8)
    xs = _out_proj(cat, a_out, xs)
    xp, xs = _ffn(xp.reshape(nb * s, D_MODEL), xs, norm_g[0, 2], ffn_w, 0, 1, tm=512)
    xp, xs = _ffn(xp, xs, norm_g[1, 0], ffn_w, 1, 0, tm=512)
    xp3 = xp.reshape(nb, s, D_MODEL)
    outs, stats = [], []
    for gi, tile in enumerate((512, 1024, 2048)):
        o, st = _swa_group_prompt(xp3, norm_g[1, 1], c_in, group=gi, tile=tile)
        outs.append(o)
        stats.append(st)
    p_kv = [_kv_tail(xp3, norm_g[1, 1], c_in, group=gi) for gi in range(3)]
    xp = _swa_merge(outs, stats, xp3, c_out, tm=1024).reshape(nb * s, D_MODEL)
    z = _proj(xs, norm_g[1, 1], c_in)
    cat = _swa_step(z.reshape(ns, 3 * 3 * HEADS, HEAD_DIM), (cache_swa_kv0, cache_swa_kv1, cache_swa_kv2), nb=4)
    xs = _out_proj(cat.reshape(ns, WIDTH), c_out, xs)
    y_prompt, y_sample = _ffn(xp, xs, norm_g[1, 2], ffn_w, 1, 1, final_norm_g, tm=512)
    y_prompt = y_prompt.reshape(nb, s, D_MODEL)
    y_sample = y_sample.reshape(ns, 1, D_MODEL)
    s_kv = [z[:, gi * C_GROUP_COLS + WIDTH:(gi + 1) * C_GROUP_COLS].reshape(1, ns, 1, 2, HEADS, HEAD_DIM)
            for gi in range(3)]

    return (y_prompt, y_sample,
            p_c.reshape(1, nb, HEADS, HEAD_DIM, HEAD_DIM), p_n.reshape(1, nb, HEADS, HEAD_DIM),
            p_m[:, :, 0, 0].reshape(1, nb, HEADS),
            s_c.reshape(1, ns, HEADS, HEAD_DIM, HEAD_DIM), s_n.reshape(1, ns, HEADS, HEAD_DIM),
            s_m[:, :HEADS].reshape(1, ns, HEADS), s_v.reshape(1, ns, 1, WIDTH),
            p_kv[0], p_kv[1], p_kv[2], s_kv[0], s_kv[1], s_kv[2])
```

```python
import functools

import jax
import jax.numpy as jnp
from jax import lax
from jax.experimental import pallas as pl
from jax.experimental.pallas import tpu as pltpu

F32 = jnp.float32
BF16 = jnp.bfloat16

D_MODEL = 1024
D_FF = 2752
HEADS = 4
HEAD_DIM = 128
WIDTH = HEADS * HEAD_DIM
CHUNK = 128
SWA_GROUPS = ((128, 1), (512, 4), (2048, 16))
NORM_EPS = 1e-6
NEG_INF = -1e30

LANES = 128
BF16_ROWS = 16
PROJ_SLAB = 256
PROJ_ROWS = 512
FF_CHUNK = 256
A_IN_PAD = 4 * WIDTH + 2 * WIDTH + LANES
COL_U = 4 * WIDTH
COL_GV = 5 * WIDTH
COL_GATES = 6 * WIDTH
C_GROUP_COLS = 3 * WIDTH

VMEM_LIMIT = 60 * 1024 * 1024


def _params(semantics):
    return pltpu.CompilerParams(dimension_semantics=semantics, vmem_limit_bytes=VMEM_LIMIT)


def _resident(shape):
    nd = len(shape)
    return pl.BlockSpec(shape, lambda *_: (0,) * nd, pipeline_mode=pl.Buffered(1))


def _rms(x, g):
    ms = jnp.mean(x * x, axis=-1, keepdims=True)
    return x * lax.rsqrt(ms + NORM_EPS) * g


def _dot(a, b):
    return jnp.dot(a, b, preferred_element_type=F32)


def _dot_nt(a, b):
    return lax.dot_general(a, b, (((1,), (1,)), ((), ())), preferred_element_type=F32)


def _log_sigmoid(x):
    return jnp.minimum(x, 0.0) - jnp.log1p(jnp.exp(-jnp.abs(x)))


def _swiglu_rows(h_scr, rows, wg_ref, wu_ref, wd_ref):
    acc = None
    for c0 in range(0, D_FF, FF_CHUNK):
        cols = slice(c0, min(c0 + FF_CHUNK, D_FF))
        h = h_scr[0:rows, :]
        gate = _dot_nt(h, wg_ref[0, 0, cols, :])
        up = _dot_nt(h, wu_ref[0, 0, cols, :])
        act = (gate * jax.nn.sigmoid(gate) * up).astype(h_scr.dtype)
        part = _dot(act, wd_ref[0, 0, cols, :])
        acc = part if acc is None else acc + part
    return acc


def _ffn_body(*refs, final):
    if final:
        x_ref, xs_ref, g_ref, wg_ref, wu_ref, wd_ref, fg_ref, o_ref, os_ref, h_scr = refs
    else:
        x_ref, xs_ref, g_ref, wg_ref, wu_ref, wd_ref, o_ref, os_ref, h_scr = refs
    tm, ns = x_ref.shape[0], xs_ref.shape[0]
    last = pl.num_programs(0) - 1

    def finish(x, acc):
        y = x + 0.5 * acc
        return _rms(y, fg_ref[...]) if final else y

    x = x_ref[...]
    h_scr[0:tm, :] = _rms(x, g_ref[...]).astype(h_scr.dtype)

    @pl.when(pl.program_id(0) != last)
    def _():
        o_ref[...] = finish(x, _swiglu_rows(h_scr, tm, wg_ref, wu_ref, wd_ref))

    @pl.when(pl.program_id(0) == last)
    def _():
        xs = xs_ref[...]
        h_scr[tm:tm + ns, :] = _rms(xs, g_ref[...]).astype(h_scr.dtype)
        acc = _swiglu_rows(h_scr, tm + ns, wg_ref, wu_ref, wd_ref)
        o_ref[...] = finish(x, acc[0:tm])
        os_ref[...] = finish(xs, acc[tm:tm + ns])


def _ffn(x, xs, g, weights, layer, which, final_g=None, *, tm):
    m, ns = x.shape[0], xs.shape[0]
    final = final_g is not None
    wg, wu, wd = weights
    row = pl.BlockSpec((tm, D_MODEL), lambda i: (i, 0))
    wspec = lambda w: pl.BlockSpec((1, 1) + w.shape[2:], lambda i: (layer, which, 0, 0), pipeline_mode=pl.Buffered(1))
    in_specs = [row, _resident(xs.shape), _resident((1, D_MODEL)), wspec(wg), wspec(wu), wspec(wd)]
    args = [x, xs, g.reshape(1, D_MODEL), wg, wu, wd]
    if final:
        in_specs.append(_resident((1, D_MODEL)))
        args.append(final_g.reshape(1, D_MODEL))
    return pl.pallas_call(
        functools.partial(_ffn_body, final=final),
        grid=(m // tm,),
        in_specs=in_specs,
        out_specs=[row, pl.BlockSpec(xs.shape, lambda i: (0, 0))],
        out_shape=[jax.ShapeDtypeStruct((m, D_MODEL), F32), jax.ShapeDtypeStruct(xs.shape, F32)],
        scratch_shapes=[pltpu.VMEM((tm + ns, D_MODEL), wg.dtype)],
        compiler_params=_params(("arbitrary",)),
        name="ffn_final" if final else "ffn",
    )(*args)


def _cummax_lanes(x):
    lane = lax.broadcasted_iota(jnp.int32, x.shape, 1)
    d = 1
    while d < x.shape[1]:
        x = jnp.maximum(x, jnp.where(lane >= d, pltpu.roll(x, d, axis=1), NEG_INF))
        d *= 2
    return x


def _exact_tri_dot(tri_bf16, x):
    x1 = x.astype(BF16)
    r1 = x - x1.astype(F32)
    x2 = r1.astype(BF16)
    x3 = (r1 - x2.astype(F32)).astype(BF16)
    n = x.shape[1]
    r = _dot(tri_bf16, jnp.concatenate([x1, x2, x3], axis=1))
    return r[:, 0:n] + r[:, n:2 * n] + r[:, 2 * n:3 * n]


def _mixer_ab_chunk(xn_ref, xp_ref, g_ref, win_ref, bif_ref, sg_ref, sw_ref, sbt_ref, wout_ref,
                   y_ref, c_ref, n_ref, m_ref, z_cur, z_nxt, cat_scr):
    nb = xn_ref.shape[0]
    row = lax.broadcasted_iota(jnp.int32, (CHUNK, CHUNK), 0)
    col = lax.broadcasted_iota(jnp.int32, (CHUNK, CHUNK), 1)
    causal = col <= row
    keys_before = row <= col
    tri = jnp.where(causal, 1.0, 0.0).astype(BF16)
    scale = HEAD_DIM ** -0.5

    hn = _rms(xn_ref[...].reshape(nb * CHUNK, D_MODEL), g_ref[...]).astype(BF16)
    slabs = [(c0, min(c0 + PROJ_SLAB, A_IN_PAD)) for c0 in range(0, A_IN_PAD, PROJ_SLAB)]

    def project(count):
        for _ in range(min(count, len(slabs))):
            c0, c1 = slabs.pop(0)
            z_nxt[:, c0:c1] = _dot(hn, win_ref[:, c0:c1])

    sgu_bias = [jnp.broadcast_to(sbt_ref[:, g:g + 1], (CHUNK, CHUNK)) for g in range(HEADS)]
    vn = [_rms(z_cur[b * CHUNK:(b + 1) * CHUNK, COL_GV:COL_GV + WIDTH], sg_ref[...]) for b in range(nb)]
    for g in range(HEADS):
        gs = slice(g * CHUNK, (g + 1) * CHUNK)
        mixed = _dot(jnp.where(causal, sw_ref[g], 0.0).astype(BF16),
                     jnp.concatenate([vn[b][:, gs] for b in range(nb)], axis=1).astype(BF16))
        for b in range(nb):
            rows = slice(b * CHUNK, (b + 1) * CHUNK)
            u = z_cur[rows, COL_U + g * CHUNK:COL_U + (g + 1) * CHUNK]
            cat_scr[rows, WIDTH + g * CHUNK:WIDTH + (g + 1) * CHUNK] = (
                u * (mixed[:, b * CHUNK:(b + 1) * CHUNK] + sgu_bias[g])).astype(BF16)
    project(1)

    pieces_per_stage = -(-(len(slabs)) // (3 * nb))
    gate_terms = []
    for b in range(nb):
        rows = slice(b * CHUNK, (b + 1) * CHUNK)
        gates = z_cur[rows, COL_GATES:COL_GATES + LANES] + bif_ref[...]
        lg = jnp.where(col < HEADS, gates, _log_sigmoid(gates))
        gate_terms.append((lg, _exact_tri_dot(tri, lg)))
        project(pieces_per_stage)

    heads = []
    for b, (lg, fcum) in enumerate(gate_terms):
        rows = slice(b * CHUNK, (b + 1) * CHUNK)
        lg_t = lg.T
        fcum_t = fcum.T
        gmax = _cummax_lanes(lg_t[0:2 * HEADS, :] - jnp.concatenate([fcum_t[HEADS:2 * HEADS, :]] * 2, axis=0))
        for h in range(HEADS):
            q = z_cur[rows, h * HEAD_DIM:(h + 1) * HEAD_DIM]
            k = z_cur[rows, WIDTH + h * HEAD_DIM:WIDTH + (h + 1) * HEAD_DIM] * scale
            v_t = z_cur[rows, 2 * WIDTH + h * HEAD_DIM:2 * WIDTH + (h + 1) * HEAD_DIM].T
            f_r = fcum_t[HEADS + h:HEADS + h + 1, :]
            i_r = lg_t[h:h + 1, :]
            g_c = lg[:, h:h + 1] - fcum[:, HEADS + h:HEADS + h + 1]
            c0, n0, m0 = c_ref[b, h], n_ref[b, h], m_ref[b, h]
            m = f_r + jnp.maximum(m0, gmax[h:h + 1, :])
            inter = jnp.exp(m0 + f_r - m)
            qb, kb = q.astype(BF16), k.astype(BF16)
            against_q = _dot_nt(
                jnp.concatenate([k, c0, jnp.broadcast_to(n0, (BF16_ROWS, HEAD_DIM))], axis=0).astype(BF16), qb)
            m_last = m[:, CHUNK - 1:CHUNK]
            f_last = f_r[:, CHUNK - 1:CHUNK]
            w = jnp.exp(f_last - f_r + i_r - m_last)
            decay = jnp.exp(m0[:, 0:1] + f_last - m_last)
            against_k = _dot(
                jnp.concatenate([v_t * w, jnp.broadcast_to(w, (BF16_ROWS, CHUNK))], axis=0).astype(BF16), kb)
            c_ref[b, h] = decay * c0 + against_k[0:HEAD_DIM]
            n_ref[b, h] = decay * n0 + against_k[HEAD_DIM:HEAD_DIM + 1]
            m_ref[b, h] = jnp.broadcast_to(m_last, (1, LANES))
            heads.append((against_q, v_t, f_r, g_c, m, inter))
        project(pieces_per_stage)

    partial = []
    for i, (against_q, v_t, f_r, g_c, m, inter) in enumerate(heads):
        kq = against_q[0:CHUNK]
        cq = against_q[CHUNK:CHUNK + HEAD_DIM]
        nq = against_q[CHUNK + HEAD_DIM:CHUNK + HEAD_DIM + 1]
        s_t = kq * jnp.exp(jnp.where(keys_before, (f_r - m) + g_c, NEG_INF))
        den = jnp.sum(s_t, axis=0, keepdims=True) + inter * nq
        partial.append((_dot(v_t.astype(BF16), s_t.astype(BF16)), inter * cq,
                        jnp.maximum(jnp.abs(den), jnp.exp(-m))))
        if i % HEADS == HEADS - 1:
            project(pieces_per_stage)

    for i, (sv, carried, den) in enumerate(partial):
        b, h = divmod(i, HEADS)
        rows = slice(b * CHUNK, (b + 1) * CHUNK)
        o = z_cur[rows, 3 * WIDTH + h * HEAD_DIM:3 * WIDTH + (h + 1) * HEAD_DIM]
        cat_scr[rows, h * HEAD_DIM:(h + 1) * HEAD_DIM] = (jax.nn.sigmoid(o) * ((sv + carried) / den).T).astype(BF16)

    project(len(slabs))
    y = xp_ref[...].reshape(nb * CHUNK, D_MODEL) + _dot(cat_scr[...], wout_ref[...])
    y_ref[...] = y.reshape(nb, CHUNK, D_MODEL)


def _mixer_ab_body(*refs):
    *io_refs, z0_scr, z1_scr, cat_scr = refs
    c_ref, n_ref, m_ref = io_refs[-3:]
    step = pl.program_id(0)

    @pl.when(step <= 1)
    def _():
        c_ref[...] = jnp.zeros_like(c_ref)
        n_ref[...] = jnp.zeros_like(n_ref)
        m_ref[...] = jnp.zeros_like(m_ref)

    @pl.when(step == 0)
    def _():
        z1_scr[...] = jnp.zeros_like(z1_scr)

    @pl.when(step % 2 == 0)
    def _():
        _mixer_ab_chunk(*io_refs, z1_scr, z0_scr, cat_scr)

    @pl.when(step % 2 == 1)
    def _():
        _mixer_ab_chunk(*io_refs, z0_scr, z1_scr, cat_scr)


def _mixer_ab_prompt(x, g, w_in, b_if, sgu_g, sgu_w, sgu_bt, w_out):
    nb, s, _ = x.shape
    n_chunks = s // CHUNK
    blk = lambda index: pl.BlockSpec((nb, CHUNK, D_MODEL), index)
    nxt = blk(lambda c: (0, jnp.minimum(c, n_chunks - 1), 0))
    prev = blk(lambda c: (0, jnp.maximum(c - 1, 0), 0))
    z_shape = pltpu.VMEM((nb * CHUNK, A_IN_PAD), F32)
    return pl.pallas_call(
        _mixer_ab_body,
        grid=(n_chunks + 1,),
        in_specs=[nxt, prev, _resident((1, D_MODEL)), _resident(w_in.shape), _resident((1, LANES)),
                  _resident((1, WIDTH)), _resident(sgu_w.shape), _resident(sgu_bt.shape), _resident(w_out.shape)],
        out_specs=[prev,
                   pl.BlockSpec((nb, HEADS, HEAD_DIM, HEAD_DIM), lambda c: (0, 0, 0, 0)),
                   pl.BlockSpec((nb, HEADS, 1, HEAD_DIM), lambda c: (0, 0, 0, 0)),
                   pl.BlockSpec((nb, HEADS, 1, LANES), lambda c: (0, 0, 0, 0))],
        out_shape=[jax.ShapeDtypeStruct(x.shape, F32),
                   jax.ShapeDtypeStruct((nb, HEADS, HEAD_DIM, HEAD_DIM), F32),
                   jax.ShapeDtypeStruct((nb, HEADS, 1, HEAD_DIM), F32),
                   jax.ShapeDtypeStruct((nb, HEADS, 1, LANES), F32)],
        scratch_shapes=[z_shape, z_shape, pltpu.VMEM((nb * CHUNK, 2 * WIDTH), BF16)],
        compiler_params=_params(("arbitrary",)),
        name="mixer_ab_prompt",
    )(x, x, g.reshape(1, D_MODEL), w_in, b_if, sgu_g.reshape(1, WIDTH), sgu_w, sgu_bt, w_out)


def _alibi_slope(group, head):
    n = len(SWA_GROUPS) * HEADS
    return 2.0 ** (-8.0 * (group * HEADS + head + 1) / n)


def _slab_pitch(dil):
    return dil + 8 if dil % 8 == 0 else dil


def _swa_group_body(*refs, group, dil, tile, others):
    x_ref, g_ref, w_ref = refs[0:3]
    if others:
        other_o = refs[3:3 + others]
        other_st = refs[3 + others:3 + 2 * others]
        wout_ref, y_ref, slab_scr, q_scr, k_scr, v_scr, o_dst, st_dst, cat_scr = refs[3 + 2 * others:]
    else:
        o_ref, st_ref, slab_scr, q_scr, k_scr, v_scr = refs[3:]
        o_dst, st_dst = o_ref.at[0], st_ref.at[0]
    step = pl.program_id(1)
    sub = tile // dil
    if sub >= PROJ_ROWS:
        pieces = [[(r, f, PROJ_ROWS)] for r in range(dil) for f in range(0, sub, PROJ_ROWS)]
    else:
        per = PROJ_ROWS // sub
        pieces = [[(r, 0, sub) for r in range(p * per, (p + 1) * per)] for p in range(dil // per)]
    scale = HEAD_DIM ** -0.5
    qi = lax.broadcasted_iota(jnp.int32, (CHUNK, 2 * CHUNK), 0)
    kc = lax.broadcasted_iota(jnp.int32, (CHUNK, 2 * CHUNK), 1)
    delta = CHUNK + qi - kc
    valid = (delta >= 0) & (delta <= CHUNK)
    valid_first = valid & (kc >= jnp.where(step > 0, 0, CHUNK))
    dist = (delta * dil).astype(F32)
    lane = lax.broadcasted_iota(jnp.int32, (CHUNK, LANES), 1)

    @pl.when(step == 0)
    def _():
        k_scr[:, 0:CHUNK, :] = jnp.zeros((dil, CHUNK, WIDTH), BF16)
        v_scr[:, 0:CHUNK, :] = jnp.zeros((dil, CHUNK, WIDTH), BF16)

    pitch = _slab_pitch(dil)
    for c in range(tile // 256):
        hn = _rms(x_ref[0, c * 256:(c + 1) * 256, :], g_ref[...])
        groups = [(0, 256)] if pitch == dil else [(g * dil, dil) for g in range(256 // dil)]
        for first, count in groups:
            dst = (c * 256 + first) // dil * pitch
            for sl in range(D_MODEL // LANES):
                slab_scr[sl, dst:dst + count, :] = hn[first:first + count, sl * LANES:(sl + 1) * LANES]

    def token_rows(r, first, count):
        start = first * dil + r
        return slice(start, start + count) if dil == 1 else pl.ds(start, count, stride=dil)

    def project(piece):
        parts = []
        for r, first, count in piece:
            src = slice(first, first + count) if dil == 1 else pl.ds(first * pitch + r, count, stride=pitch)
            parts.append(jnp.concatenate([slab_scr[sl, src, :] for sl in range(D_MODEL // LANES)],
                                         axis=1).astype(BF16))
        hn = parts[0] if len(parts) == 1 else jnp.concatenate(parts, axis=0)
        for j, scr in enumerate((q_scr, k_scr, v_scr)):
            z = _dot(hn, w_ref[:, j * WIDTH:(j + 1) * WIDTH]).astype(BF16)
            at = 0
            for r, first, count in piece:
                off = first if j == 0 else CHUNK + first
                scr[r, off:off + count, :] = z[at:at + count, :]
                at += count

    def scores(piece):
        out = []
        for r, first, count in piece:
            for j in range(first // CHUNK, (first + count) // CHUNK):
                mask = valid_first if j == 0 else valid
                for h in range(HEADS):
                    hs = slice(h * HEAD_DIM, (h + 1) * HEAD_DIM)
                    s = _dot_nt(q_scr[r, j * CHUNK:(j + 1) * CHUNK, hs], k_scr[r, j * CHUNK:(j + 2) * CHUNK, hs])
                    s = jnp.where(mask, s * scale + (-_alibi_slope(group, h)) * dist, NEG_INF)
                    m = jnp.max(s, axis=-1, keepdims=True)
                    p_ = jnp.exp(s - m)
                    out.append((r, j, h, m, jnp.sum(p_, axis=-1, keepdims=True), p_.astype(BF16)))
        return out

    def values(items):
        stats = None
        for r, j, h, m, l, p_ in items:
            hs = slice(h * HEAD_DIM, (h + 1) * HEAD_DIM)
            o_dst[h, token_rows(r, j * CHUNK, CHUNK), :] = _dot(p_, v_scr[r, j * CHUNK:(j + 2) * CHUNK, hs])
            base = jnp.zeros((CHUNK, LANES), F32) if h == 0 else stats
            stats = jnp.where(lane == h, m, jnp.where(lane == HEADS + h, l, base))
            if h == HEADS - 1:
                st_dst[token_rows(r, j * CHUNK, CHUNK), :] = stats

    def halves(piece):
        if len(piece) == 1:
            r, first, count = piece[0]
            return [[(r, first, count // 2)], [(r, first + count // 2, count // 2)]]
        return [piece[:len(piece) // 2], piece[len(piece) // 2:]]

    project(pieces[0])
    pending = None
    for p, piece in enumerate(pieces):
        for i, half in enumerate(halves(piece)):
            items = scores(half)
            if i == 0 and p + 1 < len(pieces):
                project(pieces[p + 1])
            if pending is not None:
                values(pending)
            pending = items
    values(pending)
    for r in range(dil):
        k_scr[r, 0:CHUNK, :] = k_scr[r, sub:sub + CHUNK, :]
        v_scr[r, 0:CHUNK, :] = v_scr[r, sub:sub + CHUNK, :]

    if others:
        for c in range(tile // 256):
            rows = slice(c * 256, (c + 1) * 256)
            stats = [st_dst[rows, :]] + [s_ref[0, rows, :] for s_ref in other_st]
            for h in range(HEADS):
                outs = [o_dst[h, rows, :]] + [o_ref[0, h, rows, :] for o_ref in other_o]
                ms = [st[:, h:h + 1] for st in stats]
                ls = [st[:, HEADS + h:HEADS + h + 1] for st in stats]
                top = functools.reduce(jnp.maximum, ms)
                ws = [jnp.exp(m - top) for m in ms]
                num = sum(w * o for w, o in zip(ws, outs))
                den = sum(w * l for w, l in zip(ws, ls))
                cat_scr[rows, h * HEAD_DIM:(h + 1) * HEAD_DIM] = (num / den).astype(BF16)
        y_ref[0] = x_ref[0] + _dot(cat_scr[...], wout_ref[...])


def _swa_group_prompt(x, g, c_in, *, group, tile, merge_with=None):
    nb, s, _ = x.shape
    dil = SWA_GROUPS[group][1]
    sub = tile // dil
    o_spec = pl.BlockSpec((1, HEADS, tile, HEAD_DIM), lambda b, t: (b, 0, t, 0))
    s_spec = pl.BlockSpec((1, tile, LANES), lambda b, t: (b, t, 0))
    x_spec = pl.BlockSpec((1, tile, D_MODEL), lambda b, t: (b, t, 0))
    in_specs = [x_spec, _resident((1, D_MODEL)),
                pl.BlockSpec((D_MODEL, C_GROUP_COLS), lambda b, t: (0, group), pipeline_mode=pl.Buffered(1))]
    args = [x, g.reshape(1, D_MODEL), c_in]
    scratch = [pltpu.VMEM((D_MODEL // LANES, sub * _slab_pitch(dil), LANES), F32),
               pltpu.VMEM((dil, sub, WIDTH), BF16),
               pltpu.VMEM((dil, CHUNK + sub, WIDTH), BF16),
               pltpu.VMEM((dil, CHUNK + sub, WIDTH), BF16)]
    if merge_with is None:
        others = 0
        out_specs = [o_spec, s_spec]
        out_shape = [jax.ShapeDtypeStruct((nb, HEADS, s, HEAD_DIM), F32), jax.ShapeDtypeStruct((nb, s, LANES), F32)]
    else:
        outs, stats, w_out = merge_with
        others = len(outs)
        in_specs += [o_spec] * others + [s_spec] * others + [_resident(w_out.shape)]
        args += [*outs, *stats, w_out]
        out_specs = x_spec
        out_shape = jax.ShapeDtypeStruct(x.shape, F32)
        scratch += [pltpu.VMEM((HEADS, tile, HEAD_DIM), F32), pltpu.VMEM((tile, LANES), F32),
                    pltpu.VMEM((tile, WIDTH), BF16)]
    return pl.pallas_call(
        functools.partial(_swa_group_body, group=group, dil=dil, tile=tile, others=others),
        grid=(nb, s // tile),
        in_specs=in_specs,
        out_specs=out_specs,
        out_shape=out_shape,
        scratch_shapes=scratch,
        compiler_params=_params(("arbitrary", "arbitrary")),
        name="swa_group%d" % group,
    )(*args)


def _kv_tail_body(x_ref, g_ref, wk_ref, wv_ref, kv_ref):
    tm = x_ref.shape[1]
    hn = _rms(x_ref[0], g_ref[...]).astype(BF16)
    for j, w_ref in enumerate((wk_ref, wv_ref)):
        kv = _dot(hn, w_ref[...])
        for h in range(HEADS):
            kv_ref[pl.ds(j * HEADS + h, tm, stride=2 * HEADS), :] = kv[:, h * HEAD_DIM:(h + 1) * HEAD_DIM]


def _kv_tail(x, g, c_in, *, group):
    win = SWA_GROUPS[group][0]
    nb, s, _ = x.shape
    tm = min(win, 512)
    first = (s - win) // tm
    steps = win // tm
    wcol = lambda j: pl.BlockSpec((D_MODEL, WIDTH), lambda b, t: (0, 3 * group + j), pipeline_mode=pl.Buffered(1))
    kv = pl.pallas_call(
        _kv_tail_body,
        grid=(nb, steps),
        in_specs=[pl.BlockSpec((1, tm, D_MODEL), lambda b, t: (b, first + t, 0)), _resident((1, D_MODEL)),
                  wcol(1), wcol(2)],
        out_specs=pl.BlockSpec((tm * 2 * HEADS, HEAD_DIM), lambda b, t: (b * steps + t, 0)),
        out_shape=jax.ShapeDtypeStruct((nb * win * 2 * HEADS, HEAD_DIM), F32),
        compiler_params=_params(("arbitrary", "arbitrary")),
        name="kv_tail%d" % group,
    )(x, g.reshape(1, D_MODEL), c_in, c_in)
    return kv.reshape(1, nb, win, 2, HEADS, HEAD_DIM)


def _proj_body(x_ref, g_ref, w_ref, z_ref):
    z_ref[...] = _dot(_rms(x_ref[...], g_ref[...]).astype(BF16), w_ref[...])


def _proj(x, g, w):
    m, n = x.shape[0], w.shape[1]
    return pl.pallas_call(
        _proj_body,
        grid=(1,),
        in_specs=[_resident(x.shape), _resident((1, D_MODEL)), _resident(w.shape)],
        out_specs=pl.BlockSpec((m, n), lambda i: (0, 0)),
        out_shape=jax.ShapeDtypeStruct((m, n), F32),
        compiler_params=_params(("arbitrary",)),
        name="proj_rows",
    )(x, g.reshape(1, D_MODEL), w)


def _out_proj_body(c_ref, w_ref, x_ref, y_ref):
    y_ref[...] = x_ref[...] + _dot(c_ref[...].astype(BF16), w_ref[...])


def _out_proj(cat, w, x):
    return pl.pallas_call(
        _out_proj_body,
        grid=(1,),
        in_specs=[_resident(cat.shape), _resident(w.shape), _resident(x.shape)],
        out_specs=pl.BlockSpec(x.shape, lambda i: (0, 0)),
        out_shape=jax.ShapeDtypeStruct(x.shape, F32),
        compiler_params=_params(("arbitrary",)),
        name="out_proj_rows",
    )(cat, w, x)


def _mixer_ab_step_body(z_ref, bif_ref, sg_ref, w00_ref, b0_ref, c_ref, n_ref, m_ref,
                        cat_ref, c1_ref, n1_ref, m1_ref, vn_ref):
    nb = z_ref.shape[0]
    scale = HEAD_DIM ** -0.5
    eye = (lax.broadcasted_iota(jnp.int32, (HEAD_DIM, HEAD_DIM), 0)
           == lax.broadcasted_iota(jnp.int32, (HEAD_DIM, HEAD_DIM), 1)).astype(F32)
    lane = lax.broadcasted_iota(jnp.int32, (1, LANES), 1)
    for i in range(nb):
        zr = z_ref[i:i + 1, :]
        gates = zr[:, COL_GATES:COL_GATES + LANES] + bif_ref[...]
        m_new = jnp.zeros((1, LANES), F32)
        for h in range(HEADS):
            hs = slice(h * HEAD_DIM, (h + 1) * HEAD_DIM)
            q = zr[:, hs]
            k = zr[:, WIDTH + h * HEAD_DIM:WIDTH + (h + 1) * HEAD_DIM] * scale
            v = zr[:, 2 * WIDTH + h * HEAD_DIM:2 * WIDTH + (h + 1) * HEAD_DIM]
            o = zr[:, 3 * WIDTH + h * HEAD_DIM:3 * WIDTH + (h + 1) * HEAD_DIM]
            ig = gates[:, h:h + 1]
            lf = _log_sigmoid(gates[:, HEADS + h:HEADS + h + 1])
            c0 = c_ref[i, h]
            n0 = n_ref[i, h]
            m0 = m_ref[i:i + 1, h:h + 1]
            a = m0 + lf
            m = jnp.maximum(a, ig)
            s = jnp.sum(q * k, axis=-1, keepdims=True) * jnp.exp(ig - m)
            inter = jnp.exp(a - m)
            cq_col = jnp.sum(c0 * q, axis=-1, keepdims=True)
            cq = jnp.sum(eye * cq_col, axis=0, keepdims=True)
            v_col = jnp.sum(eye * v, axis=-1, keepdims=True)
            num = s * v + inter * cq
            den = s + inter * jnp.sum(n0 * q, axis=-1, keepdims=True)
            hh = num / jnp.maximum(jnp.abs(den), jnp.exp(-m))
            w = jnp.exp(ig - m)
            c1_ref[i, h] = inter * c0 + (w * v_col) * k
            n1_ref[i, h] = inter * n0 + w * k
            m_new = jnp.where(lane == h, m, m_new)
            cat_ref[i:i + 1, hs] = jax.nn.sigmoid(o) * hh
        m1_ref[i:i + 1, :] = m_new
        vn = _rms(zr[:, COL_GV:COL_GV + WIDTH], sg_ref[...])
        vn_ref[i:i + 1, :] = vn
        cat_ref[i:i + 1, WIDTH:2 * WIDTH] = zr[:, COL_U:COL_U + WIDTH] * (w00_ref[...] * vn + b0_ref[...])


def _mixer_ab_step(z, b_if, sgu_g, w00, b0, st_c, st_n, st_m, *, nb):
    n = z.shape[0]
    rows = lambda w: pl.BlockSpec((nb, w), lambda i: (i, 0))
    c_spec = pl.BlockSpec((nb, HEADS, HEAD_DIM, HEAD_DIM), lambda i: (i, 0, 0, 0))
    n_spec = pl.BlockSpec((nb, HEADS, 1, HEAD_DIM), lambda i: (i, 0, 0, 0))
    return pl.pallas_call(
        _mixer_ab_step_body,
        grid=(n // nb,),
        in_specs=[rows(A_IN_PAD), _resident((1, LANES)), _resident((1, WIDTH)), _resident((1, WIDTH)),
                  _resident((1, WIDTH)), c_spec, n_spec, rows(LANES)],
        out_specs=[rows(2 * WIDTH), c_spec, n_spec, rows(LANES), rows(WIDTH)],
        out_shape=[jax.ShapeDtypeStruct((n, 2 * WIDTH), F32),
                   jax.ShapeDtypeStruct(st_c.shape, F32),
                   jax.ShapeDtypeStruct(st_n.shape, F32),
                   jax.ShapeDtypeStruct((n, LANES), F32),
                   jax.ShapeDtypeStruct((n, WIDTH), F32)],
        compiler_params=_params(("arbitrary",)),
        name="mixer_ab_step",
    )(z, b_if, sgu_g.reshape(1, WIDTH), w00, b0, st_c, st_n, st_m)


def _swa_step_body(z_ref, kv0_ref, kv1_ref, kv2_ref, cat_ref):
    nb = z_ref.shape[0]
    scale = HEAD_DIM ** -0.5
    steps = (CHUNK - lax.broadcasted_iota(jnp.int32, (CHUNK, 1, 1), 0)).astype(F32)
    head = lax.broadcasted_iota(jnp.int32, (1, HEADS, 1), 1)
    for i in range(nb):
        ms, ls, os_ = [], [], []
        for gi, kv_ref in enumerate((kv0_ref, kv1_ref, kv2_ref)):
            dil = SWA_GROUPS[gi][1]
            base = gi * 3 * HEADS
            q = z_ref[i, base:base + HEADS, :]
            k_new = z_ref[i, base + HEADS:base + 2 * HEADS, :]
            v_new = z_ref[i, base + 2 * HEADS:base + 3 * HEADS, :]
            kc = kv_ref[i, :, 0, 0, :, :]
            vc = kv_ref[i, :, 0, 1, :, :]
            slope = jnp.zeros((1, HEADS, 1), F32)
            for h in range(HEADS):
                slope = jnp.where(head == h, _alibi_slope(gi, h) * dil, slope)
            s = jnp.sum(kc * q[None], axis=-1, keepdims=True) * scale - slope * steps
            s_new = jnp.sum(k_new * q, axis=-1, keepdims=True) * scale
            m = jnp.maximum(jnp.max(s, axis=0), s_new)
            p = jnp.exp(s - m[None])
            p_new = jnp.exp(s_new - m)
            ms.append(m)
            ls.append(jnp.sum(p, axis=0) + p_new)
            os_.append(jnp.sum(p * vc, axis=0) + p_new * v_new)
        top = functools.reduce(jnp.maximum, ms)
        ws = [jnp.exp(m - top) for m in ms]
        num = sum(w * o for w, o in zip(ws, os_))
        den = sum(w * l for w, l in zip(ws, ls))
        cat_ref[i] = num / den


def _swa_step(z, caches, *, nb):
    n = z.shape[0]
    views = []
    specs = []
    for (win, dil), cache in zip(SWA_GROUPS, caches):
        views.append(cache.reshape(n, win // dil, dil, 2, HEADS, HEAD_DIM))
        specs.append(pl.BlockSpec((nb, CHUNK, 1, 2, HEADS, HEAD_DIM), lambda i: (i, 0, 0, 0, 0, 0)))
    return pl.pallas_call(
        _swa_step_body,
        grid=(n // nb,),
        in_specs=[pl.BlockSpec((nb,) + z.shape[1:], lambda i: (i, 0, 0))] + specs,
        out_specs=pl.BlockSpec((nb, HEADS, HEAD_DIM), lambda i: (i, 0, 0)),
        out_shape=jax.ShapeDtypeStruct((n, HEADS, HEAD_DIM), F32),
        compiler_params=_params(("arbitrary",)),
        name="swa_step",
    )(z, *views)


def _pad_cols(w, n):
    return jnp.pad(w, ((0, 0), (0, n - w.shape[1])))


def kernel(x_prompt, x_sample, state_mlstm_C, state_mlstm_n, state_mlstm_m, cache_swa_kv0, cache_swa_kv1, cache_swa_kv2, norm_g, ffn_w_gate, ffn_w_up, ffn_w_down, a_w_in, a_b_if, sgu_norm_g, sgu_w, sgu_b, a_w_out, c_w_in, c_w_out, final_norm_g):
    nb, s, _ = x_prompt.shape
    ns = x_sample.shape[0]
    assert x_sample.shape[1] == 1 and s % max(w for w, _ in SWA_GROUPS) == 0
    for (win, dil), cache in zip(SWA_GROUPS, (cache_swa_kv0, cache_swa_kv1, cache_swa_kv2)):
        assert cache.shape[2] == win and win // dil == CHUNK

    ffn_w = (jnp.swapaxes(ffn_w_gate, 2, 3), jnp.swapaxes(ffn_w_up, 2, 3), ffn_w_down)
    g_lo, g_hi = 4 * WIDTH, 4 * WIDTH + 2 * HEADS
    a_in = jnp.concatenate([a_w_in[0][:, :g_lo], a_w_in[0][:, g_hi:], _pad_cols(a_w_in[0][:, g_lo:g_hi], LANES)],
                           axis=1).astype(BF16)
    b_if = _pad_cols(a_b_if[0].reshape(1, 2 * HEADS), LANES)
    a_out = a_w_out[0].astype(BF16)
    c_in = c_w_in[0].astype(BF16)
    c_out = c_w_out[0].astype(BF16)
    sgu_bt = _pad_cols(sgu_b[0].T, LANES)
    sgu_w00 = jnp.repeat(sgu_w[0, :, 0, 0], CHUNK).reshape(1, WIDTH)
    sgu_b0 = jnp.repeat(sgu_b[0, :, 0], CHUNK).reshape(1, WIDTH)

    xp = x_prompt.reshape(nb * s, D_MODEL)
    xs = x_sample.reshape(ns, D_MODEL)
    xp, xs = _ffn(xp, xs, norm_g[0, 0], ffn_w, 0, 0, tm=512)
    xp, p_c, p_n, p_m = _mixer_ab_prompt(xp.reshape(nb, s, D_MODEL), norm_g[0, 1], a_in, b_if,
                                          sgu_norm_g[0], sgu_w[0], sgu_bt, a_out)
    z = _proj(xs, norm_g[0, 1], a_in)
    cat, s_c, s_n, s_m, s_v = _mixer_ab_step(
        z, b_if, sgu_norm_g[0], sgu_w00, sgu_b0, state_mlstm_C[0],
        state_mlstm_n[0].reshape(ns, HEADS, 1, HEAD_DIM), _pad_cols(state_mlstm_m[0], LANES), nb=8)
    xs = _out_proj(cat, a_out, xs)
    xp, xs = _ffn(xp.reshape(nb * s, D_MODEL), xs, norm_g[0, 2], ffn_w, 0, 1, tm=512)
    xp, xs = _ffn(xp, xs, norm_g[1, 0], ffn_w, 1, 0, tm=512)
    xp3 = xp.reshape(nb, s, D_MODEL)
    outs, stats = [], []
    for gi in (1, 2):
        o, st = _swa_group_prompt(xp3, norm_g[1, 1], c_in, group=gi, tile=2048)
        outs.append(o)
        stats.append(st)
    p_kv = [_kv_tail(xp3, norm_g[1, 1], c_in, group=gi) for gi in range(3)]
    xp = _swa_group_prompt(xp3, norm_g[1, 1], c_in, group=0, tile=1024, merge_with=(outs, stats, c_out))
    xp = xp.reshape(nb * s, D_MODEL)
    z = _proj(xs, norm_g[1, 1], c_in)
    cat = _swa_step(z.reshape(ns, 3 * 3 * HEADS, HEAD_DIM), (cache_swa_kv0, cache_swa_kv1, cache_swa_kv2), nb=4)
    xs = _out_proj(cat.reshape(ns, WIDTH), c_out, xs)
    y_prompt, y_sample = _ffn(xp, xs, norm_g[1, 2], ffn_w, 1, 1, final_norm_g, tm=512)
    y_prompt = y_prompt.reshape(nb, s, D_MODEL)
    y_sample = y_sample.reshape(ns, 1, D_MODEL)
    s_kv = [z[:, gi * C_GROUP_COLS + WIDTH:(gi + 1) * C_GROUP_COLS].reshape(1, ns, 1, 2, HEADS, HEAD_DIM)
            for gi in range(3)]

    return (y_prompt, y_sample,
            p_c.reshape(1, nb, HEADS, HEAD_DIM, HEAD_DIM), p_n.reshape(1, nb, HEADS, HEAD_DIM),
            p_m[:, :, 0, 0].reshape(1, nb, HEADS),
            s_c.reshape(1, ns, HEADS, HEAD_DIM, HEAD_DIM), s_n.reshape(1, ns, HEADS, HEAD_DIM),
            s_m[:, :HEADS].reshape(1, ns, HEADS), s_v.reshape(1, ns, 1, WIDTH),
            p_kv[0], p_kv[1], p_kv[2], s_kv[0], s_kv[1], s_kv[2])
```

```python
import functools

import jax
import jax.numpy as jnp
from jax import lax
from jax.experimental import pallas as pl
from jax.experimental.pallas import tpu as pltpu

F32 = jnp.float32
BF16 = jnp.bfloat16

D_MODEL = 1024
D_FF = 2752
HEADS = 4
HEAD_DIM = 128
WIDTH = HEADS * HEAD_DIM
CHUNK = 128
SWA_GROUPS = ((128, 1), (512, 4), (2048, 16))
NORM_EPS = 1e-6
NEG_INF = -1e30

LANES = 128
BF16_ROWS = 16
PROJ_SLAB = 256
PROJ_ROWS = 512
FF_CHUNK = 256
A_IN_PAD = 4 * WIDTH + 2 * WIDTH + LANES
COL_U = 4 * WIDTH
COL_GV = 5 * WIDTH
COL_GATES = 6 * WIDTH
C_GROUP_COLS = 3 * WIDTH

VMEM_LIMIT = 60 * 1024 * 1024


def _params(semantics):
    return pltpu.CompilerParams(dimension_semantics=semantics, vmem_limit_bytes=VMEM_LIMIT)


def _resident(shape):
    nd = len(shape)
    return pl.BlockSpec(shape, lambda *_: (0,) * nd, pipeline_mode=pl.Buffered(1))


def _rms(x, g):
    ms = jnp.mean(x * x, axis=-1, keepdims=True)
    return x * lax.rsqrt(ms + NORM_EPS) * g


def _dot(a, b):
    return jnp.dot(a, b, preferred_element_type=F32)


def _dot_nt(a, b):
    return lax.dot_general(a, b, (((1,), (1,)), ((), ())), preferred_element_type=F32)


def _log_sigmoid(x):
    return jnp.minimum(x, 0.0) - jnp.log1p(jnp.exp(-jnp.abs(x)))


def _swiglu_rows(h_scr, act_scr, rows, wg_ref, wu_ref, wd_ref):
    for c0 in range(0, D_FF, FF_CHUNK):
        cols = slice(c0, min(c0 + FF_CHUNK, D_FF))
        h = h_scr[0:rows, :]
        gate = _dot_nt(h, wg_ref[0, 0, cols, :])
        up = _dot_nt(h, wu_ref[0, 0, cols, :])
        act_scr[0:rows, cols] = (gate * jax.nn.sigmoid(gate) * up).astype(act_scr.dtype)
    whole = D_FF // FF_CHUNK * FF_CHUNK
    out = _dot(act_scr[0:rows, 0:whole], wd_ref[0, 0, 0:whole, :])
    if whole < D_FF:
        out = out + _dot(act_scr[0:rows, whole:D_FF], wd_ref[0, 0, whole:D_FF, :])
    return out


def _ffn_body(*refs, final):
    if final:
        x_ref, xs_ref, g_ref, wg_ref, wu_ref, wd_ref, fg_ref, o_ref, os_ref, h_scr, act_scr = refs
    else:
        x_ref, xs_ref, g_ref, wg_ref, wu_ref, wd_ref, o_ref, os_ref, h_scr, act_scr = refs
    tm, ns = x_ref.shape[0], xs_ref.shape[0]
    last = pl.num_programs(0) - 1

    def finish(x, acc):
        y = x + 0.5 * acc
        return _rms(y, fg_ref[...]) if final else y

    x = x_ref[...]
    h_scr[0:tm, :] = _rms(x, g_ref[...]).astype(h_scr.dtype)

    @pl.when(pl.program_id(0) != last)
    def _():
        o_ref[...] = finish(x, _swiglu_rows(h_scr, act_scr, tm, wg_ref, wu_ref, wd_ref))

    @pl.when(pl.program_id(0) == last)
    def _():
        xs = xs_ref[...]
        h_scr[tm:tm + ns, :] = _rms(xs, g_ref[...]).astype(h_scr.dtype)
        acc = _swiglu_rows(h_scr, act_scr, tm + ns, wg_ref, wu_ref, wd_ref)
        o_ref[...] = finish(x, acc[0:tm])
        os_ref[...] = finish(xs, acc[tm:tm + ns])


def _ffn(x, xs, g, weights, layer, which, final_g=None, *, tm):
    m, ns = x.shape[0], xs.shape[0]
    final = final_g is not None
    wg, wu, wd = weights
    row = pl.BlockSpec((tm, D_MODEL), lambda i: (i, 0))
    wspec = lambda w: pl.BlockSpec((1, 1) + w.shape[2:], lambda i: (layer, which, 0, 0), pipeline_mode=pl.Buffered(1))
    in_specs = [row, _resident(xs.shape), _resident((1, D_MODEL)), wspec(wg), wspec(wu), wspec(wd)]
    args = [x, xs, g.reshape(1, D_MODEL), wg, wu, wd]
    if final:
        in_specs.append(_resident((1, D_MODEL)))
        args.append(final_g.reshape(1, D_MODEL))
    return pl.pallas_call(
        functools.partial(_ffn_body, final=final),
        grid=(m // tm,),
        in_specs=in_specs,
        out_specs=[row, pl.BlockSpec(xs.shape, lambda i: (0, 0))],
        out_shape=[jax.ShapeDtypeStruct((m, D_MODEL), F32), jax.ShapeDtypeStruct(xs.shape, F32)],
        scratch_shapes=[pltpu.VMEM((tm + ns, D_MODEL), wg.dtype), pltpu.VMEM((tm + ns, D_FF), wg.dtype)],
        compiler_params=_params(("arbitrary",)),
        name="ffn_final" if final else "ffn",
    )(*args)


def _cummax_lanes(x):
    lane = lax.broadcasted_iota(jnp.int32, x.shape, 1)
    d = 1
    while d < x.shape[1]:
        x = jnp.maximum(x, jnp.where(lane >= d, pltpu.roll(x, d, axis=1), NEG_INF))
        d *= 2
    return x


def _exact_tri_dot(tri_bf16, x):
    x1 = x.astype(BF16)
    r1 = x - x1.astype(F32)
    x2 = r1.astype(BF16)
    x3 = (r1 - x2.astype(F32)).astype(BF16)
    n = x.shape[1]
    r = _dot(tri_bf16, jnp.concatenate([x1, x2, x3], axis=1))
    return r[:, 0:n] + r[:, n:2 * n] + r[:, 2 * n:3 * n]


def _mixer_ab_chunk(xn_ref, xp_ref, g_ref, win_ref, bif_ref, sg_ref, sw_ref, sbt_ref, wout_ref,
                   y_ref, c_ref, n_ref, m_ref, z_cur, z_nxt, cat_scr):
    nb = xn_ref.shape[0]
    row = lax.broadcasted_iota(jnp.int32, (CHUNK, CHUNK), 0)
    col = lax.broadcasted_iota(jnp.int32, (CHUNK, CHUNK), 1)
    causal = col <= row
    keys_before = row <= col
    tri = jnp.where(causal, 1.0, 0.0).astype(BF16)
    scale = HEAD_DIM ** -0.5

    hn = _rms(xn_ref[...].reshape(nb * CHUNK, D_MODEL), g_ref[...]).astype(BF16)
    slabs = [(c0, min(c0 + PROJ_SLAB, A_IN_PAD)) for c0 in range(0, A_IN_PAD, PROJ_SLAB)]

    def project(count):
        for _ in range(min(count, len(slabs))):
            c0, c1 = slabs.pop(0)
            z_nxt[:, c0:c1] = _dot(hn, win_ref[:, c0:c1])

    sgu_bias = [jnp.broadcast_to(sbt_ref[:, g:g + 1], (CHUNK, CHUNK)) for g in range(HEADS)]
    vn = [_rms(z_cur[b * CHUNK:(b + 1) * CHUNK, COL_GV:COL_GV + WIDTH], sg_ref[...]) for b in range(nb)]
    for g in range(HEADS):
        gs = slice(g * CHUNK, (g + 1) * CHUNK)
        mixed = _dot(jnp.where(causal, sw_ref[g], 0.0).astype(BF16),
                     jnp.concatenate([vn[b][:, gs] for b in range(nb)], axis=1).astype(BF16))
        for b in range(nb):
            rows = slice(b * CHUNK, (b + 1) * CHUNK)
            u = z_cur[rows, COL_U + g * CHUNK:COL_U + (g + 1) * CHUNK]
            cat_scr[rows, WIDTH + g * CHUNK:WIDTH + (g + 1) * CHUNK] = (
                u * (mixed[:, b * CHUNK:(b + 1) * CHUNK] + sgu_bias[g])).astype(BF16)
    project(1)

    pieces_per_stage = -(-(len(slabs)) // (3 * nb))
    gate_terms = []
    for b in range(nb):
        rows = slice(b * CHUNK, (b + 1) * CHUNK)
        gates = z_cur[rows, COL_GATES:COL_GATES + LANES] + bif_ref[...]
        lg = jnp.where(col < HEADS, gates, _log_sigmoid(gates))
        gate_terms.append((lg, _exact_tri_dot(tri, lg)))
        project(pieces_per_stage)

    heads = []
    for b, (lg, fcum) in enumerate(gate_terms):
        rows = slice(b * CHUNK, (b + 1) * CHUNK)
        lg_t = lg.T
        fcum_t = fcum.T
        gmax = _cummax_lanes(lg_t[0:2 * HEADS, :] - jnp.concatenate([fcum_t[HEADS:2 * HEADS, :]] * 2, axis=0))
        for h in range(HEADS):
            q = z_cur[rows, h * HEAD_DIM:(h + 1) * HEAD_DIM]
            k = z_cur[rows, WIDTH + h * HEAD_DIM:WIDTH + (h + 1) * HEAD_DIM] * scale
            v_t = z_cur[rows, 2 * WIDTH + h * HEAD_DIM:2 * WIDTH + (h + 1) * HEAD_DIM].T
            f_r = fcum_t[HEADS + h:HEADS + h + 1, :]
            i_r = lg_t[h:h + 1, :]
            g_c = lg[:, h:h + 1] - fcum[:, HEADS + h:HEADS + h + 1]
            c0, n0, m0 = c_ref[b, h], n_ref[b, h], m_ref[b, h]
            m = f_r + jnp.maximum(m0, gmax[h:h + 1, :])
            inter = jnp.exp(m0 + f_r - m)
            qb, kb = q.astype(BF16), k.astype(BF16)
            against_q = _dot_nt(
                jnp.concatenate([k, c0, jnp.broadcast_to(n0, (BF16_ROWS, HEAD_DIM))], axis=0).astype(BF16), qb)
            m_last = m[:, CHUNK - 1:CHUNK]
            f_last = f_r[:, CHUNK - 1:CHUNK]
            w = jnp.exp(f_last - f_r + i_r - m_last)
            decay = jnp.exp(m0[:, 0:1] + f_last - m_last)
            against_k = _dot(
                jnp.concatenate([v_t * w, jnp.broadcast_to(w, (BF16_ROWS, CHUNK))], axis=0).astype(BF16), kb)
            c_ref[b, h] = decay * c0 + against_k[0:HEAD_DIM]
            n_ref[b, h] = decay * n0 + against_k[HEAD_DIM:HEAD_DIM + 1]
            m_ref[b, h] = jnp.broadcast_to(m_last, (1, LANES))
            heads.append((against_q, v_t, f_r, g_c, m, inter))
        project(pieces_per_stage)

    partial = []
    for i, (against_q, v_t, f_r, g_c, m, inter) in enumerate(heads):
        kq = against_q[0:CHUNK]
        cq = against_q[CHUNK:CHUNK + HEAD_DIM]
        nq = against_q[CHUNK + HEAD_DIM:CHUNK + HEAD_DIM + 1]
        s_t = kq * jnp.exp(jnp.where(keys_before, (f_r - m) + g_c, NEG_INF))
        den = jnp.sum(s_t, axis=0, keepdims=True) + inter * nq
        partial.append((_dot(v_t.astype(BF16), s_t.astype(BF16)), inter * cq,
                        jnp.maximum(jnp.abs(den), jnp.exp(-m))))
        if i % HEADS == HEADS - 1:
            project(pieces_per_stage)

    for i, (sv, carried, den) in enumerate(partial):
        b, h = divmod(i, HEADS)
        rows = slice(b * CHUNK, (b + 1) * CHUNK)
        o = z_cur[rows, 3 * WIDTH + h * HEAD_DIM:3 * WIDTH + (h + 1) * HEAD_DIM]
        cat_scr[rows, h * HEAD_DIM:(h + 1) * HEAD_DIM] = (jax.nn.sigmoid(o) * ((sv + carried) / den).T).astype(BF16)

    project(len(slabs))
    y = xp_ref[...].reshape(nb * CHUNK, D_MODEL) + _dot(cat_scr[...], wout_ref[...])
    y_ref[...] = y.reshape(nb, CHUNK, D_MODEL)


def _mixer_ab_body(*refs):
    *io_refs, z0_scr, z1_scr, cat_scr = refs
    c_ref, n_ref, m_ref = io_refs[-3:]
    step = pl.program_id(0)

    @pl.when(step <= 1)
    def _():
        c_ref[...] = jnp.zeros_like(c_ref)
        n_ref[...] = jnp.zeros_like(n_ref)
        m_ref[...] = jnp.zeros_like(m_ref)

    @pl.when(step == 0)
    def _():
        z1_scr[...] = jnp.zeros_like(z1_scr)

    @pl.when(step % 2 == 0)
    def _():
        _mixer_ab_chunk(*io_refs, z1_scr, z0_scr, cat_scr)

    @pl.when(step % 2 == 1)
    def _():
        _mixer_ab_chunk(*io_refs, z0_scr, z1_scr, cat_scr)


def _mixer_ab_prompt(x, g, w_in, b_if, sgu_g, sgu_w, sgu_bt, w_out):
    nb, s, _ = x.shape
    n_chunks = s // CHUNK
    blk = lambda index: pl.BlockSpec((nb, CHUNK, D_MODEL), index)
    nxt = blk(lambda c: (0, jnp.minimum(c, n_chunks - 1), 0))
    prev = blk(lambda c: (0, jnp.maximum(c - 1, 0), 0))
    z_shape = pltpu.VMEM((nb * CHUNK, A_IN_PAD), F32)
    return pl.pallas_call(
        _mixer_ab_body,
        grid=(n_chunks + 1,),
        in_specs=[nxt, prev, _resident((1, D_MODEL)), _resident(w_in.shape), _resident((1, LANES)),
                  _resident((1, WIDTH)), _resident(sgu_w.shape), _resident(sgu_bt.shape), _resident(w_out.shape)],
        out_specs=[prev,
                   pl.BlockSpec((nb, HEADS, HEAD_DIM, HEAD_DIM), lambda c: (0, 0, 0, 0)),
                   pl.BlockSpec((nb, HEADS, 1, HEAD_DIM), lambda c: (0, 0, 0, 0)),
                   pl.BlockSpec((nb, HEADS, 1, LANES), lambda c: (0, 0, 0, 0))],
        out_shape=[jax.ShapeDtypeStruct(x.shape, F32),
                   jax.ShapeDtypeStruct((nb, HEADS, HEAD_DIM, HEAD_DIM), F32),
                   jax.ShapeDtypeStruct((nb, HEADS, 1, HEAD_DIM), F32),
                   jax.ShapeDtypeStruct((nb, HEADS, 1, LANES), F32)],
        scratch_shapes=[z_shape, z_shape, pltpu.VMEM((nb * CHUNK, 2 * WIDTH), BF16)],
        compiler_params=_params(("arbitrary",)),
        name="mixer_ab_prompt",
    )(x, x, g.reshape(1, D_MODEL), w_in, b_if, sgu_g.reshape(1, WIDTH), sgu_w, sgu_bt, w_out)


def _alibi_slope(group, head):
    n = len(SWA_GROUPS) * HEADS
    return 2.0 ** (-8.0 * (group * HEADS + head + 1) / n)


def _slab_pitch(dil):
    return dil + 8 if dil % 8 == 0 else dil


def _swa_group_body(*refs, group, dil, tile, others):
    x_ref, g_ref, w_ref = refs[0:3]
    if others:
        other_o = refs[3:3 + others]
        other_st = refs[3 + others:3 + 2 * others]
        wout_ref, y_ref, slab_scr, q_scr, k_scr, v_scr, o_dst, st_dst, cat_scr = refs[3 + 2 * others:]
    else:
        o_ref, st_ref, slab_scr, q_scr, k_scr, v_scr = refs[3:]
        o_dst, st_dst = o_ref.at[0], st_ref.at[0]
    step = pl.program_id(1)
    sub = tile // dil
    if sub >= PROJ_ROWS:
        pieces = [[(r, f, PROJ_ROWS)] for r in range(dil) for f in range(0, sub, PROJ_ROWS)]
    else:
        per = PROJ_ROWS // sub
        pieces = [[(r, 0, sub) for r in range(p * per, (p + 1) * per)] for p in range(dil // per)]
    scale = HEAD_DIM ** -0.5
    qi = lax.broadcasted_iota(jnp.int32, (CHUNK, 2 * CHUNK), 0)
    kc = lax.broadcasted_iota(jnp.int32, (CHUNK, 2 * CHUNK), 1)
    delta = CHUNK + qi - kc
    valid = (delta >= 0) & (delta <= CHUNK)
    valid_first = valid & (kc >= jnp.where(step > 0, 0, CHUNK))
    dist = (delta * dil).astype(F32)
    lane = lax.broadcasted_iota(jnp.int32, (CHUNK, LANES), 1)

    @pl.when(step == 0)
    def _():
        k_scr[:, 0:CHUNK, :] = jnp.zeros((dil, CHUNK, WIDTH), BF16)
        v_scr[:, 0:CHUNK, :] = jnp.zeros((dil, CHUNK, WIDTH), BF16)

    pitch = _slab_pitch(dil)
    for c in range(tile // 256):
        hn = _rms(x_ref[0, c * 256:(c + 1) * 256, :], g_ref[...])
        groups = [(0, 256)] if pitch == dil else [(g * dil, dil) for g in range(256 // dil)]
        for first, count in groups:
            dst = (c * 256 + first) // dil * pitch
            for sl in range(D_MODEL // LANES):
                slab_scr[sl, dst:dst + count, :] = hn[first:first + count, sl * LANES:(sl + 1) * LANES]

    def token_rows(r, first, count):
        start = first * dil + r
        return slice(start, start + count) if dil == 1 else pl.ds(start, count, stride=dil)

    def project(piece):
        parts = []
        for r, first, count in piece:
            src = slice(first, first + count) if dil == 1 else pl.ds(first * pitch + r, count, stride=pitch)
            parts.append(jnp.concatenate([slab_scr[sl, src, :] for sl in range(D_MODEL // LANES)],
                                         axis=1).astype(BF16))
        hn = parts[0] if len(parts) == 1 else jnp.concatenate(parts, axis=0)
        for j, scr in enumerate((q_scr, k_scr, v_scr)):
            z = _dot(hn, w_ref[:, j * WIDTH:(j + 1) * WIDTH]).astype(BF16)
            at = 0
            for r, first, count in piece:
                off = first if j == 0 else CHUNK + first
                scr[r, off:off + count, :] = z[at:at + count, :]
                at += count

    def scores(piece):
        out = []
        for r, first, count in piece:
            for j in range(first // CHUNK, (first + count) // CHUNK):
                mask = valid_first if j == 0 else valid
                for h in range(HEADS):
                    hs = slice(h * HEAD_DIM, (h + 1) * HEAD_DIM)
                    s = _dot_nt(q_scr[r, j * CHUNK:(j + 1) * CHUNK, hs], k_scr[r, j * CHUNK:(j + 2) * CHUNK, hs])
                    s = jnp.where(mask, s * scale + (-_alibi_slope(group, h)) * dist, NEG_INF)
                    m = jnp.max(s, axis=-1, keepdims=True)
                    p_ = jnp.exp(s - m)
                    out.append((r, j, h, m, jnp.sum(p_, axis=-1, keepdims=True), p_.astype(BF16)))
        return out

    def values(items):
        stats = None
        for r, j, h, m, l, p_ in items:
            hs = slice(h * HEAD_DIM, (h + 1) * HEAD_DIM)
            o_dst[h, token_rows(r, j * CHUNK, CHUNK), :] = _dot(p_, v_scr[r, j * CHUNK:(j + 2) * CHUNK, hs])
            base = jnp.zeros((CHUNK, LANES), F32) if h == 0 else stats
            stats = jnp.where(lane == h, m, jnp.where(lane == HEADS + h, l, base))
            if h == HEADS - 1:
                st_dst[token_rows(r, j * CHUNK, CHUNK), :] = stats

    def halves(piece):
        if len(piece) == 1:
            r, first, count = piece[0]
            return [[(r, first, count // 2)], [(r, first + count // 2, count // 2)]]
        return [piece[:len(piece) // 2], piece[len(piece) // 2:]]

    project(pieces[0])
    pending = None
    for p, piece in enumerate(pieces):
        for i, half in enumerate(halves(piece)):
            items = scores(half)
            if i == 0 and p + 1 < len(pieces):
                project(pieces[p + 1])
            if pending is not None:
                values(pending)
            pending = items
    values(pending)
    for r in range(dil):
        k_scr[r, 0:CHUNK, :] = k_scr[r, sub:sub + CHUNK, :]
        v_scr[r, 0:CHUNK, :] = v_scr[r, sub:sub + CHUNK, :]

    if others:
        for c in range(tile // 256):
            rows = slice(c * 256, (c + 1) * 256)
            stats = [st_dst[rows, :]] + [s_ref[0, rows, :] for s_ref in other_st]
            for h in range(HEADS):
                outs = [o_dst[h, rows, :]] + [o_ref[0, h, rows, :] for o_ref in other_o]
                ms = [st[:, h:h + 1] for st in stats]
                ls = [st[:, HEADS + h:HEADS + h + 1] for st in stats]
                top = functools.reduce(jnp.maximum, ms)
                ws = [jnp.exp(m - top) for m in ms]
                num = sum(w * o for w, o in zip(ws, outs))
                den = sum(w * l for w, l in zip(ws, ls))
                cat_scr[rows, h * HEAD_DIM:(h + 1) * HEAD_DIM] = (num / den).astype(BF16)
        y_ref[0] = x_ref[0] + _dot(cat_scr[...], wout_ref[...])


def _swa_group_prompt(x, g, c_in, *, group, tile, merge_with=None):
    nb, s, _ = x.shape
    dil = SWA_GROUPS[group][1]
    sub = tile // dil
    o_spec = pl.BlockSpec((1, HEADS, tile, HEAD_DIM), lambda b, t: (b, 0, t, 0))
    s_spec = pl.BlockSpec((1, tile, LANES), lambda b, t: (b, t, 0))
    x_spec = pl.BlockSpec((1, tile, D_MODEL), lambda b, t: (b, t, 0))
    in_specs = [x_spec, _resident((1, D_MODEL)),
                pl.BlockSpec((D_MODEL, C_GROUP_COLS), lambda b, t: (0, group), pipeline_mode=pl.Buffered(1))]
    args = [x, g.reshape(1, D_MODEL), c_in]
    scratch = [pltpu.VMEM((D_MODEL // LANES, sub * _slab_pitch(dil), LANES), F32),
               pltpu.VMEM((dil, sub, WIDTH), BF16),
               pltpu.VMEM((dil, CHUNK + sub, WIDTH), BF16),
               pltpu.VMEM((dil, CHUNK + sub, WIDTH), BF16)]
    if merge_with is None:
        others = 0
        out_specs = [o_spec, s_spec]
        out_shape = [jax.ShapeDtypeStruct((nb, HEADS, s, HEAD_DIM), F32), jax.ShapeDtypeStruct((nb, s, LANES), F32)]
    else:
        outs, stats, w_out = merge_with
        others = len(outs)
        in_specs += [o_spec] * others + [s_spec] * others + [_resident(w_out.shape)]
        args += [*outs, *stats, w_out]
        out_specs = x_spec
        out_shape = jax.ShapeDtypeStruct(x.shape, F32)
        scratch += [pltpu.VMEM((HEADS, tile, HEAD_DIM), F32), pltpu.VMEM((tile, LANES), F32),
                    pltpu.VMEM((tile, WIDTH), BF16)]
    return pl.pallas_call(
        functools.partial(_swa_group_body, group=group, dil=dil, tile=tile, others=others),
        grid=(nb, s // tile),
        in_specs=in_specs,
        out_specs=out_specs,
        out_shape=out_shape,
        scratch_shapes=scratch,
        compiler_params=_params(("arbitrary", "arbitrary")),
        name="swa_group%d" % group,
    )(*args)


def _kv_tail_body(x_ref, g_ref, wk_ref, wv_ref, kv_ref):
    tm = x_ref.shape[1]
    hn = _rms(x_ref[0], g_ref[...]).astype(BF16)
    for j, w_ref in enumerate((wk_ref, wv_ref)):
        kv = _dot(hn, w_ref[...])
        for h in range(HEADS):
            kv_ref[pl.ds(j * HEADS + h, tm, stride=2 * HEADS), :] = kv[:, h * HEAD_DIM:(h + 1) * HEAD_DIM]


def _kv_tail(x, g, c_in, *, group):
    win = SWA_GROUPS[group][0]
    nb, s, _ = x.shape
    tm = min(win, 512)
    first = (s - win) // tm
    steps = win // tm
    wcol = lambda j: pl.BlockSpec((D_MODEL, WIDTH), lambda b, t: (0, 3 * group + j), pipeline_mode=pl.Buffered(1))
    kv = pl.pallas_call(
        _kv_tail_body,
        grid=(nb, steps),
        in_specs=[pl.BlockSpec((1, tm, D_MODEL), lambda b, t: (b, first + t, 0)), _resident((1, D_MODEL)),
                  wcol(1), wcol(2)],
        out_specs=pl.BlockSpec((tm * 2 * HEADS, HEAD_DIM), lambda b, t: (b * steps + t, 0)),
        out_shape=jax.ShapeDtypeStruct((nb * win * 2 * HEADS, HEAD_DIM), F32),
        compiler_params=_params(("arbitrary", "arbitrary")),
        name="kv_tail%d" % group,
    )(x, g.reshape(1, D_MODEL), c_in, c_in)
    return kv.reshape(1, nb, win, 2, HEADS, HEAD_DIM)


def _proj_body(x_ref, g_ref, w_ref, z_ref):
    z_ref[...] = _dot(_rms(x_ref[...], g_ref[...]).astype(BF16), w_ref[...])


def _proj(x, g, w):
    m, n = x.shape[0], w.shape[1]
    return pl.pallas_call(
        _proj_body,
        grid=(1,),
        in_specs=[_resident(x.shape), _resident((1, D_MODEL)), _resident(w.shape)],
        out_specs=pl.BlockSpec((m, n), lambda i: (0, 0)),
        out_shape=jax.ShapeDtypeStruct((m, n), F32),
        compiler_params=_params(("arbitrary",)),
        name="proj_rows",
    )(x, g.reshape(1, D_MODEL), w)


def _out_proj_body(c_ref, w_ref, x_ref, y_ref):
    y_ref[...] = x_ref[...] + _dot(c_ref[...].astype(BF16), w_ref[...])


def _out_proj(cat, w, x):
    return pl.pallas_call(
        _out_proj_body,
        grid=(1,),
        in_specs=[_resident(cat.shape), _resident(w.shape), _resident(x.shape)],
        out_specs=pl.BlockSpec(x.shape, lambda i: (0, 0)),
        out_shape=jax.ShapeDtypeStruct(x.shape, F32),
        compiler_params=_params(("arbitrary",)),
        name="out_proj_rows",
    )(cat, w, x)


def _mixer_ab_step_body(z_ref, bif_ref, sg_ref, w00_ref, b0_ref, c_ref, n_ref, m_ref,
                        cat_ref, c1_ref, n1_ref, m1_ref, vn_ref):
    nb = z_ref.shape[0]
    scale = HEAD_DIM ** -0.5
    eye = (lax.broadcasted_iota(jnp.int32, (HEAD_DIM, HEAD_DIM), 0)
           == lax.broadcasted_iota(jnp.int32, (HEAD_DIM, HEAD_DIM), 1)).astype(F32)
    lane = lax.broadcasted_iota(jnp.int32, (1, LANES), 1)
    for i in range(nb):
        zr = z_ref[i:i + 1, :]
        gates = zr[:, COL_GATES:COL_GATES + LANES] + bif_ref[...]
        m_new = jnp.zeros((1, LANES), F32)
        for h in range(HEADS):
            hs = slice(h * HEAD_DIM, (h + 1) * HEAD_DIM)
            q = zr[:, hs]
            k = zr[:, WIDTH + h * HEAD_DIM:WIDTH + (h + 1) * HEAD_DIM] * scale
            v = zr[:, 2 * WIDTH + h * HEAD_DIM:2 * WIDTH + (h + 1) * HEAD_DIM]
            o = zr[:, 3 * WIDTH + h * HEAD_DIM:3 * WIDTH + (h + 1) * HEAD_DIM]
            ig = gates[:, h:h + 1]
            lf = _log_sigmoid(gates[:, HEADS + h:HEADS + h + 1])
            c0 = c_ref[i, h]
            n0 = n_ref[i, h]
            m0 = m_ref[i:i + 1, h:h + 1]
            a = m0 + lf
            m = jnp.maximum(a, ig)
            s = jnp.sum(q * k, axis=-1, keepdims=True) * jnp.exp(ig - m)
            inter = jnp.exp(a - m)
            cq_col = jnp.sum(c0 * q, axis=-1, keepdims=True)
            cq = jnp.sum(eye * cq_col, axis=0, keepdims=True)
            v_col = jnp.sum(eye * v, axis=-1, keepdims=True)
            num = s * v + inter * cq
            den = s + inter * jnp.sum(n0 * q, axis=-1, keepdims=True)
            hh = num / jnp.maximum(jnp.abs(den), jnp.exp(-m))
            w = jnp.exp(ig - m)
            c1_ref[i, h] = inter * c0 + (w * v_col) * k
            n1_ref[i, h] = inter * n0 + w * k
            m_new = jnp.where(lane == h, m, m_new)
            cat_ref[i:i + 1, hs] = jax.nn.sigmoid(o) * hh
        m1_ref[i:i + 1, :] = m_new
        vn = _rms(zr[:, COL_GV:COL_GV + WIDTH], sg_ref[...])
        vn_ref[i:i + 1, :] = vn
        cat_ref[i:i + 1, WIDTH:2 * WIDTH] = zr[:, COL_U:COL_U + WIDTH] * (w00_ref[...] * vn + b0_ref[...])


def _mixer_ab_step(z, b_if, sgu_g, w00, b0, st_c, st_n, st_m, *, nb):
    n = z.shape[0]
    rows = lambda w: pl.BlockSpec((nb, w), lambda i: (i, 0))
    c_spec = pl.BlockSpec((nb, HEADS, HEAD_DIM, HEAD_DIM), lambda i: (i, 0, 0, 0))
    n_spec = pl.BlockSpec((nb, HEADS, 1, HEAD_DIM), lambda i: (i, 0, 0, 0))
    return pl.pallas_call(
        _mixer_ab_step_body,
        grid=(n // nb,),
        in_specs=[rows(A_IN_PAD), _resident((1, LANES)), _resident((1, WIDTH)), _resident((1, WIDTH)),
                  _resident((1, WIDTH)), c_spec, n_spec, rows(LANES)],
        out_specs=[rows(2 * WIDTH), c_spec, n_spec, rows(LANES), rows(WIDTH)],
        out_shape=[jax.ShapeDtypeStruct((n, 2 * WIDTH), F32),
                   jax.ShapeDtypeStruct(st_c.shape, F32),
                   jax.ShapeDtypeStruct(st_n.shape, F32),
                   jax.ShapeDtypeStruct((n, LANES), F32),
                   jax.ShapeDtypeStruct((n, WIDTH), F32)],
        compiler_params=_params(("arbitrary",)),
        name="mixer_ab_step",
    )(z, b_if, sgu_g.reshape(1, WIDTH), w00, b0, st_c, st_n, st_m)


def _swa_step_body(z_ref, kv0_ref, kv1_ref, kv2_ref, cat_ref):
    nb = z_ref.shape[0]
    scale = HEAD_DIM ** -0.5
    steps = (CHUNK - lax.broadcasted_iota(jnp.int32, (CHUNK, 1, 1), 0)).astype(F32)
    head = lax.broadcasted_iota(jnp.int32, (1, HEADS, 1), 1)
    for i in range(nb):
        ms, ls, os_ = [], [], []
        for gi, kv_ref in enumerate((kv0_ref, kv1_ref, kv2_ref)):
            dil = SWA_GROUPS[gi][1]
            base = gi * 3 * HEADS
            q = z_ref[i, base:base + HEADS, :]
            k_new = z_ref[i, base + HEADS:base + 2 * HEADS, :]
            v_new = z_ref[i, base + 2 * HEADS:base + 3 * HEADS, :]
            kc = kv_ref[i, :, 0, 0, :, :]
            vc = kv_ref[i, :, 0, 1, :, :]
            slope = jnp.zeros((1, HEADS, 1), F32)
            for h in range(HEADS):
                slope = jnp.where(head == h, _alibi_slope(gi, h) * dil, slope)
            s = jnp.sum(kc * q[None], axis=-1, keepdims=True) * scale - slope * steps
            s_new = jnp.sum(k_new * q, axis=-1, keepdims=True) * scale
            m = jnp.maximum(jnp.max(s, axis=0), s_new)
            p = jnp.exp(s - m[None])
            p_new = jnp.exp(s_new - m)
            ms.append(m)
            ls.append(jnp.sum(p, axis=0) + p_new)
            os_.append(jnp.sum(p * vc, axis=0) + p_new * v_new)
        top = functools.reduce(jnp.maximum, ms)
        ws = [jnp.exp(m - top) for m in ms]
        num = sum(w * o for w, o in zip(ws, os_))
        den = sum(w * l for w, l in zip(ws, ls))
        cat_ref[i] = num / den


def _swa_step(z, caches, *, nb):
    n = z.shape[0]
    views = []
    specs = []
    for (win, dil), cache in zip(SWA_GROUPS, caches):
        views.append(cache.reshape(n, win // dil, dil, 2, HEADS, HEAD_DIM))
        specs.append(pl.BlockSpec((nb, CHUNK, 1, 2, HEADS, HEAD_DIM), lambda i: (i, 0, 0, 0, 0, 0)))
    return pl.pallas_call(
        _swa_step_body,
        grid=(n // nb,),
        in_specs=[pl.BlockSpec((nb,) + z.shape[1:], lambda i: (i, 0, 0))] + specs,
        out_specs=pl.BlockSpec((nb, HEADS, HEAD_DIM), lambda i: (i, 0, 0)),
        out_shape=jax.ShapeDtypeStruct((n, HEADS, HEAD_DIM), F32),
        compiler_params=_params(("arbitrary",)),
        name="swa_step",
    )(z, *views)


def _pad_cols(w, n):
    return jnp.pad(w, ((0, 0), (0, n - w.shape[1])))


def kernel(x_prompt, x_sample, state_mlstm_C, state_mlstm_n, state_mlstm_m, cache_swa_kv0, cache_swa_kv1, cache_swa_kv2, norm_g, ffn_w_gate, ffn_w_up, ffn_w_down, a_w_in, a_b_if, sgu_norm_g, sgu_w, sgu_b, a_w_out, c_w_in, c_w_out, final_norm_g):
    nb, s, _ = x_prompt.shape
    ns = x_sample.shape[0]
    assert x_sample.shape[1] == 1 and s % max(w for w, _ in SWA_GROUPS) == 0
    for (win, dil), cache in zip(SWA_GROUPS, (cache_swa_kv0, cache_swa_kv1, cache_swa_kv2)):
        assert cache.shape[2] == win and win // dil == CHUNK

    ffn_w = (jnp.swapaxes(ffn_w_gate, 2, 3), jnp.swapaxes(ffn_w_up, 2, 3), ffn_w_down)
    g_lo, g_hi = 4 * WIDTH, 4 * WIDTH + 2 * HEADS
    a_in = jnp.concatenate([a_w_in[0][:, :g_lo], a_w_in[0][:, g_hi:], _pad_cols(a_w_in[0][:, g_lo:g_hi], LANES)],
                           axis=1).astype(BF16)
    b_if = _pad_cols(a_b_if[0].reshape(1, 2 * HEADS), LANES)
    a_out = a_w_out[0].astype(BF16)
    c_in = c_w_in[0].astype(BF16)
    c_out = c_w_out[0].astype(BF16)
    sgu_bt = _pad_cols(sgu_b[0].T, LANES)
    sgu_w00 = jnp.repeat(sgu_w[0, :, 0, 0], CHUNK).reshape(1, WIDTH)
    sgu_b0 = jnp.repeat(sgu_b[0, :, 0], CHUNK).reshape(1, WIDTH)

    xp = x_prompt.reshape(nb * s, D_MODEL)
    xs = x_sample.reshape(ns, D_MODEL)
    xp, xs = _ffn(xp, xs, norm_g[0, 0], ffn_w, 0, 0, tm=512)
    xp, p_c, p_n, p_m = _mixer_ab_prompt(xp.reshape(nb, s, D_MODEL), norm_g[0, 1], a_in, b_if,
                                          sgu_norm_g[0], sgu_w[0], sgu_bt, a_out)
    z = _proj(xs, norm_g[0, 1], a_in)
    cat, s_c, s_n, s_m, s_v = _mixer_ab_step(
        z, b_if, sgu_norm_g[0], sgu_w00, sgu_b0, state_mlstm_C[0],
        state_mlstm_n[0].reshape(ns, HEADS, 1, HEAD_DIM), _pad_cols(state_mlstm_m[0], LANES), nb=8)
    xs = _out_proj(cat, a_out, xs)
    xp, xs = _ffn(xp.reshape(nb * s, D_MODEL), xs, norm_g[0, 2], ffn_w, 0, 1, tm=512)
    xp, xs = _ffn(xp, xs, norm_g[1, 0], ffn_w, 1, 0, tm=512)
    xp3 = xp.reshape(nb, s, D_MODEL)
    outs, stats = [], []
    for gi in (1, 2):
        o, st = _swa_group_prompt(xp3, norm_g[1, 1], c_in, group=gi, tile=2048)
        outs.append(o)
        stats.append(st)
    p_kv = [_kv_tail(xp3, norm_g[1, 1], c_in, group=gi) for gi in range(3)]
    xp = _swa_group_prompt(xp3, norm_g[1, 1], c_in, group=0, tile=1024, merge_with=(outs, stats, c_out))
    xp = xp.reshape(nb * s, D_MODEL)
    z = _proj(xs, norm_g[1, 1], c_in)
    cat = _swa_step(z.reshape(ns, 3 * 3 * HEADS, HEAD_DIM), (cache_swa_kv0, cache_swa_kv1, cache_swa_kv2), nb=4)
    xs = _out_proj(cat.reshape(ns, WIDTH), c_out, xs)
    y_prompt, y_sample = _ffn(xp, xs, norm_g[1, 2], ffn_w, 1, 1, final_norm_g, tm=512)
    y_prompt = y_prompt.reshape(nb, s, D_MODEL)
    y_sample = y_sample.reshape(ns, 1, D_MODEL)
    s_kv = [z[:, gi * C_GROUP_COLS + WIDTH:(gi + 1) * C_GROUP_COLS].reshape(1, ns, 1, 2, HEADS, HEAD_DIM)
            for gi in range(3)]

    return (y_prompt, y_sample,
            p_c.reshape(1, nb, HEADS, HEAD_DIM, HEAD_DIM), p_n.reshape(1, nb, HEADS, HEAD_DIM),
            p_m[:, :, 0, 0].reshape(1, nb, HEADS),
            s_c.reshape(1, ns, HEADS, HEAD_DIM, HEAD_DIM), s_n.reshape(1, ns, HEADS, HEAD_DIM),
            s_m[:, :HEADS].reshape(1, ns, HEADS), s_v.reshape(1, ns, 1, WIDTH),
            p_kv[0], p_kv[1], p_kv[2], s_kv[0], s_kv[1], s_kv[2])
```

```python
import functools

import jax
import jax.numpy as jnp
from jax import lax
from jax.experimental import pallas as pl
from jax.experimental.pallas import tpu as pltpu

F32 = jnp.float32
BF16 = jnp.bfloat16

D_MODEL = 1024
D_FF = 2752
HEADS = 4
HEAD_DIM = 128
WIDTH = HEADS * HEAD_DIM
CHUNK = 128
SWA_GROUPS = ((128, 1), (512, 4), (2048, 16))
NORM_EPS = 1e-6
NEG_INF = -1e30

LANES = 128
BF16_ROWS = 16
PROJ_SLAB = 256
PROJ_ROWS = 512
FF_CHUNK = 256
A_IN_PAD = 4 * WIDTH + 2 * WIDTH + LANES
COL_U = 4 * WIDTH
COL_GV = 5 * WIDTH
COL_GATES = 6 * WIDTH
C_GROUP_COLS = 3 * WIDTH

VMEM_LIMIT = 60 * 1024 * 1024


def _params(semantics):
    return pltpu.CompilerParams(dimension_semantics=semantics, vmem_limit_bytes=VMEM_LIMIT)


def _resident(shape):
    nd = len(shape)
    return pl.BlockSpec(shape, lambda *_: (0,) * nd, pipeline_mode=pl.Buffered(1))


def _rms(x, g):
    ms = jnp.mean(x * x, axis=-1, keepdims=True)
    return x * lax.rsqrt(ms + NORM_EPS) * g


def _dot(a, b):
    return jnp.dot(a, b, preferred_element_type=F32)


def _dot_nt(a, b):
    return lax.dot_general(a, b, (((1,), (1,)), ((), ())), preferred_element_type=F32)


def _log_sigmoid(x):
    return jnp.minimum(x, 0.0) - jnp.log1p(jnp.exp(-jnp.abs(x)))


def _swiglu_rows(h_scr, act_scr, rows, wg_ref, wu_ref, wd_ref):
    for c0 in range(0, D_FF, FF_CHUNK):
        cols = slice(c0, min(c0 + FF_CHUNK, D_FF))
        h = h_scr[0:rows, :]
        gate = _dot_nt(h, wg_ref[0, 0, cols, :])
        up = _dot_nt(h, wu_ref[0, 0, cols, :])
        act_scr[0:rows, cols] = (gate * jax.nn.sigmoid(gate) * up).astype(act_scr.dtype)
    whole = D_FF // FF_CHUNK * FF_CHUNK
    out = _dot(act_scr[0:rows, 0:whole], wd_ref[0, 0, 0:whole, :])
    if whole < D_FF:
        out = out + _dot(act_scr[0:rows, whole:D_FF], wd_ref[0, 0, whole:D_FF, :])
    return out


def _ffn_body(*refs, final):
    if final:
        x_ref, xs_ref, g_ref, wg_ref, wu_ref, wd_ref, fg_ref, o_ref, os_ref, h_scr, act_scr = refs
    else:
        x_ref, xs_ref, g_ref, wg_ref, wu_ref, wd_ref, o_ref, os_ref, h_scr, act_scr = refs
    tm, ns = x_ref.shape[0], xs_ref.shape[0]
    last = pl.num_programs(0) - 1

    def finish(x, acc):
        y = x + 0.5 * acc
        return _rms(y, fg_ref[...]) if final else y

    x = x_ref[...]
    h_scr[0:tm, :] = _rms(x, g_ref[...]).astype(h_scr.dtype)

    @pl.when(pl.program_id(0) != last)
    def _():
        o_ref[...] = finish(x, _swiglu_rows(h_scr, act_scr, tm, wg_ref, wu_ref, wd_ref))

    @pl.when(pl.program_id(0) == last)
    def _():
        xs = xs_ref[...]
        h_scr[tm:tm + ns, :] = _rms(xs, g_ref[...]).astype(h_scr.dtype)
        acc = _swiglu_rows(h_scr, act_scr, tm + ns, wg_ref, wu_ref, wd_ref)
        o_ref[...] = finish(x, acc[0:tm])
        os_ref[...] = finish(xs, acc[tm:tm + ns])


def _ffn(x, xs, g, weights, layer, which, final_g=None, *, tm):
    m, ns = x.shape[0], xs.shape[0]
    final = final_g is not None
    wg, wu, wd = weights
    row = pl.BlockSpec((tm, D_MODEL), lambda i: (i, 0))
    wspec = lambda w: pl.BlockSpec((1, 1) + w.shape[2:], lambda i: (layer, which, 0, 0), pipeline_mode=pl.Buffered(1))
    in_specs = [row, _resident(xs.shape), _resident((1, D_MODEL)), wspec(wg), wspec(wu), wspec(wd)]
    args = [x, xs, g.reshape(1, D_MODEL), wg, wu, wd]
    if final:
        in_specs.append(_resident((1, D_MODEL)))
        args.append(final_g.reshape(1, D_MODEL))
    return pl.pallas_call(
        functools.partial(_ffn_body, final=final),
        grid=(m // tm,),
        in_specs=in_specs,
        out_specs=[row, pl.BlockSpec(xs.shape, lambda i: (0, 0))],
        out_shape=[jax.ShapeDtypeStruct((m, D_MODEL), F32), jax.ShapeDtypeStruct(xs.shape, F32)],
        scratch_shapes=[pltpu.VMEM((tm + ns, D_MODEL), wg.dtype), pltpu.VMEM((tm + ns, D_FF), wg.dtype)],
        compiler_params=_params(("arbitrary",)),
        name="ffn_final" if final else "ffn",
    )(*args)


def _cummax_lanes(x):
    lane = lax.broadcasted_iota(jnp.int32, x.shape, 1)
    d = 1
    while d < x.shape[1]:
        x = jnp.maximum(x, jnp.where(lane >= d, pltpu.roll(x, d, axis=1), NEG_INF))
        d *= 2
    return x


def _exact_tri_dot(tri_bf16, x):
    x1 = x.astype(BF16)
    r1 = x - x1.astype(F32)
    x2 = r1.astype(BF16)
    x3 = (r1 - x2.astype(F32)).astype(BF16)
    n = x.shape[1]
    r = _dot(tri_bf16, jnp.concatenate([x1, x2, x3], axis=1))
    return r[:, 0:n] + r[:, n:2 * n] + r[:, 2 * n:3 * n]


def _mixer_ab_chunk(xn_ref, xp_ref, g_ref, win_ref, bif_ref, sg_ref, sw_ref, sbt_ref, wout_ref,
                   y_ref, c_ref, n_ref, m_ref, z_cur, z_nxt, cat_scr):
    nb = xn_ref.shape[0]
    row = lax.broadcasted_iota(jnp.int32, (CHUNK, CHUNK), 0)
    col = lax.broadcasted_iota(jnp.int32, (CHUNK, CHUNK), 1)
    causal = col <= row
    keys_before = row <= col
    tri = jnp.where(causal, 1.0, 0.0).astype(BF16)
    scale = HEAD_DIM ** -0.5

    hn = _rms(xn_ref[...].reshape(nb * CHUNK, D_MODEL), g_ref[...]).astype(BF16)
    slabs = [(c0, min(c0 + PROJ_SLAB, A_IN_PAD)) for c0 in range(0, A_IN_PAD, PROJ_SLAB)]

    def project(count):
        for _ in range(min(count, len(slabs))):
            c0, c1 = slabs.pop(0)
            z_nxt[:, c0:c1] = _dot(hn, win_ref[:, c0:c1])

    sgu_bias = [jnp.broadcast_to(sbt_ref[:, g:g + 1], (CHUNK, CHUNK)) for g in range(HEADS)]
    vn = [_rms(z_cur[b * CHUNK:(b + 1) * CHUNK, COL_GV:COL_GV + WIDTH], sg_ref[...]) for b in range(nb)]
    for g in range(HEADS):
        gs = slice(g * CHUNK, (g + 1) * CHUNK)
        mixed = _dot(jnp.where(causal, sw_ref[g], 0.0).astype(BF16),
                     jnp.concatenate([vn[b][:, gs] for b in range(nb)], axis=1).astype(BF16))
        for b in range(nb):
            rows = slice(b * CHUNK, (b + 1) * CHUNK)
            u = z_cur[rows, COL_U + g * CHUNK:COL_U + (g + 1) * CHUNK]
            cat_scr[rows, WIDTH + g * CHUNK:WIDTH + (g + 1) * CHUNK] = (
                u * (mixed[:, b * CHUNK:(b + 1) * CHUNK] + sgu_bias[g])).astype(BF16)
    project(1)

    pieces_per_stage = -(-(len(slabs)) // (3 * nb))
    gate_terms = []
    for b in range(nb):
        rows = slice(b * CHUNK, (b + 1) * CHUNK)
        gates = z_cur[rows, COL_GATES:COL_GATES + LANES] + bif_ref[...]
        lg = jnp.where(col < HEADS, gates, _log_sigmoid(gates))
        gate_terms.append((lg, _exact_tri_dot(tri, lg)))
        project(pieces_per_stage)

    heads = []
    for b, (lg, fcum) in enumerate(gate_terms):
        rows = slice(b * CHUNK, (b + 1) * CHUNK)
        lg_t = lg.T
        fcum_t = fcum.T
        gmax = _cummax_lanes(lg_t[0:2 * HEADS, :] - jnp.concatenate([fcum_t[HEADS:2 * HEADS, :]] * 2, axis=0))
        for h in range(HEADS):
            q = z_cur[rows, h * HEAD_DIM:(h + 1) * HEAD_DIM]
            k = z_cur[rows, WIDTH + h * HEAD_DIM:WIDTH + (h + 1) * HEAD_DIM] * scale
            v_t = z_cur[rows, 2 * WIDTH + h * HEAD_DIM:2 * WIDTH + (h + 1) * HEAD_DIM].T
            f_r = fcum_t[HEADS + h:HEADS + h + 1, :]
            i_r = lg_t[h:h + 1, :]
            g_c = lg[:, h:h + 1] - fcum[:, HEADS + h:HEADS + h + 1]
            c0, n0, m0 = c_ref[b, h], n_ref[b, h], m_ref[b, h]
            m = f_r + jnp.maximum(m0, gmax[h:h + 1, :])
            inter = jnp.exp(m0 + f_r - m)
            qb, kb = q.astype(BF16), k.astype(BF16)
            against_q = _dot_nt(
                jnp.concatenate([k, c0, jnp.broadcast_to(n0, (BF16_ROWS, HEAD_DIM))], axis=0).astype(BF16), qb)
            m_last = m[:, CHUNK - 1:CHUNK]
            f_last = f_r[:, CHUNK - 1:CHUNK]
            w = jnp.exp(f_last - f_r + i_r - m_last)
            decay = jnp.exp(m0[:, 0:1] + f_last - m_last)
            against_k = _dot(
                jnp.concatenate([v_t * w, jnp.broadcast_to(w, (BF16_ROWS, CHUNK))], axis=0).astype(BF16), kb)
            c_ref[b, h] = decay * c0 + against_k[0:HEAD_DIM]
            n_ref[b, h] = decay * n0 + against_k[HEAD_DIM:HEAD_DIM + 1]
            m_ref[b, h] = jnp.broadcast_to(m_last, (1, LANES))
            heads.append((against_q, v_t, f_r, g_c, m, inter))
        project(pieces_per_stage)

    partial = []
    for i, (against_q, v_t, f_r, g_c, m, inter) in enumerate(heads):
        kq = against_q[0:CHUNK]
        cq = against_q[CHUNK:CHUNK + HEAD_DIM]
        nq = against_q[CHUNK + HEAD_DIM:CHUNK + HEAD_DIM + 1]
        s_t = kq * jnp.exp(jnp.where(keys_before, (f_r - m) + g_c, NEG_INF))
        den = jnp.sum(s_t, axis=0, keepdims=True) + inter * nq
        partial.append((_dot(v_t.astype(BF16), s_t.astype(BF16)), inter * cq,
                        jnp.maximum(jnp.abs(den), jnp.exp(-m))))
        if i % HEADS == HEADS - 1:
            project(pieces_per_stage)

    for i, (sv, carried, den) in enumerate(partial):
        b, h = divmod(i, HEADS)
        rows = slice(b * CHUNK, (b + 1) * CHUNK)
        o = z_cur[rows, 3 * WIDTH + h * HEAD_DIM:3 * WIDTH + (h + 1) * HEAD_DIM]
        cat_scr[rows, h * HEAD_DIM:(h + 1) * HEAD_DIM] = (jax.nn.sigmoid(o) * ((sv + carried) / den).T).astype(BF16)

    project(len(slabs))
    y = xp_ref[...].reshape(nb * CHUNK, D_MODEL) + _dot(cat_scr[...], wout_ref[...])
    y_ref[...] = y.reshape(nb, CHUNK, D_MODEL)


def _mixer_ab_body(*refs):
    *io_refs, z0_scr, z1_scr, cat_scr = refs
    c_ref, n_ref, m_ref = io_refs[-3:]
    step = pl.program_id(0)

    @pl.when(step <= 1)
    def _():
        c_ref[...] = jnp.zeros_like(c_ref)
        n_ref[...] = jnp.zeros_like(n_ref)
        m_ref[...] = jnp.zeros_like(m_ref)

    @pl.when(step == 0)
    def _():
        z1_scr[...] = jnp.zeros_like(z1_scr)

    @pl.when(step % 2 == 0)
    def _():
        _mixer_ab_chunk(*io_refs, z1_scr, z0_scr, cat_scr)

    @pl.when(step % 2 == 1)
    def _():
        _mixer_ab_chunk(*io_refs, z0_scr, z1_scr, cat_scr)


def _mixer_ab_prompt(x, g, w_in, b_if, sgu_g, sgu_w, sgu_bt, w_out):
    nb, s, _ = x.shape
    n_chunks = s // CHUNK
    blk = lambda index: pl.BlockSpec((nb, CHUNK, D_MODEL), index)
    nxt = blk(lambda c: (0, jnp.minimum(c, n_chunks - 1), 0))
    prev = blk(lambda c: (0, jnp.maximum(c - 1, 0), 0))
    z_shape = pltpu.VMEM((nb * CHUNK, A_IN_PAD), F32)
    return pl.pallas_call(
        _mixer_ab_body,
        grid=(n_chunks + 1,),
        in_specs=[nxt, prev, _resident((1, D_MODEL)), _resident(w_in.shape), _resident((1, LANES)),
                  _resident((1, WIDTH)), _resident(sgu_w.shape), _resident(sgu_bt.shape), _resident(w_out.shape)],
        out_specs=[prev,
                   pl.BlockSpec((nb, HEADS, HEAD_DIM, HEAD_DIM), lambda c: (0, 0, 0, 0)),
                   pl.BlockSpec((nb, HEADS, 1, HEAD_DIM), lambda c: (0, 0, 0, 0)),
                   pl.BlockSpec((nb, HEADS, 1, LANES), lambda c: (0, 0, 0, 0))],
        out_shape=[jax.ShapeDtypeStruct(x.shape, F32),
                   jax.ShapeDtypeStruct((nb, HEADS, HEAD_DIM, HEAD_DIM), F32),
                   jax.ShapeDtypeStruct((nb, HEADS, 1, HEAD_DIM), F32),
                   jax.ShapeDtypeStruct((nb, HEADS, 1, LANES), F32)],
        scratch_shapes=[z_shape, z_shape, pltpu.VMEM((nb * CHUNK, 2 * WIDTH), BF16)],
        compiler_params=_params(("arbitrary",)),
        name="mixer_ab_prompt",
    )(x, x, g.reshape(1, D_MODEL), w_in, b_if, sgu_g.reshape(1, WIDTH), sgu_w, sgu_bt, w_out)


def _alibi_slope(group, head):
    n = len(SWA_GROUPS) * HEADS
    return 2.0 ** (-8.0 * (group * HEADS + head + 1) / n)


def _slab_pitch(dil):
    return dil + 8 if dil % 8 == 0 else dil


def _swa_group_body(*refs, group, dil, tile, others):
    x_ref, g_ref, w_ref = refs[0:3]
    if others:
        other_o = refs[3:3 + others]
        other_st = refs[3 + others:3 + 2 * others]
        wout_ref, y_ref, slab_scr, q_scr, k_scr, v_scr, o_dst, st_dst, cat_scr = refs[3 + 2 * others:]
    else:
        o_ref, st_ref, slab_scr, q_scr, k_scr, v_scr = refs[3:]
        o_dst, st_dst = o_ref.at[0], st_ref.at[0]
    step = pl.program_id(1)
    sub = tile // dil
    if sub >= PROJ_ROWS:
        pieces = [[(r, f, PROJ_ROWS)] for r in range(dil) for f in range(0, sub, PROJ_ROWS)]
    else:
        per = PROJ_ROWS // sub
        pieces = [[(r, 0, sub) for r in range(p * per, (p + 1) * per)] for p in range(dil // per)]
    scale = HEAD_DIM ** -0.5
    qi = lax.broadcasted_iota(jnp.int32, (CHUNK, 2 * CHUNK), 0)
    kc = lax.broadcasted_iota(jnp.int32, (CHUNK, 2 * CHUNK), 1)
    delta = CHUNK + qi - kc
    valid = (delta >= 0) & (delta <= CHUNK)
    valid_first = valid & (kc >= jnp.where(step > 0, 0, CHUNK))
    dist = (delta * dil).astype(F32)
    lane = lax.broadcasted_iota(jnp.int32, (CHUNK, LANES), 1)

    @pl.when(step == 0)
    def _():
        k_scr[:, 0:CHUNK, :] = jnp.zeros((dil, CHUNK, WIDTH), BF16)
        v_scr[:, 0:CHUNK, :] = jnp.zeros((dil, CHUNK, WIDTH), BF16)

    pitch = _slab_pitch(dil)
    for c in range(tile // 256):
        hn = _rms(x_ref[0, c * 256:(c + 1) * 256, :], g_ref[...])
        groups = [(0, 256)] if pitch == dil else [(g * dil, dil) for g in range(256 // dil)]
        for first, count in groups:
            dst = (c * 256 + first) // dil * pitch
            for sl in range(D_MODEL // LANES):
                slab_scr[sl, dst:dst + count, :] = hn[first:first + count, sl * LANES:(sl + 1) * LANES]

    def token_rows(r, first, count):
        start = first * dil + r
        return slice(start, start + count) if dil == 1 else pl.ds(start, count, stride=dil)

    def project(piece):
        parts = []
        for r, first, count in piece:
            src = slice(first, first + count) if dil == 1 else pl.ds(first * pitch + r, count, stride=pitch)
            parts.append(jnp.concatenate([slab_scr[sl, src, :] for sl in range(D_MODEL // LANES)],
                                         axis=1).astype(BF16))
        hn = parts[0] if len(parts) == 1 else jnp.concatenate(parts, axis=0)
        for j, scr in enumerate((q_scr, k_scr, v_scr)):
            z = _dot(hn, w_ref[:, j * WIDTH:(j + 1) * WIDTH]).astype(BF16)
            at = 0
            for r, first, count in piece:
                off = first if j == 0 else CHUNK + first
                scr[r, off:off + count, :] = z[at:at + count, :]
                at += count

    def scores(piece):
        out = []
        for r, first, count in piece:
            for j in range(first // CHUNK, (first + count) // CHUNK):
                mask = valid_first if j == 0 else valid
                for h in range(HEADS):
                    hs = slice(h * HEAD_DIM, (h + 1) * HEAD_DIM)
                    s = _dot_nt(q_scr[r, j * CHUNK:(j + 1) * CHUNK, hs], k_scr[r, j * CHUNK:(j + 2) * CHUNK, hs])
                    s = jnp.where(mask, s * scale + (-_alibi_slope(group, h)) * dist, NEG_INF)
                    m = jnp.max(s, axis=-1, keepdims=True)
                    p_ = jnp.exp(s - m)
                    out.append((r, j, h, m, jnp.sum(p_, axis=-1, keepdims=True), p_.astype(BF16)))
        return out

    def values(items):
        stats = None
        for r, j, h, m, l, p_ in items:
            hs = slice(h * HEAD_DIM, (h + 1) * HEAD_DIM)
            o_dst[h, token_rows(r, j * CHUNK, CHUNK), :] = _dot(p_, v_scr[r, j * CHUNK:(j + 2) * CHUNK, hs])
            base = jnp.zeros((CHUNK, LANES), F32) if h == 0 else stats
            stats = jnp.where(lane == h, m, jnp.where(lane == HEADS + h, l, base))
            if h == HEADS - 1:
                st_dst[token_rows(r, j * CHUNK, CHUNK), :] = stats

    def halves(piece):
        if len(piece) == 1:
            r, first, count = piece[0]
            return [[(r, first, count // 2)], [(r, first + count // 2, count // 2)]]
        return [piece[:len(piece) // 2], piece[len(piece) // 2:]]

    def merge(piece):
        (_, first, count), = piece
        for c in range(first // 256, (first + count) // 256):
            rows = slice(c * 256, (c + 1) * 256)
            stats = [st_dst[rows, :]] + [s_ref[0, rows, :] for s_ref in other_st]
            for h in range(HEADS):
                outs = [o_dst[h, rows, :]] + [o_ref[0, h, rows, :] for o_ref in other_o]
                ms = [st[:, h:h + 1] for st in stats]
                ls = [st[:, HEADS + h:HEADS + h + 1] for st in stats]
                top = functools.reduce(jnp.maximum, ms)
                ws = [jnp.exp(m - top) for m in ms]
                num = sum(w * o for w, o in zip(ws, outs))
                den = sum(w * l for w, l in zip(ws, ls))
                cat_scr[rows, h * HEAD_DIM:(h + 1) * HEAD_DIM] = (num / den).astype(BF16)
        rows = slice(first, first + count)
        y_ref[0, rows, :] = x_ref[0, rows, :] + _dot(cat_scr[rows, :], wout_ref[...])

    def finish(done):
        items, p, i = done
        values(items)
        if others and i == 1:
            merge(pieces[p])

    project(pieces[0])
    pending = None
    for p, piece in enumerate(pieces):
        for i, half in enumerate(halves(piece)):
            items = scores(half)
            if i == 0 and p + 1 < len(pieces):
                project(pieces[p + 1])
            if pending is not None:
                finish(pending)
            pending = (items, p, i)
    finish(pending)
    for r in range(dil):
        k_scr[r, 0:CHUNK, :] = k_scr[r, sub:sub + CHUNK, :]
        v_scr[r, 0:CHUNK, :] = v_scr[r, sub:sub + CHUNK, :]


def _swa_group_prompt(x, g, c_in, *, group, tile, merge_with=None):
    nb, s, _ = x.shape
    dil = SWA_GROUPS[group][1]
    sub = tile // dil
    o_spec = pl.BlockSpec((1, HEADS, tile, HEAD_DIM), lambda b, t: (b, 0, t, 0))
    s_spec = pl.BlockSpec((1, tile, LANES), lambda b, t: (b, t, 0))
    x_spec = pl.BlockSpec((1, tile, D_MODEL), lambda b, t: (b, t, 0))
    in_specs = [x_spec, _resident((1, D_MODEL)),
                pl.BlockSpec((D_MODEL, C_GROUP_COLS), lambda b, t: (0, group), pipeline_mode=pl.Buffered(1))]
    args = [x, g.reshape(1, D_MODEL), c_in]
    scratch = [pltpu.VMEM((D_MODEL // LANES, sub * _slab_pitch(dil), LANES), F32),
               pltpu.VMEM((dil, sub, WIDTH), BF16),
               pltpu.VMEM((dil, CHUNK + sub, WIDTH), BF16),
               pltpu.VMEM((dil, CHUNK + sub, WIDTH), BF16)]
    if merge_with is None:
        others = 0
        out_specs = [o_spec, s_spec]
        out_shape = [jax.ShapeDtypeStruct((nb, HEADS, s, HEAD_DIM), F32), jax.ShapeDtypeStruct((nb, s, LANES), F32)]
    else:
        outs, stats, w_out = merge_with
        others = len(outs)
        in_specs += [o_spec] * others + [s_spec] * others + [_resident(w_out.shape)]
        args += [*outs, *stats, w_out]
        out_specs = x_spec
        out_shape = jax.ShapeDtypeStruct(x.shape, F32)
        scratch += [pltpu.VMEM((HEADS, tile, HEAD_DIM), F32), pltpu.VMEM((tile, LANES), F32),
                    pltpu.VMEM((tile, WIDTH), BF16)]
    return pl.pallas_call(
        functools.partial(_swa_group_body, group=group, dil=dil, tile=tile, others=others),
        grid=(nb, s // tile),
        in_specs=in_specs,
        out_specs=out_specs,
        out_shape=out_shape,
        scratch_shapes=scratch,
        compiler_params=_params(("arbitrary", "arbitrary")),
        name="swa_group%d" % group,
    )(*args)


def _kv_tail_body(x_ref, g_ref, wk_ref, wv_ref, kv_ref):
    tm = x_ref.shape[1]
    hn = _rms(x_ref[0], g_ref[...]).astype(BF16)
    for j, w_ref in enumerate((wk_ref, wv_ref)):
        kv = _dot(hn, w_ref[...])
        for h in range(HEADS):
            kv_ref[pl.ds(j * HEADS + h, tm, stride=2 * HEADS), :] = kv[:, h * HEAD_DIM:(h + 1) * HEAD_DIM]


def _kv_tail(x, g, c_in, *, group):
    win = SWA_GROUPS[group][0]
    nb, s, _ = x.shape
    tm = min(win, 512)
    first = (s - win) // tm
    steps = win // tm
    wcol = lambda j: pl.BlockSpec((D_MODEL, WIDTH), lambda b, t: (0, 3 * group + j), pipeline_mode=pl.Buffered(1))
    kv = pl.pallas_call(
        _kv_tail_body,
        grid=(nb, steps),
        in_specs=[pl.BlockSpec((1, tm, D_MODEL), lambda b, t: (b, first + t, 0)), _resident((1, D_MODEL)),
                  wcol(1), wcol(2)],
        out_specs=pl.BlockSpec((tm * 2 * HEADS, HEAD_DIM), lambda b, t: (b * steps + t, 0)),
        out_shape=jax.ShapeDtypeStruct((nb * win * 2 * HEADS, HEAD_DIM), F32),
        compiler_params=_params(("arbitrary", "arbitrary")),
        name="kv_tail%d" % group,
    )(x, g.reshape(1, D_MODEL), c_in, c_in)
    return kv.reshape(1, nb, win, 2, HEADS, HEAD_DIM)


def _proj_body(x_ref, g_ref, w_ref, z_ref):
    z_ref[...] = _dot(_rms(x_ref[...], g_ref[...]).astype(BF16), w_ref[...])


def _proj(x, g, w):
    m, n = x.shape[0], w.shape[1]
    return pl.pallas_call(
        _proj_body,
        grid=(1,),
        in_specs=[_resident(x.shape), _resident((1, D_MODEL)), _resident(w.shape)],
        out_specs=pl.BlockSpec((m, n), lambda i: (0, 0)),
        out_shape=jax.ShapeDtypeStruct((m, n), F32),
        compiler_params=_params(("arbitrary",)),
        name="proj_rows",
    )(x, g.reshape(1, D_MODEL), w)


def _out_proj_body(c_ref, w_ref, x_ref, y_ref):
    y_ref[...] = x_ref[...] + _dot(c_ref[...].astype(BF16), w_ref[...])


def _out_proj(cat, w, x):
    return pl.pallas_call(
        _out_proj_body,
        grid=(1,),
        in_specs=[_resident(cat.shape), _resident(w.shape), _resident(x.shape)],
        out_specs=pl.BlockSpec(x.shape, lambda i: (0, 0)),
        out_shape=jax.ShapeDtypeStruct(x.shape, F32),
        compiler_params=_params(("arbitrary",)),
        name="out_proj_rows",
    )(cat, w, x)


def _mixer_ab_step_body(z_ref, bif_ref, sg_ref, w00_ref, b0_ref, c_ref, n_ref, m_ref,
                        cat_ref, c1_ref, n1_ref, m1_ref, vn_ref):
    nb = z_ref.shape[0]
    scale = HEAD_DIM ** -0.5
    eye = (lax.broadcasted_iota(jnp.int32, (HEAD_DIM, HEAD_DIM), 0)
           == lax.broadcasted_iota(jnp.int32, (HEAD_DIM, HEAD_DIM), 1)).astype(F32)
    lane = lax.broadcasted_iota(jnp.int32, (1, LANES), 1)
    for i in range(nb):
        zr = z_ref[i:i + 1, :]
        gates = zr[:, COL_GATES:COL_GATES + LANES] + bif_ref[...]
        m_new = jnp.zeros((1, LANES), F32)
        for h in range(HEADS):
            hs = slice(h * HEAD_DIM, (h + 1) * HEAD_DIM)
            q = zr[:, hs]
            k = zr[:, WIDTH + h * HEAD_DIM:WIDTH + (h + 1) * HEAD_DIM] * scale
            v = zr[:, 2 * WIDTH + h * HEAD_DIM:2 * WIDTH + (h + 1) * HEAD_DIM]
            o = zr[:, 3 * WIDTH + h * HEAD_DIM:3 * WIDTH + (h + 1) * HEAD_DIM]
            ig = gates[:, h:h + 1]
            lf = _log_sigmoid(gates[:, HEADS + h:HEADS + h + 1])
            c0 = c_ref[i, h]
            n0 = n_ref[i, h]
            m0 = m_ref[i:i + 1, h:h + 1]
            a = m0 + lf
            m = jnp.maximum(a, ig)
            s = jnp.sum(q * k, axis=-1, keepdims=True) * jnp.exp(ig - m)
            inter = jnp.exp(a - m)
            cq_col = jnp.sum(c0 * q, axis=-1, keepdims=True)
            cq = jnp.sum(eye * cq_col, axis=0, keepdims=True)
            v_col = jnp.sum(eye * v, axis=-1, keepdims=True)
            num = s * v + inter * cq
            den = s + inter * jnp.sum(n0 * q, axis=-1, keepdims=True)
            hh = num / jnp.maximum(jnp.abs(den), jnp.exp(-m))
            w = jnp.exp(ig - m)
            c1_ref[i, h] = inter * c0 + (w * v_col) * k
            n1_ref[i, h] = inter * n0 + w * k
            m_new = jnp.where(lane == h, m, m_new)
            cat_ref[i:i + 1, hs] = jax.nn.sigmoid(o) * hh
        m1_ref[i:i + 1, :] = m_new
        vn = _rms(zr[:, COL_GV:COL_GV + WIDTH], sg_ref[...])
        vn_ref[i:i + 1, :] = vn
        cat_ref[i:i + 1, WIDTH:2 * WIDTH] = zr[:, COL_U:COL_U + WIDTH] * (w00_ref[...] * vn + b0_ref[...])


def _mixer_ab_step(z, b_if, sgu_g, w00, b0, st_c, st_n, st_m, *, nb):
    n = z.shape[0]
    rows = lambda w: pl.BlockSpec((nb, w), lambda i: (i, 0))
    c_spec = pl.BlockSpec((nb, HEADS, HEAD_DIM, HEAD_DIM), lambda i: (i, 0, 0, 0))
    n_spec = pl.BlockSpec((nb, HEADS, 1, HEAD_DIM), lambda i: (i, 0, 0, 0))
    return pl.pallas_call(
        _mixer_ab_step_body,
        grid=(n // nb,),
        in_specs=[rows(A_IN_PAD), _resident((1, LANES)), _resident((1, WIDTH)), _resident((1, WIDTH)),
                  _resident((1, WIDTH)), c_spec, n_spec, rows(LANES)],
        out_specs=[rows(2 * WIDTH), c_spec, n_spec, rows(LANES), rows(WIDTH)],
        out_shape=[jax.ShapeDtypeStruct((n, 2 * WIDTH), F32),
                   jax.ShapeDtypeStruct(st_c.shape, F32),
                   jax.ShapeDtypeStruct(st_n.shape, F32),
                   jax.ShapeDtypeStruct((n, LANES), F32),
                   jax.ShapeDtypeStruct((n, WIDTH), F32)],
        compiler_params=_params(("arbitrary",)),
        name="mixer_ab_step",
    )(z, b_if, sgu_g.reshape(1, WIDTH), w00, b0, st_c, st_n, st_m)


def _swa_step_body(z_ref, kv0_ref, kv1_ref, kv2_ref, cat_ref):
    nb = z_ref.shape[0]
    scale = HEAD_DIM ** -0.5
    steps = (CHUNK - lax.broadcasted_iota(jnp.int32, (CHUNK, 1, 1), 0)).astype(F32)
    head = lax.broadcasted_iota(jnp.int32, (1, HEADS, 1), 1)
    for i in range(nb):
        ms, ls, os_ = [], [], []
        for gi, kv_ref in enumerate((kv0_ref, kv1_ref, kv2_ref)):
            dil = SWA_GROUPS[gi][1]
            base = gi * 3 * HEADS
            q = z_ref[i, base:base + HEADS, :]
            k_new = z_ref[i, base + HEADS:base + 2 * HEADS, :]
            v_new = z_ref[i, base + 2 * HEADS:base + 3 * HEADS, :]
            kc = kv_ref[i, :, 0, 0, :, :]
            vc = kv_ref[i, :, 0, 1, :, :]
            slope = jnp.zeros((1, HEADS, 1), F32)
            for h in range(HEADS):
                slope = jnp.where(head == h, _alibi_slope(gi, h) * dil, slope)
            s = jnp.sum(kc * q[None], axis=-1, keepdims=True) * scale - slope * steps
            s_new = jnp.sum(k_new * q, axis=-1, keepdims=True) * scale
            m = jnp.maximum(jnp.max(s, axis=0), s_new)
            p = jnp.exp(s - m[None])
            p_new = jnp.exp(s_new - m)
            ms.append(m)
            ls.append(jnp.sum(p, axis=0) + p_new)
            os_.append(jnp.sum(p * vc, axis=0) + p_new * v_new)
        top = functools.reduce(jnp.maximum, ms)
        ws = [jnp.exp(m - top) for m in ms]
        num = sum(w * o for w, o in zip(ws, os_))
        den = sum(w * l for w, l in zip(ws, ls))
        cat_ref[i] = num / den


def _swa_step(z, caches, *, nb):
    n = z.shape[0]
    views = []
    specs = []
    for (win, dil), cache in zip(SWA_GROUPS, caches):
        views.append(cache.reshape(n, win // dil, dil, 2, HEADS, HEAD_DIM))
        specs.append(pl.BlockSpec((nb, CHUNK, 1, 2, HEADS, HEAD_DIM), lambda i: (i, 0, 0, 0, 0, 0)))
    return pl.pallas_call(
        _swa_step_body,
        grid=(n // nb,),
        in_specs=[pl.BlockSpec((nb,) + z.shape[1:], lambda i: (i, 0, 0))] + specs,
        out_specs=pl.BlockSpec((nb, HEADS, HEAD_DIM), lambda i: (i, 0, 0)),
        out_shape=jax.ShapeDtypeStruct((n, HEADS, HEAD_DIM), F32),
        compiler_params=_params(("arbitrary",)),
        name="swa_step",
    )(z, *views)


def _pad_cols(w, n):
    return jnp.pad(w, ((0, 0), (0, n - w.shape[1])))


def kernel(x_prompt, x_sample, state_mlstm_C, state_mlstm_n, state_mlstm_m, cache_swa_kv0, cache_swa_kv1, cache_swa_kv2, norm_g, ffn_w_gate, ffn_w_up, ffn_w_down, a_w_in, a_b_if, sgu_norm_g, sgu_w, sgu_b, a_w_out, c_w_in, c_w_out, final_norm_g):
    nb, s, _ = x_prompt.shape
    ns = x_sample.shape[0]
    assert x_sample.shape[1] == 1 and s % max(w for w, _ in SWA_GROUPS) == 0
    for (win, dil), cache in zip(SWA_GROUPS, (cache_swa_kv0, cache_swa_kv1, cache_swa_kv2)):
        assert cache.shape[2] == win and win // dil == CHUNK

    ffn_w = (jnp.swapaxes(ffn_w_gate, 2, 3), jnp.swapaxes(ffn_w_up, 2, 3), ffn_w_down)
    g_lo, g_hi = 4 * WIDTH, 4 * WIDTH + 2 * HEADS
    a_in = jnp.concatenate([a_w_in[0][:, :g_lo], a_w_in[0][:, g_hi:], _pad_cols(a_w_in[0][:, g_lo:g_hi], LANES)],
                           axis=1).astype(BF16)
    b_if = _pad_cols(a_b_if[0].reshape(1, 2 * HEADS), LANES)
    a_out = a_w_out[0].astype(BF16)
    c_in = c_w_in[0].astype(BF16)
    c_out = c_w_out[0].astype(BF16)
    sgu_bt = _pad_cols(sgu_b[0].T, LANES)
    sgu_w00 = jnp.repeat(sgu_w[0, :, 0, 0], CHUNK).reshape(1, WIDTH)
    sgu_b0 = jnp.repeat(sgu_b[0, :, 0], CHUNK).reshape(1, WIDTH)

    xp = x_prompt.reshape(nb * s, D_MODEL)
    xs = x_sample.reshape(ns, D_MODEL)
    xp, xs = _ffn(xp, xs, norm_g[0, 0], ffn_w, 0, 0, tm=512)
    xp, p_c, p_n, p_m = _mixer_ab_prompt(xp.reshape(nb, s, D_MODEL), norm_g[0, 1], a_in, b_if,
                                          sgu_norm_g[0], sgu_w[0], sgu_bt, a_out)
    z = _proj(xs, norm_g[0, 1], a_in)
    cat, s_c, s_n, s_m, s_v = _mixer_ab_step(
        z, b_if, sgu_norm_g[0], sgu_w00, sgu_b0, state_mlstm_C[0],
        state_mlstm_n[0].reshape(ns, HEADS, 1, HEAD_DIM), _pad_cols(state_mlstm_m[0], LANES), nb=8)
    xs = _out_proj(cat, a_out, xs)
    xp, xs = _ffn(xp.reshape(nb * s, D_MODEL), xs, norm_g[0, 2], ffn_w, 0, 1, tm=512)
    xp, xs = _ffn(xp, xs, norm_g[1, 0], ffn_w, 1, 0, tm=512)
    xp3 = xp.reshape(nb, s, D_MODEL)
    outs, stats = [], []
    for gi in (1, 2):
        o, st = _swa_group_prompt(xp3, norm_g[1, 1], c_in, group=gi, tile=2048)
        outs.append(o)
        stats.append(st)
    p_kv = [_kv_tail(xp3, norm_g[1, 1], c_in, group=gi) for gi in range(3)]
    xp = _swa_group_prompt(xp3, norm_g[1, 1], c_in, group=0, tile=1024, merge_with=(outs, stats, c_out))
    xp = xp.reshape(nb * s, D_MODEL)
    z = _proj(xs, norm_g[1, 1], c_in)
    cat = _swa_step(z.reshape(ns, 3 * 3 * HEADS, HEAD_DIM), (cache_swa_kv0, cache_swa_kv1, cache_swa_kv2), nb=4)
    xs = _out_proj(cat.reshape(ns, WIDTH), c_out, xs)
    y_prompt, y_sample = _ffn(xp, xs, norm_g[1, 2], ffn_w, 1, 1, final_norm_g, tm=512)
    y_prompt = y_prompt.reshape(nb, s, D_MODEL)
    y_sample = y_sample.reshape(ns, 1, D_MODEL)
    s_kv = [z[:, gi * C_GROUP_COLS + WIDTH:(gi + 1) * C_GROUP_COLS].reshape(1, ns, 1, 2, HEADS, HEAD_DIM)
            for gi in range(3)]

    return (y_prompt, y_sample,
            p_c.reshape(1, nb, HEADS, HEAD_DIM, HEAD_DIM), p_n.reshape(1, nb, HEADS, HEAD_DIM),
            p_m[:, :, 0, 0].reshape(1, nb, HEADS),
            s_c.reshape(1, ns, HEADS, HEAD_DIM, HEAD_DIM), s_n.reshape(1, ns, HEADS, HEAD_DIM),
            s_m[:, :HEADS].reshape(1, ns, HEADS), s_v.reshape(1, ns, 1, WIDTH),
            p_kv[0], p_kv[1], p_kv[2], s_kv[0], s_kv[1], s_kv[2])
```

```python
import functools

import jax
import jax.numpy as jnp
from jax import lax
from jax.experimental import pallas as pl
from jax.experimental.pallas import tpu as pltpu

F32 = jnp.float32
BF16 = jnp.bfloat16

D_MODEL = 1024
D_FF = 2752
HEADS = 4
HEAD_DIM = 128
WIDTH = HEADS * HEAD_DIM
CHUNK = 128
SWA_GROUPS = ((128, 1), (512, 4), (2048, 16))
NORM_EPS = 1e-6
NEG_INF = -1e30

LANES = 128
BF16_ROWS = 16
PROJ_SLAB = 256
PROJ_ROWS = 512
FF_CHUNK = 256
A_IN_PAD = 4 * WIDTH + 2 * WIDTH + LANES
COL_U = 4 * WIDTH
COL_GV = 5 * WIDTH
COL_GATES = 6 * WIDTH
C_GROUP_COLS = 3 * WIDTH

VMEM_LIMIT = 60 * 1024 * 1024


def _params(semantics):
    return pltpu.CompilerParams(dimension_semantics=semantics, vmem_limit_bytes=VMEM_LIMIT)


def _resident(shape):
    nd = len(shape)
    return pl.BlockSpec(shape, lambda *_: (0,) * nd, pipeline_mode=pl.Buffered(1))


def _rms(x, g):
    ms = jnp.mean(x * x, axis=-1, keepdims=True)
    return x * lax.rsqrt(ms + NORM_EPS) * g


def _dot(a, b):
    return jnp.dot(a, b, preferred_element_type=F32)


def _dot_nt(a, b):
    return lax.dot_general(a, b, (((1,), (1,)), ((), ())), preferred_element_type=F32)


def _log_sigmoid(x):
    return jnp.minimum(x, 0.0) - jnp.log1p(jnp.exp(-jnp.abs(x)))


def _swiglu_rows(h_scr, act_scr, rows, wg_ref, wu_ref, wd_ref):
    for c0 in range(0, D_FF, FF_CHUNK):
        cols = slice(c0, min(c0 + FF_CHUNK, D_FF))
        h = h_scr[0:rows, :]
        gate = _dot_nt(h, wg_ref[0, 0, cols, :])
        up = _dot_nt(h, wu_ref[0, 0, cols, :])
        act_scr[0:rows, cols] = (gate * jax.nn.sigmoid(gate) * up).astype(act_scr.dtype)
    whole = D_FF // FF_CHUNK * FF_CHUNK
    out = _dot(act_scr[0:rows, 0:whole], wd_ref[0, 0, 0:whole, :])
    if whole < D_FF:
        out = out + _dot(act_scr[0:rows, whole:D_FF], wd_ref[0, 0, whole:D_FF, :])
    return out


def _ffn_body(*refs, final):
    if final:
        x_ref, xs_ref, g_ref, wg_ref, wu_ref, wd_ref, fg_ref, o_ref, os_ref, h_scr, act_scr = refs
    else:
        x_ref, xs_ref, g_ref, wg_ref, wu_ref, wd_ref, o_ref, os_ref, h_scr, act_scr = refs
    tm, ns = x_ref.shape[0], xs_ref.shape[0]
    last = pl.num_programs(0) - 1

    def finish(x, acc):
        y = x + 0.5 * acc
        return _rms(y, fg_ref[...]) if final else y

    x = x_ref[...]
    h_scr[0:tm, :] = _rms(x, g_ref[...]).astype(h_scr.dtype)

    @pl.when(pl.program_id(0) != last)
    def _():
        o_ref[...] = finish(x, _swiglu_rows(h_scr, act_scr, tm, wg_ref, wu_ref, wd_ref))

    @pl.when(pl.program_id(0) == last)
    def _():
        xs = xs_ref[...]
        h_scr[tm:tm + ns, :] = _rms(xs, g_ref[...]).astype(h_scr.dtype)
        acc = _swiglu_rows(h_scr, act_scr, tm + ns, wg_ref, wu_ref, wd_ref)
        o_ref[...] = finish(x, acc[0:tm])
        os_ref[...] = finish(xs, acc[tm:tm + ns])


def _ffn(x, xs, g, weights, layer, which, final_g=None, *, tm):
    m, ns = x.shape[0], xs.shape[0]
    final = final_g is not None
    wg, wu, wd = weights
    row = pl.BlockSpec((tm, D_MODEL), lambda i: (i, 0))
    wspec = lambda w: pl.BlockSpec((1, 1) + w.shape[2:], lambda i: (layer, which, 0, 0), pipeline_mode=pl.Buffered(1))
    in_specs = [row, _resident(xs.shape), _resident((1, D_MODEL)), wspec(wg), wspec(wu), wspec(wd)]
    args = [x, xs, g.reshape(1, D_MODEL), wg, wu, wd]
    if final:
        in_specs.append(_resident((1, D_MODEL)))
        args.append(final_g.reshape(1, D_MODEL))
    return pl.pallas_call(
        functools.partial(_ffn_body, final=final),
        grid=(m // tm,),
        in_specs=in_specs,
        out_specs=[row, pl.BlockSpec(xs.shape, lambda i: (0, 0))],
        out_shape=[jax.ShapeDtypeStruct((m, D_MODEL), F32), jax.ShapeDtypeStruct(xs.shape, F32)],
        scratch_shapes=[pltpu.VMEM((tm + ns, D_MODEL), wg.dtype), pltpu.VMEM((tm + ns, D_FF), wg.dtype)],
        compiler_params=_params(("arbitrary",)),
        name="ffn_final" if final else "ffn",
    )(*args)


def _cummax_lanes(x):
    lane = lax.broadcasted_iota(jnp.int32, x.shape, 1)
    d = 1
    while d < x.shape[1]:
        x = jnp.maximum(x, jnp.where(lane >= d, pltpu.roll(x, d, axis=1), NEG_INF))
        d *= 2
    return x


def _exact_tri_dot(tri_bf16, x):
    x1 = x.astype(BF16)
    r1 = x - x1.astype(F32)
    x2 = r1.astype(BF16)
    x3 = (r1 - x2.astype(F32)).astype(BF16)
    n = x.shape[1]
    r = _dot(tri_bf16, jnp.concatenate([x1, x2, x3], axis=1))
    return r[:, 0:n] + r[:, n:2 * n] + r[:, 2 * n:3 * n]


def _block_diag(a, b):
    zero = jnp.zeros_like(a)
    return jnp.concatenate([jnp.concatenate([a, zero], axis=1), jnp.concatenate([zero, b], axis=1)], axis=0)


def _mixer_ab_chunk(xn_ref, xp_ref, g_ref, win_ref, bif_ref, sg_ref, sw_ref, sbt_ref, wout_ref,
                   y_ref, c_ref, n_ref, m_ref, z_cur, z_nxt, cat_scr):
    nb = xn_ref.shape[0]
    row = lax.broadcasted_iota(jnp.int32, (CHUNK, CHUNK), 0)
    col = lax.broadcasted_iota(jnp.int32, (CHUNK, CHUNK), 1)
    causal = col <= row
    keys_before = row <= col
    tri = jnp.where(causal, 1.0, 0.0).astype(BF16)
    scale = HEAD_DIM ** -0.5

    hn = _rms(xn_ref[...].reshape(nb * CHUNK, D_MODEL), g_ref[...]).astype(BF16)
    slabs = [(c0, min(c0 + PROJ_SLAB, A_IN_PAD)) for c0 in range(0, A_IN_PAD, PROJ_SLAB)]

    def project(count):
        for _ in range(min(count, len(slabs))):
            c0, c1 = slabs.pop(0)
            z_nxt[:, c0:c1] = _dot(hn, win_ref[:, c0:c1])

    sgu_bias = [jnp.broadcast_to(sbt_ref[:, g:g + 1], (CHUNK, CHUNK)) for g in range(HEADS)]
    vn = [_rms(z_cur[b * CHUNK:(b + 1) * CHUNK, COL_GV:COL_GV + WIDTH], sg_ref[...]) for b in range(nb)]
    for g in range(HEADS):
        gs = slice(g * CHUNK, (g + 1) * CHUNK)
        mixed = _dot(jnp.where(causal, sw_ref[g], 0.0).astype(BF16),
                     jnp.concatenate([vn[b][:, gs] for b in range(nb)], axis=1).astype(BF16))
        for b in range(nb):
            rows = slice(b * CHUNK, (b + 1) * CHUNK)
            u = z_cur[rows, COL_U + g * CHUNK:COL_U + (g + 1) * CHUNK]
            cat_scr[rows, WIDTH + g * CHUNK:WIDTH + (g + 1) * CHUNK] = (
                u * (mixed[:, b * CHUNK:(b + 1) * CHUNK] + sgu_bias[g])).astype(BF16)
    project(1)

    pieces_per_stage = -(-(len(slabs)) // (3 * nb))
    gate_terms = []
    for b in range(nb):
        rows = slice(b * CHUNK, (b + 1) * CHUNK)
        gates = z_cur[rows, COL_GATES:COL_GATES + LANES] + bif_ref[...]
        lg = jnp.where(col < HEADS, gates, _log_sigmoid(gates))
        gate_terms.append((lg, _exact_tri_dot(tri, lg)))
        project(pieces_per_stage)

    heads = []
    for b, (lg, fcum) in enumerate(gate_terms):
        rows = slice(b * CHUNK, (b + 1) * CHUNK)
        lg_t = lg.T
        fcum_t = fcum.T
        gmax = _cummax_lanes(lg_t[0:2 * HEADS, :] - jnp.concatenate([fcum_t[HEADS:2 * HEADS, :]] * 2, axis=0))
        for h0 in range(0, HEADS, 2):
            pair = []
            for h in (h0, h0 + 1):
                q = z_cur[rows, h * HEAD_DIM:(h + 1) * HEAD_DIM]
                k = z_cur[rows, WIDTH + h * HEAD_DIM:WIDTH + (h + 1) * HEAD_DIM] * scale
                v_t = z_cur[rows, 2 * WIDTH + h * HEAD_DIM:2 * WIDTH + (h + 1) * HEAD_DIM].T
                f_r = fcum_t[HEADS + h:HEADS + h + 1, :]
                i_r = lg_t[h:h + 1, :]
                g_c = lg[:, h:h + 1] - fcum[:, HEADS + h:HEADS + h + 1]
                c0, n0, m0 = c_ref[b, h], n_ref[b, h], m_ref[b, h]
                m = f_r + jnp.maximum(m0, gmax[h:h + 1, :])
                inter = jnp.exp(m0 + f_r - m)
                m_last = m[:, CHUNK - 1:CHUNK]
                f_last = f_r[:, CHUNK - 1:CHUNK]
                w = jnp.exp(f_last - f_r + i_r - m_last)
                decay = jnp.exp(m0[:, 0:1] + f_last - m_last)
                m_ref[b, h] = jnp.broadcast_to(m_last, (1, LANES))
                pair.append(dict(q=q, k=k, v_t=v_t, f_r=f_r, g_c=g_c, c0=c0, n0=n0, m=m, inter=inter, w=w,
                                 decay=decay))
            against_q = _dot_nt(
                jnp.concatenate([jnp.concatenate([d["k"], d["c0"], jnp.broadcast_to(d["n0"], (BF16_ROWS, HEAD_DIM))],
                                                 axis=0) for d in pair], axis=1).astype(BF16),
                _block_diag(pair[0]["q"], pair[1]["q"]).astype(BF16))
            against_k = _dot(
                jnp.concatenate([jnp.concatenate([d["v_t"] * d["w"], jnp.broadcast_to(d["w"], (BF16_ROWS, CHUNK))],
                                                 axis=0) for d in pair], axis=1).astype(BF16),
                _block_diag(pair[0]["k"], pair[1]["k"]).astype(BF16))
            for i, d in enumerate(pair):
                h = h0 + i
                mine = slice(i * HEAD_DIM, (i + 1) * HEAD_DIM)
                c_ref[b, h] = d["decay"] * d["c0"] + against_k[0:HEAD_DIM, mine]
                n_ref[b, h] = d["decay"] * d["n0"] + against_k[HEAD_DIM:HEAD_DIM + 1, mine]
            heads.append((against_q, pair))
        project(pieces_per_stage)

    partial = []
    for i, (against_q, pair) in enumerate(heads):
        s_ts, dens = [], []
        for j, d in enumerate(pair):
            mine = slice(j * CHUNK, (j + 1) * CHUNK)
            kq = against_q[0:CHUNK, mine]
            nq = against_q[CHUNK + HEAD_DIM:CHUNK + HEAD_DIM + 1, mine]
            s_t = kq * jnp.exp(jnp.where(keys_before, (d["f_r"] - d["m"]) + d["g_c"], NEG_INF))
            s_ts.append(s_t)
            dens.append(jnp.maximum(jnp.abs(jnp.sum(s_t, axis=0, keepdims=True) + d["inter"] * nq),
                                    jnp.exp(-d["m"])))
        sv = _dot(jnp.concatenate([d["v_t"] for d in pair], axis=1).astype(BF16),
                  _block_diag(s_ts[0], s_ts[1]).astype(BF16))
        for j, d in enumerate(pair):
            mine = slice(j * CHUNK, (j + 1) * CHUNK)
            partial.append((sv[:, mine], d["inter"] * against_q[CHUNK:CHUNK + HEAD_DIM, mine], dens[j]))
        if i % (HEADS // 2) == HEADS // 2 - 1:
            project(pieces_per_stage)

    for i, (sv, carried, den) in enumerate(partial):
        b, h = divmod(i, HEADS)
        rows = slice(b * CHUNK, (b + 1) * CHUNK)
        o = z_cur[rows, 3 * WIDTH + h * HEAD_DIM:3 * WIDTH + (h + 1) * HEAD_DIM]
        cat_scr[rows, h * HEAD_DIM:(h + 1) * HEAD_DIM] = (jax.nn.sigmoid(o) * ((sv + carried) / den).T).astype(BF16)

    project(len(slabs))
    y = xp_ref[...].reshape(nb * CHUNK, D_MODEL) + _dot(cat_scr[...], wout_ref[...])
    y_ref[...] = y.reshape(nb, CHUNK, D_MODEL)


def _mixer_ab_body(*refs):
    *io_refs, z0_scr, z1_scr, cat_scr = refs
    c_ref, n_ref, m_ref = io_refs[-3:]
    step = pl.program_id(0)

    @pl.when(step <= 1)
    def _():
        c_ref[...] = jnp.zeros_like(c_ref)
        n_ref[...] = jnp.zeros_like(n_ref)
        m_ref[...] = jnp.zeros_like(m_ref)

    @pl.when(step == 0)
    def _():
        z1_scr[...] = jnp.zeros_like(z1_scr)

    @pl.when(step % 2 == 0)
    def _():
        _mixer_ab_chunk(*io_refs, z1_scr, z0_scr, cat_scr)

    @pl.when(step % 2 == 1)
    def _():
        _mixer_ab_chunk(*io_refs, z0_scr, z1_scr, cat_scr)


def _mixer_ab_prompt(x, g, w_in, b_if, sgu_g, sgu_w, sgu_bt, w_out):
    nb, s, _ = x.shape
    n_chunks = s // CHUNK
    blk = lambda index: pl.BlockSpec((nb, CHUNK, D_MODEL), index)
    nxt = blk(lambda c: (0, jnp.minimum(c, n_chunks - 1), 0))
    prev = blk(lambda c: (0, jnp.maximum(c - 1, 0), 0))
    z_shape = pltpu.VMEM((nb * CHUNK, A_IN_PAD), F32)
    return pl.pallas_call(
        _mixer_ab_body,
        grid=(n_chunks + 1,),
        in_specs=[nxt, prev, _resident((1, D_MODEL)), _resident(w_in.shape), _resident((1, LANES)),
                  _resident((1, WIDTH)), _resident(sgu_w.shape), _resident(sgu_bt.shape), _resident(w_out.shape)],
        out_specs=[prev,
                   pl.BlockSpec((nb, HEADS, HEAD_DIM, HEAD_DIM), lambda c: (0, 0, 0, 0)),
                   pl.BlockSpec((nb, HEADS, 1, HEAD_DIM), lambda c: (0, 0, 0, 0)),
                   pl.BlockSpec((nb, HEADS, 1, LANES), lambda c: (0, 0, 0, 0))],
        out_shape=[jax.ShapeDtypeStruct(x.shape, F32),
                   jax.ShapeDtypeStruct((nb, HEADS, HEAD_DIM, HEAD_DIM), F32),
                   jax.ShapeDtypeStruct((nb, HEADS, 1, HEAD_DIM), F32),
                   jax.ShapeDtypeStruct((nb, HEADS, 1, LANES), F32)],
        scratch_shapes=[z_shape, z_shape, pltpu.VMEM((nb * CHUNK, 2 * WIDTH), BF16)],
        compiler_params=_params(("arbitrary",)),
        name="mixer_ab_prompt",
    )(x, x, g.reshape(1, D_MODEL), w_in, b_if, sgu_g.reshape(1, WIDTH), sgu_w, sgu_bt, w_out)


def _alibi_slope(group, head):
    n = len(SWA_GROUPS) * HEADS
    return 2.0 ** (-8.0 * (group * HEADS + head + 1) / n)


def _slab_pitch(dil):
    return dil + 8 if dil % 8 == 0 else dil


def _swa_group_body(*refs, group, dil, tile, others):
    x_ref, g_ref, w_ref = refs[0:3]
    if others:
        other_o = refs[3:3 + others]
        other_st = refs[3 + others:3 + 2 * others]
        wout_ref, y_ref, slab_scr, q_scr, k_scr, v_scr, o_dst, st_dst, cat_scr = refs[3 + 2 * others:]
    else:
        o_ref, st_ref, slab_scr, q_scr, k_scr, v_scr = refs[3:]
        o_dst, st_dst = o_ref.at[0], st_ref.at[0]
    step = pl.program_id(1)
    sub = tile // dil
    if sub >= PROJ_ROWS:
        pieces = [[(r, f, PROJ_ROWS)] for r in range(dil) for f in range(0, sub, PROJ_ROWS)]
    else:
        per = PROJ_ROWS // sub
        pieces = [[(r, 0, sub) for r in range(p * per, (p + 1) * per)] for p in range(dil // per)]
    scale = HEAD_DIM ** -0.5
    qi = lax.broadcasted_iota(jnp.int32, (CHUNK, 2 * CHUNK), 0)
    kc = lax.broadcasted_iota(jnp.int32, (CHUNK, 2 * CHUNK), 1)
    delta = CHUNK + qi - kc
    valid = (delta >= 0) & (delta <= CHUNK)
    valid_first = valid & (kc >= jnp.where(step > 0, 0, CHUNK))
    dist = (delta * dil).astype(F32)
    lane = lax.broadcasted_iota(jnp.int32, (CHUNK, LANES), 1)

    @pl.when(step == 0)
    def _():
        k_scr[:, 0:CHUNK, :] = jnp.zeros((dil, CHUNK, WIDTH), BF16)
        v_scr[:, 0:CHUNK, :] = jnp.zeros((dil, CHUNK, WIDTH), BF16)

    pitch = _slab_pitch(dil)
    for c in range(tile // 256):
        hn = _rms(x_ref[0, c * 256:(c + 1) * 256, :], g_ref[...])
        groups = [(0, 256)] if pitch == dil else [(g * dil, dil) for g in range(256 // dil)]
        for first, count in groups:
            dst = (c * 256 + first) // dil * pitch
            for sl in range(D_MODEL // LANES):
                slab_scr[sl, dst:dst + count, :] = hn[first:first + count, sl * LANES:(sl + 1) * LANES]

    def token_rows(r, first, count):
        start = first * dil + r
        return slice(start, start + count) if dil == 1 else pl.ds(start, count, stride=dil)

    def project(piece):
        parts = []
        for r, first, count in piece:
            src = slice(first, first + count) if dil == 1 else pl.ds(first * pitch + r, count, stride=pitch)
            parts.append(jnp.concatenate([slab_scr[sl, src, :] for sl in range(D_MODEL // LANES)],
                                         axis=1).astype(BF16))
        hn = parts[0] if len(parts) == 1 else jnp.concatenate(parts, axis=0)
        for j, scr in enumerate((q_scr, k_scr, v_scr)):
            z = _dot(hn, w_ref[:, j * WIDTH:(j + 1) * WIDTH]).astype(BF16)
            at = 0
            for r, first, count in piece:
                off = first if j == 0 else CHUNK + first
                scr[r, off:off + count, :] = z[at:at + count, :]
                at += count

    def scores(piece):
        out = []
        for r, first, count in piece:
            for j in range(first // CHUNK, (first + count) // CHUNK):
                mask = valid_first if j == 0 else valid
                for h in range(HEADS):
                    hs = slice(h * HEAD_DIM, (h + 1) * HEAD_DIM)
                    s = _dot_nt(q_scr[r, j * CHUNK:(j + 1) * CHUNK, hs], k_scr[r, j * CHUNK:(j + 2) * CHUNK, hs])
                    s = jnp.where(mask, s * scale + (-_alibi_slope(group, h)) * dist, NEG_INF)
                    m = jnp.max(s, axis=-1, keepdims=True)
                    p_ = jnp.exp(s - m)
                    out.append((r, j, h, m, jnp.sum(p_, axis=-1, keepdims=True), p_.astype(BF16)))
        return out

    def values(items):
        stats = None
        for r, j, h, m, l, p_ in items:
            hs = slice(h * HEAD_DIM, (h + 1) * HEAD_DIM)
            o_dst[h, token_rows(r, j * CHUNK, CHUNK), :] = _dot(p_, v_scr[r, j * CHUNK:(j + 2) * CHUNK, hs])
            base = jnp.zeros((CHUNK, LANES), F32) if h == 0 else stats
            stats = jnp.where(lane == h, m, jnp.where(lane == HEADS + h, l, base))
            if h == HEADS - 1:
                st_dst[token_rows(r, j * CHUNK, CHUNK), :] = stats

    def halves(piece):
        if len(piece) == 1:
            r, first, count = piece[0]
            return [[(r, first, count // 2)], [(r, first + count // 2, count // 2)]]
        return [piece[:len(piece) // 2], piece[len(piece) // 2:]]

    def merge(piece):
        (_, first, count), = piece
        for c in range(first // 256, (first + count) // 256):
            rows = slice(c * 256, (c + 1) * 256)
            stats = [st_dst[rows, :]] + [s_ref[0, rows, :] for s_ref in other_st]
            for h in range(HEADS):
                outs = [o_dst[h, rows, :]] + [o_ref[0, h, rows, :] for o_ref in other_o]
                ms = [st[:, h:h + 1] for st in stats]
                ls = [st[:, HEADS + h:HEADS + h + 1] for st in stats]
                top = functools.reduce(jnp.maximum, ms)
                ws = [jnp.exp(m - top) for m in ms]
                num = sum(w * o for w, o in zip(ws, outs))
                den = sum(w * l for w, l in zip(ws, ls))
                cat_scr[rows, h * HEAD_DIM:(h + 1) * HEAD_DIM] = (num / den).astype(BF16)
        rows = slice(first, first + count)
        y_ref[0, rows, :] = x_ref[0, rows, :] + _dot(cat_scr[rows, :], wout_ref[...])

    def finish(done):
        items, p, i = done
        values(items)
        if others and i == 1:
            merge(pieces[p])

    project(pieces[0])
    pending = None
    for p, piece in enumerate(pieces):
        for i, half in enumerate(halves(piece)):
            items = scores(half)
            if i == 0 and p + 1 < len(pieces):
                project(pieces[p + 1])
            if pending is not None:
                finish(pending)
            pending = (items, p, i)
    finish(pending)
    for r in range(dil):
        k_scr[r, 0:CHUNK, :] = k_scr[r, sub:sub + CHUNK, :]
        v_scr[r, 0:CHUNK, :] = v_scr[r, sub:sub + CHUNK, :]


def _swa_group_prompt(x, g, c_in, *, group, tile, merge_with=None):
    nb, s, _ = x.shape
    dil = SWA_GROUPS[group][1]
    sub = tile // dil
    o_spec = pl.BlockSpec((1, HEADS, tile, HEAD_DIM), lambda b, t: (b, 0, t, 0))
    s_spec = pl.BlockSpec((1, tile, LANES), lambda b, t: (b, t, 0))
    x_spec = pl.BlockSpec((1, tile, D_MODEL), lambda b, t: (b, t, 0))
    in_specs = [x_spec, _resident((1, D_MODEL)),
                pl.BlockSpec((D_MODEL, C_GROUP_COLS), lambda b, t: (0, group), pipeline_mode=pl.Buffered(1))]
    args = [x, g.reshape(1, D_MODEL), c_in]
    scratch = [pltpu.VMEM((D_MODEL // LANES, sub * _slab_pitch(dil), LANES), F32),
               pltpu.VMEM((dil, sub, WIDTH), BF16),
               pltpu.VMEM((dil, CHUNK + sub, WIDTH), BF16),
               pltpu.VMEM((dil, CHUNK + sub, WIDTH), BF16)]
    if merge_with is None:
        others = 0
        out_specs = [o_spec, s_spec]
        out_shape = [jax.ShapeDtypeStruct((nb, HEADS, s, HEAD_DIM), F32), jax.ShapeDtypeStruct((nb, s, LANES), F32)]
    else:
        outs, stats, w_out = merge_with
        others = len(outs)
        in_specs += [o_spec] * others + [s_spec] * others + [_resident(w_out.shape)]
        args += [*outs, *stats, w_out]
        out_specs = x_spec
        out_shape = jax.ShapeDtypeStruct(x.shape, F32)
        scratch += [pltpu.VMEM((HEADS, tile, HEAD_DIM), F32), pltpu.VMEM((tile, LANES), F32),
                    pltpu.VMEM((tile, WIDTH), BF16)]
    return pl.pallas_call(
        functools.partial(_swa_group_body, group=group, dil=dil, tile=tile, others=others),
        grid=(nb, s // tile),
        in_specs=in_specs,
        out_specs=out_specs,
        out_shape=out_shape,
        scratch_shapes=scratch,
        compiler_params=_params(("arbitrary", "arbitrary")),
        name="swa_group%d" % group,
    )(*args)


def _kv_tail_body(x_ref, g_ref, wk_ref, wv_ref, kv_ref):
    tm = x_ref.shape[1]
    hn = _rms(x_ref[0], g_ref[...]).astype(BF16)
    for j, w_ref in enumerate((wk_ref, wv_ref)):
        kv = _dot(hn, w_ref[...])
        for h in range(HEADS):
            kv_ref[pl.ds(j * HEADS + h, tm, stride=2 * HEADS), :] = kv[:, h * HEAD_DIM:(h + 1) * HEAD_DIM]


def _kv_tail(x, g, c_in, *, group):
    win = SWA_GROUPS[group][0]
    nb, s, _ = x.shape
    tm = min(win, 512)
    first = (s - win) // tm
    steps = win // tm
    wcol = lambda j: pl.BlockSpec((D_MODEL, WIDTH), lambda b, t: (0, 3 * group + j), pipeline_mode=pl.Buffered(1))
    kv = pl.pallas_call(
        _kv_tail_body,
        grid=(nb, steps),
        in_specs=[pl.BlockSpec((1, tm, D_MODEL), lambda b, t: (b, first + t, 0)), _resident((1, D_MODEL)),
                  wcol(1), wcol(2)],
        out_specs=pl.BlockSpec((tm * 2 * HEADS, HEAD_DIM), lambda b, t: (b * steps + t, 0)),
        out_shape=jax.ShapeDtypeStruct((nb * win * 2 * HEADS, HEAD_DIM), F32),
        compiler_params=_params(("arbitrary", "arbitrary")),
        name="kv_tail%d" % group,
    )(x, g.reshape(1, D_MODEL), c_in, c_in)
    return kv.reshape(1, nb, win, 2, HEADS, HEAD_DIM)


def _proj_body(x_ref, g_ref, w_ref, z_ref):
    z_ref[...] = _dot(_rms(x_ref[...], g_ref[...]).astype(BF16), w_ref[...])


def _proj(x, g, w):
    m, n = x.shape[0], w.shape[1]
    return pl.pallas_call(
        _proj_body,
        grid=(1,),
        in_specs=[_resident(x.shape), _resident((1, D_MODEL)), _resident(w.shape)],
        out_specs=pl.BlockSpec((m, n), lambda i: (0, 0)),
        out_shape=jax.ShapeDtypeStruct((m, n), F32),
        compiler_params=_params(("arbitrary",)),
        name="proj_rows",
    )(x, g.reshape(1, D_MODEL), w)


def _out_proj_body(c_ref, w_ref, x_ref, y_ref):
    y_ref[...] = x_ref[...] + _dot(c_ref[...].astype(BF16), w_ref[...])


def _out_proj(cat, w, x):
    return pl.pallas_call(
        _out_proj_body,
        grid=(1,),
        in_specs=[_resident(cat.shape), _resident(w.shape), _resident(x.shape)],
        out_specs=pl.BlockSpec(x.shape, lambda i: (0, 0)),
        out_shape=jax.ShapeDtypeStruct(x.shape, F32),
        compiler_params=_params(("arbitrary",)),
        name="out_proj_rows",
    )(cat, w, x)


def _mixer_ab_step_body(z_ref, bif_ref, sg_ref, w00_ref, b0_ref, c_ref, n_ref, m_ref,
                        cat_ref, c1_ref, n1_ref, m1_ref, vn_ref):
    nb = z_ref.shape[0]
    scale = HEAD_DIM ** -0.5
    eye = (lax.broadcasted_iota(jnp.int32, (HEAD_DIM, HEAD_DIM), 0)
           == lax.broadcasted_iota(jnp.int32, (HEAD_DIM, HEAD_DIM), 1)).astype(F32)
    lane = lax.broadcasted_iota(jnp.int32, (1, LANES), 1)
    for i in range(nb):
        zr = z_ref[i:i + 1, :]
        gates = zr[:, COL_GATES:COL_GATES + LANES] + bif_ref[...]
        m_new = jnp.zeros((1, LANES), F32)
        for h in range(HEADS):
            hs = slice(h * HEAD_DIM, (h + 1) * HEAD_DIM)
            q = zr[:, hs]
            k = zr[:, WIDTH + h * HEAD_DIM:WIDTH + (h + 1) * HEAD_DIM] * scale
            v = zr[:, 2 * WIDTH + h * HEAD_DIM:2 * WIDTH + (h + 1) * HEAD_DIM]
            o = zr[:, 3 * WIDTH + h * HEAD_DIM:3 * WIDTH + (h + 1) * HEAD_DIM]
            ig = gates[:, h:h + 1]
            lf = _log_sigmoid(gates[:, HEADS + h:HEADS + h + 1])
            c0 = c_ref[i, h]
            n0 = n_ref[i, h]
            m0 = m_ref[i:i + 1, h:h + 1]
            a = m0 + lf
            m = jnp.maximum(a, ig)
            s = jnp.sum(q * k, axis=-1, keepdims=True) * jnp.exp(ig - m)
            inter = jnp.exp(a - m)
            cq_col = jnp.sum(c0 * q, axis=-1, keepdims=True)
            cq = jnp.sum(eye * cq_col, axis=0, keepdims=True)
            v_col = jnp.sum(eye * v, axis=-1, keepdims=True)
            num = s * v + inter * cq
            den = s + inter * jnp.sum(n0 * q, axis=-1, keepdims=True)
            hh = num / jnp.maximum(jnp.abs(den), jnp.exp(-m))
            w = jnp.exp(ig - m)
            c1_ref[i, h] = inter * c0 + (w * v_col) * k
            n1_ref[i, h] = inter * n0 + w * k
            m_new = jnp.where(lane == h, m, m_new)
            cat_ref[i:i + 1, hs] = jax.nn.sigmoid(o) * hh
        m1_ref[i:i + 1, :] = m_new
        vn = _rms(zr[:, COL_GV:COL_GV + WIDTH], sg_ref[...])
        vn_ref[i:i + 1, :] = vn
        cat_ref[i:i + 1, WIDTH:2 * WIDTH] = zr[:, COL_U:COL_U + WIDTH] * (w00_ref[...] * vn + b0_ref[...])


def _mixer_ab_step(z, b_if, sgu_g, w00, b0, st_c, st_n, st_m, *, nb):
    n = z.shape[0]
    rows = lambda w: pl.BlockSpec((nb, w), lambda i: (i, 0))
    c_spec = pl.BlockSpec((nb, HEADS, HEAD_DIM, HEAD_DIM), lambda i: (i, 0, 0, 0))
    n_spec = pl.BlockSpec((nb, HEADS, 1, HEAD_DIM), lambda i: (i, 0, 0, 0))
    return pl.pallas_call(
        _mixer_ab_step_body,
        grid=(n // nb,),
        in_specs=[rows(A_IN_PAD), _resident((1, LANES)), _resident((1, WIDTH)), _resident((1, WIDTH)),
                  _resident((1, WIDTH)), c_spec, n_spec, rows(LANES)],
        out_specs=[rows(2 * WIDTH), c_spec, n_spec, rows(LANES), rows(WIDTH)],
        out_shape=[jax.ShapeDtypeStruct((n, 2 * WIDTH), F32),
                   jax.ShapeDtypeStruct(st_c.shape, F32),
                   jax.ShapeDtypeStruct(st_n.shape, F32),
                   jax.ShapeDtypeStruct((n, LANES), F32),
                   jax.ShapeDtypeStruct((n, WIDTH), F32)],
        compiler_params=_params(("arbitrary",)),
        name="mixer_ab_step",
    )(z, b_if, sgu_g.reshape(1, WIDTH), w00, b0, st_c, st_n, st_m)


def _swa_step_body(z_ref, kv0_ref, kv1_ref, kv2_ref, cat_ref):
    nb = z_ref.shape[0]
    scale = HEAD_DIM ** -0.5
    steps = (CHUNK - lax.broadcasted_iota(jnp.int32, (CHUNK, 1, 1), 0)).astype(F32)
    head = lax.broadcasted_iota(jnp.int32, (1, HEADS, 1), 1)
    for i in range(nb):
        ms, ls, os_ = [], [], []
        for gi, kv_ref in enumerate((kv0_ref, kv1_ref, kv2_ref)):
            dil = SWA_GROUPS[gi][1]
            base = gi * 3 * HEADS
            q = z_ref[i, base:base + HEADS, :]
            k_new = z_ref[i, base + HEADS:base + 2 * HEADS, :]
            v_new = z_ref[i, base + 2 * HEADS:base + 3 * HEADS, :]
            kc = kv_ref[i, :, 0, 0, :, :]
            vc = kv_ref[i, :, 0, 1, :, :]
            slope = jnp.zeros((1, HEADS, 1), F32)
            for h in range(HEADS):
                slope = jnp.where(head == h, _alibi_slope(gi, h) * dil, slope)
            s = jnp.sum(kc * q[None], axis=-1, keepdims=True) * scale - slope * steps
            s_new = jnp.sum(k_new * q, axis=-1, keepdims=True) * scale
            m = jnp.maximum(jnp.max(s, axis=0), s_new)
            p = jnp.exp(s - m[None])
            p_new = jnp.exp(s_new - m)
            ms.append(m)
            ls.append(jnp.sum(p, axis=0) + p_new)
            os_.append(jnp.sum(p * vc, axis=0) + p_new * v_new)
        top = functools.reduce(jnp.maximum, ms)
        ws = [jnp.exp(m - top) for m in ms]
        num = sum(w * o for w, o in zip(ws, os_))
        den = sum(w * l for w, l in zip(ws, ls))
        cat_ref[i] = num / den


def _swa_step(z, caches, *, nb):
    n = z.shape[0]
    views = []
    specs = []
    for (win, dil), cache in zip(SWA_GROUPS, caches):
        views.append(cache.reshape(n, win // dil, dil, 2, HEADS, HEAD_DIM))
        specs.append(pl.BlockSpec((nb, CHUNK, 1, 2, HEADS, HEAD_DIM), lambda i: (i, 0, 0, 0, 0, 0)))
    return pl.pallas_call(
        _swa_step_body,
        grid=(n // nb,),
        in_specs=[pl.BlockSpec((nb,) + z.shape[1:], lambda i: (i, 0, 0))] + specs,
        out_specs=pl.BlockSpec((nb, HEADS, HEAD_DIM), lambda i: (i, 0, 0)),
        out_shape=jax.ShapeDtypeStruct((n, HEADS, HEAD_DIM), F32),
        compiler_params=_params(("arbitrary",)),
        name="swa_step",
    )(z, *views)


def _pad_cols(w, n):
    return jnp.pad(w, ((0, 0), (0, n - w.shape[1])))


def kernel(x_prompt, x_sample, state_mlstm_C, state_mlstm_n, state_mlstm_m, cache_swa_kv0, cache_swa_kv1, cache_swa_kv2, norm_g, ffn_w_gate, ffn_w_up, ffn_w_down, a_w_in, a_b_if, sgu_norm_g, sgu_w, sgu_b, a_w_out, c_w_in, c_w_out, final_norm_g):
    nb, s, _ = x_prompt.shape
    ns = x_sample.shape[0]
    assert x_sample.shape[1] == 1 and s % max(w for w, _ in SWA_GROUPS) == 0
    for (win, dil), cache in zip(SWA_GROUPS, (cache_swa_kv0, cache_swa_kv1, cache_swa_kv2)):
        assert cache.shape[2] == win and win // dil == CHUNK

    ffn_w = (jnp.swapaxes(ffn_w_gate, 2, 3), jnp.swapaxes(ffn_w_up, 2, 3), ffn_w_down)
    g_lo, g_hi = 4 * WIDTH, 4 * WIDTH + 2 * HEADS
    a_in = jnp.concatenate([a_w_in[0][:, :g_lo], a_w_in[0][:, g_hi:], _pad_cols(a_w_in[0][:, g_lo:g_hi], LANES)],
                           axis=1).astype(BF16)
    b_if = _pad_cols(a_b_if[0].reshape(1, 2 * HEADS), LANES)
    a_out = a_w_out[0].astype(BF16)
    c_in = c_w_in[0].astype(BF16)
    c_out = c_w_out[0].astype(BF16)
    sgu_bt = _pad_cols(sgu_b[0].T, LANES)
    sgu_w00 = jnp.repeat(sgu_w[0, :, 0, 0], CHUNK).reshape(1, WIDTH)
    sgu_b0 = jnp.repeat(sgu_b[0, :, 0], CHUNK).reshape(1, WIDTH)

    xp = x_prompt.reshape(nb * s, D_MODEL)
    xs = x_sample.reshape(ns, D_MODEL)
    xp, xs = _ffn(xp, xs, norm_g[0, 0], ffn_w, 0, 0, tm=512)
    xp, p_c, p_n, p_m = _mixer_ab_prompt(xp.reshape(nb, s, D_MODEL), norm_g[0, 1], a_in, b_if,
                                          sgu_norm_g[0], sgu_w[0], sgu_bt, a_out)
    z = _proj(xs, norm_g[0, 1], a_in)
    cat, s_c, s_n, s_m, s_v = _mixer_ab_step(
        z, b_if, sgu_norm_g[0], sgu_w00, sgu_b0, state_mlstm_C[0],
        state_mlstm_n[0].reshape(ns, HEADS, 1, HEAD_DIM), _pad_cols(state_mlstm_m[0], LANES), nb=8)
    xs = _out_proj(cat, a_out, xs)
    xp, xs = _ffn(xp.reshape(nb * s, D_MODEL), xs, norm_g[0, 2], ffn_w, 0, 1, tm=512)
    xp, xs = _ffn(xp, xs, norm_g[1, 0], ffn_w, 1, 0, tm=512)
    xp3 = xp.reshape(nb, s, D_MODEL)
    outs, stats = [], []
    for gi in (1, 2):
        o, st = _swa_group_prompt(xp3, norm_g[1, 1], c_in, group=gi, tile=2048)
        outs.append(o)
        stats.append(st)
    p_kv = [_kv_tail(xp3, norm_g[1, 1], c_in, group=gi) for gi in range(3)]
    xp = _swa_group_prompt(xp3, norm_g[1, 1], c_in, group=0, tile=1024, merge_with=(outs, stats, c_out))
    xp = xp.reshape(nb * s, D_MODEL)
    z = _proj(xs, norm_g[1, 1], c_in)
    cat = _swa_step(z.reshape(ns, 3 * 3 * HEADS, HEAD_DIM), (cache_swa_kv0, cache_swa_kv1, cache_swa_kv2), nb=4)
    xs = _out_proj(cat.reshape(ns, WIDTH), c_out, xs)
    y_prompt, y_sample = _ffn(xp, xs, norm_g[1, 2], ffn_w, 1, 1, final_norm_g, tm=512)
    y_prompt = y_prompt.reshape(nb, s, D_MODEL)
    y_sample = y_sample.reshape(ns, 1, D_MODEL)
    s_kv = [z[:, gi * C_GROUP_COLS + WIDTH:(gi + 1) * C_GROUP_COLS].reshape(1, ns, 1, 2, HEADS, HEAD_DIM)
            for gi in range(3)]

    return (y_prompt, y_sample,
            p_c.reshape(1, nb, HEADS, HEAD_DIM, HEAD_DIM), p_n.reshape(1, nb, HEADS, HEAD_DIM),
            p_m[:, :, 0, 0].reshape(1, nb, HEADS),
            s_c.reshape(1, ns, HEADS, HEAD_DIM, HEAD_DIM), s_n.reshape(1, ns, HEADS, HEAD_DIM),
            s_m[:, :HEADS].reshape(1, ns, HEADS), s_v.reshape(1, ns, 1, WIDTH),
            p_kv[0], p_kv[1], p_kv[2], s_kv[0], s_kv[1], s_kv[2])
```

```python
import functools

import jax
import jax.numpy as jnp
from jax import lax
from jax.experimental import pallas as pl
from jax.experimental.pallas import tpu as pltpu

F32 = jnp.float32
BF16 = jnp.bfloat16

D_MODEL = 1024
D_FF = 2752
HEADS = 4
HEAD_DIM = 128
WIDTH = HEADS * HEAD_DIM
CHUNK = 128
SWA_GROUPS = ((128, 1), (512, 4), (2048, 16))
NORM_EPS = 1e-6
NEG_INF = -1e30

LANES = 128
BF16_ROWS = 16
PROJ_SLAB = 256
PROJ_ROWS = 512
FF_CHUNK = 256
ROW_PASS = 256
A_IN_PAD = 4 * WIDTH + 2 * WIDTH + LANES
COL_U = 4 * WIDTH
COL_GV = 5 * WIDTH
COL_GATES = 6 * WIDTH
C_GROUP_COLS = 3 * WIDTH

VMEM_LIMIT = 60 * 1024 * 1024


def _params(semantics):
    return pltpu.CompilerParams(dimension_semantics=semantics, vmem_limit_bytes=VMEM_LIMIT)


def _resident(shape):
    nd = len(shape)
    return pl.BlockSpec(shape, lambda *_: (0,) * nd, pipeline_mode=pl.Buffered(1))


def _rms(x, g):
    ms = jnp.mean(x * x, axis=-1, keepdims=True)
    return x * lax.rsqrt(ms + NORM_EPS) * g


def _dot(a, b):
    return jnp.dot(a, b, preferred_element_type=F32)


def _dot_nt(a, b):
    return lax.dot_general(a, b, (((1,), (1,)), ((), ())), preferred_element_type=F32)


def _log_sigmoid(x):
    return jnp.minimum(x, 0.0) - jnp.log1p(jnp.exp(-jnp.abs(x)))


def _swiglu_rows(h_scr, act_scr, rows, wg_ref, wu_ref, wd_ref):
    for c0 in range(0, D_FF, FF_CHUNK):
        cols = slice(c0, min(c0 + FF_CHUNK, D_FF))
        h = h_scr[0:rows, :]
        gate = _dot_nt(h, wg_ref[0, 0, cols, :])
        up = _dot_nt(h, wu_ref[0, 0, cols, :])
        act_scr[0:rows, cols] = (gate * jax.nn.sigmoid(gate) * up).astype(act_scr.dtype)
    whole = D_FF // FF_CHUNK * FF_CHUNK
    out = _dot(act_scr[0:rows, 0:whole], wd_ref[0, 0, 0:whole, :])
    if whole < D_FF:
        out = out + _dot(act_scr[0:rows, whole:D_FF], wd_ref[0, 0, whole:D_FF, :])
    return out


def _ffn_body(*refs, final):
    if final:
        x_ref, xs_ref, g_ref, wg_ref, wu_ref, wd_ref, fg_ref, o_ref, os_ref, h_scr, act_scr = refs
    else:
        x_ref, xs_ref, g_ref, wg_ref, wu_ref, wd_ref, o_ref, os_ref, h_scr, act_scr = refs
    tm, ns = x_ref.shape[0], xs_ref.shape[0]
    last = pl.num_programs(0) - 1

    def finish(x, acc):
        y = x + 0.5 * acc
        return _rms(y, fg_ref[...]) if final else y

    x = x_ref[...]
    h_scr[0:tm, :] = _rms(x, g_ref[...]).astype(h_scr.dtype)

    @pl.when(pl.program_id(0) != last)
    def _():
        o_ref[...] = finish(x, _swiglu_rows(h_scr, act_scr, tm, wg_ref, wu_ref, wd_ref))

    @pl.when(pl.program_id(0) == last)
    def _():
        xs = xs_ref[...]
        h_scr[tm:tm + ns, :] = _rms(xs, g_ref[...]).astype(h_scr.dtype)
        acc = _swiglu_rows(h_scr, act_scr, tm + ns, wg_ref, wu_ref, wd_ref)
        o_ref[...] = finish(x, acc[0:tm])
        os_ref[...] = finish(xs, acc[tm:tm + ns])


def _ffn(x, xs, g, weights, layer, which, final_g=None, *, tm):
    m, ns = x.shape[0], xs.shape[0]
    final = final_g is not None
    wg, wu, wd = weights
    row = pl.BlockSpec((tm, D_MODEL), lambda i: (i, 0))
    wspec = lambda w: pl.BlockSpec((1, 1) + w.shape[2:], lambda i: (layer, which, 0, 0), pipeline_mode=pl.Buffered(1))
    in_specs = [row, _resident(xs.shape), _resident((1, D_MODEL)), wspec(wg), wspec(wu), wspec(wd)]
    args = [x, xs, g.reshape(1, D_MODEL), wg, wu, wd]
    if final:
        in_specs.append(_resident((1, D_MODEL)))
        args.append(final_g.reshape(1, D_MODEL))
    return pl.pallas_call(
        functools.partial(_ffn_body, final=final),
        grid=(m // tm,),
        in_specs=in_specs,
        out_specs=[row, pl.BlockSpec(xs.shape, lambda i: (0, 0))],
        out_shape=[jax.ShapeDtypeStruct((m, D_MODEL), F32), jax.ShapeDtypeStruct(xs.shape, F32)],
        scratch_shapes=[pltpu.VMEM((tm + ns, D_MODEL), wg.dtype), pltpu.VMEM((tm + ns, D_FF), wg.dtype)],
        compiler_params=_params(("arbitrary",)),
        name="ffn_final" if final else "ffn",
    )(*args)


def _cummax_lanes(x):
    lane = lax.broadcasted_iota(jnp.int32, x.shape, 1)
    d = 1
    while d < x.shape[1]:
        x = jnp.maximum(x, jnp.where(lane >= d, pltpu.roll(x, d, axis=1), NEG_INF))
        d *= 2
    return x


def _exact_tri_dot(tri_bf16, x):
    x1 = x.astype(BF16)
    r1 = x - x1.astype(F32)
    x2 = r1.astype(BF16)
    x3 = (r1 - x2.astype(F32)).astype(BF16)
    n = x.shape[1]
    r = _dot(tri_bf16, jnp.concatenate([x1, x2, x3], axis=1))
    return r[:, 0:n] + r[:, n:2 * n] + r[:, 2 * n:3 * n]


def _block_diag(a, b):
    zero = jnp.zeros_like(a)
    return jnp.concatenate([jnp.concatenate([a, zero], axis=1), jnp.concatenate([zero, b], axis=1)], axis=0)


def _mixer_ab_chunk(xn_ref, xp_ref, g_ref, win_ref, bif_ref, sg_ref, sw_ref, sbt_ref, wout_ref,
                   y_ref, c_ref, n_ref, m_ref, z_cur, z_nxt, cat_scr):
    nb = xn_ref.shape[0]
    row = lax.broadcasted_iota(jnp.int32, (CHUNK, CHUNK), 0)
    col = lax.broadcasted_iota(jnp.int32, (CHUNK, CHUNK), 1)
    causal = col <= row
    keys_before = row <= col
    tri = jnp.where(causal, 1.0, 0.0).astype(BF16)
    scale = HEAD_DIM ** -0.5

    hn = _rms(xn_ref[...].reshape(nb * CHUNK, D_MODEL), g_ref[...]).astype(BF16)
    slabs = [(c0, min(c0 + PROJ_SLAB, A_IN_PAD)) for c0 in range(0, A_IN_PAD, PROJ_SLAB)]

    def project(count):
        for _ in range(min(count, len(slabs))):
            c0, c1 = slabs.pop(0)
            z_nxt[:, c0:c1] = _dot(hn, win_ref[:, c0:c1])

    sgu_bias = [jnp.broadcast_to(sbt_ref[:, g:g + 1], (CHUNK, CHUNK)) for g in range(HEADS)]
    vn = [_rms(z_cur[b * CHUNK:(b + 1) * CHUNK, COL_GV:COL_GV + WIDTH], sg_ref[...]) for b in range(nb)]
    for g in range(HEADS):
        gs = slice(g * CHUNK, (g + 1) * CHUNK)
        mixed = _dot(jnp.where(causal, sw_ref[g], 0.0).astype(BF16),
                     jnp.concatenate([vn[b][:, gs] for b in range(nb)], axis=1).astype(BF16))
        for b in range(nb):
            rows = slice(b * CHUNK, (b + 1) * CHUNK)
            u = z_cur[rows, COL_U + g * CHUNK:COL_U + (g + 1) * CHUNK]
            cat_scr[rows, WIDTH + g * CHUNK:WIDTH + (g + 1) * CHUNK] = (
                u * (mixed[:, b * CHUNK:(b + 1) * CHUNK] + sgu_bias[g])).astype(BF16)
    project(1)

    pieces_per_stage = -(-(len(slabs)) // (3 * nb))
    gate_terms = []
    for b in range(nb):
        rows = slice(b * CHUNK, (b + 1) * CHUNK)
        gates = z_cur[rows, COL_GATES:COL_GATES + LANES] + bif_ref[...]
        lg = jnp.where(col < HEADS, gates, _log_sigmoid(gates))
        gate_terms.append((lg, _exact_tri_dot(tri, lg)))
        project(pieces_per_stage)

    heads = []
    for b, (lg, fcum) in enumerate(gate_terms):
        rows = slice(b * CHUNK, (b + 1) * CHUNK)
        lg_t = lg.T
        fcum_t = fcum.T
        gmax = _cummax_lanes(lg_t[0:2 * HEADS, :] - jnp.concatenate([fcum_t[HEADS:2 * HEADS, :]] * 2, axis=0))
        for h0 in range(0, HEADS, 2):
            pair = []
            for h in (h0, h0 + 1):
                q = z_cur[rows, h * HEAD_DIM:(h + 1) * HEAD_DIM]
                k = z_cur[rows, WIDTH + h * HEAD_DIM:WIDTH + (h + 1) * HEAD_DIM] * scale
                v_t = z_cur[rows, 2 * WIDTH + h * HEAD_DIM:2 * WIDTH + (h + 1) * HEAD_DIM].T
                f_r = fcum_t[HEADS + h:HEADS + h + 1, :]
                i_r = lg_t[h:h + 1, :]
                g_c = lg[:, h:h + 1] - fcum[:, HEADS + h:HEADS + h + 1]
                c0, n0, m0 = c_ref[b, h], n_ref[b, h], m_ref[b, h]
                m = f_r + jnp.maximum(m0, gmax[h:h + 1, :])
                inter = jnp.exp(m0 + f_r - m)
                m_last = m[:, CHUNK - 1:CHUNK]
                f_last = f_r[:, CHUNK - 1:CHUNK]
                w = jnp.exp(f_last - f_r + i_r - m_last)
                decay = jnp.exp(m0[:, 0:1] + f_last - m_last)
                m_ref[b, h] = jnp.broadcast_to(m_last, (1, LANES))
                pair.append(dict(q=q, k=k, v_t=v_t, f_r=f_r, g_c=g_c, c0=c0, n0=n0, m=m, inter=inter, w=w,
                                 decay=decay))
            against_q = _dot_nt(
                jnp.concatenate([jnp.concatenate([d["k"], d["c0"], jnp.broadcast_to(d["n0"], (BF16_ROWS, HEAD_DIM))],
                                                 axis=0) for d in pair], axis=1).astype(BF16),
                _block_diag(pair[0]["q"], pair[1]["q"]).astype(BF16))
            against_k = _dot(
                jnp.concatenate([jnp.concatenate([d["v_t"] * d["w"], jnp.broadcast_to(d["w"], (BF16_ROWS, CHUNK))],
                                                 axis=0) for d in pair], axis=1).astype(BF16),
                _block_diag(pair[0]["k"], pair[1]["k"]).astype(BF16))
            for i, d in enumerate(pair):
                h = h0 + i
                mine = slice(i * HEAD_DIM, (i + 1) * HEAD_DIM)
                c_ref[b, h] = d["decay"] * d["c0"] + against_k[0:HEAD_DIM, mine]
                n_ref[b, h] = d["decay"] * d["n0"] + against_k[HEAD_DIM:HEAD_DIM + 1, mine]
            heads.append((against_q, pair))
        project(pieces_per_stage)

    partial = []
    for i, (against_q, pair) in enumerate(heads):
        s_ts, dens = [], []
        for j, d in enumerate(pair):
            mine = slice(j * CHUNK, (j + 1) * CHUNK)
            kq = against_q[0:CHUNK, mine]
            nq = against_q[CHUNK + HEAD_DIM:CHUNK + HEAD_DIM + 1, mine]
            s_t = kq * jnp.exp(jnp.where(keys_before, (d["f_r"] - d["m"]) + d["g_c"], NEG_INF))
            s_ts.append(s_t)
            dens.append(jnp.maximum(jnp.abs(jnp.sum(s_t, axis=0, keepdims=True) + d["inter"] * nq),
                                    jnp.exp(-d["m"])))
        sv = _dot(jnp.concatenate([d["v_t"] for d in pair], axis=1).astype(BF16),
                  _block_diag(s_ts[0], s_ts[1]).astype(BF16))
        for j, d in enumerate(pair):
            mine = slice(j * CHUNK, (j + 1) * CHUNK)
            partial.append((sv[:, mine], d["inter"] * against_q[CHUNK:CHUNK + HEAD_DIM, mine], dens[j]))
        if i % (HEADS // 2) == HEADS // 2 - 1:
            project(pieces_per_stage)

    for i, (sv, carried, den) in enumerate(partial):
        b, h = divmod(i, HEADS)
        rows = slice(b * CHUNK, (b + 1) * CHUNK)
        o = z_cur[rows, 3 * WIDTH + h * HEAD_DIM:3 * WIDTH + (h + 1) * HEAD_DIM]
        cat_scr[rows, h * HEAD_DIM:(h + 1) * HEAD_DIM] = (jax.nn.sigmoid(o) * ((sv + carried) / den).T).astype(BF16)

    project(len(slabs))
    y = xp_ref[...].reshape(nb * CHUNK, D_MODEL) + _dot(cat_scr[...], wout_ref[...])
    y_ref[...] = y.reshape(nb, CHUNK, D_MODEL)


def _mixer_ab_body(*refs):
    *io_refs, z0_scr, z1_scr, cat_scr = refs
    c_ref, n_ref, m_ref = io_refs[-3:]
    step = pl.program_id(0)

    @pl.when(step <= 1)
    def _():
        c_ref[...] = jnp.zeros_like(c_ref)
        n_ref[...] = jnp.zeros_like(n_ref)
        m_ref[...] = jnp.zeros_like(m_ref)

    @pl.when(step == 0)
    def _():
        z1_scr[...] = jnp.zeros_like(z1_scr)

    @pl.when(step % 2 == 0)
    def _():
        _mixer_ab_chunk(*io_refs, z1_scr, z0_scr, cat_scr)

    @pl.when(step % 2 == 1)
    def _():
        _mixer_ab_chunk(*io_refs, z0_scr, z1_scr, cat_scr)


def _mixer_ab_prompt(x, g, w_in, b_if, sgu_g, sgu_w, sgu_bt, w_out):
    nb, s, _ = x.shape
    n_chunks = s // CHUNK
    blk = lambda index: pl.BlockSpec((nb, CHUNK, D_MODEL), index)
    nxt = blk(lambda c: (0, jnp.minimum(c, n_chunks - 1), 0))
    prev = blk(lambda c: (0, jnp.maximum(c - 1, 0), 0))
    z_shape = pltpu.VMEM((nb * CHUNK, A_IN_PAD), F32)
    return pl.pallas_call(
        _mixer_ab_body,
        grid=(n_chunks + 1,),
        in_specs=[nxt, prev, _resident((1, D_MODEL)), _resident(w_in.shape), _resident((1, LANES)),
                  _resident((1, WIDTH)), _resident(sgu_w.shape), _resident(sgu_bt.shape), _resident(w_out.shape)],
        out_specs=[prev,
                   pl.BlockSpec((nb, HEADS, HEAD_DIM, HEAD_DIM), lambda c: (0, 0, 0, 0)),
                   pl.BlockSpec((nb, HEADS, 1, HEAD_DIM), lambda c: (0, 0, 0, 0)),
                   pl.BlockSpec((nb, HEADS, 1, LANES), lambda c: (0, 0, 0, 0))],
        out_shape=[jax.ShapeDtypeStruct(x.shape, F32),
                   jax.ShapeDtypeStruct((nb, HEADS, HEAD_DIM, HEAD_DIM), F32),
                   jax.ShapeDtypeStruct((nb, HEADS, 1, HEAD_DIM), F32),
                   jax.ShapeDtypeStruct((nb, HEADS, 1, LANES), F32)],
        scratch_shapes=[z_shape, z_shape, pltpu.VMEM((nb * CHUNK, 2 * WIDTH), BF16)],
        compiler_params=_params(("arbitrary",)),
        name="mixer_ab_prompt",
    )(x, x, g.reshape(1, D_MODEL), w_in, b_if, sgu_g.reshape(1, WIDTH), sgu_w, sgu_bt, w_out)


def _alibi_slope(group, head):
    n = len(SWA_GROUPS) * HEADS
    return 2.0 ** (-8.0 * (group * HEADS + head + 1) / n)


def _slab_pitch(dil):
    return dil + 8 if dil % 8 == 0 else dil


def _swa_group_body(*refs, group, dil, tile, others):
    x_ref, g_ref, w_ref = refs[0:3]
    if others:
        other_o = refs[3:3 + others]
        other_st = refs[3 + others:3 + 2 * others]
        wout_ref, y_ref, slab_scr, q_scr, k_scr, v_scr, o_dst, st_dst, cat_scr = refs[3 + 2 * others:]
    else:
        o_ref, st_ref, slab_scr, q_scr, k_scr, v_scr = refs[3:]
        o_dst, st_dst = o_ref.at[0], st_ref.at[0]
    step = pl.program_id(1)
    sub = tile // dil
    if sub >= PROJ_ROWS:
        pieces = [[(r, f, PROJ_ROWS)] for r in range(dil) for f in range(0, sub, PROJ_ROWS)]
    else:
        per = PROJ_ROWS // sub
        pieces = [[(r, 0, sub) for r in range(p * per, (p + 1) * per)] for p in range(dil // per)]
    scale = HEAD_DIM ** -0.5
    qi = lax.broadcasted_iota(jnp.int32, (CHUNK, 2 * CHUNK), 0)
    kc = lax.broadcasted_iota(jnp.int32, (CHUNK, 2 * CHUNK), 1)
    delta = CHUNK + qi - kc
    valid = (delta >= 0) & (delta <= CHUNK)
    valid_first = valid & (kc >= jnp.where(step > 0, 0, CHUNK))
    dist = (delta * dil).astype(F32)
    lane = lax.broadcasted_iota(jnp.int32, (CHUNK, LANES), 1)

    @pl.when(step == 0)
    def _():
        k_scr[:, 0:CHUNK, :] = jnp.zeros((dil, CHUNK, WIDTH), BF16)
        v_scr[:, 0:CHUNK, :] = jnp.zeros((dil, CHUNK, WIDTH), BF16)

    pitch = _slab_pitch(dil)
    for c in range(tile // ROW_PASS):
        hn = _rms(x_ref[0, c * ROW_PASS:(c + 1) * ROW_PASS, :], g_ref[...])
        groups = [(0, ROW_PASS)] if pitch == dil else [(g * dil, dil) for g in range(ROW_PASS // dil)]
        for first, count in groups:
            dst = (c * ROW_PASS + first) // dil * pitch
            for sl in range(D_MODEL // LANES):
                slab_scr[sl, dst:dst + count, :] = hn[first:first + count, sl * LANES:(sl + 1) * LANES]

    def token_rows(r, first, count):
        start = first * dil + r
        return slice(start, start + count) if dil == 1 else pl.ds(start, count, stride=dil)

    def project(piece):
        parts = []
        for r, first, count in piece:
            src = slice(first, first + count) if dil == 1 else pl.ds(first * pitch + r, count, stride=pitch)
            parts.append(jnp.concatenate([slab_scr[sl, src, :] for sl in range(D_MODEL // LANES)],
                                         axis=1).astype(BF16))
        hn = parts[0] if len(parts) == 1 else jnp.concatenate(parts, axis=0)
        for j, scr in enumerate((q_scr, k_scr, v_scr)):
            z = _dot(hn, w_ref[:, j * WIDTH:(j + 1) * WIDTH]).astype(BF16)
            at = 0
            for r, first, count in piece:
                off = first if j == 0 else CHUNK + first
                scr[r, off:off + count, :] = z[at:at + count, :]
                at += count

    def scores(piece):
        out = []
        for r, first, count in piece:
            for j in range(first // CHUNK, (first + count) // CHUNK):
                mask = valid_first if j == 0 else valid
                for h in range(HEADS):
                    hs = slice(h * HEAD_DIM, (h + 1) * HEAD_DIM)
                    s = _dot_nt(q_scr[r, j * CHUNK:(j + 1) * CHUNK, hs], k_scr[r, j * CHUNK:(j + 2) * CHUNK, hs])
                    s = jnp.where(mask, s * scale + (-_alibi_slope(group, h)) * dist, NEG_INF)
                    m = jnp.max(s, axis=-1, keepdims=True)
                    p_ = jnp.exp(s - m)
                    out.append((r, j, h, m, jnp.sum(p_, axis=-1, keepdims=True), p_.astype(BF16)))
        return out

    def values(items):
        stats = None
        for r, j, h, m, l, p_ in items:
            hs = slice(h * HEAD_DIM, (h + 1) * HEAD_DIM)
            o_dst[h, token_rows(r, j * CHUNK, CHUNK), :] = _dot(p_, v_scr[r, j * CHUNK:(j + 2) * CHUNK, hs])
            base = jnp.zeros((CHUNK, LANES), F32) if h == 0 else stats
            stats = jnp.where(lane == h, m, jnp.where(lane == HEADS + h, l, base))
            if h == HEADS - 1:
                st_dst[token_rows(r, j * CHUNK, CHUNK), :] = stats

    def halves(piece):
        if len(piece) == 1:
            r, first, count = piece[0]
            return [[(r, first, count // 2)], [(r, first + count // 2, count // 2)]]
        return [piece[:len(piece) // 2], piece[len(piece) // 2:]]

    def merge(piece):
        (_, first, count), = piece
        for c in range(first // ROW_PASS, (first + count) // ROW_PASS):
            rows = slice(c * ROW_PASS, (c + 1) * ROW_PASS)
            stats = [st_dst[rows, :]] + [s_ref[0, rows, :] for s_ref in other_st]
            for h in range(HEADS):
                outs = [o_dst[h, rows, :]] + [o_ref[0, h, rows, :] for o_ref in other_o]
                ms = [st[:, h:h + 1] for st in stats]
                ls = [st[:, HEADS + h:HEADS + h + 1] for st in stats]
                top = functools.reduce(jnp.maximum, ms)
                ws = [jnp.exp(m - top) for m in ms]
                num = sum(w * o for w, o in zip(ws, outs))
                den = sum(w * l for w, l in zip(ws, ls))
                cat_scr[rows, h * HEAD_DIM:(h + 1) * HEAD_DIM] = (num / den).astype(BF16)
        rows = slice(first, first + count)
        y_ref[0, rows, :] = x_ref[0, rows, :] + _dot(cat_scr[rows, :], wout_ref[...])

    def finish(done):
        items, p, i = done
        values(items)
        if others and i == 1:
            merge(pieces[p])

    project(pieces[0])
    pending = None
    for p, piece in enumerate(pieces):
        for i, half in enumerate(halves(piece)):
            items = scores(half)
            if i == 0 and p + 1 < len(pieces):
                project(pieces[p + 1])
            if pending is not None:
                finish(pending)
            pending = (items, p, i)
    finish(pending)
    for r in range(dil):
        k_scr[r, 0:CHUNK, :] = k_scr[r, sub:sub + CHUNK, :]
        v_scr[r, 0:CHUNK, :] = v_scr[r, sub:sub + CHUNK, :]


def _swa_group_prompt(x, g, c_in, *, group, tile, merge_with=None):
    nb, s, _ = x.shape
    dil = SWA_GROUPS[group][1]
    sub = tile // dil
    o_spec = pl.BlockSpec((1, HEADS, tile, HEAD_DIM), lambda b, t: (b, 0, t, 0))
    s_spec = pl.BlockSpec((1, tile, LANES), lambda b, t: (b, t, 0))
    x_spec = pl.BlockSpec((1, tile, D_MODEL), lambda b, t: (b, t, 0))
    in_specs = [x_spec, _resident((1, D_MODEL)),
                pl.BlockSpec((D_MODEL, C_GROUP_COLS), lambda b, t: (0, group), pipeline_mode=pl.Buffered(1))]
    args = [x, g.reshape(1, D_MODEL), c_in]
    scratch = [pltpu.VMEM((D_MODEL // LANES, sub * _slab_pitch(dil), LANES), F32),
               pltpu.VMEM((dil, sub, WIDTH), BF16),
               pltpu.VMEM((dil, CHUNK + sub, WIDTH), BF16),
               pltpu.VMEM((dil, CHUNK + sub, WIDTH), BF16)]
    if merge_with is None:
        others = 0
        out_specs = [o_spec, s_spec]
        out_shape = [jax.ShapeDtypeStruct((nb, HEADS, s, HEAD_DIM), F32), jax.ShapeDtypeStruct((nb, s, LANES), F32)]
    else:
        outs, stats, w_out = merge_with
        others = len(outs)
        in_specs += [o_spec] * others + [s_spec] * others + [_resident(w_out.shape)]
        args += [*outs, *stats, w_out]
        out_specs = x_spec
        out_shape = jax.ShapeDtypeStruct(x.shape, F32)
        scratch += [pltpu.VMEM((HEADS, tile, HEAD_DIM), F32), pltpu.VMEM((tile, LANES), F32),
                    pltpu.VMEM((tile, WIDTH), BF16)]
    return pl.pallas_call(
        functools.partial(_swa_group_body, group=group, dil=dil, tile=tile, others=others),
        grid=(nb, s // tile),
        in_specs=in_specs,
        out_specs=out_specs,
        out_shape=out_shape,
        scratch_shapes=scratch,
        compiler_params=_params(("arbitrary", "arbitrary")),
        name="swa_group%d" % group,
    )(*args)


def _kv_tail_body(x_ref, g_ref, wk_ref, wv_ref, kv_ref):
    tm = x_ref.shape[1]
    hn = _rms(x_ref[0], g_ref[...]).astype(BF16)
    for j, w_ref in enumerate((wk_ref, wv_ref)):
        kv = _dot(hn, w_ref[...])
        for h in range(HEADS):
            kv_ref[pl.ds(j * HEADS + h, tm, stride=2 * HEADS), :] = kv[:, h * HEAD_DIM:(h + 1) * HEAD_DIM]


def _kv_tail(x, g, c_in, *, group):
    win = SWA_GROUPS[group][0]
    nb, s, _ = x.shape
    tm = min(win, 1024)
    first = (s - win) // tm
    steps = win // tm
    wcol = lambda j: pl.BlockSpec((D_MODEL, WIDTH), lambda b, t: (0, 3 * group + j), pipeline_mode=pl.Buffered(1))
    kv = pl.pallas_call(
        _kv_tail_body,
        grid=(nb, steps),
        in_specs=[pl.BlockSpec((1, tm, D_MODEL), lambda b, t: (b, first + t, 0)), _resident((1, D_MODEL)),
                  wcol(1), wcol(2)],
        out_specs=pl.BlockSpec((tm * 2 * HEADS, HEAD_DIM), lambda b, t: (b * steps + t, 0)),
        out_shape=jax.ShapeDtypeStruct((nb * win * 2 * HEADS, HEAD_DIM), F32),
        compiler_params=_params(("arbitrary", "arbitrary")),
        name="kv_tail%d" % group,
    )(x, g.reshape(1, D_MODEL), c_in, c_in)
    return kv.reshape(1, nb, win, 2, HEADS, HEAD_DIM)


def _proj_body(x_ref, g_ref, w_ref, z_ref):
    z_ref[...] = _dot(_rms(x_ref[...], g_ref[...]).astype(BF16), w_ref[...])


def _proj(x, g, w):
    m, n = x.shape[0], w.shape[1]
    return pl.pallas_call(
        _proj_body,
        grid=(1,),
        in_specs=[_resident(x.shape), _resident((1, D_MODEL)), _resident(w.shape)],
        out_specs=pl.BlockSpec((m, n), lambda i: (0, 0)),
        out_shape=jax.ShapeDtypeStruct((m, n), F32),
        compiler_params=_params(("arbitrary",)),
        name="proj_rows",
    )(x, g.reshape(1, D_MODEL), w)


def _out_proj_body(c_ref, w_ref, x_ref, y_ref):
    y_ref[...] = x_ref[...] + _dot(c_ref[...].astype(BF16), w_ref[...])


def _out_proj(cat, w, x):
    return pl.pallas_call(
        _out_proj_body,
        grid=(1,),
        in_specs=[_resident(cat.shape), _resident(w.shape), _resident(x.shape)],
        out_specs=pl.BlockSpec(x.shape, lambda i: (0, 0)),
        out_shape=jax.ShapeDtypeStruct(x.shape, F32),
        compiler_params=_params(("arbitrary",)),
        name="out_proj_rows",
    )(cat, w, x)


def _mixer_ab_step_body(z_ref, bif_ref, sg_ref, w00_ref, b0_ref, c_ref, n_ref, m_ref,
                        cat_ref, c1_ref, n1_ref, m1_ref, vn_ref):
    nb = z_ref.shape[0]
    scale = HEAD_DIM ** -0.5
    eye = (lax.broadcasted_iota(jnp.int32, (HEAD_DIM, HEAD_DIM), 0)
           == lax.broadcasted_iota(jnp.int32, (HEAD_DIM, HEAD_DIM), 1)).astype(F32)
    lane = lax.broadcasted_iota(jnp.int32, (1, LANES), 1)
    for i in range(nb):
        zr = z_ref[i:i + 1, :]
        gates = zr[:, COL_GATES:COL_GATES + LANES] + bif_ref[...]
        m_new = jnp.zeros((1, LANES), F32)
        for h in range(HEADS):
            hs = slice(h * HEAD_DIM, (h + 1) * HEAD_DIM)
            q = zr[:, hs]
            k = zr[:, WIDTH + h * HEAD_DIM:WIDTH + (h + 1) * HEAD_DIM] * scale
            v = zr[:, 2 * WIDTH + h * HEAD_DIM:2 * WIDTH + (h + 1) * HEAD_DIM]
            o = zr[:, 3 * WIDTH + h * HEAD_DIM:3 * WIDTH + (h + 1) * HEAD_DIM]
            ig = gates[:, h:h + 1]
            lf = _log_sigmoid(gates[:, HEADS + h:HEADS + h + 1])
            c0 = c_ref[i, h]
            n0 = n_ref[i, h]
            m0 = m_ref[i:i + 1, h:h + 1]
            a = m0 + lf
            m = jnp.maximum(a, ig)
            s = jnp.sum(q * k, axis=-1, keepdims=True) * jnp.exp(ig - m)
            inter = jnp.exp(a - m)
            cq_col = jnp.sum(c0 * q, axis=-1, keepdims=True)
            cq = jnp.sum(eye * cq_col, axis=0, keepdims=True)
            v_col = jnp.sum(eye * v, axis=-1, keepdims=True)
            num = s * v + inter * cq
            den = s + inter * jnp.sum(n0 * q, axis=-1, keepdims=True)
            hh = num / jnp.maximum(jnp.abs(den), jnp.exp(-m))
            w = jnp.exp(ig - m)
            c1_ref[i, h] = inter * c0 + (w * v_col) * k
            n1_ref[i, h] = inter * n0 + w * k
            m_new = jnp.where(lane == h, m, m_new)
            cat_ref[i:i + 1, hs] = jax.nn.sigmoid(o) * hh
        m1_ref[i:i + 1, :] = m_new
        vn = _rms(zr[:, COL_GV:COL_GV + WIDTH], sg_ref[...])
        vn_ref[i:i + 1, :] = vn
        cat_ref[i:i + 1, WIDTH:2 * WIDTH] = zr[:, COL_U:COL_U + WIDTH] * (w00_ref[...] * vn + b0_ref[...])


def _mixer_ab_step(z, b_if, sgu_g, w00, b0, st_c, st_n, st_m, *, nb):
    n = z.shape[0]
    rows = lambda w: pl.BlockSpec((nb, w), lambda i: (i, 0))
    c_spec = pl.BlockSpec((nb, HEADS, HEAD_DIM, HEAD_DIM), lambda i: (i, 0, 0, 0))
    n_spec = pl.BlockSpec((nb, HEADS, 1, HEAD_DIM), lambda i: (i, 0, 0, 0))
    return pl.pallas_call(
        _mixer_ab_step_body,
        grid=(n // nb,),
        in_specs=[rows(A_IN_PAD), _resident((1, LANES)), _resident((1, WIDTH)), _resident((1, WIDTH)),
                  _resident((1, WIDTH)), c_spec, n_spec, rows(LANES)],
        out_specs=[rows(2 * WIDTH), c_spec, n_spec, rows(LANES), rows(WIDTH)],
        out_shape=[jax.ShapeDtypeStruct((n, 2 * WIDTH), F32),
                   jax.ShapeDtypeStruct(st_c.shape, F32),
                   jax.ShapeDtypeStruct(st_n.shape, F32),
                   jax.ShapeDtypeStruct((n, LANES), F32),
                   jax.ShapeDtypeStruct((n, WIDTH), F32)],
        compiler_params=_params(("arbitrary",)),
        name="mixer_ab_step",
    )(z, b_if, sgu_g.reshape(1, WIDTH), w00, b0, st_c, st_n, st_m)


def _swa_step_body(z_ref, kv0_ref, kv1_ref, kv2_ref, cat_ref):
    nb = z_ref.shape[0]
    scale = HEAD_DIM ** -0.5
    steps = (CHUNK - lax.broadcasted_iota(jnp.int32, (CHUNK, 1, 1), 0)).astype(F32)
    head = lax.broadcasted_iota(jnp.int32, (1, HEADS, 1), 1)
    for i in range(nb):
        ms, ls, os_ = [], [], []
        for gi, kv_ref in enumerate((kv0_ref, kv1_ref, kv2_ref)):
            dil = SWA_GROUPS[gi][1]
            base = gi * 3 * HEADS
            q = z_ref[i, base:base + HEADS, :]
            k_new = z_ref[i, base + HEADS:base + 2 * HEADS, :]
            v_new = z_ref[i, base + 2 * HEADS:base + 3 * HEADS, :]
            kc = kv_ref[i, :, 0, 0, :, :]
            vc = kv_ref[i, :, 0, 1, :, :]
            slope = jnp.zeros((1, HEADS, 1), F32)
            for h in range(HEADS):
                slope = jnp.where(head == h, _alibi_slope(gi, h) * dil, slope)
            s = jnp.sum(kc * q[None], axis=-1, keepdims=True) * scale - slope * steps
            s_new = jnp.sum(k_new * q, axis=-1, keepdims=True) * scale
            m = jnp.maximum(jnp.max(s, axis=0), s_new)
            p = jnp.exp(s - m[None])
            p_new = jnp.exp(s_new - m)
            ms.append(m)
            ls.append(jnp.sum(p, axis=0) + p_new)
            os_.append(jnp.sum(p * vc, axis=0) + p_new * v_new)
        top = functools.reduce(jnp.maximum, ms)
        ws = [jnp.exp(m - top) for m in ms]
        num = sum(w * o for w, o in zip(ws, os_))
        den = sum(w * l for w, l in zip(ws, ls))
        cat_ref[i] = num / den


def _swa_step(z, caches, *, nb):
    n = z.shape[0]
    views = []
    specs = []
    for (win, dil), cache in zip(SWA_GROUPS, caches):
        views.append(cache.reshape(n, win // dil, dil, 2, HEADS, HEAD_DIM))
        specs.append(pl.BlockSpec((nb, CHUNK, 1, 2, HEADS, HEAD_DIM), lambda i: (i, 0, 0, 0, 0, 0)))
    return pl.pallas_call(
        _swa_step_body,
        grid=(n // nb,),
        in_specs=[pl.BlockSpec((nb,) + z.shape[1:], lambda i: (i, 0, 0))] + specs,
        out_specs=pl.BlockSpec((nb, HEADS, HEAD_DIM), lambda i: (i, 0, 0)),
        out_shape=jax.ShapeDtypeStruct((n, HEADS, HEAD_DIM), F32),
        compiler_params=_params(("arbitrary",)),
        name="swa_step",
    )(z, *views)


def _pad_cols(w, n):
    return jnp.pad(w, ((0, 0), (0, n - w.shape[1])))


def kernel(x_prompt, x_sample, state_mlstm_C, state_mlstm_n, state_mlstm_m, cache_swa_kv0, cache_swa_kv1, cache_swa_kv2, norm_g, ffn_w_gate, ffn_w_up, ffn_w_down, a_w_in, a_b_if, sgu_norm_g, sgu_w, sgu_b, a_w_out, c_w_in, c_w_out, final_norm_g):
    nb, s, _ = x_prompt.shape
    ns = x_sample.shape[0]
    assert x_sample.shape[1] == 1 and s % max(w for w, _ in SWA_GROUPS) == 0
    for (win, dil), cache in zip(SWA_GROUPS, (cache_swa_kv0, cache_swa_kv1, cache_swa_kv2)):
        assert cache.shape[2] == win and win // dil == CHUNK

    ffn_w =(jnp.swapaxes(ffn_w_gate, 2, 3), jnp.swapaxes(ffn_w_up, 2, 3), ffn_w_down)
    g_lo, g_hi = 4 * WIDTH, 4 * WIDTH + 2 * HEADS
    a_in = jnp.concatenate([a_w_in[0][:, :g_lo], a_w_in[0][:, g_hi:], _pad_cols(a_w_in[0][:, g_lo:g_hi], LANES)],
                           axis=1).astype(BF16)
    b_if = _pad_cols(a_b_if[0].reshape(1, 2 * HEADS), LANES)
    a_out = a_w_out[0].astype(BF16)
    c_in = c_w_in[0].astype(BF16)
    c_out = c_w_out[0].astype(BF16)
    sgu_bt = _pad_cols(sgu_b[0].T, LANES)
    sgu_w00 = jnp.repeat(sgu_w[0, :, 0, 0], CHUNK).reshape(1, WIDTH)
    sgu_b0 = jnp.repeat(sgu_b[0, :, 0], CHUNK).reshape(1, WIDTH)

    xp = x_prompt.reshape(nb * s, D_MODEL)
    xs = x_sample.reshape(ns, D_MODEL)
    xp, xs = _ffn(xp, xs, norm_g[0, 0], ffn_w, 0, 0, tm=512)
    xp, p_c, p_n, p_m = _mixer_ab_prompt(xp.reshape(nb, s, D_MODEL), norm_g[0, 1], a_in, b_if,
                                          sgu_norm_g[0], sgu_w[0], sgu_bt, a_out)
    z = _proj(xs, norm_g[0, 1], a_in)
    cat, s_c, s_n, s_m, s_v = _mixer_ab_step(
        z, b_if, sgu_norm_g[0], sgu_w00, sgu_b0, state_mlstm_C[0],
        state_mlstm_n[0].reshape(ns, HEADS, 1, HEAD_DIM), _pad_cols(state_mlstm_m[0], LANES), nb=8)
    xs = _out_proj(cat, a_out, xs)
    xp, xs = _ffn(xp.reshape(nb * s, D_MODEL), xs, norm_g[0, 2], ffn_w, 0, 1, tm=512)
    xp, xs = _ffn(xp, xs, norm_g[1, 0], ffn_w, 1, 0, tm=512)
    xp3 = xp.reshape(nb, s, D_MODEL)
    outs, stats = [], []
    for gi in (1, 2):
        o, st = _swa_group_prompt(xp3, norm_g[1, 1], c_in, group=gi, tile=2048)
        outs.append(o)
        stats.append(st)
    p_kv = [_kv_tail(xp3, norm_g[1, 1], c_in, group=gi) for gi in range(3)]
    xp = _swa_group_prompt(xp3, norm_g[1, 1], c_in, group=0, tile=1024, merge_with=(outs, stats, c_out))
    xp = xp.reshape(nb * s, D_MODEL)
    z = _proj(xs, norm_g[1, 1], c_in)
    cat = _swa_step(z.reshape(ns, 3 * 3 * HEADS, HEAD_DIM), (cache_swa_kv0, cache_swa_kv1, cache_swa_kv2), nb=4)
    xs = _out_proj(cat.reshape(ns, WIDTH), c_out, xs)
    y_prompt, y_sample = _ffn(xp, xs, norm_g[1, 2], ffn_w, 1, 1, final_norm_g, tm=512)
    y_prompt = y_prompt.reshape(nb, s, D_MODEL)
    y_sample = y_sample.reshape(ns, 1, D_MODEL)
    s_kv = [z[:, gi * C_GROUP_COLS + WIDTH:(gi + 1) * C_GROUP_COLS].reshape(1, ns, 1, 2, HEADS, HEAD_DIM)
            for gi in range(3)]

    return (y_prompt, y_sample,
            p_c.reshape(1, nb, HEADS, HEAD_DIM, HEAD_DIM), p_n.reshape(1, nb, HEADS, HEAD_DIM),
            p_m[:, :, 0, 0].reshape(1, nb, HEADS),
            s_c.reshape(1, ns, HEADS, HEAD_DIM, HEAD_DIM), s_n.reshape(1, ns, HEADS, HEAD_DIM),
            s_m[:, :HEADS].reshape(1, ns, HEADS), s_v.reshape(1, ns, 1, WIDTH),
            p_kv[0], p_kv[1], p_kv[2], s_kv[0], s_kv[1], s_kv[2])
```

```python
import functools

import jax
import jax.numpy as jnp
from jax import lax
from jax.experimental import pallas as pl
from jax.experimental.pallas import tpu as pltpu

F32 = jnp.float32
BF16 = jnp.bfloat16

D_MODEL = 1024
D_FF = 2752
HEADS = 4
HEAD_DIM = 128
WIDTH = HEADS * HEAD_DIM
CHUNK = 128
SWA_GROUPS = ((128, 1), (512, 4), (2048, 16))
NORM_EPS = 1e-6
NEG_INF = -1e30

LANES = 128
BF16_ROWS = 16
PROJ_SLAB = 256
PROJ_ROWS = 512
FF_CHUNK = 256
ROW_PASS = 256
A_IN_PAD = 4 * WIDTH + 2 * WIDTH + LANES
COL_U = 4 * WIDTH
COL_GV = 5 * WIDTH
COL_GATES = 6 * WIDTH
C_GROUP_COLS = 3 * WIDTH

VMEM_LIMIT = 60 * 1024 * 1024


def _params(semantics):
    return pltpu.CompilerParams(dimension_semantics=semantics, vmem_limit_bytes=VMEM_LIMIT)


def _resident(shape):
    nd = len(shape)
    return pl.BlockSpec(shape, lambda *_: (0,) * nd, pipeline_mode=pl.Buffered(1))


def _rms(x, g):
    ms = jnp.mean(x * x, axis=-1, keepdims=True)
    return x * lax.rsqrt(ms + NORM_EPS) * g


def _dot(a, b):
    return jnp.dot(a, b, preferred_element_type=F32)


def _dot_nt(a, b):
    return lax.dot_general(a, b, (((1,), (1,)), ((), ())), preferred_element_type=F32)


def _log_sigmoid(x):
    return jnp.minimum(x, 0.0) - jnp.log1p(jnp.exp(-jnp.abs(x)))


def _swiglu_rows(h_scr, act_scr, rows, wg_ref, wu_ref, wd_ref):
    for c0 in range(0, D_FF, FF_CHUNK):
        cols = slice(c0, min(c0 + FF_CHUNK, D_FF))
        h = h_scr[0:rows, :]
        gate = _dot_nt(h, wg_ref[0, 0, cols, :])
        up = _dot_nt(h, wu_ref[0, 0, cols, :])
        act_scr[0:rows, cols] = (gate * jax.nn.sigmoid(gate) * up).astype(act_scr.dtype)
    whole = D_FF // FF_CHUNK * FF_CHUNK
    out = _dot(act_scr[0:rows, 0:whole], wd_ref[0, 0, 0:whole, :])
    if whole < D_FF:
        out = out + _dot(act_scr[0:rows, whole:D_FF], wd_ref[0, 0, whole:D_FF, :])
    return out


def _ffn_body(*refs, final):
    if final:
        x_ref, xs_ref, g_ref, wg_ref, wu_ref, wd_ref, fg_ref, o_ref, os_ref, h_scr, act_scr = refs
    else:
        x_ref, xs_ref, g_ref, wg_ref, wu_ref, wd_ref, o_ref, os_ref, h_scr, act_scr = refs
    tm, ns = x_ref.shape[0], xs_ref.shape[0]
    last = pl.num_programs(0) - 1

    def finish(x, acc):
        y = x + 0.5 * acc
        return _rms(y, fg_ref[...]) if final else y

    x = x_ref[...]
    h_scr[0:tm, :] = _rms(x, g_ref[...]).astype(h_scr.dtype)

    @pl.when(pl.program_id(0) != last)
    def _():
        o_ref[...] = finish(x, _swiglu_rows(h_scr, act_scr, tm, wg_ref, wu_ref, wd_ref))

    @pl.when(pl.program_id(0) == last)
    def _():
        xs = xs_ref[...]
        h_scr[tm:tm + ns, :] = _rms(xs, g_ref[...]).astype(h_scr.dtype)
        acc = _swiglu_rows(h_scr, act_scr, tm + ns, wg_ref, wu_ref, wd_ref)
        o_ref[...] = finish(x, acc[0:tm])
        os_ref[...] = finish(xs, acc[tm:tm + ns])


def _ffn(x, xs, g, weights, layer, which, final_g=None, *, tm):
    m, ns = x.shape[0], xs.shape[0]
    final = final_g is not None
    wg, wu, wd = weights
    row = pl.BlockSpec((tm, D_MODEL), lambda i: (i, 0))
    wspec = lambda w: pl.BlockSpec((1, 1) + w.shape[2:], lambda i: (layer, which, 0, 0), pipeline_mode=pl.Buffered(1))
    in_specs = [row, _resident(xs.shape), _resident((1, D_MODEL)), wspec(wg), wspec(wu), wspec(wd)]
    args = [x, xs, g.reshape(1, D_MODEL), wg, wu, wd]
    if final:
        in_specs.append(_resident((1, D_MODEL)))
        args.append(final_g.reshape(1, D_MODEL))
    return pl.pallas_call(
        functools.partial(_ffn_body, final=final),
        grid=(m // tm,),
        in_specs=in_specs,
        out_specs=[row, pl.BlockSpec(xs.shape, lambda i: (0, 0))],
        out_shape=[jax.ShapeDtypeStruct((m, D_MODEL), F32), jax.ShapeDtypeStruct(xs.shape, F32)],
        scratch_shapes=[pltpu.VMEM((tm + ns, D_MODEL), wg.dtype), pltpu.VMEM((tm + ns, D_FF), wg.dtype)],
        compiler_params=_params(("arbitrary",)),
        name="ffn_final" if final else "ffn",
    )(*args)


def _cummax_lanes(x):
    lane = lax.broadcasted_iota(jnp.int32, x.shape, 1)
    d = 1
    while d < x.shape[1]:
        x = jnp.maximum(x, jnp.where(lane >= d, pltpu.roll(x, d, axis=1), NEG_INF))
        d *= 2
    return x


def _exact_tri_dot(tri_bf16, x):
    x1 = x.astype(BF16)
    r1 = x - x1.astype(F32)
    x2 = r1.astype(BF16)
    x3 = (r1 - x2.astype(F32)).astype(BF16)
    n = x.shape[1]
    r = _dot(tri_bf16, jnp.concatenate([x1, x2, x3], axis=1))
    return r[:, 0:n] + r[:, n:2 * n] + r[:, 2 * n:3 * n]


def _block_diag(a, b):
    zero = jnp.zeros_like(a)
    return jnp.concatenate([jnp.concatenate([a, zero], axis=1), jnp.concatenate([zero, b], axis=1)], axis=0)


def _mixer_ab_chunk(xn_ref, xp_ref, g_ref, win_ref, bif_ref, sg_ref, sw_ref, sbt_ref, wout_ref,
                   y_ref, c_ref, n_ref, m_ref, z_cur, z_nxt, cat_scr):
    nb = xn_ref.shape[0]
    row = lax.broadcasted_iota(jnp.int32, (CHUNK, CHUNK), 0)
    col = lax.broadcasted_iota(jnp.int32, (CHUNK, CHUNK), 1)
    causal = col <= row
    keys_before = row <= col
    tri = jnp.where(causal, 1.0, 0.0).astype(BF16)
    scale = HEAD_DIM ** -0.5

    hn = _rms(xn_ref[...].reshape(nb * CHUNK, D_MODEL), g_ref[...]).astype(BF16)
    slabs = [(c0, min(c0 + PROJ_SLAB, A_IN_PAD)) for c0 in range(0, A_IN_PAD, PROJ_SLAB)]

    def project(count):
        for _ in range(min(count, len(slabs))):
            c0, c1 = slabs.pop(0)
            z_nxt[:, c0:c1] = _dot(hn, win_ref[:, c0:c1])

    sgu_bias = [jnp.broadcast_to(sbt_ref[:, g:g + 1], (CHUNK, CHUNK)) for g in range(HEADS)]
    vn = [_rms(z_cur[b * CHUNK:(b + 1) * CHUNK, COL_GV:COL_GV + WIDTH], sg_ref[...]) for b in range(nb)]
    for g in range(HEADS):
        gs = slice(g * CHUNK, (g + 1) * CHUNK)
        mixed = _dot(jnp.where(causal, sw_ref[g], 0.0).astype(BF16),
                     jnp.concatenate([vn[b][:, gs] for b in range(nb)], axis=1).astype(BF16))
        for b in range(nb):
            rows = slice(b * CHUNK, (b + 1) * CHUNK)
            u = z_cur[rows, COL_U + g * CHUNK:COL_U + (g + 1) * CHUNK]
            cat_scr[rows, WIDTH + g * CHUNK:WIDTH + (g + 1) * CHUNK] = (
                u * (mixed[:, b * CHUNK:(b + 1) * CHUNK] + sgu_bias[g])).astype(BF16)
    project(1)

    pieces_per_stage = -(-(len(slabs)) // (3 * nb))
    gate_terms = []
    for b in range(nb):
        rows = slice(b * CHUNK, (b + 1) * CHUNK)
        gates = z_cur[rows, COL_GATES:COL_GATES + LANES] + bif_ref[...]
        lg = jnp.where(col < HEADS, gates, _log_sigmoid(gates))
        gate_terms.append((lg, _exact_tri_dot(tri, lg)))
        project(pieces_per_stage)

    heads = []
    for b, (lg, fcum) in enumerate(gate_terms):
        rows = slice(b * CHUNK, (b + 1) * CHUNK)
        lg_t = lg.T
        fcum_t = fcum.T
        gmax = _cummax_lanes(lg_t[0:2 * HEADS, :] - jnp.concatenate([fcum_t[HEADS:2 * HEADS, :]] * 2, axis=0))
        for h0 in range(0, HEADS, 2):
            pair = []
            for h in (h0, h0 + 1):
                q = z_cur[rows, h * HEAD_DIM:(h + 1) * HEAD_DIM]
                k = z_cur[rows, WIDTH + h * HEAD_DIM:WIDTH + (h + 1) * HEAD_DIM] * scale
                v_t = z_cur[rows, 2 * WIDTH + h * HEAD_DIM:2 * WIDTH + (h + 1) * HEAD_DIM].T
                f_r = fcum_t[HEADS + h:HEADS + h + 1, :]
                i_r = lg_t[h:h + 1, :]
                g_c = lg[:, h:h + 1] - fcum[:, HEADS + h:HEADS + h + 1]
                c0, n0, m0 = c_ref[b, h], n_ref[b, h], m_ref[b, h]
                m = f_r + jnp.maximum(m0, gmax[h:h + 1, :])
                inter = jnp.exp(m0 + f_r - m)
                m_last = m[:, CHUNK - 1:CHUNK]
                f_last = f_r[:, CHUNK - 1:CHUNK]
                w = jnp.exp(f_last - f_r + i_r - m_last)
                decay = jnp.exp(m0[:, 0:1] + f_last - m_last)
                m_ref[b, h] = jnp.broadcast_to(m_last, (1, LANES))
                pair.append(dict(q=q, k=k, v_t=v_t, f_r=f_r, g_c=g_c, c0=c0, n0=n0, m=m, inter=inter, w=w,
                                 decay=decay))
            against_q = _dot_nt(
                jnp.concatenate([jnp.concatenate([d["k"], d["c0"], jnp.broadcast_to(d["n0"], (BF16_ROWS, HEAD_DIM))],
                                                 axis=0) for d in pair], axis=1).astype(BF16),
                _block_diag(pair[0]["q"], pair[1]["q"]).astype(BF16))
            against_k = _dot(
                jnp.concatenate([jnp.concatenate([d["v_t"] * d["w"], jnp.broadcast_to(d["w"], (BF16_ROWS, CHUNK))],
                                                 axis=0) for d in pair], axis=1).astype(BF16),
                _block_diag(pair[0]["k"], pair[1]["k"]).astype(BF16))
            for i, d in enumerate(pair):
                h = h0 + i
                mine = slice(i * HEAD_DIM, (i + 1) * HEAD_DIM)
                c_ref[b, h] = d["decay"] * d["c0"] + against_k[0:HEAD_DIM, mine]
                n_ref[b, h] = d["decay"] * d["n0"] + against_k[HEAD_DIM:HEAD_DIM + 1, mine]
            heads.append((against_q, pair))
        project(pieces_per_stage)

    partial = []
    for i, (against_q, pair) in enumerate(heads):
        s_ts, dens = [], []
        for j, d in enumerate(pair):
            mine = slice(j * CHUNK, (j + 1) * CHUNK)
            kq = against_q[0:CHUNK, mine]
            nq = against_q[CHUNK + HEAD_DIM:CHUNK + HEAD_DIM + 1, mine]
            s_t = kq * jnp.exp(jnp.where(keys_before, (d["f_r"] - d["m"]) + d["g_c"], NEG_INF))
            s_ts.append(s_t)
            dens.append(jnp.maximum(jnp.abs(jnp.sum(s_t, axis=0, keepdims=True) + d["inter"] * nq),
                                    jnp.exp(-d["m"])))
        sv = _dot(jnp.concatenate([d["v_t"] for d in pair], axis=1).astype(BF16),
                  _block_diag(s_ts[0], s_ts[1]).astype(BF16))
        for j, d in enumerate(pair):
            mine = slice(j * CHUNK, (j + 1) * CHUNK)
            partial.append((sv[:, mine], d["inter"] * against_q[CHUNK:CHUNK + HEAD_DIM, mine], dens[j]))
        if i % (HEADS // 2) == HEADS // 2 - 1:
            project(pieces_per_stage)

    for i, (sv, carried, den) in enumerate(partial):
        b, h = divmod(i, HEADS)
        rows = slice(b * CHUNK, (b + 1) * CHUNK)
        o = z_cur[rows, 3 * WIDTH + h * HEAD_DIM:3 * WIDTH + (h + 1) * HEAD_DIM]
        cat_scr[rows, h * HEAD_DIM:(h + 1) * HEAD_DIM] = (jax.nn.sigmoid(o) * ((sv + carried) / den).T).astype(BF16)

    project(len(slabs))
    y = xp_ref[...].reshape(nb * CHUNK, D_MODEL) + _dot(cat_scr[...], wout_ref[...])
    y_ref[...] = y.reshape(nb, CHUNK, D_MODEL)


def _mixer_ab_body(*refs):
    *io_refs, z0_scr, z1_scr, cat_scr = refs
    c_ref, n_ref, m_ref = io_refs[-3:]
    step = pl.program_id(0)

    @pl.when(step <= 1)
    def _():
        c_ref[...] = jnp.zeros_like(c_ref)
        n_ref[...] = jnp.zeros_like(n_ref)
        m_ref[...] = jnp.zeros_like(m_ref)

    @pl.when(step == 0)
    def _():
        z1_scr[...] = jnp.zeros_like(z1_scr)

    @pl.when(step % 2 == 0)
    def _():
        _mixer_ab_chunk(*io_refs, z1_scr, z0_scr, cat_scr)

    @pl.when(step % 2 == 1)
    def _():
        _mixer_ab_chunk(*io_refs, z0_scr, z1_scr, cat_scr)


def _mixer_ab_prompt(x, g, w_in, b_if, sgu_g, sgu_w, sgu_bt, w_out):
    nb, s, _ = x.shape
    n_chunks = s // CHUNK
    blk = lambda index: pl.BlockSpec((nb, CHUNK, D_MODEL), index)
    nxt = blk(lambda c: (0, jnp.minimum(c, n_chunks - 1), 0))
    prev = blk(lambda c: (0, jnp.maximum(c - 1, 0), 0))
    z_shape = pltpu.VMEM((nb * CHUNK, A_IN_PAD), F32)
    return pl.pallas_call(
        _mixer_ab_body,
        grid=(n_chunks + 1,),
        in_specs=[nxt, prev, _resident((1, D_MODEL)), _resident(w_in.shape), _resident((1, LANES)),
                  _resident((1, WIDTH)), _resident(sgu_w.shape), _resident(sgu_bt.shape), _resident(w_out.shape)],
        out_specs=[prev,
                   pl.BlockSpec((nb, HEADS, HEAD_DIM, HEAD_DIM), lambda c: (0, 0, 0, 0)),
                   pl.BlockSpec((nb, HEADS, 1, HEAD_DIM), lambda c: (0, 0, 0, 0)),
                   pl.BlockSpec((nb, HEADS, 1, LANES), lambda c: (0, 0, 0, 0))],
        out_shape=[jax.ShapeDtypeStruct(x.shape, F32),
                   jax.ShapeDtypeStruct((nb, HEADS, HEAD_DIM, HEAD_DIM), F32),
                   jax.ShapeDtypeStruct((nb, HEADS, 1, HEAD_DIM), F32),
                   jax.ShapeDtypeStruct((nb, HEADS, 1, LANES), F32)],
        scratch_shapes=[z_shape, z_shape, pltpu.VMEM((nb * CHUNK, 2 * WIDTH), BF16)],
        compiler_params=_params(("arbitrary",)),
        name="mixer_ab_prompt",
    )(x, x, g.reshape(1, D_MODEL), w_in, b_if, sgu_g.reshape(1, WIDTH), sgu_w, sgu_bt, w_out)


def _alibi_slope(group, head):
    n = len(SWA_GROUPS) * HEADS
    return 2.0 ** (-8.0 * (group * HEADS + head + 1) / n)


def _slab_pitch(dil):
    return dil + 8 if dil % 8 == 0 else dil


def _swa_group_body(*refs, group, dil, tile, others):
    x_ref, g_ref, w_ref = refs[0:3]
    if others:
        other_o = refs[3:3 + others]
        other_st = refs[3 + others:3 + 2 * others]
        wout_ref, y_ref, slab_scr, q_scr, k_scr, v_scr, o_dst, st_dst, cat_scr = refs[3 + 2 * others:]
    else:
        o_ref, st_ref, slab_scr, q_scr, k_scr, v_scr = refs[3:]
        o_dst, st_dst = o_ref.at[0], st_ref.at[0]
    step = pl.program_id(1)
    sub = tile // dil
    if sub >= PROJ_ROWS:
        pieces = [[(r, f, PROJ_ROWS)] for r in range(dil) for f in range(0, sub, PROJ_ROWS)]
    else:
        per = PROJ_ROWS // sub
        pieces = [[(r, 0, sub) for r in range(p * per, (p + 1) * per)] for p in range(dil // per)]
    scale = HEAD_DIM ** -0.5
    qi = lax.broadcasted_iota(jnp.int32, (CHUNK, 2 * CHUNK), 0)
    kc = lax.broadcasted_iota(jnp.int32, (CHUNK, 2 * CHUNK), 1)
    delta = CHUNK + qi - kc
    valid = (delta >= 0) & (delta <= CHUNK)
    valid_first = valid & (kc >= jnp.where(step > 0, 0, CHUNK))
    dist = (delta * dil).astype(F32)
    lane = lax.broadcasted_iota(jnp.int32, (CHUNK, LANES), 1)

    @pl.when(step == 0)
    def _():
        k_scr[:, 0:CHUNK, :] = jnp.zeros((dil, CHUNK, WIDTH), BF16)
        v_scr[:, 0:CHUNK, :] = jnp.zeros((dil, CHUNK, WIDTH), BF16)

    pitch = _slab_pitch(dil)
    for c in range(tile // ROW_PASS):
        hn = _rms(x_ref[0, c * ROW_PASS:(c + 1) * ROW_PASS, :], g_ref[...])
        groups = [(0, ROW_PASS)] if pitch == dil else [(g * dil, dil) for g in range(ROW_PASS // dil)]
        for first, count in groups:
            dst = (c * ROW_PASS + first) // dil * pitch
            for sl in range(D_MODEL // LANES):
                slab_scr[sl, dst:dst + count, :] = hn[first:first + count, sl * LANES:(sl + 1) * LANES]

    def token_rows(r, first, count):
        start = first * dil + r
        return slice(start, start + count) if dil == 1 else pl.ds(start, count, stride=dil)

    def project(piece):
        parts = []
        for r, first, count in piece:
            src = slice(first, first + count) if dil == 1 else pl.ds(first * pitch + r, count, stride=pitch)
            parts.append(jnp.concatenate([slab_scr[sl, src, :] for sl in range(D_MODEL // LANES)],
                                         axis=1).astype(BF16))
        hn = parts[0] if len(parts) == 1 else jnp.concatenate(parts, axis=0)
        for j, scr in enumerate((q_scr, k_scr, v_scr)):
            z = _dot(hn, w_ref[:, j * WIDTH:(j + 1) * WIDTH]).astype(BF16)
            at = 0
            for r, first, count in piece:
                off = first if j == 0 else CHUNK + first
                scr[r, off:off + count, :] = z[at:at + count, :]
                at += count

    def scores(piece):
        out = []
        for r, first, count in piece:
            for j in range(first // CHUNK, (first + count) // CHUNK):
                mask = valid_first if j == 0 else valid
                for h in range(HEADS):
                    hs = slice(h * HEAD_DIM, (h + 1) * HEAD_DIM)
                    s = _dot_nt(q_scr[r, j * CHUNK:(j + 1) * CHUNK, hs], k_scr[r, j * CHUNK:(j + 2) * CHUNK, hs])
                    s = jnp.where(mask, s * scale + (-_alibi_slope(group, h)) * dist, NEG_INF)
                    m = jnp.max(s, axis=-1, keepdims=True)
                    p_ = jnp.exp(s - m)
                    out.append((r, j, h, m, jnp.sum(p_, axis=-1, keepdims=True), p_.astype(BF16)))
        return out

    def values(items):
        stats = None
        for r, j, h, m, l, p_ in items:
            hs = slice(h * HEAD_DIM, (h + 1) * HEAD_DIM)
            o_dst[h, token_rows(r, j * CHUNK, CHUNK), :] = _dot(p_, v_scr[r, j * CHUNK:(j + 2) * CHUNK, hs])
            base = jnp.zeros((CHUNK, LANES), F32) if h == 0 else stats
            stats = jnp.where(lane == h, m, jnp.where(lane == HEADS + h, l, base))
            if h == HEADS - 1:
                st_dst[token_rows(r, j * CHUNK, CHUNK), :] = stats

    def halves(piece):
        if len(piece) == 1:
            r, first, count = piece[0]
            return [[(r, first, count // 2)], [(r, first + count // 2, count // 2)]]
        return [piece[:len(piece) // 2], piece[len(piece) // 2:]]

    def merge(piece):
        (_, first, count), = piece
        for c in range(first // ROW_PASS, (first + count) // ROW_PASS):
            rows = slice(c * ROW_PASS, (c + 1) * ROW_PASS)
            stats = [st_dst[rows, :]] + [s_ref[0, rows, :] for s_ref in other_st]
            top = functools.reduce(jnp.maximum, stats)
            ws = [jnp.exp(st - top) for st in stats]
            den = sum(w * pltpu.roll(st, LANES - HEADS, axis=1) for w, st in zip(ws, stats))
            den = jnp.where(lane[0:1, :] < HEADS, den, 1.0)
            shares = [w / den for w in ws]
            for h in range(HEADS):
                outs = [o_dst[h, rows, :]] + [o_ref[0, h, rows, :] for o_ref in other_o]
                merged = sum(a[:, h:h + 1] * o for a, o in zip(shares, outs))
                cat_scr[rows, h * HEAD_DIM:(h + 1) * HEAD_DIM] = merged.astype(BF16)
        rows = slice(first, first + count)
        y_ref[0, rows, :] = x_ref[0, rows, :] + _dot(cat_scr[rows, :], wout_ref[...])

    def finish(done):
        items, p, i = done
        values(items)
        if others and i == 1:
            merge(pieces[p])

    project(pieces[0])
    pending = None
    for p, piece in enumerate(pieces):
        for i, half in enumerate(halves(piece)):
            items = scores(half)
            if i == 0 and p + 1 < len(pieces):
                project(pieces[p + 1])
            if pending is not None:
                finish(pending)
            pending = (items, p, i)
    finish(pending)
    for r in range(dil):
        k_scr[r, 0:CHUNK, :] = k_scr[r, sub:sub + CHUNK, :]
        v_scr[r, 0:CHUNK, :] = v_scr[r, sub:sub + CHUNK, :]


def _swa_group_prompt(x, g, c_in, *, group, tile, merge_with=None):
    nb, s, _ = x.shape
    dil = SWA_GROUPS[group][1]
    sub = tile // dil
    o_spec = pl.BlockSpec((1, HEADS, tile, HEAD_DIM), lambda b, t: (b, 0, t, 0))
    s_spec = pl.BlockSpec((1, tile, LANES), lambda b, t: (b, t, 0))
    x_spec = pl.BlockSpec((1, tile, D_MODEL), lambda b, t: (b, t, 0))
    in_specs = [x_spec, _resident((1, D_MODEL)),
                pl.BlockSpec((D_MODEL, C_GROUP_COLS), lambda b, t: (0, group), pipeline_mode=pl.Buffered(1))]
    args = [x, g.reshape(1, D_MODEL), c_in]
    scratch = [pltpu.VMEM((D_MODEL // LANES, sub * _slab_pitch(dil), LANES), F32),
               pltpu.VMEM((dil, sub, WIDTH), BF16),
               pltpu.VMEM((dil, CHUNK + sub, WIDTH), BF16),
               pltpu.VMEM((dil, CHUNK + sub, WIDTH), BF16)]
    if merge_with is None:
        others = 0
        out_specs = [o_spec, s_spec]
        out_shape = [jax.ShapeDtypeStruct((nb, HEADS, s, HEAD_DIM), F32), jax.ShapeDtypeStruct((nb, s, LANES), F32)]
    else:
        outs, stats, w_out = merge_with
        others = len(outs)
        in_specs += [o_spec] * others + [s_spec] * others + [_resident(w_out.shape)]
        args += [*outs, *stats, w_out]
        out_specs = x_spec
        out_shape = jax.ShapeDtypeStruct(x.shape, F32)
        scratch += [pltpu.VMEM((HEADS, tile, HEAD_DIM), F32), pltpu.VMEM((tile, LANES), F32),
                    pltpu.VMEM((tile, WIDTH), BF16)]
    return pl.pallas_call(
        functools.partial(_swa_group_body, group=group, dil=dil, tile=tile, others=others),
        grid=(nb, s // tile),
        in_specs=in_specs,
        out_specs=out_specs,
        out_shape=out_shape,
        scratch_shapes=scratch,
        compiler_params=_params(("arbitrary", "arbitrary")),
        name="swa_group%d" % group,
    )(*args)


def _kv_tail_body(x_ref, g_ref, wk_ref, wv_ref, kv_ref):
    tm = x_ref.shape[1]
    hn = _rms(x_ref[0], g_ref[...]).astype(BF16)
    for j, w_ref in enumerate((wk_ref, wv_ref)):
        kv = _dot(hn, w_ref[...])
        for h in range(HEADS):
            kv_ref[pl.ds(j * HEADS + h, tm, stride=2 * HEADS), :] = kv[:, h * HEAD_DIM:(h + 1) * HEAD_DIM]


def _kv_tail(x, g, c_in, *, group):
    win = SWA_GROUPS[group][0]
    nb, s, _ = x.shape
    tm = min(win, 1024)
    first = (s - win) // tm
    steps = win // tm
    wcol = lambda j: pl.BlockSpec((D_MODEL, WIDTH), lambda b, t: (0, 3 * group + j), pipeline_mode=pl.Buffered(1))
    kv = pl.pallas_call(
        _kv_tail_body,
        grid=(nb, steps),
        in_specs=[pl.BlockSpec((1, tm, D_MODEL), lambda b, t: (b, first + t, 0)), _resident((1, D_MODEL)),
                  wcol(1), wcol(2)],
        out_specs=pl.BlockSpec((tm * 2 * HEADS, HEAD_DIM), lambda b, t: (b * steps + t, 0)),
        out_shape=jax.ShapeDtypeStruct((nb * win * 2 * HEADS, HEAD_DIM), F32),
        compiler_params=_params(("arbitrary", "arbitrary")),
        name="kv_tail%d" % group,
    )(x, g.reshape(1, D_MODEL), c_in, c_in)
    return kv.reshape(1, nb, win, 2, HEADS, HEAD_DIM)


def _proj_body(x_ref, g_ref, w_ref, z_ref):
    z_ref[...] = _dot(_rms(x_ref[...], g_ref[...]).astype(BF16), w_ref[...])


def _proj(x, g, w):
    m, n = x.shape[0], w.shape[1]
    return pl.pallas_call(
        _proj_body,
        grid=(1,),
        in_specs=[_resident(x.shape), _resident((1, D_MODEL)), _resident(w.shape)],
        out_specs=pl.BlockSpec((m, n), lambda i: (0, 0)),
        out_shape=jax.ShapeDtypeStruct((m, n), F32),
        compiler_params=_params(("arbitrary",)),
        name="proj_rows",
    )(x, g.reshape(1, D_MODEL), w)


def _out_proj_body(c_ref, w_ref, x_ref, y_ref):
    y_ref[...] = x_ref[...] + _dot(c_ref[...].astype(BF16), w_ref[...])


def _out_proj(cat, w, x):
    return pl.pallas_call(
        _out_proj_body,
        grid=(1,),
        in_specs=[_resident(cat.shape), _resident(w.shape), _resident(x.shape)],
        out_specs=pl.BlockSpec(x.shape, lambda i: (0, 0)),
        out_shape=jax.ShapeDtypeStruct(x.shape, F32),
        compiler_params=_params(("arbitrary",)),
        name="out_proj_rows",
    )(cat, w, x)


def _mixer_ab_step_body(z_ref, bif_ref, sg_ref, w00_ref, b0_ref, c_ref, n_ref, m_ref,
                        cat_ref, c1_ref, n1_ref, m1_ref, vn_ref):
    nb = z_ref.shape[0]
    scale = HEAD_DIM ** -0.5
    eye = (lax.broadcasted_iota(jnp.int32, (HEAD_DIM, HEAD_DIM), 0)
           == lax.broadcasted_iota(jnp.int32, (HEAD_DIM, HEAD_DIM), 1)).astype(F32)
    lane = lax.broadcasted_iota(jnp.int32, (1, LANES), 1)
    for i in range(nb):
        zr = z_ref[i:i + 1, :]
        gates = zr[:, COL_GATES:COL_GATES + LANES] + bif_ref[...]
        m_new = jnp.zeros((1, LANES), F32)
        for h in range(HEADS):
            hs = slice(h * HEAD_DIM, (h + 1) * HEAD_DIM)
            q = zr[:, hs]
            k = zr[:, WIDTH + h * HEAD_DIM:WIDTH + (h + 1) * HEAD_DIM] * scale
            v = zr[:, 2 * WIDTH + h * HEAD_DIM:2 * WIDTH + (h + 1) * HEAD_DIM]
            o = zr[:, 3 * WIDTH + h * HEAD_DIM:3 * WIDTH + (h + 1) * HEAD_DIM]
            ig = gates[:, h:h + 1]
            lf = _log_sigmoid(gates[:, HEADS + h:HEADS + h + 1])
            c0 = c_ref[i, h]
            n0 = n_ref[i, h]
            m0 = m_ref[i:i + 1, h:h + 1]
            a = m0 + lf
            m = jnp.maximum(a, ig)
            s = jnp.sum(q * k, axis=-1, keepdims=True) * jnp.exp(ig - m)
            inter = jnp.exp(a - m)
            cq_col = jnp.sum(c0 * q, axis=-1, keepdims=True)
            cq = jnp.sum(eye * cq_col, axis=0, keepdims=True)
            v_col = jnp.sum(eye * v, axis=-1, keepdims=True)
            num = s * v + inter * cq
            den = s + inter * jnp.sum(n0 * q, axis=-1, keepdims=True)
            hh = num / jnp.maximum(jnp.abs(den), jnp.exp(-m))
            w = jnp.exp(ig - m)
            c1_ref[i, h] = inter * c0 + (w * v_col) * k
            n1_ref[i, h] = inter * n0 + w * k
            m_new = jnp.where(lane == h, m, m_new)
            cat_ref[i:i + 1, hs] = jax.nn.sigmoid(o) * hh
        m1_ref[i:i + 1, :] = m_new
        vn = _rms(zr[:, COL_GV:COL_GV + WIDTH], sg_ref[...])
        vn_ref[i:i + 1, :] = vn
        cat_ref[i:i + 1, WIDTH:2 * WIDTH] = zr[:, COL_U:COL_U + WIDTH] * (w00_ref[...] * vn + b0_ref[...])


def _mixer_ab_step(z, b_if, sgu_g, w00, b0, st_c, st_n, st_m, *, nb):
    n = z.shape[0]
    rows = lambda w: pl.BlockSpec((nb, w), lambda i: (i, 0))
    c_spec = pl.BlockSpec((nb, HEADS, HEAD_DIM, HEAD_DIM), lambda i: (i, 0, 0, 0))
    n_spec = pl.BlockSpec((nb, HEADS, 1, HEAD_DIM), lambda i: (i, 0, 0, 0))
    return pl.pallas_call(
        _mixer_ab_step_body,
        grid=(n // nb,),
        in_specs=[rows(A_IN_PAD), _resident((1, LANES)), _resident((1, WIDTH)), _resident((1, WIDTH)),
                  _resident((1, WIDTH)), c_spec, n_spec, rows(LANES)],
        out_specs=[rows(2 * WIDTH), c_spec, n_spec, rows(LANES), rows(WIDTH)],
        out_shape=[jax.ShapeDtypeStruct((n, 2 * WIDTH), F32),
                   jax.ShapeDtypeStruct(st_c.shape, F32),
                   jax.ShapeDtypeStruct(st_n.shape, F32),
                   jax.ShapeDtypeStruct((n, LANES), F32),
                   jax.ShapeDtypeStruct((n, WIDTH), F32)],
        compiler_params=_params(("arbitrary",)),
        name="mixer_ab_step",
    )(z, b_if, sgu_g.reshape(1, WIDTH), w00, b0, st_c, st_n, st_m)


def _swa_step_body(z_ref, kv0_ref, kv1_ref, kv2_ref, cat_ref):
    nb = z_ref.shape[0]
    scale = HEAD_DIM ** -0.5
    steps = (CHUNK - lax.broadcasted_iota(jnp.int32, (CHUNK, 1, 1), 0)).astype(F32)
    head = lax.broadcasted_iota(jnp.int32, (1, HEADS, 1), 1)
    for i in range(nb):
        ms, ls, os_ = [], [], []
        for gi, kv_ref in enumerate((kv0_ref, kv1_ref, kv2_ref)):
            dil = SWA_GROUPS[gi][1]
            base = gi * 3 * HEADS
            q = z_ref[i, base:base + HEADS, :]
            k_new = z_ref[i, base + HEADS:base + 2 * HEADS, :]
            v_new = z_ref[i, base + 2 * HEADS:base + 3 * HEADS, :]
            kc = kv_ref[i, :, 0, 0, :, :]
            vc = kv_ref[i, :, 0, 1, :, :]
            slope = jnp.zeros((1, HEADS, 1), F32)
            for h in range(HEADS):
                slope = jnp.where(head == h, _alibi_slope(gi, h) * dil, slope)
            s = jnp.sum(kc * q[None], axis=-1, keepdims=True) * scale - slope * steps
            s_new = jnp.sum(k_new * q, axis=-1, keepdims=True) * scale
            m = jnp.maximum(jnp.max(s, axis=0), s_new)
            p = jnp.exp(s - m[None])
            p_new = jnp.exp(s_new - m)
            ms.append(m)
            ls.append(jnp.sum(p, axis=0) + p_new)
            os_.append(jnp.sum(p * vc, axis=0) + p_new * v_new)
        top = functools.reduce(jnp.maximum, ms)
        ws = [jnp.exp(m - top) for m in ms]
        num = sum(w * o for w, o in zip(ws, os_))
        den = sum(w * l for w, l in zip(ws, ls))
        cat_ref[i] = num / den


def _swa_step(z, caches, *, nb):
    n = z.shape[0]
    views = []
    specs = []
    for (win, dil), cache in zip(SWA_GROUPS, caches):
        views.append(cache.reshape(n, win // dil, dil, 2, HEADS, HEAD_DIM))
        specs.append(pl.BlockSpec((nb, CHUNK, 1, 2, HEADS, HEAD_DIM), lambda i: (i, 0, 0, 0, 0, 0)))
    return pl.pallas_call(
        _swa_step_body,
        grid=(n // nb,),
        in_specs=[pl.BlockSpec((nb,) + z.shape[1:], lambda i: (i, 0, 0))] + specs,
        out_specs=pl.BlockSpec((nb, HEADS, HEAD_DIM), lambda i: (i, 0, 0)),
        out_shape=jax.ShapeDtypeStruct((n, HEADS, HEAD_DIM), F32),
        compiler_params=_params(("arbitrary",)),
        name="swa_step",
    )(z, *views)


def _pad_cols(w, n):
    return jnp.pad(w, ((0, 0), (0, n - w.shape[1])))


def kernel(x_prompt, x_sample, state_mlstm_C, state_mlstm_n, state_mlstm_m, cache_swa_kv0, cache_swa_kv1, cache_swa_kv2, norm_g, ffn_w_gate, ffn_w_up, ffn_w_down, a_w_in, a_b_if, sgu_norm_g, sgu_w, sgu_b, a_w_out, c_w_in, c_w_out, final_norm_g):
    nb, s, _ = x_prompt.shape
    ns = x_sample.shape[0]
    assert x_sample.shape[1] == 1 and s % max(w for w, _ in SWA_GROUPS) == 0
    for (win, dil), cache in zip(SWA_GROUPS, (cache_swa_kv0, cache_swa_kv1, cache_swa_kv2)):
        assert cache.shape[2] == win and win // dil == CHUNK

    ffn_w =(jnp.swapaxes(ffn_w_gate, 2, 3), jnp.swapaxes(ffn_w_up, 2, 3), ffn_w_down)
    g_lo, g_hi = 4 * WIDTH, 4 * WIDTH + 2 * HEADS
    a_in = jnp.concatenate([a_w_in[0][:, :g_lo], a_w_in[0][:, g_hi:], _pad_cols(a_w_in[0][:, g_lo:g_hi], LANES)],
                           axis=1).astype(BF16)
    b_if = _pad_cols(a_b_if[0].reshape(1, 2 * HEADS), LANES)
    a_out = a_w_out[0].astype(BF16)
    c_in = c_w_in[0].astype(BF16)
    c_out = c_w_out[0].astype(BF16)
    sgu_bt = _pad_cols(sgu_b[0].T, LANES)
    sgu_w00 = jnp.repeat(sgu_w[0, :, 0, 0], CHUNK).reshape(1, WIDTH)
    sgu_b0 = jnp.repeat(sgu_b[0, :, 0], CHUNK).reshape(1, WIDTH)

    xp = x_prompt.reshape(nb * s, D_MODEL)
    xs = x_sample.reshape(ns, D_MODEL)
    xp, xs = _ffn(xp, xs, norm_g[0, 0], ffn_w, 0, 0, tm=512)
    xp, p_c, p_n, p_m = _mixer_ab_prompt(xp.reshape(nb, s, D_MODEL), norm_g[0, 1], a_in, b_if,
                                          sgu_norm_g[0], sgu_w[0], sgu_bt, a_out)
    z = _proj(xs, norm_g[0, 1], a_in)
    cat, s_c, s_n, s_m, s_v = _mixer_ab_step(
        z, b_if, sgu_norm_g[0], sgu_w00, sgu_b0, state_mlstm_C[0],
        state_mlstm_n[0].reshape(ns, HEADS, 1, HEAD_DIM), _pad_cols(state_mlstm_m[0], LANES), nb=8)
    xs = _out_proj(cat, a_out, xs)
    xp, xs = _ffn(xp.reshape(nb * s, D_MODEL), xs, norm_g[0, 2], ffn_w, 0, 1, tm=512)
    xp, xs = _ffn(xp, xs, norm_g[1, 0], ffn_w, 1, 0, tm=512)
    xp3 = xp.reshape(nb, s, D_MODEL)
    outs, stats = [], []
    for gi in (1, 2):
        o, st = _swa_group_prompt(xp3, norm_g[1, 1], c_in, group=gi, tile=2048)
        outs.append(o)
        stats.append(st)
    p_kv = [_kv_tail(xp3, norm_g[1, 1], c_in, group=gi) for gi in range(3)]
    xp = _swa_group_prompt(xp3, norm_g[1, 1], c_in, group=0, tile=1024, merge_with=(outs, stats, c_out))
    xp = xp.reshape(nb * s, D_MODEL)
    z = _proj(xs, norm_g[1, 1], c_in)
    cat = _swa_step(z.reshape(ns, 3 * 3 * HEADS, HEAD_DIM), (cache_swa_kv0, cache_swa_kv1, cache_swa_kv2), nb=4)
    xs = _out_proj(cat.reshape(ns, WIDTH), c_out, xs)
    y_prompt, y_sample = _ffn(xp, xs, norm_g[1, 2], ffn_w, 1, 1, final_norm_g, tm=512)
    y_prompt = y_prompt.reshape(nb, s, D_MODEL)
    y_sample = y_sample.reshape(ns, 1, D_MODEL)
    s_kv = [z[:, gi * C_GROUP_COLS + WIDTH:(gi + 1) * C_GROUP_COLS].reshape(1, ns, 1, 2, HEADS, HEAD_DIM)
            for gi in range(3)]

    return (y_prompt, y_sample,
            p_c.reshape(1, nb, HEADS, HEAD_DIM, HEAD_DIM), p_n.reshape(1, nb, HEADS, HEAD_DIM),
            p_m[:, :, 0, 0].reshape(1, nb, HEADS),
            s_c.reshape(1, ns, HEADS, HEAD_DIM, HEAD_DIM), s_n.reshape(1, ns, HEADS, HEAD_DIM),
            s_m[:, :HEADS].reshape(1, ns, HEADS), s_v.reshape(1, ns, 1, WIDTH),
            p_kv[0], p_kv[1], p_kv[2], s_kv[0], s_kv[1], s_kv[2])
```

```python
import functools

import jax
import jax.numpy as jnp
from jax import lax
from jax.experimental import pallas as pl
from jax.experimental.pallas import tpu as pltpu

F32 = jnp.float32
BF16 = jnp.bfloat16

D_MODEL = 1024
D_FF = 2752
HEADS = 4
HEAD_DIM = 128
WIDTH = HEADS * HEAD_DIM
CHUNK = 128
SWA_GROUPS = ((128, 1), (512, 4), (2048, 16))
NORM_EPS = 1e-6
NEG_INF = -1e30

LANES = 128
BF16_ROWS = 16
PROJ_SLAB = 256
PROJ_ROWS = 512
FF_CHUNK = 256
ROW_PASS = 256
A_IN_PAD = 4 * WIDTH + 2 * WIDTH + LANES
COL_U = 4 * WIDTH
COL_GV = 5 * WIDTH
COL_GATES = 6 * WIDTH
C_GROUP_COLS = 3 * WIDTH

VMEM_LIMIT = 60 * 1024 * 1024


def _params(semantics):
    return pltpu.CompilerParams(dimension_semantics=semantics, vmem_limit_bytes=VMEM_LIMIT)


def _resident(shape):
    nd = len(shape)
    return pl.BlockSpec(shape, lambda *_: (0,) * nd, pipeline_mode=pl.Buffered(1))


def _rms(x, g):
    ms = jnp.mean(x * x, axis=-1, keepdims=True)
    return x * lax.rsqrt(ms + NORM_EPS) * g


def _dot(a, b):
    return jnp.dot(a, b, preferred_element_type=F32)


def _dot_nt(a, b):
    return lax.dot_general(a, b, (((1,), (1,)), ((), ())), preferred_element_type=F32)


def _log_sigmoid(x):
    return jnp.minimum(x, 0.0) - jnp.log1p(jnp.exp(-jnp.abs(x)))


def _swiglu_rows(h_scr, act_scr, rows, wg_ref, wu_ref, wd_ref):
    for c0 in range(0, D_FF, FF_CHUNK):
        cols = slice(c0, min(c0 + FF_CHUNK, D_FF))
        h = h_scr[0:rows, :]
        gate = _dot_nt(h, wg_ref[0, 0, cols, :])
        up = _dot_nt(h, wu_ref[0, 0, cols, :])
        act_scr[0:rows, cols] = (gate * jax.nn.sigmoid(gate) * up).astype(act_scr.dtype)
    whole = D_FF // FF_CHUNK * FF_CHUNK
    out = _dot(act_scr[0:rows, 0:whole], wd_ref[0, 0, 0:whole, :])
    if whole < D_FF:
        out = out + _dot(act_scr[0:rows, whole:D_FF], wd_ref[0, 0, whole:D_FF, :])
    return out


def _ffn_body(*refs, final):
    if final:
        x_ref, xs_ref, g_ref, wg_ref, wu_ref, wd_ref, fg_ref, o_ref, os_ref, h_scr, act_scr = refs
    else:
        x_ref, xs_ref, g_ref, wg_ref, wu_ref, wd_ref, o_ref, os_ref, h_scr, act_scr = refs
    tm, ns = x_ref.shape[0], xs_ref.shape[0]
    last = pl.num_programs(0) - 1

    def finish(x, acc):
        y = x + 0.5 * acc
        return _rms(y, fg_ref[...]) if final else y

    x = x_ref[...]
    h_scr[0:tm, :] = _rms(x, g_ref[...]).astype(h_scr.dtype)

    @pl.when(pl.program_id(0) != last)
    def _():
        o_ref[...] = finish(x, _swiglu_rows(h_scr, act_scr, tm, wg_ref, wu_ref, wd_ref))

    @pl.when(pl.program_id(0) == last)
    def _():
        xs = xs_ref[...]
        h_scr[tm:tm + ns, :] = _rms(xs, g_ref[...]).astype(h_scr.dtype)
        acc = _swiglu_rows(h_scr, act_scr, tm + ns, wg_ref, wu_ref, wd_ref)
        o_ref[...] = finish(x, acc[0:tm])
        os_ref[...] = finish(xs, acc[tm:tm + ns])


def _ffn(x, xs, g, weights, layer, which, final_g=None, *, tm):
    m, ns = x.shape[0], xs.shape[0]
    final = final_g is not None
    wg, wu, wd = weights
    row = pl.BlockSpec((tm, D_MODEL), lambda i: (i, 0))
    wspec = lambda w: pl.BlockSpec((1, 1) + w.shape[2:], lambda i: (layer, which, 0, 0), pipeline_mode=pl.Buffered(1))
    in_specs = [row, _resident(xs.shape), _resident((1, D_MODEL)), wspec(wg), wspec(wu), wspec(wd)]
    args = [x, xs, g.reshape(1, D_MODEL), wg, wu, wd]
    if final:
        in_specs.append(_resident((1, D_MODEL)))
        args.append(final_g.reshape(1, D_MODEL))
    return pl.pallas_call(
        functools.partial(_ffn_body, final=final),
        grid=(m // tm,),
        in_specs=in_specs,
        out_specs=[row, pl.BlockSpec(xs.shape, lambda i: (0, 0))],
        out_shape=[jax.ShapeDtypeStruct((m, D_MODEL), F32), jax.ShapeDtypeStruct(xs.shape, F32)],
        scratch_shapes=[pltpu.VMEM((tm + ns, D_MODEL), wg.dtype), pltpu.VMEM((tm + ns, D_FF), wg.dtype)],
        compiler_params=_params(("arbitrary",)),
        name="ffn_final" if final else "ffn",
    )(*args)


def _cummax_lanes(x):
    lane = lax.broadcasted_iota(jnp.int32, x.shape, 1)
    d = 1
    while d < x.shape[1]:
        x = jnp.maximum(x, jnp.where(lane >= d, pltpu.roll(x, d, axis=1), NEG_INF))
        d *= 2
    return x


def _exact_tri_dot(tri_bf16, x):
    x1 = x.astype(BF16)
    r1 = x - x1.astype(F32)
    x2 = r1.astype(BF16)
    x3 = (r1 - x2.astype(F32)).astype(BF16)
    n = x.shape[1]
    r = _dot(tri_bf16, jnp.concatenate([x1, x2, x3], axis=1))
    return r[:, 0:n] + r[:, n:2 * n] + r[:, 2 * n:3 * n]


def _block_diag(a, b):
    zero = jnp.zeros_like(a)
    return jnp.concatenate([jnp.concatenate([a, zero], axis=1), jnp.concatenate([zero, b], axis=1)], axis=0)


def _mixer_ab_chunk(xn_ref, xp_ref, g_ref, win_ref, bif_ref, sg_ref, sw_ref, sbt_ref, wout_ref,
                   y_ref, c_ref, n_ref, m_ref, z_cur, z_nxt, cat_scr):
    nb = xn_ref.shape[0]
    row = lax.broadcasted_iota(jnp.int32, (CHUNK, CHUNK), 0)
    col = lax.broadcasted_iota(jnp.int32, (CHUNK, CHUNK), 1)
    causal = col <= row
    keys_before = row <= col
    tri = jnp.where(causal, 1.0, 0.0).astype(BF16)
    scale = HEAD_DIM ** -0.5

    hn = _rms(xn_ref[...].reshape(nb * CHUNK, D_MODEL), g_ref[...]).astype(BF16)
    slabs = [(c0, min(c0 + PROJ_SLAB, A_IN_PAD)) for c0 in range(0, A_IN_PAD, PROJ_SLAB)]

    def project(count):
        for _ in range(min(count, len(slabs))):
            c0, c1 = slabs.pop(0)
            z_nxt[:, c0:c1] = _dot(hn, win_ref[:, c0:c1])

    if z_cur is None:
        project(len(slabs))
        return

    sgu_bias = [jnp.broadcast_to(sbt_ref[:, g:g + 1], (CHUNK, CHUNK)) for g in range(HEADS)]
    vn = [_rms(z_cur[b * CHUNK:(b + 1) * CHUNK, COL_GV:COL_GV + WIDTH], sg_ref[...]) for b in range(nb)]
    for g in range(HEADS):
        gs = slice(g * CHUNK, (g + 1) * CHUNK)
        mixed = _dot(jnp.where(causal, sw_ref[g], 0.0).astype(BF16),
                     jnp.concatenate([vn[b][:, gs] for b in range(nb)], axis=1).astype(BF16))
        for b in range(nb):
            rows = slice(b * CHUNK, (b + 1) * CHUNK)
            u = z_cur[rows, COL_U + g * CHUNK:COL_U + (g + 1) * CHUNK]
            cat_scr[rows, WIDTH + g * CHUNK:WIDTH + (g + 1) * CHUNK] = (
                u * (mixed[:, b * CHUNK:(b + 1) * CHUNK] + sgu_bias[g])).astype(BF16)
    project(1)

    pieces_per_stage = -(-(len(slabs)) // (3 * nb))
    gate_terms = []
    for b in range(nb):
        rows = slice(b * CHUNK, (b + 1) * CHUNK)
        gates = z_cur[rows, COL_GATES:COL_GATES + LANES] + bif_ref[...]
        lg = jnp.where(col < HEADS, gates, _log_sigmoid(gates))
        gate_terms.append((lg, _exact_tri_dot(tri, lg)))
        project(pieces_per_stage)

    heads = []
    for b, (lg, fcum) in enumerate(gate_terms):
        rows = slice(b * CHUNK, (b + 1) * CHUNK)
        lg_t = lg.T
        fcum_t = fcum.T
        gmax = _cummax_lanes(lg_t[0:2 * HEADS, :] - jnp.concatenate([fcum_t[HEADS:2 * HEADS, :]] * 2, axis=0))
        for h0 in range(0, HEADS, 2):
            pair = []
            for h in (h0, h0 + 1):
                q = z_cur[rows, h * HEAD_DIM:(h + 1) * HEAD_DIM]
                k = z_cur[rows, WIDTH + h * HEAD_DIM:WIDTH + (h + 1) * HEAD_DIM] * scale
                v_t = z_cur[rows, 2 * WIDTH + h * HEAD_DIM:2 * WIDTH + (h + 1) * HEAD_DIM].T
                f_r = fcum_t[HEADS + h:HEADS + h + 1, :]
                i_r = lg_t[h:h + 1, :]
                g_c = lg[:, h:h + 1] - fcum[:, HEADS + h:HEADS + h + 1]
                c0, n0, m0 = c_ref[b, h], n_ref[b, h], m_ref[b, h]
                m = f_r + jnp.maximum(m0, gmax[h:h + 1, :])
                inter = jnp.exp(m0 + f_r - m)
                m_last = m[:, CHUNK - 1:CHUNK]
                f_last = f_r[:, CHUNK - 1:CHUNK]
                w = jnp.exp(f_last - f_r + i_r - m_last)
                decay = jnp.exp(m0[:, 0:1] + f_last - m_last)
                m_ref[b, h] = jnp.broadcast_to(m_last, (1, LANES))
                pair.append(dict(q=q, k=k, v_t=v_t, f_r=f_r, g_c=g_c, c0=c0, n0=n0, m=m, inter=inter, w=w,
                                 decay=decay))
            against_q = _dot_nt(
                jnp.concatenate([jnp.concatenate([d["k"], d["c0"], jnp.broadcast_to(d["n0"], (BF16_ROWS, HEAD_DIM))],
                                                 axis=0) for d in pair], axis=1).astype(BF16),
                _block_diag(pair[0]["q"], pair[1]["q"]).astype(BF16))
            against_k = _dot(
                jnp.concatenate([jnp.concatenate([d["v_t"] * d["w"], jnp.broadcast_to(d["w"], (BF16_ROWS, CHUNK))],
                                                 axis=0) for d in pair], axis=1).astype(BF16),
                _block_diag(pair[0]["k"], pair[1]["k"]).astype(BF16))
            for i, d in enumerate(pair):
                h = h0 + i
                mine = slice(i * HEAD_DIM, (i + 1) * HEAD_DIM)
                c_ref[b, h] = d["decay"] * d["c0"] + against_k[0:HEAD_DIM, mine]
                n_ref[b, h] = d["decay"] * d["n0"] + against_k[HEAD_DIM:HEAD_DIM + 1, mine]
            heads.append((against_q, pair))
        project(pieces_per_stage)

    partial = []
    for i, (against_q, pair) in enumerate(heads):
        s_ts, dens = [], []
        for j, d in enumerate(pair):
            mine = slice(j * CHUNK, (j + 1) * CHUNK)
            kq = against_q[0:CHUNK, mine]
            nq = against_q[CHUNK + HEAD_DIM:CHUNK + HEAD_DIM + 1, mine]
            s_t = kq * jnp.exp(jnp.where(keys_before, (d["f_r"] - d["m"]) + d["g_c"], NEG_INF))
            s_ts.append(s_t)
            dens.append(jnp.maximum(jnp.abs(jnp.sum(s_t, axis=0, keepdims=True) + d["inter"] * nq),
                                    jnp.exp(-d["m"])))
        sv = _dot(jnp.concatenate([d["v_t"] for d in pair], axis=1).astype(BF16),
                  _block_diag(s_ts[0], s_ts[1]).astype(BF16))
        for j, d in enumerate(pair):
            mine = slice(j * CHUNK, (j + 1) * CHUNK)
            partial.append((sv[:, mine], d["inter"] * against_q[CHUNK:CHUNK + HEAD_DIM, mine], dens[j]))
        if i % (HEADS // 2) == HEADS // 2 - 1:
            project(pieces_per_stage)

    for i, (sv, carried, den) in enumerate(partial):
        b, h = divmod(i, HEADS)
        rows = slice(b * CHUNK, (b + 1) * CHUNK)
        o = z_cur[rows, 3 * WIDTH + h * HEAD_DIM:3 * WIDTH + (h + 1) * HEAD_DIM]
        cat_scr[rows, h * HEAD_DIM:(h + 1) * HEAD_DIM] = (jax.nn.sigmoid(o) * ((sv + carried) / den).T).astype(BF16)

    project(len(slabs))
    y = xp_ref[...].reshape(nb * CHUNK, D_MODEL) + _dot(cat_scr[...], wout_ref[...])
    y_ref[...] = y.reshape(nb, CHUNK, D_MODEL)


def _mixer_ab_body(*refs):
    *io_refs, z0_scr, z1_scr, cat_scr = refs
    c_ref, n_ref, m_ref = io_refs[-3:]
    step = pl.program_id(0)

    @pl.when(step == 0)
    def _():
        c_ref[...] = jnp.zeros_like(c_ref)
        n_ref[...] = jnp.zeros_like(n_ref)
        m_ref[...] = jnp.zeros_like(m_ref)
        _mixer_ab_chunk(*io_refs, None, z0_scr, cat_scr)

    @pl.when((step > 0) & (step % 2 == 0))
    def _():
        _mixer_ab_chunk(*io_refs, z1_scr, z0_scr, cat_scr)

    @pl.when(step % 2 == 1)
    def _():
        _mixer_ab_chunk(*io_refs, z0_scr, z1_scr, cat_scr)


def _mixer_ab_prompt(x, g, w_in, b_if, sgu_g, sgu_w, sgu_bt, w_out):
    nb, s, _ = x.shape
    n_chunks = s // CHUNK
    blk = lambda index: pl.BlockSpec((nb, CHUNK, D_MODEL), index)
    nxt = blk(lambda c: (0, jnp.minimum(c, n_chunks - 1), 0))
    prev = blk(lambda c: (0, jnp.maximum(c - 1, 0), 0))
    z_shape = pltpu.VMEM((nb * CHUNK, A_IN_PAD), F32)
    return pl.pallas_call(
        _mixer_ab_body,
        grid=(n_chunks + 1,),
        in_specs=[nxt, prev, _resident((1, D_MODEL)), _resident(w_in.shape), _resident((1, LANES)),
                  _resident((1, WIDTH)), _resident(sgu_w.shape), _resident(sgu_bt.shape), _resident(w_out.shape)],
        out_specs=[prev,
                   pl.BlockSpec((nb, HEADS, HEAD_DIM, HEAD_DIM), lambda c: (0, 0, 0, 0)),
                   pl.BlockSpec((nb, HEADS, 1, HEAD_DIM), lambda c: (0, 0, 0, 0)),
                   pl.BlockSpec((nb, HEADS, 1, LANES), lambda c: (0, 0, 0, 0))],
        out_shape=[jax.ShapeDtypeStruct(x.shape, F32),
                   jax.ShapeDtypeStruct((nb, HEADS, HEAD_DIM, HEAD_DIM), F32),
                   jax.ShapeDtypeStruct((nb, HEADS, 1, HEAD_DIM), F32),
                   jax.ShapeDtypeStruct((nb, HEADS, 1, LANES), F32)],
        scratch_shapes=[z_shape, z_shape, pltpu.VMEM((nb * CHUNK, 2 * WIDTH), BF16)],
        compiler_params=_params(("arbitrary",)),
        name="mixer_ab_prompt",
    )(x, x, g.reshape(1, D_MODEL), w_in, b_if, sgu_g.reshape(1, WIDTH), sgu_w, sgu_bt, w_out)


def _alibi_slope(group, head):
    n = len(SWA_GROUPS) * HEADS
    return 2.0 ** (-8.0 * (group * HEADS + head + 1) / n)


def _slab_pitch(dil):
    return dil + 8 if dil % 8 == 0 else dil


def _swa_group_body(*refs, group, dil, tile, others):
    x_ref, g_ref, w_ref = refs[0:3]
    if others:
        other_o = refs[3:3 + others]
        other_st = refs[3 + others:3 + 2 * others]
        wout_ref, y_ref, slab_scr, q_scr, k_scr, v_scr, o_dst, st_dst, cat_scr = refs[3 + 2 * others:]
    else:
        o_ref, st_ref, slab_scr, q_scr, k_scr, v_scr = refs[3:]
        o_dst, st_dst = o_ref.at[0], st_ref.at[0]
    step = pl.program_id(1)
    sub = tile // dil
    if sub >= PROJ_ROWS:
        pieces = [[(r, f, PROJ_ROWS)] for r in range(dil) for f in range(0, sub, PROJ_ROWS)]
    else:
        per = PROJ_ROWS // sub
        pieces = [[(r, 0, sub) for r in range(p * per, (p + 1) * per)] for p in range(dil // per)]
    scale = HEAD_DIM ** -0.5
    qi = lax.broadcasted_iota(jnp.int32, (CHUNK, 2 * CHUNK), 0)
    kc = lax.broadcasted_iota(jnp.int32, (CHUNK, 2 * CHUNK), 1)
    delta = CHUNK + qi - kc
    valid = (delta >= 0) & (delta <= CHUNK)
    valid_first = valid & (kc >= jnp.where(step > 0, 0, CHUNK))
    dist = (delta * dil).astype(F32)
    lane = lax.broadcasted_iota(jnp.int32, (CHUNK, LANES), 1)

    @pl.when(step == 0)
    def _():
        k_scr[:, 0:CHUNK, :] = jnp.zeros((dil, CHUNK, WIDTH), BF16)
        v_scr[:, 0:CHUNK, :] = jnp.zeros((dil, CHUNK, WIDTH), BF16)

    pitch = _slab_pitch(dil)
    for c in range(tile // ROW_PASS):
        hn = _rms(x_ref[0, c * ROW_PASS:(c + 1) * ROW_PASS, :], g_ref[...])
        groups = [(0, ROW_PASS)] if pitch == dil else [(g * dil, dil) for g in range(ROW_PASS // dil)]
        for first, count in groups:
            dst = (c * ROW_PASS + first) // dil * pitch
            for sl in range(D_MODEL // LANES):
                slab_scr[sl, dst:dst + count, :] = hn[first:first + count, sl * LANES:(sl + 1) * LANES]

    def token_rows(r, first, count):
        start = first * dil + r
        return slice(start, start + count) if dil == 1 else pl.ds(start, count, stride=dil)

    def project(piece):
        parts = []
        for r, first, count in piece:
            src = slice(first, first + count) if dil == 1 else pl.ds(first * pitch + r, count, stride=pitch)
            parts.append(jnp.concatenate([slab_scr[sl, src, :] for sl in range(D_MODEL // LANES)],
                                         axis=1).astype(BF16))
        hn = parts[0] if len(parts) == 1 else jnp.concatenate(parts, axis=0)
        for j, scr in enumerate((q_scr, k_scr, v_scr)):
            z = _dot(hn, w_ref[:, j * WIDTH:(j + 1) * WIDTH]).astype(BF16)
            at = 0
            for r, first, count in piece:
                off = first if j == 0 else CHUNK + first
                scr[r, off:off + count, :] = z[at:at + count, :]
                at += count

    def scores(piece):
        out = []
        for r, first, count in piece:
            for j in range(first // CHUNK, (first + count) // CHUNK):
                mask = valid_first if j == 0 else valid
                for h in range(HEADS):
                    hs = slice(h * HEAD_DIM, (h + 1) * HEAD_DIM)
                    s = _dot_nt(q_scr[r, j * CHUNK:(j + 1) * CHUNK, hs], k_scr[r, j * CHUNK:(j + 2) * CHUNK, hs])
                    s = jnp.where(mask, s * scale + (-_alibi_slope(group, h)) * dist, NEG_INF)
                    m = jnp.max(s, axis=-1, keepdims=True)
                    p_ = jnp.exp(s - m)
                    out.append((r, j, h, m, jnp.sum(p_, axis=-1, keepdims=True), p_.astype(BF16)))
        return out

    def values(items):
        stats = None
        for r, j, h, m, l, p_ in items:
            hs = slice(h * HEAD_DIM, (h + 1) * HEAD_DIM)
            o_dst[h, token_rows(r, j * CHUNK, CHUNK), :] = _dot(p_, v_scr[r, j * CHUNK:(j + 2) * CHUNK, hs])
            base = jnp.zeros((CHUNK, LANES), F32) if h == 0 else stats
            stats = jnp.where(lane == h, m, jnp.where(lane == HEADS + h, l, base))
            if h == HEADS - 1:
                st_dst[token_rows(r, j * CHUNK, CHUNK), :] = stats

    def halves(piece):
        if len(piece) == 1:
            r, first, count = piece[0]
            return [[(r, first, count // 2)], [(r, first + count // 2, count // 2)]]
        return [piece[:len(piece) // 2], piece[len(piece) // 2:]]

    def merge(piece):
        (_, first, count), = piece
        for c in range(first // ROW_PASS, (first + count) // ROW_PASS):
            rows = slice(c * ROW_PASS, (c + 1) * ROW_PASS)
            stats = [st_dst[rows, :]] + [s_ref[0, rows, :] for s_ref in other_st]
            top = functools.reduce(jnp.maximum, stats)
            ws = [jnp.exp(st - top) for st in stats]
            den = sum(w * pltpu.roll(st, LANES - HEADS, axis=1) for w, st in zip(ws, stats))
            den = jnp.where(lane[0:1, :] < HEADS, den, 1.0)
            shares = [w / den for w in ws]
            for h in range(HEADS):
                outs = [o_dst[h, rows, :]] + [o_ref[0, h, rows, :] for o_ref in other_o]
                merged = sum(a[:, h:h + 1] * o for a, o in zip(shares, outs))
                cat_scr[rows, h * HEAD_DIM:(h + 1) * HEAD_DIM] = merged.astype(BF16)
        rows = slice(first, first + count)
        y_ref[0, rows, :] = x_ref[0, rows, :] + _dot(cat_scr[rows, :], wout_ref[...])

    def finish(done):
        items, p, i = done
        values(items)
        if others and i == 1:
            merge(pieces[p])

    project(pieces[0])
    pending = None
    for p, piece in enumerate(pieces):
        for i, half in enumerate(halves(piece)):
            items = scores(half)
            if i == 0 and p + 1 < len(pieces):
                project(pieces[p + 1])
            if pending is not None:
                finish(pending)
            pending = (items, p, i)
    finish(pending)
    for r in range(dil):
        k_scr[r, 0:CHUNK, :] = k_scr[r, sub:sub + CHUNK, :]
        v_scr[r, 0:CHUNK, :] = v_scr[r, sub:sub + CHUNK, :]


def _swa_group_prompt(x, g, c_in, *, group, tile, merge_with=None):
    nb, s, _ = x.shape
    dil = SWA_GROUPS[group][1]
    sub = tile // dil
    o_spec = pl.BlockSpec((1, HEADS, tile, HEAD_DIM), lambda b, t: (b, 0, t, 0))
    s_spec = pl.BlockSpec((1, tile, LANES), lambda b, t: (b, t, 0))
    x_spec = pl.BlockSpec((1, tile, D_MODEL), lambda b, t: (b, t, 0))
    in_specs = [x_spec, _resident((1, D_MODEL)),
                pl.BlockSpec((D_MODEL, C_GROUP_COLS), lambda b, t: (0, group), pipeline_mode=pl.Buffered(1))]
    args = [x, g.reshape(1, D_MODEL), c_in]
    scratch = [pltpu.VMEM((D_MODEL // LANES, sub * _slab_pitch(dil), LANES), F32),
               pltpu.VMEM((dil, sub, WIDTH), BF16),
               pltpu.VMEM((dil, CHUNK + sub, WIDTH), BF16),
               pltpu.VMEM((dil, CHUNK + sub, WIDTH), BF16)]
    if merge_with is None:
        others = 0
        out_specs = [o_spec, s_spec]
        out_shape = [jax.ShapeDtypeStruct((nb, HEADS, s, HEAD_DIM), F32), jax.ShapeDtypeStruct((nb, s, LANES), F32)]
    else:
        outs, stats, w_out = merge_with
        others = len(outs)
        in_specs += [o_spec] * others + [s_spec] * others + [_resident(w_out.shape)]
        args += [*outs, *stats, w_out]
        out_specs = x_spec
        out_shape = jax.ShapeDtypeStruct(x.shape, F32)
        scratch += [pltpu.VMEM((HEADS, tile, HEAD_DIM), F32), pltpu.VMEM((tile, LANES), F32),
                    pltpu.VMEM((tile, WIDTH), BF16)]
    return pl.pallas_call(
        functools.partial(_swa_group_body, group=group, dil=dil, tile=tile, others=others),
        grid=(nb, s // tile),
        in_specs=in_specs,
        out_specs=out_specs,
        out_shape=out_shape,
        scratch_shapes=scratch,
        compiler_params=_params(("arbitrary", "arbitrary")),
        name="swa_group%d" % group,
    )(*args)


def _kv_tail_body(x_ref, g_ref, wk_ref, wv_ref, kv_ref):
    tm = x_ref.shape[1]
    hn = _rms(x_ref[0], g_ref[...]).astype(BF16)
    for j, w_ref in enumerate((wk_ref, wv_ref)):
        kv = _dot(hn, w_ref[...])
        for h in range(HEADS):
            kv_ref[pl.ds(j * HEADS + h, tm, stride=2 * HEADS), :] = kv[:, h * HEAD_DIM:(h + 1) * HEAD_DIM]


def _kv_tail(x, g, c_in, *, group):
    win = SWA_GROUPS[group][0]
    nb, s, _ = x.shape
    tm = min(win, 1024)
    first = (s - win) // tm
    steps = win // tm
    wcol = lambda j: pl.BlockSpec((D_MODEL, WIDTH), lambda b, t: (0, 3 * group + j), pipeline_mode=pl.Buffered(1))
    kv = pl.pallas_call(
        _kv_tail_body,
        grid=(nb, steps),
        in_specs=[pl.BlockSpec((1, tm, D_MODEL), lambda b, t: (b, first + t, 0)), _resident((1, D_MODEL)),
                  wcol(1), wcol(2)],
        out_specs=pl.BlockSpec((tm * 2 * HEADS, HEAD_DIM), lambda b, t: (b * steps + t, 0)),
        out_shape=jax.ShapeDtypeStruct((nb * win * 2 * HEADS, HEAD_DIM), F32),
        compiler_params=_params(("arbitrary", "arbitrary")),
        name="kv_tail%d" % group,
    )(x, g.reshape(1, D_MODEL), c_in, c_in)
    return kv.reshape(1, nb, win, 2, HEADS, HEAD_DIM)


def _proj_body(x_ref, g_ref, w_ref, z_ref):
    z_ref[...] = _dot(_rms(x_ref[...], g_ref[...]).astype(BF16), w_ref[...])


def _proj(x, g, w):
    m, n = x.shape[0], w.shape[1]
    return pl.pallas_call(
        _proj_body,
        grid=(1,),
        in_specs=[_resident(x.shape), _resident((1, D_MODEL)), _resident(w.shape)],
        out_specs=pl.BlockSpec((m, n), lambda i: (0, 0)),
        out_shape=jax.ShapeDtypeStruct((m, n), F32),
        compiler_params=_params(("arbitrary",)),
        name="proj_rows",
    )(x, g.reshape(1, D_MODEL), w)


def _out_proj_body(c_ref, w_ref, x_ref, y_ref):
    y_ref[...] = x_ref[...] + _dot(c_ref[...].astype(BF16), w_ref[...])


def _out_proj(cat, w, x):
    return pl.pallas_call(
        _out_proj_body,
        grid=(1,),
        in_specs=[_resident(cat.shape), _resident(w.shape), _resident(x.shape)],
        out_specs=pl.BlockSpec(x.shape, lambda i: (0, 0)),
        out_shape=jax.ShapeDtypeStruct(x.shape, F32),
        compiler_params=_params(("arbitrary",)),
        name="out_proj_rows",
    )(cat, w, x)


def _mixer_ab_step_body(z_ref, bif_ref, sg_ref, w00_ref, b0_ref, c_ref, n_ref, m_ref,
                        cat_ref, c1_ref, n1_ref, m1_ref, vn_ref):
    nb = z_ref.shape[0]
    scale = HEAD_DIM ** -0.5
    eye = (lax.broadcasted_iota(jnp.int32, (HEAD_DIM, HEAD_DIM), 0)
           == lax.broadcasted_iota(jnp.int32, (HEAD_DIM, HEAD_DIM), 1)).astype(F32)
    lane = lax.broadcasted_iota(jnp.int32, (1, LANES), 1)
    for i in range(nb):
        zr = z_ref[i:i + 1, :]
        gates = zr[:, COL_GATES:COL_GATES + LANES] + bif_ref[...]
        m_new = jnp.zeros((1, LANES), F32)
        for h in range(HEADS):
            hs = slice(h * HEAD_DIM, (h + 1) * HEAD_DIM)
            q = zr[:, hs]
            k = zr[:, WIDTH + h * HEAD_DIM:WIDTH + (h + 1) * HEAD_DIM] * scale
            v = zr[:, 2 * WIDTH + h * HEAD_DIM:2 * WIDTH + (h + 1) * HEAD_DIM]
            o = zr[:, 3 * WIDTH + h * HEAD_DIM:3 * WIDTH + (h + 1) * HEAD_DIM]
            ig = gates[:, h:h + 1]
            lf = _log_sigmoid(gates[:, HEADS + h:HEADS + h + 1])
            c0 = c_ref[i, h]
            n0 = n_ref[i, h]
            m0 = m_ref[i:i + 1, h:h + 1]
            a = m0 + lf
            m = jnp.maximum(a, ig)
            s = jnp.sum(q * k, axis=-1, keepdims=True) * jnp.exp(ig - m)
            inter = jnp.exp(a - m)
            cq_col = jnp.sum(c0 * q, axis=-1, keepdims=True)
            cq = jnp.sum(eye * cq_col, axis=0, keepdims=True)
            v_col = jnp.sum(eye * v, axis=-1, keepdims=True)
            num = s * v + inter * cq
            den = s + inter * jnp.sum(n0 * q, axis=-1, keepdims=True)
            hh = num / jnp.maximum(jnp.abs(den), jnp.exp(-m))
            w = jnp.exp(ig - m)
            c1_ref[i, h] = inter * c0 + (w * v_col) * k
            n1_ref[i, h] = inter * n0 + w * k
            m_new = jnp.where(lane == h, m, m_new)
            cat_ref[i:i + 1, hs] = jax.nn.sigmoid(o) * hh
        m1_ref[i:i + 1, :] = m_new
        vn = _rms(zr[:, COL_GV:COL_GV + WIDTH], sg_ref[...])
        vn_ref[i:i + 1, :] = vn
        cat_ref[i:i + 1, WIDTH:2 * WIDTH] = zr[:, COL_U:COL_U + WIDTH] * (w00_ref[...] * vn + b0_ref[...])


def _mixer_ab_step(z, b_if, sgu_g, w00, b0, st_c, st_n, st_m, *, nb):
    n = z.shape[0]
    rows = lambda w: pl.BlockSpec((nb, w), lambda i: (i, 0))
    c_spec = pl.BlockSpec((nb, HEADS, HEAD_DIM, HEAD_DIM), lambda i: (i, 0, 0, 0))
    n_spec = pl.BlockSpec((nb, HEADS, 1, HEAD_DIM), lambda i: (i, 0, 0, 0))
    return pl.pallas_call(
        _mixer_ab_step_body,
        grid=(n // nb,),
        in_specs=[rows(A_IN_PAD), _resident((1, LANES)), _resident((1, WIDTH)), _resident((1, WIDTH)),
                  _resident((1, WIDTH)), c_spec, n_spec, rows(LANES)],
        out_specs=[rows(2 * WIDTH), c_spec, n_spec, rows(LANES), rows(WIDTH)],
        out_shape=[jax.ShapeDtypeStruct((n, 2 * WIDTH), F32),
                   jax.ShapeDtypeStruct(st_c.shape, F32),
                   jax.ShapeDtypeStruct(st_n.shape, F32),
                   jax.ShapeDtypeStruct((n, LANES), F32),
                   jax.ShapeDtypeStruct((n, WIDTH), F32)],
        compiler_params=_params(("arbitrary",)),
        name="mixer_ab_step",
    )(z, b_if, sgu_g.reshape(1, WIDTH), w00, b0, st_c, st_n, st_m)


def _swa_step_body(z_ref, kv0_ref, kv1_ref, kv2_ref, cat_ref):
    nb = z_ref.shape[0]
    scale = HEAD_DIM ** -0.5
    steps = (CHUNK - lax.broadcasted_iota(jnp.int32, (CHUNK, 1, 1), 0)).astype(F32)
    head = lax.broadcasted_iota(jnp.int32, (1, HEADS, 1), 1)
    for i in range(nb):
        ms, ls, os_ = [], [], []
        for gi, kv_ref in enumerate((kv0_ref, kv1_ref, kv2_ref)):
            dil = SWA_GROUPS[gi][1]
            base = gi * 3 * HEADS
            q = z_ref[i, base:base + HEADS, :]
            k_new = z_ref[i, base + HEADS:base + 2 * HEADS, :]
            v_new = z_ref[i, base + 2 * HEADS:base + 3 * HEADS, :]
            kc = kv_ref[i, :, 0, 0, :, :]
            vc = kv_ref[i, :, 0, 1, :, :]
            slope = jnp.zeros((1, HEADS, 1), F32)
            for h in range(HEADS):
                slope = jnp.where(head == h, _alibi_slope(gi, h) * dil, slope)
            s = jnp.sum(kc * q[None], axis=-1, keepdims=True) * scale - slope * steps
            s_new = jnp.sum(k_new * q, axis=-1, keepdims=True) * scale
            m = jnp.maximum(jnp.max(s, axis=0), s_new)
            p = jnp.exp(s - m[None])
            p_new = jnp.exp(s_new - m)
            ms.append(m)
            ls.append(jnp.sum(p, axis=0) + p_new)
            os_.append(jnp.sum(p * vc, axis=0) + p_new * v_new)
        top = functools.reduce(jnp.maximum, ms)
        ws = [jnp.exp(m - top) for m in ms]
        num = sum(w * o for w, o in zip(ws, os_))
        den = sum(w * l for w, l in zip(ws, ls))
        cat_ref[i] = num / den


def _swa_step(z, caches, *, nb):
    n = z.shape[0]
    views = []
    specs = []
    for (win, dil), cache in zip(SWA_GROUPS, caches):
        views.append(cache.reshape(n, win // dil, dil, 2, HEADS, HEAD_DIM))
        specs.append(pl.BlockSpec((nb, CHUNK, 1, 2, HEADS, HEAD_DIM), lambda i: (i, 0, 0, 0, 0, 0)))
    return pl.pallas_call(
        _swa_step_body,
        grid=(n // nb,),
        in_specs=[pl.BlockSpec((nb,) + z.shape[1:], lambda i: (i, 0, 0))] + specs,
        out_specs=pl.BlockSpec((nb, HEADS, HEAD_DIM), lambda i: (i, 0, 0)),
        out_shape=jax.ShapeDtypeStruct((n, HEADS, HEAD_DIM), F32),
        compiler_params=_params(("arbitrary",)),
        name="swa_step",
    )(z, *views)


def _pad_cols(w, n):
    return jnp.pad(w, ((0, 0), (0, n - w.shape[1])))


def kernel(x_prompt, x_sample, state_mlstm_C, state_mlstm_n, state_mlstm_m, cache_swa_kv0, cache_swa_kv1, cache_swa_kv2, norm_g, ffn_w_gate, ffn_w_up, ffn_w_down, a_w_in, a_b_if, sgu_norm_g, sgu_w, sgu_b, a_w_out, c_w_in, c_w_out, final_norm_g):
    nb, s, _ = x_prompt.shape
    ns = x_sample.shape[0]
    assert x_sample.shape[1] == 1 and s % max(w for w, _ in SWA_GROUPS) == 0
    for (win, dil), cache in zip(SWA_GROUPS, (cache_swa_kv0, cache_swa_kv1, cache_swa_kv2)):
        assert cache.shape[2] == win and win // dil == CHUNK

    ffn_w =(jnp.swapaxes(ffn_w_gate, 2, 3), jnp.swapaxes(ffn_w_up, 2, 3), ffn_w_down)
    g_lo, g_hi = 4 * WIDTH, 4 * WIDTH + 2 * HEADS
    a_in = jnp.concatenate([a_w_in[0][:, :g_lo], a_w_in[0][:, g_hi:], _pad_cols(a_w_in[0][:, g_lo:g_hi], LANES)],
                           axis=1).astype(BF16)
    b_if = _pad_cols(a_b_if[0].reshape(1, 2 * HEADS), LANES)
    a_out = a_w_out[0].astype(BF16)
    c_in = c_w_in[0].astype(BF16)
    c_out = c_w_out[0].astype(BF16)
    sgu_bt = _pad_cols(sgu_b[0].T, LANES)
    sgu_w00 = jnp.repeat(sgu_w[0, :, 0, 0], CHUNK).reshape(1, WIDTH)
    sgu_b0 = jnp.repeat(sgu_b[0, :, 0], CHUNK).reshape(1, WIDTH)

    xp = x_prompt.reshape(nb * s, D_MODEL)
    xs = x_sample.reshape(ns, D_MODEL)
    xp, xs = _ffn(xp, xs, norm_g[0, 0], ffn_w, 0, 0, tm=512)
    xp, p_c, p_n, p_m = _mixer_ab_prompt(xp.reshape(nb, s, D_MODEL), norm_g[0, 1], a_in, b_if,
                                          sgu_norm_g[0], sgu_w[0], sgu_bt, a_out)
    z = _proj(xs, norm_g[0, 1], a_in)
    cat, s_c, s_n, s_m, s_v = _mixer_ab_step(
        z, b_if, sgu_norm_g[0], sgu_w00, sgu_b0, state_mlstm_C[0],
        state_mlstm_n[0].reshape(ns, HEADS, 1, HEAD_DIM), _pad_cols(state_mlstm_m[0], LANES), nb=8)
    xs = _out_proj(cat, a_out, xs)
    xp, xs = _ffn(xp.reshape(nb * s, D_MODEL), xs, norm_g[0, 2], ffn_w, 0, 1, tm=512)
    xp, xs = _ffn(xp, xs, norm_g[1, 0], ffn_w, 1, 0, tm=512)
    xp3 = xp.reshape(nb, s, D_MODEL)
    outs, stats = [], []
    for gi in (1, 2):
        o, st = _swa_group_prompt(xp3, norm_g[1, 1], c_in, group=gi, tile=2048)
        outs.append(o)
        stats.append(st)
    p_kv = [_kv_tail(xp3, norm_g[1, 1], c_in, group=gi) for gi in range(3)]
    xp = _swa_group_prompt(xp3, norm_g[1, 1], c_in, group=0, tile=1024, merge_with=(outs, stats, c_out))
    xp = xp.reshape(nb * s, D_MODEL)
    z = _proj(xs, norm_g[1, 1], c_in)
    cat = _swa_step(z.reshape(ns, 3 * 3 * HEADS, HEAD_DIM), (cache_swa_kv0, cache_swa_kv1, cache_swa_kv2), nb=4)
    xs = _out_proj(cat.reshape(ns, WIDTH), c_out, xs)
    y_prompt, y_sample = _ffn(xp, xs, norm_g[1, 2], ffn_w, 1, 1, final_norm_g, tm=512)
    y_prompt = y_prompt.reshape(nb, s, D_MODEL)
    y_sample = y_sample.reshape(ns, 1, D_MODEL)
    s_kv = [z[:, gi * C_GROUP_COLS + WIDTH:(gi + 1) * C_GROUP_COLS].reshape(1, ns, 1, 2, HEADS, HEAD_DIM)
            for gi in range(3)]

    return (y_prompt, y_sample,
            p_c.reshape(1, nb, HEADS, HEAD_DIM, HEAD_DIM), p_n.reshape(1, nb, HEADS, HEAD_DIM),
            p_m[:, :, 0, 0].reshape(1, nb, HEADS),
            s_c.reshape(1, ns, HEADS, HEAD_DIM, HEAD_DIM), s_n.reshape(1, ns, HEADS, HEAD_DIM),
            s_m[:, :HEADS].reshape(1, ns, HEADS), s_v.reshape(1, ns, 1, WIDTH),
            p_kv[0], p_kv[1], p_kv[2], s_kv[0], s_kv[1], s_kv[2])
```

```python
import functools

import jax
import jax.numpy as jnp
from jax import lax
from jax.experimental import pallas as pl
from jax.experimental.pallas import tpu as pltpu

F32 = jnp.float32
BF16 = jnp.bfloat16

D_MODEL = 1024
D_FF = 2752
HEADS = 4
HEAD_DIM = 128
WIDTH = HEADS * HEAD_DIM
CHUNK = 128
SWA_GROUPS = ((128, 1), (512, 4), (2048, 16))
NORM_EPS = 1e-6
NEG_INF = -1e30

LANES = 128
BF16_ROWS = 16
PROJ_SLAB = 256
PROJ_ROWS = 512
FF_CHUNK = 256
ROW_PASS = 256
A_IN_PAD = 4 * WIDTH + 2 * WIDTH + LANES
COL_U = 4 * WIDTH
COL_GV = 5 * WIDTH
COL_GATES = 6 * WIDTH
C_GROUP_COLS = 3 * WIDTH

VMEM_LIMIT = 60 * 1024 * 1024


def _params(semantics):
    return pltpu.CompilerParams(dimension_semantics=semantics, vmem_limit_bytes=VMEM_LIMIT)


def _resident(shape):
    nd = len(shape)
    return pl.BlockSpec(shape, lambda *_: (0,) * nd, pipeline_mode=pl.Buffered(1))


def _rms(x, g):
    ms = jnp.mean(x * x, axis=-1, keepdims=True)
    return x * lax.rsqrt(ms + NORM_EPS) * g


def _dot(a, b):
    return jnp.dot(a, b, preferred_element_type=F32)


def _dot_nt(a, b):
    return lax.dot_general(a, b, (((1,), (1,)), ((), ())), preferred_element_type=F32)


def _log_sigmoid(x):
    return jnp.minimum(x, 0.0) - jnp.log1p(jnp.exp(-jnp.abs(x)))


def _swiglu_rows(h_scr, act_scr, rows, wg_ref, wu_ref, wd_ref):
    for c0 in range(0, D_FF, FF_CHUNK):
        cols = slice(c0, min(c0 + FF_CHUNK, D_FF))
        h = h_scr[0:rows, :]
        gate = _dot_nt(h, wg_ref[0, 0, cols, :])
        up = _dot_nt(h, wu_ref[0, 0, cols, :])
        act_scr[0:rows, cols] = (gate * jax.nn.sigmoid(gate) * up).astype(act_scr.dtype)
    whole = D_FF // FF_CHUNK * FF_CHUNK
    out = _dot(act_scr[0:rows, 0:whole], wd_ref[0, 0, 0:whole, :])
    if whole < D_FF:
        out = out + _dot(act_scr[0:rows, whole:D_FF], wd_ref[0, 0, whole:D_FF, :])
    return out


def _ffn_body(*refs, final):
    if final:
        x_ref, xs_ref, g_ref, wg_ref, wu_ref, wd_ref, fg_ref, o_ref, os_ref, h_scr, act_scr = refs
    else:
        x_ref, xs_ref, g_ref, wg_ref, wu_ref, wd_ref, o_ref, os_ref, h_scr, act_scr = refs
    tm, ns = x_ref.shape[0], xs_ref.shape[0]
    last = pl.num_programs(0) - 1

    def finish(x, acc):
        y = x + 0.5 * acc
        return _rms(y, fg_ref[...]) if final else y

    x = x_ref[...]
    h_scr[0:tm, :] = _rms(x, g_ref[...]).astype(h_scr.dtype)

    @pl.when(pl.program_id(0) != last)
    def _():
        o_ref[...] = finish(x, _swiglu_rows(h_scr, act_scr, tm, wg_ref, wu_ref, wd_ref))

    @pl.when(pl.program_id(0) == last)
    def _():
        xs = xs_ref[...]
        h_scr[tm:tm + ns, :] = _rms(xs, g_ref[...]).astype(h_scr.dtype)
        acc = _swiglu_rows(h_scr, act_scr, tm + ns, wg_ref, wu_ref, wd_ref)
        o_ref[...] = finish(x, acc[0:tm])
        os_ref[...] = finish(xs, acc[tm:tm + ns])


def _ffn(x, xs, g, weights, layer, which, final_g=None, *, tm):
    m, ns = x.shape[0], xs.shape[0]
    final = final_g is not None
    wg, wu, wd = weights
    row = pl.BlockSpec((tm, D_MODEL), lambda i: (i, 0))
    wspec = lambda w: pl.BlockSpec((1, 1) + w.shape[2:], lambda i: (layer, which, 0, 0), pipeline_mode=pl.Buffered(1))
    in_specs = [row, _resident(xs.shape), _resident((1, D_MODEL)), wspec(wg), wspec(wu), wspec(wd)]
    args = [x, xs, g.reshape(1, D_MODEL), wg, wu, wd]
    if final:
        in_specs.append(_resident((1, D_MODEL)))
        args.append(final_g.reshape(1, D_MODEL))
    return pl.pallas_call(
        functools.partial(_ffn_body, final=final),
        grid=(m // tm,),
        in_specs=in_specs,
        out_specs=[row, pl.BlockSpec(xs.shape, lambda i: (0, 0))],
        out_shape=[jax.ShapeDtypeStruct((m, D_MODEL), F32), jax.ShapeDtypeStruct(xs.shape, F32)],
        scratch_shapes=[pltpu.VMEM((tm + ns, D_MODEL), wg.dtype), pltpu.VMEM((tm + ns, D_FF), wg.dtype)],
        compiler_params=_params(("arbitrary",)),
        name="ffn_final" if final else "ffn",
    )(*args)


def _cummax_lanes(x):
    lane = lax.broadcasted_iota(jnp.int32, x.shape, 1)
    d = 1
    while d < x.shape[1]:
        x = jnp.maximum(x, jnp.where(lane >= d, pltpu.roll(x, d, axis=1), NEG_INF))
        d *= 2
    return x


def _exact_tri_dot(tri_bf16, x):
    x1 = x.astype(BF16)
    r1 = x - x1.astype(F32)
    x2 = r1.astype(BF16)
    x3 = (r1 - x2.astype(F32)).astype(BF16)
    n = x.shape[1]
    r = _dot(tri_bf16, jnp.concatenate([x1, x2, x3], axis=1))
    return r[:, 0:n] + r[:, n:2 * n] + r[:, 2 * n:3 * n]


def _block_diag(a, b):
    zero = jnp.zeros_like(a)
    return jnp.concatenate([jnp.concatenate([a, zero], axis=1), jnp.concatenate([zero, b], axis=1)], axis=0)


def _mixer_ab_chunk(xn_ref, xp_ref, g_ref, win_ref, bif_ref, sg_ref, sw_ref, sbt_ref, wout_ref,
                   y_ref, c_ref, n_ref, m_ref, z_cur, z_nxt, cat_scr):
    nb = xn_ref.shape[0]
    row = lax.broadcasted_iota(jnp.int32, (CHUNK, CHUNK), 0)
    col = lax.broadcasted_iota(jnp.int32, (CHUNK, CHUNK), 1)
    causal = col <= row
    keys_before = row <= col
    tri = jnp.where(causal, 1.0, 0.0).astype(BF16)
    scale = HEAD_DIM ** -0.5

    hn = _rms(xn_ref[...].reshape(nb * CHUNK, D_MODEL), g_ref[...]).astype(BF16)
    slabs = [(c0, min(c0 + PROJ_SLAB, A_IN_PAD)) for c0 in range(0, A_IN_PAD, PROJ_SLAB)]

    def project(count):
        for _ in range(min(count, len(slabs))):
            c0, c1 = slabs.pop(0)
            z_nxt[:, c0:c1] = _dot(hn, win_ref[:, c0:c1])

    if z_cur is None:
        project(len(slabs))
        return

    sgu_bias = [jnp.broadcast_to(sbt_ref[:, g:g + 1], (CHUNK, CHUNK)) for g in range(HEADS)]
    vn = [_rms(z_cur[b * CHUNK:(b + 1) * CHUNK, COL_GV:COL_GV + WIDTH], sg_ref[...]) for b in range(nb)]
    for g in range(HEADS):
        gs = slice(g * CHUNK, (g + 1) * CHUNK)
        mixed = _dot(jnp.where(causal, sw_ref[g], 0.0).astype(BF16),
                     jnp.concatenate([vn[b][:, gs] for b in range(nb)], axis=1).astype(BF16))
        for b in range(nb):
            rows = slice(b * CHUNK, (b + 1) * CHUNK)
            u = z_cur[rows, COL_U + g * CHUNK:COL_U + (g + 1) * CHUNK]
            cat_scr[rows, WIDTH + g * CHUNK:WIDTH + (g + 1) * CHUNK] = (
                u * (mixed[:, b * CHUNK:(b + 1) * CHUNK] + sgu_bias[g])).astype(BF16)
    project(1)

    pieces_per_stage = -(-(len(slabs)) // (3 * nb))
    gate_terms = []
    for b in range(nb):
        rows = slice(b * CHUNK, (b + 1) * CHUNK)
        gates = z_cur[rows, COL_GATES:COL_GATES + LANES] + bif_ref[...]
        lg = jnp.where(col < HEADS, gates, _log_sigmoid(gates))
        gate_terms.append((lg, _exact_tri_dot(tri, lg)))
        project(pieces_per_stage)

    heads = []
    for b, (lg, fcum) in enumerate(gate_terms):
        rows = slice(b * CHUNK, (b + 1) * CHUNK)
        lg_t = lg.T
        fcum_t = fcum.T
        gmax = _cummax_lanes(lg_t[0:2 * HEADS, :] - jnp.concatenate([fcum_t[HEADS:2 * HEADS, :]] * 2, axis=0))
        gdiff = lg - pltpu.roll(fcum, LANES - HEADS, axis=1)
        for h0 in range(0, HEADS, 2):
            pair = []
            for h in (h0, h0 + 1):
                q = z_cur[rows, h * HEAD_DIM:(h + 1) * HEAD_DIM]
                k = z_cur[rows, WIDTH + h * HEAD_DIM:WIDTH + (h + 1) * HEAD_DIM] * scale
                v_t = z_cur[rows, 2 * WIDTH + h * HEAD_DIM:2 * WIDTH + (h + 1) * HEAD_DIM].T
                f_r = fcum_t[HEADS + h:HEADS + h + 1, :]
                i_r = lg_t[h:h + 1, :]
                g_c = gdiff[:, h:h + 1]
                c0, n0, m0 = c_ref[b, h], n_ref[b, h], m_ref[b, h]
                m = f_r + jnp.maximum(m0, gmax[h:h + 1, :])
                inter = jnp.exp(m0 + f_r - m)
                m_last = m[:, CHUNK - 1:CHUNK]
                f_last = f_r[:, CHUNK - 1:CHUNK]
                w = jnp.exp(f_last - f_r + i_r - m_last)
                decay = jnp.exp(m0[:, 0:1] + f_last - m_last)
                m_ref[b, h] = jnp.broadcast_to(m_last, (1, LANES))
                pair.append(dict(q=q, k=k, v_t=v_t, f_r=f_r, g_c=g_c, c0=c0, n0=n0, m=m, inter=inter, w=w,
                                 decay=decay))
            against_q = _dot_nt(
                jnp.concatenate([jnp.concatenate([d["k"], d["c0"], jnp.broadcast_to(d["n0"], (BF16_ROWS, HEAD_DIM))],
                                                 axis=0) for d in pair], axis=1).astype(BF16),
                _block_diag(pair[0]["q"], pair[1]["q"]).astype(BF16))
            against_k = _dot(
                jnp.concatenate([jnp.concatenate([d["v_t"] * d["w"], jnp.broadcast_to(d["w"], (BF16_ROWS, CHUNK))],
                                                 axis=0) for d in pair], axis=1).astype(BF16),
                _block_diag(pair[0]["k"], pair[1]["k"]).astype(BF16))
            for i, d in enumerate(pair):
                h = h0 + i
                mine = slice(i * HEAD_DIM, (i + 1) * HEAD_DIM)
                c_ref[b, h] = d["decay"] * d["c0"] + against_k[0:HEAD_DIM, mine]
                n_ref[b, h] = d["decay"] * d["n0"] + against_k[HEAD_DIM:HEAD_DIM + 1, mine]
            heads.append((against_q, pair))
        project(pieces_per_stage)

    partial = []
    for i, (against_q, pair) in enumerate(heads):
        s_ts, dens = [], []
        for j, d in enumerate(pair):
            mine = slice(j * CHUNK, (j + 1) * CHUNK)
            kq = against_q[0:CHUNK, mine]
            nq = against_q[CHUNK + HEAD_DIM:CHUNK + HEAD_DIM + 1, mine]
            s_t = kq * jnp.exp(jnp.where(keys_before, (d["f_r"] - d["m"]) + d["g_c"], NEG_INF))
            s_ts.append(s_t)
            dens.append(jnp.maximum(jnp.abs(jnp.sum(s_t, axis=0, keepdims=True) + d["inter"] * nq),
                                    jnp.exp(-d["m"])))
        sv = _dot(jnp.concatenate([d["v_t"] for d in pair], axis=1).astype(BF16),
                  _block_diag(s_ts[0], s_ts[1]).astype(BF16))
        for j, d in enumerate(pair):
            mine = slice(j * CHUNK, (j + 1) * CHUNK)
            partial.append((sv[:, mine], d["inter"] * against_q[CHUNK:CHUNK + HEAD_DIM, mine], dens[j]))
        if i % (HEADS // 2) == HEADS // 2 - 1:
            project(pieces_per_stage)

    for i, (sv, carried, den) in enumerate(partial):
        b, h = divmod(i, HEADS)
        rows = slice(b * CHUNK, (b + 1) * CHUNK)
        o = z_cur[rows, 3 * WIDTH + h * HEAD_DIM:3 * WIDTH + (h + 1) * HEAD_DIM]
        cat_scr[rows, h * HEAD_DIM:(h + 1) * HEAD_DIM] = (jax.nn.sigmoid(o) * ((sv + carried) / den).T).astype(BF16)

    project(len(slabs))
    y = xp_ref[...].reshape(nb * CHUNK, D_MODEL) + _dot(cat_scr[...], wout_ref[...])
    y_ref[...] = y.reshape(nb, CHUNK, D_MODEL)


def _mixer_ab_body(*refs):
    *io_refs, z0_scr, z1_scr, cat_scr = refs
    c_ref, n_ref, m_ref = io_refs[-3:]
    step = pl.program_id(0)

    @pl.when(step == 0)
    def _():
        c_ref[...] = jnp.zeros_like(c_ref)
        n_ref[...] = jnp.zeros_like(n_ref)
        m_ref[...] = jnp.zeros_like(m_ref)
        _mixer_ab_chunk(*io_refs, None, z0_scr, cat_scr)

    @pl.when((step > 0) & (step % 2 == 0))
    def _():
        _mixer_ab_chunk(*io_refs, z1_scr, z0_scr, cat_scr)

    @pl.when(step % 2 == 1)
    def _():
        _mixer_ab_chunk(*io_refs, z0_scr, z1_scr, cat_scr)


def _mixer_ab_prompt(x, g, w_in, b_if, sgu_g, sgu_w, sgu_bt, w_out):
    nb, s, _ = x.shape
    n_chunks = s // CHUNK
    blk = lambda index: pl.BlockSpec((nb, CHUNK, D_MODEL), index)
    nxt = blk(lambda c: (0, jnp.minimum(c, n_chunks - 1), 0))
    prev = blk(lambda c: (0, jnp.maximum(c - 1, 0), 0))
    z_shape = pltpu.VMEM((nb * CHUNK, A_IN_PAD), F32)
    return pl.pallas_call(
        _mixer_ab_body,
        grid=(n_chunks + 1,),
        in_specs=[nxt, prev, _resident((1, D_MODEL)), _resident(w_in.shape), _resident((1, LANES)),
                  _resident((1, WIDTH)), _resident(sgu_w.shape), _resident(sgu_bt.shape), _resident(w_out.shape)],
        out_specs=[prev,
                   pl.BlockSpec((nb, HEADS, HEAD_DIM, HEAD_DIM), lambda c: (0, 0, 0, 0)),
                   pl.BlockSpec((nb, HEADS, 1, HEAD_DIM), lambda c: (0, 0, 0, 0)),
                   pl.BlockSpec((nb, HEADS, 1, LANES), lambda c: (0, 0, 0, 0))],
        out_shape=[jax.ShapeDtypeStruct(x.shape, F32),
                   jax.ShapeDtypeStruct((nb, HEADS, HEAD_DIM, HEAD_DIM), F32),
                   jax.ShapeDtypeStruct((nb, HEADS, 1, HEAD_DIM), F32),
                   jax.ShapeDtypeStruct((nb, HEADS, 1, LANES), F32)],
        scratch_shapes=[z_shape, z_shape, pltpu.VMEM((nb * CHUNK, 2 * WIDTH), BF16)],
        compiler_params=_params(("arbitrary",)),
        name="mixer_ab_prompt",
    )(x, x, g.reshape(1, D_MODEL), w_in, b_if, sgu_g.reshape(1, WIDTH), sgu_w, sgu_bt, w_out)


def _alibi_slope(group, head):
    n = len(SWA_GROUPS) * HEADS
    return 2.0 ** (-8.0 * (group * HEADS + head + 1) / n)


def _slab_pitch(dil):
    return dil + 8 if dil % 8 == 0 else dil


def _swa_group_body(*refs, group, dil, tile, others):
    x_ref, g_ref, w_ref = refs[0:3]
    if others:
        other_o = refs[3:3 + others]
        other_st = refs[3 + others:3 + 2 * others]
        wout_ref, y_ref, slab_scr, q_scr, k_scr, v_scr, o_dst, st_dst, cat_scr = refs[3 + 2 * others:]
    else:
        o_ref, st_ref, slab_scr, q_scr, k_scr, v_scr = refs[3:]
        o_dst, st_dst = o_ref.at[0], st_ref.at[0]
    step = pl.program_id(1)
    sub = tile // dil
    if sub >= PROJ_ROWS:
        pieces = [[(r, f, PROJ_ROWS)] for r in range(dil) for f in range(0, sub, PROJ_ROWS)]
    else:
        per = PROJ_ROWS // sub
        pieces = [[(r, 0, sub) for r in range(p * per, (p + 1) * per)] for p in range(dil // per)]
    scale = HEAD_DIM ** -0.5
    qi = lax.broadcasted_iota(jnp.int32, (CHUNK, 2 * CHUNK), 0)
    kc = lax.broadcasted_iota(jnp.int32, (CHUNK, 2 * CHUNK), 1)
    delta = CHUNK + qi - kc
    valid = (delta >= 0) & (delta <= CHUNK)
    valid_first = valid & (kc >= jnp.where(step > 0, 0, CHUNK))
    dist = (delta * dil).astype(F32)
    lane = lax.broadcasted_iota(jnp.int32, (CHUNK, LANES), 1)

    @pl.when(step == 0)
    def _():
        k_scr[:, 0:CHUNK, :] = jnp.zeros((dil, CHUNK, WIDTH), BF16)
        v_scr[:, 0:CHUNK, :] = jnp.zeros((dil, CHUNK, WIDTH), BF16)

    pitch = _slab_pitch(dil)
    for c in range(tile // ROW_PASS):
        hn = _rms(x_ref[0, c * ROW_PASS:(c + 1) * ROW_PASS, :], g_ref[...])
        groups = [(0, ROW_PASS)] if pitch == dil else [(g * dil, dil) for g in range(ROW_PASS // dil)]
        for first, count in groups:
            dst = (c * ROW_PASS + first) // dil * pitch
            for sl in range(D_MODEL // LANES):
                slab_scr[sl, dst:dst + count, :] = hn[first:first + count, sl * LANES:(sl + 1) * LANES]

    def token_rows(r, first, count):
        start = first * dil + r
        return slice(start, start + count) if dil == 1 else pl.ds(start, count, stride=dil)

    def project(piece):
        parts = []
        for r, first, count in piece:
            src = slice(first, first + count) if dil == 1 else pl.ds(first * pitch + r, count, stride=pitch)
            parts.append(jnp.concatenate([slab_scr[sl, src, :] for sl in range(D_MODEL // LANES)],
                                         axis=1).astype(BF16))
        hn = parts[0] if len(parts) == 1 else jnp.concatenate(parts, axis=0)
        for j, scr in enumerate((q_scr, k_scr, v_scr)):
            z = _dot(hn, w_ref[:, j * WIDTH:(j + 1) * WIDTH]).astype(BF16)
            at = 0
            for r, first, count in piece:
                off = first if j == 0 else CHUNK + first
                scr[r, off:off + count, :] = z[at:at + count, :]
                at += count

    def scores(piece):
        out = []
        for r, first, count in piece:
            for j in range(first // CHUNK, (first + count) // CHUNK):
                mask = valid_first if j == 0 else valid
                for h in range(HEADS):
                    hs = slice(h * HEAD_DIM, (h + 1) * HEAD_DIM)
                    s = _dot_nt(q_scr[r, j * CHUNK:(j + 1) * CHUNK, hs], k_scr[r, j * CHUNK:(j + 2) * CHUNK, hs])
                    s = jnp.where(mask, s * scale + (-_alibi_slope(group, h)) * dist, NEG_INF)
                    m = jnp.max(s, axis=-1, keepdims=True)
                    p_ = jnp.exp(s - m)
                    out.append((r, j, h, m, jnp.sum(p_, axis=-1, keepdims=True), p_.astype(BF16)))
        return out

    def values(items):
        stats = None
        for r, j, h, m, l, p_ in items:
            hs = slice(h * HEAD_DIM, (h + 1) * HEAD_DIM)
            o_dst[h, token_rows(r, j * CHUNK, CHUNK), :] = _dot(p_, v_scr[r, j * CHUNK:(j + 2) * CHUNK, hs])
            base = jnp.zeros((CHUNK, LANES), F32) if h == 0 else stats
            stats = jnp.where(lane == h, m, jnp.where(lane == HEADS + h, l, base))
            if h == HEADS - 1:
                st_dst[token_rows(r, j * CHUNK, CHUNK), :] = stats

    def halves(piece):
        if len(piece) == 1:
            r, first, count = piece[0]
            return [[(r, first, count // 2)], [(r, first + count // 2, count // 2)]]
        return [piece[:len(piece) // 2], piece[len(piece) // 2:]]

    def merge(piece):
        (_, first, count), = piece
        for c in range(first // ROW_PASS, (first + count) // ROW_PASS):
            rows = slice(c * ROW_PASS, (c + 1) * ROW_PASS)
            stats = [st_dst[rows, :]] + [s_ref[0, rows, :] for s_ref in other_st]
            top = functools.reduce(jnp.maximum, stats)
            ws = [jnp.exp(st - top) for st in stats]
            den = sum(w * pltpu.roll(st, LANES - HEADS, axis=1) for w, st in zip(ws, stats))
            den = jnp.where(lane[0:1, :] < HEADS, den, 1.0)
            shares = [w / den for w in ws]
            for h in range(HEADS):
                outs = [o_dst[h, rows, :]] + [o_ref[0, h, rows, :] for o_ref in other_o]
                merged = sum(a[:, h:h + 1] * o for a, o in zip(shares, outs))
                cat_scr[rows, h * HEAD_DIM:(h + 1) * HEAD_DIM] = merged.astype(BF16)
        rows = slice(first, first + count)
        y_ref[0, rows, :] = x_ref[0, rows, :] + _dot(cat_scr[rows, :], wout_ref[...])

    def finish(done):
        items, p, i = done
        values(items)
        if others and i == 1:
            merge(pieces[p])

    project(pieces[0])
    pending = None
    for p, piece in enumerate(pieces):
        for i, half in enumerate(halves(piece)):
            items = scores(half)
            if i == 0 and p + 1 < len(pieces):
                project(pieces[p + 1])
            if pending is not None:
                finish(pending)
            pending = (items, p, i)
    finish(pending)
    for r in range(dil):
        k_scr[r, 0:CHUNK, :] = k_scr[r, sub:sub + CHUNK, :]
        v_scr[r, 0:CHUNK, :] = v_scr[r, sub:sub + CHUNK, :]


def _swa_group_prompt(x, g, c_in, *, group, tile, merge_with=None):
    nb, s, _ = x.shape
    dil = SWA_GROUPS[group][1]
    sub = tile // dil
    o_spec = pl.BlockSpec((1, HEADS, tile, HEAD_DIM), lambda b, t: (b, 0, t, 0))
    s_spec = pl.BlockSpec((1, tile, LANES), lambda b, t: (b, t, 0))
    x_spec = pl.BlockSpec((1, tile, D_MODEL), lambda b, t: (b, t, 0))
    in_specs = [x_spec, _resident((1, D_MODEL)),
                pl.BlockSpec((D_MODEL, C_GROUP_COLS), lambda b, t: (0, group), pipeline_mode=pl.Buffered(1))]
    args = [x, g.reshape(1, D_MODEL), c_in]
    scratch = [pltpu.VMEM((D_MODEL // LANES, sub * _slab_pitch(dil), LANES), F32),
               pltpu.VMEM((dil, sub, WIDTH), BF16),
               pltpu.VMEM((dil, CHUNK + sub, WIDTH), BF16),
               pltpu.VMEM((dil, CHUNK + sub, WIDTH), BF16)]
    if merge_with is None:
        others = 0
        out_specs = [o_spec, s_spec]
        out_shape = [jax.ShapeDtypeStruct((nb, HEADS, s, HEAD_DIM), F32), jax.ShapeDtypeStruct((nb, s, LANES), F32)]
    else:
        outs, stats, w_out = merge_with
        others = len(outs)
        in_specs += [o_spec] * others + [s_spec] * others + [_resident(w_out.shape)]
        args += [*outs, *stats, w_out]
        out_specs = x_spec
        out_shape = jax.ShapeDtypeStruct(x.shape, F32)
        scratch += [pltpu.VMEM((HEADS, tile, HEAD_DIM), F32), pltpu.VMEM((tile, LANES), F32),
                    pltpu.VMEM((tile, WIDTH), BF16)]
    return pl.pallas_call(
        functools.partial(_swa_group_body, group=group, dil=dil, tile=tile, others=others),
        grid=(nb, s // tile),
        in_specs=in_specs,
        out_specs=out_specs,
        out_shape=out_shape,
        scratch_shapes=scratch,
        compiler_params=_params(("arbitrary", "arbitrary")),
        name="swa_group%d" % group,
    )(*args)


def _kv_tail_body(x_ref, g_ref, wk_ref, wv_ref, kv_ref):
    tm = x_ref.shape[1]
    hn = _rms(x_ref[0], g_ref[...]).astype(BF16)
    for j, w_ref in enumerate((wk_ref, wv_ref)):
        kv = _dot(hn, w_ref[...])
        for h in range(HEADS):
            kv_ref[pl.ds(j * HEADS + h, tm, stride=2 * HEADS), :] = kv[:, h * HEAD_DIM:(h + 1) * HEAD_DIM]


def _kv_tail(x, g, c_in, *, group):
    win = SWA_GROUPS[group][0]
    nb, s, _ = x.shape
    tm = min(win, 1024)
    first = (s - win) // tm
    steps = win // tm
    wcol = lambda j: pl.BlockSpec((D_MODEL, WIDTH), lambda b, t: (0, 3 * group + j), pipeline_mode=pl.Buffered(1))
    kv = pl.pallas_call(
        _kv_tail_body,
        grid=(nb, steps),
        in_specs=[pl.BlockSpec((1, tm, D_MODEL), lambda b, t: (b, first + t, 0)), _resident((1, D_MODEL)),
                  wcol(1), wcol(2)],
        out_specs=pl.BlockSpec((tm * 2 * HEADS, HEAD_DIM), lambda b, t: (b * steps + t, 0)),
        out_shape=jax.ShapeDtypeStruct((nb * win * 2 * HEADS, HEAD_DIM), F32),
        compiler_params=_params(("arbitrary", "arbitrary")),
        name="kv_tail%d" % group,
    )(x, g.reshape(1, D_MODEL), c_in, c_in)
    return kv.reshape(1, nb, win, 2, HEADS, HEAD_DIM)


def _proj_body(x_ref, g_ref, w_ref, z_ref):
    z_ref[...] = _dot(_rms(x_ref[...], g_ref[...]).astype(BF16), w_ref[...])


def _proj(x, g, w):
    m, n = x.shape[0], w.shape[1]
    return pl.pallas_call(
        _proj_body,
        grid=(1,),
        in_specs=[_resident(x.shape), _resident((1, D_MODEL)), _resident(w.shape)],
        out_specs=pl.BlockSpec((m, n), lambda i: (0, 0)),
        out_shape=jax.ShapeDtypeStruct((m, n), F32),
        compiler_params=_params(("arbitrary",)),
        name="proj_rows",
    )(x, g.reshape(1, D_MODEL), w)


def _out_proj_body(c_ref, w_ref, x_ref, y_ref):
    y_ref[...] = x_ref[...] + _dot(c_ref[...].astype(BF16), w_ref[...])


def _out_proj(cat, w, x):
    return pl.pallas_call(
        _out_proj_body,
        grid=(1,),
        in_specs=[_resident(cat.shape), _resident(w.shape), _resident(x.shape)],
        out_specs=pl.BlockSpec(x.shape, lambda i: (0, 0)),
        out_shape=jax.ShapeDtypeStruct(x.shape, F32),
        compiler_params=_params(("arbitrary",)),
        name="out_proj_rows",
    )(cat, w, x)


def _mixer_ab_step_body(z_ref, bif_ref, sg_ref, w00_ref, b0_ref, c_ref, n_ref, m_ref,
                        cat_ref, c1_ref, n1_ref, m1_ref, vn_ref):
    nb = z_ref.shape[0]
    scale = HEAD_DIM ** -0.5
    eye = (lax.broadcasted_iota(jnp.int32, (HEAD_DIM, HEAD_DIM), 0)
           == lax.broadcasted_iota(jnp.int32, (HEAD_DIM, HEAD_DIM), 1)).astype(F32)
    lane = lax.broadcasted_iota(jnp.int32, (1, LANES), 1)
    for i in range(nb):
        zr = z_ref[i:i + 1, :]
        gates = zr[:, COL_GATES:COL_GATES + LANES] + bif_ref[...]
        m_new = jnp.zeros((1, LANES), F32)
        for h in range(HEADS):
            hs = slice(h * HEAD_DIM, (h + 1) * HEAD_DIM)
            q = zr[:, hs]
            k = zr[:, WIDTH + h * HEAD_DIM:WIDTH + (h + 1) * HEAD_DIM] * scale
            v = zr[:, 2 * WIDTH + h * HEAD_DIM:2 * WIDTH + (h + 1) * HEAD_DIM]
            o = zr[:, 3 * WIDTH + h * HEAD_DIM:3 * WIDTH + (h + 1) * HEAD_DIM]
            ig = gates[:, h:h + 1]
            lf = _log_sigmoid(gates[:, HEADS + h:HEADS + h + 1])
            c0 = c_ref[i, h]
            n0 = n_ref[i, h]
            m0 = m_ref[i:i + 1, h:h + 1]
            a = m0 + lf
            m = jnp.maximum(a, ig)
            s = jnp.sum(q * k, axis=-1, keepdims=True) * jnp.exp(ig - m)
            inter = jnp.exp(a - m)
            cq_col = jnp.sum(c0 * q, axis=-1, keepdims=True)
            cq = jnp.sum(eye * cq_col, axis=0, keepdims=True)
            v_col = jnp.sum(eye * v, axis=-1, keepdims=True)
            num = s * v + inter * cq
            den = s + inter * jnp.sum(n0 * q, axis=-1, keepdims=True)
            hh = num / jnp.maximum(jnp.abs(den), jnp.exp(-m))
            w = jnp.exp(ig - m)
            c1_ref[i, h] = inter * c0 + (w * v_col) * k
            n1_ref[i, h] = inter * n0 + w * k
            m_new = jnp.where(lane == h, m, m_new)
            cat_ref[i:i + 1, hs] = jax.nn.sigmoid(o) * hh
        m1_ref[i:i + 1, :] = m_new
        vn = _rms(zr[:, COL_GV:COL_GV + WIDTH], sg_ref[...])
        vn_ref[i:i + 1, :] = vn
        cat_ref[i:i + 1, WIDTH:2 * WIDTH] = zr[:, COL_U:COL_U + WIDTH] * (w00_ref[...] * vn + b0_ref[...])


def _mixer_ab_step(z, b_if, sgu_g, w00, b0, st_c, st_n, st_m, *, nb):
    n = z.shape[0]
    rows = lambda w: pl.BlockSpec((nb, w), lambda i: (i, 0))
    c_spec = pl.BlockSpec((nb, HEADS, HEAD_DIM, HEAD_DIM), lambda i: (i, 0, 0, 0))
    n_spec = pl.BlockSpec((nb, HEADS, 1, HEAD_DIM), lambda i: (i, 0, 0, 0))
    return pl.pallas_call(
        _mixer_ab_step_body,
        grid=(n // nb,),
        in_specs=[rows(A_IN_PAD), _resident((1, LANES)), _resident((1, WIDTH)), _resident((1, WIDTH)),
                  _resident((1, WIDTH)), c_spec, n_spec, rows(LANES)],
        out_specs=[rows(2 * WIDTH), c_spec, n_spec, rows(LANES), rows(WIDTH)],
        out_shape=[jax.ShapeDtypeStruct((n, 2 * WIDTH), F32),
                   jax.ShapeDtypeStruct(st_c.shape, F32),
                   jax.ShapeDtypeStruct(st_n.shape, F32),
                   jax.ShapeDtypeStruct((n, LANES), F32),
                   jax.ShapeDtypeStruct((n, WIDTH), F32)],
        compiler_params=_params(("arbitrary",)),
        name="mixer_ab_step",
    )(z, b_if, sgu_g.reshape(1, WIDTH), w00, b0, st_c, st_n, st_m)


def _swa_step_body(z_ref, kv0_ref, kv1_ref, kv2_ref, cat_ref):
    nb = z_ref.shape[0]
    scale = HEAD_DIM ** -0.5
    steps = (CHUNK - lax.broadcasted_iota(jnp.int32, (CHUNK, 1, 1), 0)).astype(F32)
    head = lax.broadcasted_iota(jnp.int32, (1, HEADS, 1), 1)
    for i in range(nb):
        ms, ls, os_ = [], [], []
        for gi, kv_ref in enumerate((kv0_ref, kv1_ref, kv2_ref)):
            dil = SWA_GROUPS[gi][1]
            base = gi * 3 * HEADS
            q = z_ref[i, base:base + HEADS, :]
            k_new = z_ref[i, base + HEADS:base + 2 * HEADS, :]
            v_new = z_ref[i, base + 2 * HEADS:base + 3 * HEADS, :]
            kc = kv_ref[i, :, 0, 0, :, :]
            vc = kv_ref[i, :, 0, 1, :, :]
            slope = jnp.zeros((1, HEADS, 1), F32)
            for h in range(HEADS):
                slope = jnp.where(head == h, _alibi_slope(gi, h) * dil, slope)
            s = jnp.sum(kc * q[None], axis=-1, keepdims=True) * scale - slope * steps
            s_new = jnp.sum(k_new * q, axis=-1, keepdims=True) * scale
            m = jnp.maximum(jnp.max(s, axis=0), s_new)
            p = jnp.exp(s - m[None])
            p_new = jnp.exp(s_new - m)
            ms.append(m)
            ls.append(jnp.sum(p, axis=0) + p_new)
            os_.append(jnp.sum(p * vc, axis=0) + p_new * v_new)
        top = functools.reduce(jnp.maximum, ms)
        ws = [jnp.exp(m - top) for m in ms]
        num = sum(w * o for w, o in zip(ws, os_))
        den = sum(w * l for w, l in zip(ws, ls))
        cat_ref[i] = num / den


def _swa_step(z, caches, *, nb):
    n = z.shape[0]
    views = []
    specs = []
    for (win, dil), cache in zip(SWA_GROUPS, caches):
        views.append(cache.reshape(n, win // dil, dil, 2, HEADS, HEAD_DIM))
        specs.append(pl.BlockSpec((nb, CHUNK, 1, 2, HEADS, HEAD_DIM), lambda i: (i, 0, 0, 0, 0, 0)))
    return pl.pallas_call(
        _swa_step_body,
        grid=(n // nb,),
        in_specs=[pl.BlockSpec((nb,) + z.shape[1:], lambda i: (i, 0, 0))] + specs,
        out_specs=pl.BlockSpec((nb, HEADS, HEAD_DIM), lambda i: (i, 0, 0)),
        out_shape=jax.ShapeDtypeStruct((n, HEADS, HEAD_DIM), F32),
        compiler_params=_params(("arbitrary",)),
        name="swa_step",
    )(z, *views)


def _pad_cols(w, n):
    return jnp.pad(w, ((0, 0), (0, n - w.shape[1])))


def kernel(x_prompt, x_sample, state_mlstm_C, state_mlstm_n, state_mlstm_m, cache_swa_kv0, cache_swa_kv1, cache_swa_kv2, norm_g, ffn_w_gate, ffn_w_up, ffn_w_down, a_w_in, a_b_if, sgu_norm_g, sgu_w, sgu_b, a_w_out, c_w_in, c_w_out, final_norm_g):
    nb, s, _ = x_prompt.shape
    ns = x_sample.shape[0]
    assert x_sample.shape[1] == 1 and s % max(w for w, _ in SWA_GROUPS) == 0
    for (win, dil), cache in zip(SWA_GROUPS, (cache_swa_kv0, cache_swa_kv1, cache_swa_kv2)):
        assert cache.shape[2] == win and win // dil == CHUNK

    ffn_w =(jnp.swapaxes(ffn_w_gate, 2, 3), jnp.swapaxes(ffn_w_up, 2, 3), ffn_w_down)
    g_lo, g_hi = 4 * WIDTH, 4 * WIDTH + 2 * HEADS
    a_in = jnp.concatenate([a_w_in[0][:, :g_lo], a_w_in[0][:, g_hi:], _pad_cols(a_w_in[0][:, g_lo:g_hi], LANES)],
                           axis=1).astype(BF16)
    b_if = _pad_cols(a_b_if[0].reshape(1, 2 * HEADS), LANES)
    a_out = a_w_out[0].astype(BF16)
    c_in = c_w_in[0].astype(BF16)
    c_out = c_w_out[0].astype(BF16)
    sgu_bt = _pad_cols(sgu_b[0].T, LANES)
    sgu_w00 = jnp.repeat(sgu_w[0, :, 0, 0], CHUNK).reshape(1, WIDTH)
    sgu_b0 = jnp.repeat(sgu_b[0, :, 0], CHUNK).reshape(1, WIDTH)

    xp = x_prompt.reshape(nb * s, D_MODEL)
    xs = x_sample.reshape(ns, D_MODEL)
    xp, xs = _ffn(xp, xs, norm_g[0, 0], ffn_w, 0, 0, tm=512)
    xp, p_c, p_n, p_m = _mixer_ab_prompt(xp.reshape(nb, s, D_MODEL), norm_g[0, 1], a_in, b_if,
                                          sgu_norm_g[0], sgu_w[0], sgu_bt, a_out)
    z = _proj(xs, norm_g[0, 1], a_in)
    cat, s_c, s_n, s_m, s_v = _mixer_ab_step(
        z, b_if, sgu_norm_g[0], sgu_w00, sgu_b0, state_mlstm_C[0],
        state_mlstm_n[0].reshape(ns, HEADS, 1, HEAD_DIM), _pad_cols(state_mlstm_m[0], LANES), nb=8)
    xs = _out_proj(cat, a_out, xs)
    xp, xs = _ffn(xp.reshape(nb * s, D_MODEL), xs, norm_g[0, 2], ffn_w, 0, 1, tm=512)
    xp, xs = _ffn(xp, xs, norm_g[1, 0], ffn_w, 1, 0, tm=512)
    xp3 = xp.reshape(nb, s, D_MODEL)
    outs, stats = [], []
    for gi in (1, 2):
        o, st = _swa_group_prompt(xp3, norm_g[1, 1], c_in, group=gi, tile=2048)
        outs.append(o)
        stats.append(st)
    p_kv = [_kv_tail(xp3, norm_g[1, 1], c_in, group=gi) for gi in range(3)]
    xp = _swa_group_prompt(xp3, norm_g[1, 1], c_in, group=0, tile=1024, merge_with=(outs, stats, c_out))
    xp = xp.reshape(nb * s, D_MODEL)
    z = _proj(xs, norm_g[1, 1], c_in)
    cat = _swa_step(z.reshape(ns, 3 * 3 * HEADS, HEAD_DIM), (cache_swa_kv0, cache_swa_kv1, cache_swa_kv2), nb=4)
    xs = _out_proj(cat.reshape(ns, WIDTH), c_out, xs)
    y_prompt, y_sample = _ffn(xp, xs, norm_g[1, 2], ffn_w, 1, 1, final_norm_g, tm=512)
    y_prompt = y_prompt.reshape(nb, s, D_MODEL)
    y_sample = y_sample.reshape(ns, 1, D_MODEL)
    s_kv = [z[:, gi * C_GROUP_COLS + WIDTH:(gi + 1) * C_GROUP_COLS].reshape(1, ns, 1, 2, HEADS, HEAD_DIM)
            for gi in range(3)]

    return (y_prompt, y_sample,
            p_c.reshape(1, nb, HEADS, HEAD_DIM, HEAD_DIM), p_n.reshape(1, nb, HEADS, HEAD_DIM),
            p_m[:, :, 0, 0].reshape(1, nb, HEADS),
            s_c.reshape(1, ns, HEADS, HEAD_DIM, HEAD_DIM), s_n.reshape(1, ns, HEADS, HEAD_DIM),
            s_m[:, :HEADS].reshape(1, ns, HEADS), s_v.reshape(1, ns, 1, WIDTH),
            p_kv[0], p_kv[1], p_kv[2], s_kv[0], s_kv[1], s_kv[2])
```

```python
import functools

import jax
import jax.numpy as jnp
from jax import lax
from jax.experimental import pallas as pl
from jax.experimental.pallas import tpu as pltpu

F32 = jnp.float32
BF16 = jnp.bfloat16

D_MODEL = 1024
D_FF = 2752
HEADS = 4
HEAD_DIM = 128
WIDTH = HEADS * HEAD_DIM
CHUNK = 128
SWA_GROUPS = ((128, 1), (512, 4), (2048, 16))
NORM_EPS = 1e-6
NEG_INF = -1e30

LANES = 128
BF16_ROWS = 16
PROJ_SLAB = 256
PROJ_ROWS = 512
FF_CHUNK = 256
ROW_PASS = 256
A_IN_PAD = 4 * WIDTH + 2 * WIDTH + LANES
COL_U = 4 * WIDTH
COL_GV = 5 * WIDTH
COL_GATES = 6 * WIDTH
C_GROUP_COLS = 3 * WIDTH

VMEM_LIMIT = 60 * 1024 * 1024


def _params(semantics):
    return pltpu.CompilerParams(dimension_semantics=semantics, vmem_limit_bytes=VMEM_LIMIT)


def _resident(shape):
    nd = len(shape)
    return pl.BlockSpec(shape, lambda *_: (0,) * nd, pipeline_mode=pl.Buffered(1))


def _rms(x, g):
    ms = jnp.mean(x * x, axis=-1, keepdims=True)
    return x * lax.rsqrt(ms + NORM_EPS) * g


def _dot(a, b):
    return jnp.dot(a, b, preferred_element_type=F32)


def _dot_nt(a, b):
    return lax.dot_general(a, b, (((1,), (1,)), ((), ())), preferred_element_type=F32)


def _log_sigmoid(x):
    return jnp.minimum(x, 0.0) - jnp.log1p(jnp.exp(-jnp.abs(x)))


def _ff_chunks():
    return [(c0, min(c0 + FF_CHUNK, D_FF) - c0) for c0 in range(0, D_FF, FF_CHUNK)]


def _swiglu_rows(h_scr, act_scr, rows, wg_ref, wu_ref, wd_ref, before_chunk=None, before_down=None):
    for c, (c0, n) in enumerate(_ff_chunks()):
        if before_chunk is not None:
            before_chunk(c)
        cols = slice(c0, c0 + n)
        h = h_scr[0:rows, :]
        gate = _dot_nt(h, wg_ref[cols, :])
        up = _dot_nt(h, wu_ref[cols, :])
        act_scr[0:rows, cols] = (gate * jax.nn.sigmoid(gate) * up).astype(act_scr.dtype)
    if before_down is not None:
        before_down()
    whole = D_FF // FF_CHUNK * FF_CHUNK
    out = _dot(act_scr[0:rows, 0:whole], wd_ref[0:whole, :])
    if whole < D_FF:
        out = out + _dot(act_scr[0:rows, whole:D_FF], wd_ref[whole:D_FF, :])
    return out


def _ffn_body(*refs, final, layer, which):
    if final:
        x_ref, xs_ref, g_ref, wg_hbm, wu_hbm, wd_hbm, fg_ref, o_ref, os_ref, h_scr, act_scr, wg, wu, wd, sem = refs
    else:
        x_ref, xs_ref, g_ref, wg_hbm, wu_hbm, wd_hbm, o_ref, os_ref, h_scr, act_scr, wg, wu, wd, sem = refs
    tm, ns = x_ref.shape[0], xs_ref.shape[0]
    step = pl.program_id(0)
    last = pl.num_programs(0) - 1

    def weight_copy(m, c):
        src, dst = ((wg_hbm, wg), (wu_hbm, wu), (wd_hbm, wd))[m]
        c0, n = _ff_chunks()[c]
        return pltpu.make_async_copy(src.at[layer, which, pl.ds(c0, n), :], dst.at[pl.ds(c0, n), :], sem.at[m, c])

    n_chunks = len(_ff_chunks())

    @pl.when(step == 0)
    def _():
        for c in range(n_chunks):
            weight_copy(0, c).start()
            weight_copy(1, c).start()
        for c in range(n_chunks):
            weight_copy(2, c).start()

    def finish(x, acc):
        y = x + 0.5 * acc
        return _rms(y, fg_ref[...]) if final else y

    x = x_ref[...]
    h_scr[0:tm, :] = _rms(x, g_ref[...]).astype(h_scr.dtype)

    def wait_chunk(c):
        weight_copy(0, c).wait()
        weight_copy(1, c).wait()

    def wait_down():
        for c in range(n_chunks):
            weight_copy(2, c).wait()

    @pl.when(step == 0)
    def _():
        o_ref[...] = finish(x, _swiglu_rows(h_scr, act_scr, tm, wg, wu, wd, wait_chunk, wait_down))

    @pl.when((step != 0) & (step != last))
    def _():
        o_ref[...] = finish(x, _swiglu_rows(h_scr, act_scr, tm, wg, wu, wd))

    @pl.when(step == last)
    def _():
        xs = xs_ref[...]
        h_scr[tm:tm + ns, :] = _rms(xs, g_ref[...]).astype(h_scr.dtype)
        acc = _swiglu_rows(h_scr, act_scr, tm + ns, wg, wu, wd)
        o_ref[...] = finish(x, acc[0:tm])
        os_ref[...] = finish(xs, acc[tm:tm + ns])


def _ffn(x, xs, g, weights, layer, which, final_g=None, *, tm):
    m, ns = x.shape[0], xs.shape[0]
    assert m // tm >= 2
    final = final_g is not None
    wg, wu, wd = weights
    row = pl.BlockSpec((tm, D_MODEL), lambda i: (i, 0))
    in_hbm = pl.BlockSpec(memory_space=pl.ANY)
    in_specs = [row, _resident(xs.shape), _resident((1, D_MODEL)), in_hbm, in_hbm, in_hbm]
    args = [x, xs, g.reshape(1, D_MODEL), wg, wu, wd]
    if final:
        in_specs.append(_resident((1, D_MODEL)))
        args.append(final_g.reshape(1, D_MODEL))
    w_scratch = pltpu.VMEM((D_FF, D_MODEL), wg.dtype)
    return pl.pallas_call(
        functools.partial(_ffn_body, final=final, layer=layer, which=which),
        grid=(m // tm,),
        in_specs=in_specs,
        out_specs=[row, pl.BlockSpec(xs.shape, lambda i: (0, 0))],
        out_shape=[jax.ShapeDtypeStruct((m, D_MODEL), F32), jax.ShapeDtypeStruct(xs.shape, F32)],
        scratch_shapes=[pltpu.VMEM((tm + ns, D_MODEL), wg.dtype), pltpu.VMEM((tm + ns, D_FF), wg.dtype),
                        w_scratch, w_scratch, w_scratch, pltpu.SemaphoreType.DMA((3, len(_ff_chunks())))],
        compiler_params=_params(("arbitrary",)),
        name="ffn_final" if final else "ffn",
    )(*args)


def _cummax_lanes(x):
    lane = lax.broadcasted_iota(jnp.int32, x.shape, 1)
    d = 1
    while d < x.shape[1]:
        x = jnp.maximum(x, jnp.where(lane >= d, pltpu.roll(x, d, axis=1), NEG_INF))
        d *= 2
    return x


def _exact_tri_dot(tri_bf16, x):
    x1 = x.astype(BF16)
    r1 = x - x1.astype(F32)
    x2 = r1.astype(BF16)
    x3 = (r1 - x2.astype(F32)).astype(BF16)
    n = x.shape[1]
    r = _dot(tri_bf16, jnp.concatenate([x1, x2, x3], axis=1))
    return r[:, 0:n] + r[:, n:2 * n] + r[:, 2 * n:3 * n]


def _block_diag(a, b):
    zero = jnp.zeros_like(a)
    return jnp.concatenate([jnp.concatenate([a, zero], axis=1), jnp.concatenate([zero, b], axis=1)], axis=0)


def _mixer_ab_chunk(xn_ref, xp_ref, g_ref, win_ref, bif_ref, sg_ref, sw_ref, sbt_ref, wout_ref,
                   y_ref, c_ref, n_ref, m_ref, z_cur, z_nxt, cat_scr):
    nb = xn_ref.shape[0]
    row = lax.broadcasted_iota(jnp.int32, (CHUNK, CHUNK), 0)
    col = lax.broadcasted_iota(jnp.int32, (CHUNK, CHUNK), 1)
    causal = col <= row
    keys_before = row <= col
    tri = jnp.where(causal, 1.0, 0.0).astype(BF16)
    scale = HEAD_DIM ** -0.5

    hn = _rms(xn_ref[...].reshape(nb * CHUNK, D_MODEL), g_ref[...]).astype(BF16)
    slabs = [(c0, min(c0 + PROJ_SLAB, A_IN_PAD)) for c0 in range(0, A_IN_PAD, PROJ_SLAB)]

    def project(count):
        for _ in range(min(count, len(slabs))):
            c0, c1 = slabs.pop(0)
            z_nxt[:, c0:c1] = _dot(hn, win_ref[:, c0:c1])

    if z_cur is None:
        project(len(slabs))
        return

    sgu_bias = [jnp.broadcast_to(sbt_ref[:, g:g + 1], (CHUNK, CHUNK)) for g in range(HEADS)]
    vn = [_rms(z_cur[b * CHUNK:(b + 1) * CHUNK, COL_GV:COL_GV + WIDTH], sg_ref[...]) for b in range(nb)]
    for g in range(HEADS):
        gs = slice(g * CHUNK, (g + 1) * CHUNK)
        mixed = _dot(jnp.where(causal, sw_ref[g], 0.0).astype(BF16),
                     jnp.concatenate([vn[b][:, gs] for b in range(nb)], axis=1).astype(BF16))
        for b in range(nb):
            rows = slice(b * CHUNK, (b + 1) * CHUNK)
            u = z_cur[rows, COL_U + g * CHUNK:COL_U + (g + 1) * CHUNK]
            cat_scr[rows, WIDTH + g * CHUNK:WIDTH + (g + 1) * CHUNK] = (
                u * (mixed[:, b * CHUNK:(b + 1) * CHUNK] + sgu_bias[g])).astype(BF16)
    project(1)

    pieces_per_stage = -(-(len(slabs)) // (3 * nb))
    gate_terms = []
    for b in range(nb):
        rows = slice(b * CHUNK, (b + 1) * CHUNK)
        gates = z_cur[rows, COL_GATES:COL_GATES + LANES] + bif_ref[...]
        lg = jnp.where(col < HEADS, gates, _log_sigmoid(gates))
        gate_terms.append((lg, _exact_tri_dot(tri, lg)))
        project(pieces_per_stage)

    heads = []
    for b, (lg, fcum) in enumerate(gate_terms):
        rows = slice(b * CHUNK, (b + 1) * CHUNK)
        lg_t = lg.T
        fcum_t = fcum.T
        gmax = _cummax_lanes(lg_t[0:2 * HEADS, :] - jnp.concatenate([fcum_t[HEADS:2 * HEADS, :]] * 2, axis=0))
        gdiff = lg - pltpu.roll(fcum, LANES - HEADS, axis=1)
        for h0 in range(0, HEADS, 2):
            pair = []
            for h in (h0, h0 + 1):
                q = z_cur[rows, h * HEAD_DIM:(h + 1) * HEAD_DIM]
                k = z_cur[rows, WIDTH + h * HEAD_DIM:WIDTH + (h + 1) * HEAD_DIM] * scale
                v_t = z_cur[rows, 2 * WIDTH + h * HEAD_DIM:2 * WIDTH + (h + 1) * HEAD_DIM].T
                f_r = fcum_t[HEADS + h:HEADS + h + 1, :]
                i_r = lg_t[h:h + 1, :]
                g_c = gdiff[:, h:h + 1]
                c0, n0, m0 = c_ref[b, h], n_ref[b, h], m_ref[b, h]
                m = f_r + jnp.maximum(m0, gmax[h:h + 1, :])
                inter = jnp.exp(m0 + f_r - m)
                m_last = m[:, CHUNK - 1:CHUNK]
                f_last = f_r[:, CHUNK - 1:CHUNK]
                w = jnp.exp(f_last - f_r + i_r - m_last)
                decay = jnp.exp(m0[:, 0:1] + f_last - m_last)
                m_ref[b, h] = jnp.broadcast_to(m_last, (1, LANES))
                pair.append(dict(q=q, k=k, v_t=v_t, f_r=f_r, g_c=g_c, c0=c0, n0=n0, m=m, inter=inter, w=w,
                                 decay=decay))
            against_q = _dot_nt(
                jnp.concatenate([jnp.concatenate([d["k"], d["c0"], jnp.broadcast_to(d["n0"], (BF16_ROWS, HEAD_DIM))],
                                                 axis=0) for d in pair], axis=1).astype(BF16),
                _block_diag(pair[0]["q"], pair[1]["q"]).astype(BF16))
            against_k = _dot(
                jnp.concatenate([jnp.concatenate([d["v_t"] * d["w"], jnp.broadcast_to(d["w"], (BF16_ROWS, CHUNK))],
                                                 axis=0) for d in pair], axis=1).astype(BF16),
                _block_diag(pair[0]["k"], pair[1]["k"]).astype(BF16))
            for i, d in enumerate(pair):
                h = h0 + i
                mine = slice(i * HEAD_DIM, (i + 1) * HEAD_DIM)
                c_ref[b, h] = d["decay"] * d["c0"] + against_k[0:HEAD_DIM, mine]
                n_ref[b, h] = d["decay"] * d["n0"] + against_k[HEAD_DIM:HEAD_DIM + 1, mine]
            heads.append((against_q, pair))
        project(pieces_per_stage)

    partial = []
    for i, (against_q, pair) in enumerate(heads):
        s_ts, dens = [], []
        for j, d in enumerate(pair):
            mine = slice(j * CHUNK, (j + 1) * CHUNK)
            kq = against_q[0:CHUNK, mine]
            nq = against_q[CHUNK + HEAD_DIM:CHUNK + HEAD_DIM + 1, mine]
            s_t = kq * jnp.exp(jnp.where(keys_before, (d["f_r"] - d["m"]) + d["g_c"], NEG_INF))
            s_ts.append(s_t)
            dens.append(jnp.maximum(jnp.abs(jnp.sum(s_t, axis=0, keepdims=True) + d["inter"] * nq),
                                    jnp.exp(-d["m"])))
        sv = _dot(jnp.concatenate([d["v_t"] for d in pair], axis=1).astype(BF16),
                  _block_diag(s_ts[0], s_ts[1]).astype(BF16))
        for j, d in enumerate(pair):
            mine = slice(j * CHUNK, (j + 1) * CHUNK)
            partial.append((sv[:, mine], d["inter"] * against_q[CHUNK:CHUNK + HEAD_DIM, mine], dens[j]))
        if i % (HEADS // 2) == HEADS // 2 - 1:
            project(pieces_per_stage)

    for i, (sv, carried, den) in enumerate(partial):
        b, h = divmod(i, HEADS)
        rows = slice(b * CHUNK, (b + 1) * CHUNK)
        o = z_cur[rows, 3 * WIDTH + h * HEAD_DIM:3 * WIDTH + (h + 1) * HEAD_DIM]
        cat_scr[rows, h * HEAD_DIM:(h + 1) * HEAD_DIM] = (jax.nn.sigmoid(o) * ((sv + carried) / den).T).astype(BF16)

    project(len(slabs))
    y = xp_ref[...].reshape(nb * CHUNK, D_MODEL) + _dot(cat_scr[...], wout_ref[...])
    y_ref[...] = y.reshape(nb, CHUNK, D_MODEL)


def _mixer_ab_body(*refs):
    *io_refs, z0_scr, z1_scr, cat_scr = refs
    c_ref, n_ref, m_ref = io_refs[-3:]
    step = pl.program_id(0)

    @pl.when(step == 0)
    def _():
        c_ref[...] = jnp.zeros_like(c_ref)
        n_ref[...] = jnp.zeros_like(n_ref)
        m_ref[...] = jnp.zeros_like(m_ref)
        _mixer_ab_chunk(*io_refs, None, z0_scr, cat_scr)

    @pl.when((step > 0) & (step % 2 == 0))
    def _():
        _mixer_ab_chunk(*io_refs, z1_scr, z0_scr, cat_scr)

    @pl.when(step % 2 == 1)
    def _():
        _mixer_ab_chunk(*io_refs, z0_scr, z1_scr, cat_scr)


def _mixer_ab_prompt(x, g, w_in, b_if, sgu_g, sgu_w, sgu_bt, w_out):
    nb, s, _ = x.shape
    n_chunks = s // CHUNK
    blk = lambda index: pl.BlockSpec((nb, CHUNK, D_MODEL), index)
    nxt = blk(lambda c: (0, jnp.minimum(c, n_chunks - 1), 0))
    prev = blk(lambda c: (0, jnp.maximum(c - 1, 0), 0))
    z_shape = pltpu.VMEM((nb * CHUNK, A_IN_PAD), F32)
    return pl.pallas_call(
        _mixer_ab_body,
        grid=(n_chunks + 1,),
        in_specs=[nxt, prev, _resident((1, D_MODEL)), _resident(w_in.shape), _resident((1, LANES)),
                  _resident((1, WIDTH)), _resident(sgu_w.shape), _resident(sgu_bt.shape), _resident(w_out.shape)],
        out_specs=[prev,
                   pl.BlockSpec((nb, HEADS, HEAD_DIM, HEAD_DIM), lambda c: (0, 0, 0, 0)),
                   pl.BlockSpec((nb, HEADS, 1, HEAD_DIM), lambda c: (0, 0, 0, 0)),
                   pl.BlockSpec((nb, HEADS, 1, LANES), lambda c: (0, 0, 0, 0))],
        out_shape=[jax.ShapeDtypeStruct(x.shape, F32),
                   jax.ShapeDtypeStruct((nb, HEADS, HEAD_DIM, HEAD_DIM), F32),
                   jax.ShapeDtypeStruct((nb, HEADS, 1, HEAD_DIM), F32),
                   jax.ShapeDtypeStruct((nb, HEADS, 1, LANES), F32)],
        scratch_shapes=[z_shape, z_shape, pltpu.VMEM((nb * CHUNK, 2 * WIDTH), BF16)],
        compiler_params=_params(("arbitrary",)),
        name="mixer_ab_prompt",
    )(x, x, g.reshape(1, D_MODEL), w_in, b_if, sgu_g.reshape(1, WIDTH), sgu_w, sgu_bt, w_out)


def _alibi_slope(group, head):
    n = len(SWA_GROUPS) * HEADS
    return 2.0 ** (-8.0 * (group * HEADS + head + 1) / n)


def _slab_pitch(dil):
    return dil + 8 if dil % 8 == 0 else dil


def _swa_group_body(*refs, group, dil, tile, others):
    x_ref, g_ref, w_ref = refs[0:3]
    if others:
        other_o = refs[3:3 + others]
        other_st = refs[3 + others:3 + 2 * others]
        wout_ref, y_ref, slab_scr, q_scr, k_scr, v_scr, o_dst, st_dst, cat_scr = refs[3 + 2 * others:]
    else:
        o_ref, st_ref, slab_scr, q_scr, k_scr, v_scr = refs[3:]
        o_dst, st_dst = o_ref.at[0], st_ref.at[0]
    step = pl.program_id(1)
    sub = tile // dil
    if sub >= PROJ_ROWS:
        pieces = [[(r, f, PROJ_ROWS)] for r in range(dil) for f in range(0, sub, PROJ_ROWS)]
    else:
        per = PROJ_ROWS // sub
        pieces = [[(r, 0, sub) for r in range(p * per, (p + 1) * per)] for p in range(dil // per)]
    scale = HEAD_DIM ** -0.5
    qi = lax.broadcasted_iota(jnp.int32, (CHUNK, 2 * CHUNK), 0)
    kc = lax.broadcasted_iota(jnp.int32, (CHUNK, 2 * CHUNK), 1)
    delta = CHUNK + qi - kc
    valid = (delta >= 0) & (delta <= CHUNK)
    valid_first = valid & (kc >= jnp.where(step > 0, 0, CHUNK))
    dist = (delta * dil).astype(F32)
    lane = lax.broadcasted_iota(jnp.int32, (CHUNK, LANES), 1)

    @pl.when(step == 0)
    def _():
        k_scr[:, 0:CHUNK, :] = jnp.zeros((dil, CHUNK, WIDTH), BF16)
        v_scr[:, 0:CHUNK, :] = jnp.zeros((dil, CHUNK, WIDTH), BF16)

    pitch = _slab_pitch(dil)
    for c in range(tile // ROW_PASS):
        hn = _rms(x_ref[0, c * ROW_PASS:(c + 1) * ROW_PASS, :], g_ref[...])
        groups = [(0, ROW_PASS)] if pitch == dil else [(g * dil, dil) for g in range(ROW_PASS // dil)]
        for first, count in groups:
            dst = (c * ROW_PASS + first) // dil * pitch
            for sl in range(D_MODEL // LANES):
                slab_scr[sl, dst:dst + count, :] = hn[first:first + count, sl * LANES:(sl + 1) * LANES]

    def token_rows(r, first, count):
        start = first * dil + r
        return slice(start, start + count) if dil == 1 else pl.ds(start, count, stride=dil)

    def project(piece):
        parts = []
        for r, first, count in piece:
            src = slice(first, first + count) if dil == 1 else pl.ds(first * pitch + r, count, stride=pitch)
            parts.append(jnp.concatenate([slab_scr[sl, src, :] for sl in range(D_MODEL // LANES)],
                                         axis=1).astype(BF16))
        hn = parts[0] if len(parts) == 1 else jnp.concatenate(parts, axis=0)
        for j, scr in enumerate((q_scr, k_scr, v_scr)):
            z = _dot(hn, w_ref[:, j * WIDTH:(j + 1) * WIDTH]).astype(BF16)
            at = 0
            for r, first, count in piece:
                off = first if j == 0 else CHUNK + first
                scr[r, off:off + count, :] = z[at:at + count, :]
                at += count

    def scores(piece):
        out = []
        for r, first, count in piece:
            for j in range(first // CHUNK, (first + count) // CHUNK):
                mask = valid_first if j == 0 else valid
                for h in range(HEADS):
                    hs = slice(h * HEAD_DIM, (h + 1) * HEAD_DIM)
                    s = _dot_nt(q_scr[r, j * CHUNK:(j + 1) * CHUNK, hs], k_scr[r, j * CHUNK:(j + 2) * CHUNK, hs])
                    s = jnp.where(mask, s * scale + (-_alibi_slope(group, h)) * dist, NEG_INF)
                    m = jnp.max(s, axis=-1, keepdims=True)
                    p_ = jnp.exp(s - m)
                    out.append((r, j, h, m, jnp.sum(p_, axis=-1, keepdims=True), p_.astype(BF16)))
        return out

    def values(items):
        stats = None
        for r, j, h, m, l, p_ in items:
            hs = slice(h * HEAD_DIM, (h + 1) * HEAD_DIM)
            o_dst[h, token_rows(r, j * CHUNK, CHUNK), :] = _dot(p_, v_scr[r, j * CHUNK:(j + 2) * CHUNK, hs])
            base = jnp.zeros((CHUNK, LANES), F32) if h == 0 else stats
            stats = jnp.where(lane == h, m, jnp.where(lane == HEADS + h, l, base))
            if h == HEADS - 1:
                st_dst[token_rows(r, j * CHUNK, CHUNK), :] = stats

    def halves(piece):
        if len(piece) == 1:
            r, first, count = piece[0]
            return [[(r, first, count // 2)], [(r, first + count // 2, count // 2)]]
        return [piece[:len(piece) // 2], piece[len(piece) // 2:]]

    def merge(piece):
        (_, first, count), = piece
        for c in range(first // ROW_PASS, (first + count) // ROW_PASS):
            rows = slice(c * ROW_PASS, (c + 1) * ROW_PASS)
            stats = [st_dst[rows, :]] + [s_ref[0, rows, :] for s_ref in other_st]
            top = functools.reduce(jnp.maximum, stats)
            ws = [jnp.exp(st - top) for st in stats]
            den = sum(w * pltpu.roll(st, LANES - HEADS, axis=1) for w, st in zip(ws, stats))
            den = jnp.where(lane[0:1, :] < HEADS, den, 1.0)
            shares = [w / den for w in ws]
            for h in range(HEADS):
                outs = [o_dst[h, rows, :]] + [o_ref[0, h, rows, :] for o_ref in other_o]
                merged = sum(a[:, h:h + 1] * o for a, o in zip(shares, outs))
                cat_scr[rows, h * HEAD_DIM:(h + 1) * HEAD_DIM] = merged.astype(BF16)
        rows = slice(first, first + count)
        y_ref[0, rows, :] = x_ref[0, rows, :] + _dot(cat_scr[rows, :], wout_ref[...])

    def finish(done):
        items, p, i = done
        values(items)
        if others and i == 1:
            merge(pieces[p])

    project(pieces[0])
    pending = None
    for p, piece in enumerate(pieces):
        for i, half in enumerate(halves(piece)):
            items = scores(half)
            if i == 0 and p + 1 < len(pieces):
                project(pieces[p + 1])
            if pending is not None:
                finish(pending)
            pending = (items, p, i)
    finish(pending)
    for r in range(dil):
        k_scr[r, 0:CHUNK, :] = k_scr[r, sub:sub + CHUNK, :]
        v_scr[r, 0:CHUNK, :] = v_scr[r, sub:sub + CHUNK, :]


def _swa_group_prompt(x, g, c_in, *, group, tile, merge_with=None):
    nb, s, _ = x.shape
    dil = SWA_GROUPS[group][1]
    sub = tile // dil
    o_spec = pl.BlockSpec((1, HEADS, tile, HEAD_DIM), lambda b, t: (b, 0, t, 0))
    s_spec = pl.BlockSpec((1, tile, LANES), lambda b, t: (b, t, 0))
    x_spec = pl.BlockSpec((1, tile, D_MODEL), lambda b, t: (b, t, 0))
    in_specs = [x_spec, _resident((1, D_MODEL)),
                pl.BlockSpec((D_MODEL, C_GROUP_COLS), lambda b, t: (0, group), pipeline_mode=pl.Buffered(1))]
    args = [x, g.reshape(1, D_MODEL), c_in]
    scratch = [pltpu.VMEM((D_MODEL // LANES, sub * _slab_pitch(dil), LANES), F32),
               pltpu.VMEM((dil, sub, WIDTH), BF16),
               pltpu.VMEM((dil, CHUNK + sub, WIDTH), BF16),
               pltpu.VMEM((dil, CHUNK + sub, WIDTH), BF16)]
    if merge_with is None:
        others = 0
        out_specs = [o_spec, s_spec]
        out_shape = [jax.ShapeDtypeStruct((nb, HEADS, s, HEAD_DIM), F32), jax.ShapeDtypeStruct((nb, s, LANES), F32)]
    else:
        outs, stats, w_out = merge_with
        others = len(outs)
        in_specs += [o_spec] * others + [s_spec] * others + [_resident(w_out.shape)]
        args += [*outs, *stats, w_out]
        out_specs = x_spec
        out_shape = jax.ShapeDtypeStruct(x.shape, F32)
        scratch += [pltpu.VMEM((HEADS, tile, HEAD_DIM), F32), pltpu.VMEM((tile, LANES), F32),
                    pltpu.VMEM((tile, WIDTH), BF16)]
    return pl.pallas_call(
        functools.partial(_swa_group_body, group=group, dil=dil, tile=tile, others=others),
        grid=(nb, s // tile),
        in_specs=in_specs,
        out_specs=out_specs,
        out_shape=out_shape,
        scratch_shapes=scratch,
        compiler_params=_params(("arbitrary", "arbitrary")),
        name="swa_group%d" % group,
    )(*args)


def _kv_tail_body(x_ref, g_ref, wk_ref, wv_ref, kv_ref):
    tm = x_ref.shape[1]
    hn = _rms(x_ref[0], g_ref[...]).astype(BF16)
    for j, w_ref in enumerate((wk_ref, wv_ref)):
        kv = _dot(hn, w_ref[...])
        for h in range(HEADS):
            kv_ref[pl.ds(j * HEADS + h, tm, stride=2 * HEADS), :] = kv[:, h * HEAD_DIM:(h + 1) * HEAD_DIM]


def _kv_tail(x, g, c_in, *, group):
    win = SWA_GROUPS[group][0]
    nb, s, _ = x.shape
    tm = min(win, 1024)
    first = (s - win) // tm
    steps = win // tm
    wcol = lambda j: pl.BlockSpec((D_MODEL, WIDTH), lambda b, t: (0, 3 * group + j), pipeline_mode=pl.Buffered(1))
    kv = pl.pallas_call(
        _kv_tail_body,
        grid=(nb, steps),
        in_specs=[pl.BlockSpec((1, tm, D_MODEL), lambda b, t: (b, first + t, 0)), _resident((1, D_MODEL)),
                  wcol(1), wcol(2)],
        out_specs=pl.BlockSpec((tm * 2 * HEADS, HEAD_DIM), lambda b, t: (b * steps + t, 0)),
        out_shape=jax.ShapeDtypeStruct((nb * win * 2 * HEADS, HEAD_DIM), F32),
        compiler_params=_params(("arbitrary", "arbitrary")),
        name="kv_tail%d" % group,
    )(x, g.reshape(1, D_MODEL), c_in, c_in)
    return kv.reshape(1, nb, win, 2, HEADS, HEAD_DIM)


def _proj_body(x_ref, g_ref, w_ref, z_ref):
    z_ref[...] = _dot(_rms(x_ref[...], g_ref[...]).astype(BF16), w_ref[...])


def _proj(x, g, w):
    m, n = x.shape[0], w.shape[1]
    return pl.pallas_call(
        _proj_body,
        grid=(1,),
        in_specs=[_resident(x.shape), _resident((1, D_MODEL)), _resident(w.shape)],
        out_specs=pl.BlockSpec((m, n), lambda i: (0, 0)),
        out_shape=jax.ShapeDtypeStruct((m, n), F32),
        compiler_params=_params(("arbitrary",)),
        name="proj_rows",
    )(x, g.reshape(1, D_MODEL), w)


def _out_proj_body(c_ref, w_ref, x_ref, y_ref):
    y_ref[...] = x_ref[...] + _dot(c_ref[...].astype(BF16), w_ref[...])


def _out_proj(cat, w, x):
    return pl.pallas_call(
        _out_proj_body,
        grid=(1,),
        in_specs=[_resident(cat.shape), _resident(w.shape), _resident(x.shape)],
        out_specs=pl.BlockSpec(x.shape, lambda i: (0, 0)),
        out_shape=jax.ShapeDtypeStruct(x.shape, F32),
        compiler_params=_params(("arbitrary",)),
        name="out_proj_rows",
    )(cat, w, x)


def _mixer_ab_step_body(z_ref, bif_ref, sg_ref, w00_ref, b0_ref, c_ref, n_ref, m_ref,
                        cat_ref, c1_ref, n1_ref, m1_ref, vn_ref):
    nb = z_ref.shape[0]
    scale = HEAD_DIM ** -0.5
    eye = (lax.broadcasted_iota(jnp.int32, (HEAD_DIM, HEAD_DIM), 0)
           == lax.broadcasted_iota(jnp.int32, (HEAD_DIM, HEAD_DIM), 1)).astype(F32)
    lane = lax.broadcasted_iota(jnp.int32, (1, LANES), 1)
    for i in range(nb):
        zr = z_ref[i:i + 1, :]
        gates = zr[:, COL_GATES:COL_GATES + LANES] + bif_ref[...]
        m_new = jnp.zeros((1, LANES), F32)
        for h in range(HEADS):
            hs = slice(h * HEAD_DIM, (h + 1) * HEAD_DIM)
            q = zr[:, hs]
            k = zr[:, WIDTH + h * HEAD_DIM:WIDTH + (h + 1) * HEAD_DIM] * scale
            v = zr[:, 2 * WIDTH + h * HEAD_DIM:2 * WIDTH + (h + 1) * HEAD_DIM]
            o = zr[:, 3 * WIDTH + h * HEAD_DIM:3 * WIDTH + (h + 1) * HEAD_DIM]
            ig = gates[:, h:h + 1]
            lf = _log_sigmoid(gates[:, HEADS + h:HEADS + h + 1])
            c0 = c_ref[i, h]
            n0 = n_ref[i, h]
            m0 = m_ref[i:i + 1, h:h + 1]
            a = m0 + lf
            m = jnp.maximum(a, ig)
            s = jnp.sum(q * k, axis=-1, keepdims=True) * jnp.exp(ig - m)
            inter = jnp.exp(a - m)
            cq_col = jnp.sum(c0 * q, axis=-1, keepdims=True)
            cq = jnp.sum(eye * cq_col, axis=0, keepdims=True)
            v_col = jnp.sum(eye * v, axis=-1, keepdims=True)
            num = s * v + inter * cq
            den = s + inter * jnp.sum(n0 * q, axis=-1, keepdims=True)
            hh = num / jnp.maximum(jnp.abs(den), jnp.exp(-m))
            w = jnp.exp(ig - m)
            c1_ref[i, h] = inter * c0 + (w * v_col) * k
            n1_ref[i, h] = inter * n0 + w * k
            m_new = jnp.where(lane == h, m, m_new)
            cat_ref[i:i + 1, hs] = jax.nn.sigmoid(o) * hh
        m1_ref[i:i + 1, :] = m_new
        vn = _rms(zr[:, COL_GV:COL_GV + WIDTH], sg_ref[...])
        vn_ref[i:i + 1, :] = vn
        cat_ref[i:i + 1, WIDTH:2 * WIDTH] = zr[:, COL_U:COL_U + WIDTH] * (w00_ref[...] * vn + b0_ref[...])


def _mixer_ab_step(z, b_if, sgu_g, w00, b0, st_c, st_n, st_m, *, nb):
    n = z.shape[0]
    rows = lambda w: pl.BlockSpec((nb, w), lambda i: (i, 0))
    c_spec = pl.BlockSpec((nb, HEADS, HEAD_DIM, HEAD_DIM), lambda i: (i, 0, 0, 0))
    n_spec = pl.BlockSpec((nb, HEADS, 1, HEAD_DIM), lambda i: (i, 0, 0, 0))
    return pl.pallas_call(
        _mixer_ab_step_body,
        grid=(n // nb,),
        in_specs=[rows(A_IN_PAD), _resident((1, LANES)), _resident((1, WIDTH)), _resident((1, WIDTH)),
                  _resident((1, WIDTH)), c_spec, n_spec, rows(LANES)],
        out_specs=[rows(2 * WIDTH), c_spec, n_spec, rows(LANES), rows(WIDTH)],
        out_shape=[jax.ShapeDtypeStruct((n, 2 * WIDTH), F32),
                   jax.ShapeDtypeStruct(st_c.shape, F32),
                   jax.ShapeDtypeStruct(st_n.shape, F32),
                   jax.ShapeDtypeStruct((n, LANES), F32),
                   jax.ShapeDtypeStruct((n, WIDTH), F32)],
        compiler_params=_params(("arbitrary",)),
        name="mixer_ab_step",
    )(z, b_if, sgu_g.reshape(1, WIDTH), w00, b0, st_c, st_n, st_m)


def _swa_step_body(z_ref, kv0_ref, kv1_ref, kv2_ref, cat_ref):
    nb = z_ref.shape[0]
    scale = HEAD_DIM ** -0.5
    steps = (CHUNK - lax.broadcasted_iota(jnp.int32, (CHUNK, 1, 1), 0)).astype(F32)
    head = lax.broadcasted_iota(jnp.int32, (1, HEADS, 1), 1)
    for i in range(nb):
        ms, ls, os_ = [], [], []
        for gi, kv_ref in enumerate((kv0_ref, kv1_ref, kv2_ref)):
            dil = SWA_GROUPS[gi][1]
            base = gi * 3 * HEADS
            q = z_ref[i, base:base + HEADS, :]
            k_new = z_ref[i, base + HEADS:base + 2 * HEADS, :]
            v_new = z_ref[i, base + 2 * HEADS:base + 3 * HEADS, :]
            kc = kv_ref[i, :, 0, 0, :, :]
            vc = kv_ref[i, :, 0, 1, :, :]
            slope = jnp.zeros((1, HEADS, 1), F32)
            for h in range(HEADS):
                slope = jnp.where(head == h, _alibi_slope(gi, h) * dil, slope)
            s = jnp.sum(kc * q[None], axis=-1, keepdims=True) * scale - slope * steps
            s_new = jnp.sum(k_new * q, axis=-1, keepdims=True) * scale
            m = jnp.maximum(jnp.max(s, axis=0), s_new)
            p = jnp.exp(s - m[None])
            p_new = jnp.exp(s_new - m)
            ms.append(m)
            ls.append(jnp.sum(p, axis=0) + p_new)
            os_.append(jnp.sum(p * vc, axis=0) + p_new * v_new)
        top = functools.reduce(jnp.maximum, ms)
        ws = [jnp.exp(m - top) for m in ms]
        num = sum(w * o for w, o in zip(ws, os_))
        den = sum(w * l for w, l in zip(ws, ls))
        cat_ref[i] = num / den


def _swa_step(z, caches, *, nb):
    n = z.shape[0]
    views = []
    specs = []
    for (win, dil), cache in zip(SWA_GROUPS, caches):
        views.append(cache.reshape(n, win // dil, dil, 2, HEADS, HEAD_DIM))
        specs.append(pl.BlockSpec((nb, CHUNK, 1, 2, HEADS, HEAD_DIM), lambda i: (i, 0, 0, 0, 0, 0)))
    return pl.pallas_call(
        _swa_step_body,
        grid=(n // nb,),
        in_specs=[pl.BlockSpec((nb,) + z.shape[1:], lambda i: (i, 0, 0))] + specs,
        out_specs=pl.BlockSpec((nb, HEADS, HEAD_DIM), lambda i: (i, 0, 0)),
        out_shape=jax.ShapeDtypeStruct((n, HEADS, HEAD_DIM), F32),
        compiler_params=_params(("arbitrary",)),
        name="swa_step",
    )(z, *views)


def _pad_cols(w, n):
    return jnp.pad(w, ((0, 0), (0, n - w.shape[1])))


def kernel(x_prompt, x_sample, state_mlstm_C, state_mlstm_n, state_mlstm_m, cache_swa_kv0, cache_swa_kv1, cache_swa_kv2, norm_g, ffn_w_gate, ffn_w_up, ffn_w_down, a_w_in, a_b_if, sgu_norm_g, sgu_w, sgu_b, a_w_out, c_w_in, c_w_out, final_norm_g):
    nb, s, _ = x_prompt.shape
    ns = x_sample.shape[0]
    assert x_sample.shape[1] == 1 and s % max(w for w, _ in SWA_GROUPS) == 0
    for (win, dil), cache in zip(SWA_GROUPS, (cache_swa_kv0, cache_swa_kv1, cache_swa_kv2)):
        assert cache.shape[2] == win and win // dil == CHUNK

    ffn_w =(jnp.swapaxes(ffn_w_gate, 2, 3), jnp.swapaxes(ffn_w_up, 2, 3), ffn_w_down)
    g_lo, g_hi = 4 * WIDTH, 4 * WIDTH + 2 * HEADS
    a_in = jnp.concatenate([a_w_in[0][:, :g_lo], a_w_in[0][:, g_hi:], _pad_cols(a_w_in[0][:, g_lo:g_hi], LANES)],
                           axis=1).astype(BF16)
    b_if = _pad_cols(a_b_if[0].reshape(1, 2 * HEADS), LANES)
    a_out = a_w_out[0].astype(BF16)
    c_in = c_w_in[0].astype(BF16)
    c_out = c_w_out[0].astype(BF16)
    sgu_bt = _pad_cols(sgu_b[0].T, LANES)
    sgu_w00 = jnp.repeat(sgu_w[0, :, 0, 0], CHUNK).reshape(1, WIDTH)
    sgu_b0 = jnp.repeat(sgu_b[0, :, 0], CHUNK).reshape(1, WIDTH)

    xp = x_prompt.reshape(nb * s, D_MODEL)
    xs = x_sample.reshape(ns, D_MODEL)
    xp, xs = _ffn(xp, xs, norm_g[0, 0], ffn_w, 0, 0, tm=512)
    xp, p_c, p_n, p_m = _mixer_ab_prompt(xp.reshape(nb, s, D_MODEL), norm_g[0, 1], a_in, b_if,
                                          sgu_norm_g[0], sgu_w[0], sgu_bt, a_out)
    z = _proj(xs, norm_g[0, 1], a_in)
    cat, s_c, s_n, s_m, s_v = _mixer_ab_step(
        z, b_if, sgu_norm_g[0], sgu_w00, sgu_b0, state_mlstm_C[0],
        state_mlstm_n[0].reshape(ns, HEADS, 1, HEAD_DIM), _pad_cols(state_mlstm_m[0], LANES), nb=8)
    xs = _out_proj(cat, a_out, xs)
    xp, xs = _ffn(xp.reshape(nb * s, D_MODEL), xs, norm_g[0, 2], ffn_w, 0, 1, tm=512)
    xp, xs = _ffn(xp, xs, norm_g[1, 0], ffn_w, 1, 0, tm=512)
    xp3 = xp.reshape(nb, s, D_MODEL)
    outs, stats = [], []
    for gi in (1, 2):
        o, st = _swa_group_prompt(xp3, norm_g[1, 1], c_in, group=gi, tile=2048)
        outs.append(o)
        stats.append(st)
    p_kv = [_kv_tail(xp3, norm_g[1, 1], c_in, group=gi) for gi in range(3)]
    xp = _swa_group_prompt(xp3, norm_g[1, 1], c_in, group=0, tile=1024, merge_with=(outs, stats, c_out))
    xp = xp.reshape(nb * s, D_MODEL)
    z = _proj(xs, norm_g[1, 1], c_in)
    cat = _swa_step(z.reshape(ns, 3 * 3 * HEADS, HEAD_DIM), (cache_swa_kv0, cache_swa_kv1, cache_swa_kv2), nb=4)
    xs = _out_proj(cat.reshape(ns, WIDTH), c_out, xs)
    y_prompt, y_sample = _ffn(xp, xs, norm_g[1, 2], ffn_w, 1, 1, final_norm_g, tm=512)
    y_prompt = y_prompt.reshape(nb, s, D_MODEL)
    y_sample = y_sample.reshape(ns, 1, D_MODEL)
    s_kv = [z[:, gi * C_GROUP_COLS + WIDTH:(gi + 1) * C_GROUP_COLS].reshape(1, ns, 1, 2, HEADS, HEAD_DIM)
            for gi in range(3)]

    return (y_prompt, y_sample,
            p_c.reshape(1, nb, HEADS, HEAD_DIM, HEAD_DIM), p_n.reshape(1, nb, HEADS, HEAD_DIM),
            p_m[:, :, 0, 0].reshape(1, nb, HEADS),
            s_c.reshape(1, ns, HEADS, HEAD_DIM, HEAD_DIM), s_n.reshape(1, ns, HEADS, HEAD_DIM),
            s_m[:, :HEADS].reshape(1, ns, HEADS), s_v.reshape(1, ns, 1, WIDTH),
            p_kv[0], p_kv[1], p_kv[2], s_kv[0], s_kv[1], s_kv[2])
```
